```python
import math
import jax
import jax.numpy as jnp
from jax import lax
import numpy as np

D_MODEL = 2048
BATCH = 32
SEQ = 256
DEPTH = 2
DEC_BATCH = 8
DEC_SEQ = 2048
PAST_LEN = 256

GRID_W = 64
EPS = 1e-6
S5_GROUP_CH = 16
S5_GROUPS = 32
S5_WIDTH = S5_GROUPS * S5_GROUP_CH
S5_STATE = 64
FNET_GROUPS = 4
FNET_GROUP_CH = 128
FNET_WIDTH = FNET_GROUPS * FNET_GROUP_CH
N_HEADS = 8
HEAD_DIM = 64
VALUE_DIM = 2 * HEAD_DIM
QK_WIDTH = N_HEADS * 2 * HEAD_DIM
V_WIDTH = N_HEADS * VALUE_DIM
IN_WIDTH = S5_WIDTH + FNET_WIDTH + 2 * QK_WIDTH + V_WIDTH
N_BRANCHES = 3
Q_BLOCK = 128
ROPE_BASE = 10000.0
ROPE_AXIS_DIM = HEAD_DIM // 2
N_EXPERTS = 16
EXPERT_FF = 2048
CAPACITY_FACTOR = 2

kernel_name = 'hybrid_s5_fnet_diffattn_ecmoe_diffusion_step'


def rmsnorm(x, g):
    xf = x.astype(jnp.float32)
    xf = xf * lax.rsqrt(jnp.mean(xf * xf, axis=-1, keepdims=True) + EPS)
    return xf.astype(x.dtype) * g


def axial_rope_tables(t_len):
    rows = t_len // GRID_W
    pos_row = jnp.broadcast_to(jnp.arange(rows, dtype=jnp.float32)[:, None], (rows, GRID_W)).reshape(-1)
    pos_col = jnp.broadcast_to(jnp.arange(GRID_W, dtype=jnp.float32)[None, :], (rows, GRID_W)).reshape(-1)
    inv = ROPE_BASE ** (-jnp.arange(0, ROPE_AXIS_DIM, 2, dtype=jnp.float32) / ROPE_AXIS_DIM)
    ang_r = pos_row[:, None] * inv
    ang_c = pos_col[:, None] * inv
    cos = jnp.concatenate([jnp.cos(ang_r), jnp.cos(ang_r), jnp.cos(ang_c), jnp.cos(ang_c)], axis=-1)
    sin = jnp.concatenate([jnp.sin(ang_r), jnp.sin(ang_r), jnp.sin(ang_c), jnp.sin(ang_c)], axis=-1)
    return cos, sin


def apply_axial_rope(x, cos, sin):
    xr = x.reshape(x.shape[:-1] + (2, 2, ROPE_AXIS_DIM // 2))
    rot = jnp.concatenate([-xr[..., 1:, :], xr[..., :1, :]], axis=-2).reshape(x.shape)
    cos = cos[None, :, None, None, :].astype(x.dtype)
    sin = sin[None, :, None, None, :].astype(x.dtype)
    return x * cos + rot * sin


def s5_mixer(u, lp, init_re, init_im):
    f32 = jnp.float32
    bsz, t_len, _ = u.shape
    ug = u.reshape(bsz, t_len, S5_GROUPS, S5_GROUP_CH).astype(f32)
    lam_re = lp['s5_lam_re'].astype(f32)
    lam_im = lp['s5_lam_im'].astype(f32)
    dt = jnp.exp(lp['s5_log_dt'].astype(f32))[..., None]
    b_re = lp['s5_b_re'].astype(f32)
    b_im = lp['s5_b_im'].astype(f32)
    c_re = lp['s5_c_re'].astype(f32)
    c_im = lp['s5_c_im'].astype(f32)
    mag = jnp.exp(lam_re * dt)
    ab_re = mag * jnp.cos(lam_im * dt)
    ab_im = mag * jnp.sin(lam_im * dt)
    den = lam_re * lam_re + lam_im * lam_im
    f_re = ((ab_re - 1.0) * lam_re + ab_im * lam_im) / den
    f_im = (ab_im * lam_re - (ab_re - 1.0) * lam_im) / den
    bb_re = f_re[..., None] * b_re - f_im[..., None] * b_im
    bb_im = f_re[..., None] * b_im + f_im[..., None] * b_re
    u_dir = jnp.stack([ug, jnp.flip(ug, axis=1)])
    bu_re = jnp.einsum('dbtgh,dgph->dbtgp', u_dir, bb_re)
    bu_im = jnp.einsum('dbtgh,dgph->dbtgp', u_dir, bb_im)
    if init_re is not None:
        h0_re = jnp.swapaxes(init_re.astype(f32), 0, 1)
        h0_im = jnp.swapaxes(init_im.astype(f32), 0, 1)
        ar = ab_re[:, None]
        ai = ab_im[:, None]
        bu_re = bu_re.at[:, :, 0].add(ar * h0_re - ai * h0_im)
        bu_im = bu_im.at[:, :, 0].add(ar * h0_im + ai * h0_re)
    a_re = jnp.broadcast_to(ab_re[:, None, None], bu_re.shape)
    a_im = jnp.broadcast_to(ab_im[:, None, None], bu_im.shape)

    def combine(e1, e2):
        a1r, a1i, b1r, b1i = e1
        a2r, a2i, b2r, b2i = e2
        return (a2r * a1r - a2i * a1i, a2r * a1i + a2i * a1r,
                a2r * b1r - a2i * b1i + b2r, a2r * b1i + a2i * b1r + b2i)

    _, _, xs_re, xs_im = lax.associative_scan(combine, (a_re, a_im, bu_re, bu_im), axis=2)
    fin_re = jnp.swapaxes(xs_re[:, :, -1], 0, 1)
    fin_im = jnp.swapaxes(xs_im[:, :, -1], 0, 1)
    xs_re = jnp.stack([xs_re[0], jnp.flip(xs_re[1], axis=1)])
    xs_im = jnp.stack([xs_im[0], jnp.flip(xs_im[1], axis=1)])
    y = (jnp.einsum('dghp,dbtgp->btgh', c_re, xs_re) - jnp.einsum('dghp,dbtgp->btgh', c_im, xs_im)
         + lp['s5_d'].astype(f32) * ug)
    y = y.reshape(bsz, t_len, S5_WIDTH).astype(u.dtype)
    return y, fin_re.astype(u.dtype), fin_im.astype(u.dtype)


def fourier_mixer(u):
    bsz, t_len, _ = u.shape
    ug = u.reshape(bsz, t_len, FNET_GROUPS, FNET_GROUP_CH).astype(jnp.float32)
    f = jnp.fft.fft2(ug, axes=(1, 3), norm='ortho').real
    return f.reshape(bsz, t_len, FNET_WIDTH).astype(u.dtype)


def diff_attention(q, k, v, lp, lam_init):
    bsz, t_len = q.shape[:2]
    f32 = jnp.float32
    lam = (jnp.exp(jnp.sum(lp['lam_q1'].astype(f32) * lp['lam_k1'].astype(f32)))
           - jnp.exp(jnp.sum(lp['lam_q2'].astype(f32) * lp['lam_k2'].astype(f32))) + lam_init)
    scale = HEAD_DIM ** -0.5
    nb = t_len // Q_BLOCK
    qb = jnp.moveaxis(q.reshape(bsz, nb, Q_BLOCK, N_HEADS, 2, HEAD_DIM), 1, 0)

    def block(qi):
        s = jnp.einsum('bqhnd,bkhnd->bhnqk', qi, k).astype(f32) * scale
        a = jax.nn.softmax(s, axis=-1)
        w = a[:, :, 0] - lam * a[:, :, 1]
        return jnp.einsum('bhqk,bkhe->bqhe', w.astype(v.dtype), v)

    o = lax.map(block, qb)
    o = jnp.moveaxis(o, 0, 1).reshape(bsz, t_len, N_HEADS, VALUE_DIM)
    o = rmsnorm(o, lp['subln_g']) * (1.0 - lam_init)
    return o.reshape(bsz, t_len, V_WIDTH)


def expert_choice_moe(h, lp):
    bsz, t_len, _ = h.shape
    cap = CAPACITY_FACTOR * t_len // N_EXPERTS
    aff = jax.nn.softmax((h @ lp['w_router']).astype(jnp.float32), axis=-1)
    g, idx = lax.top_k(jnp.swapaxes(aff, 1, 2), cap)
    bidx = jnp.arange(bsz)[:, None, None]
    xg = h[bidx, idx]
    a = jnp.einsum('becd,edf->becf', xg, lp['moe_w1'])
    b = jnp.einsum('becd,edf->becf', xg, lp['moe_w3'])
    y = jnp.einsum('becf,efd->becd', jax.nn.silu(a) * b, lp['moe_w2']) * g[..., None].astype(h.dtype)
    return jnp.zeros_like(h).at[bidx, idx].add(y)


def trunk_layer(x, mod, lp, lam_init, rope, ctx_k, ctx_v, init_re, init_im):
    bsz, t_len, _ = x.shape
    shift1, scale1, gate1, shift2, scale2, gate2 = jnp.split(mod, 6, axis=-1)
    h = rmsnorm(x, lp['norm1_g']) * (1.0 + scale1) + shift1
    u = h @ lp['w_in']
    o1 = S5_WIDTH
    o2 = o1 + FNET_WIDTH
    o3 = o2 + QK_WIDTH
    o4 = o3 + QK_WIDTH
    u_a, u_b, u_q, u_k, u_v = jnp.split(u, [o1, o2, o3, o4], axis=-1)
    y_a, fin_re, fin_im = s5_mixer(u_a, lp, init_re, init_im)
    y_a = jax.nn.gelu(y_a)
    y_a = y_a * jax.nn.sigmoid(y_a @ lp['w_glu'] + lp['b_glu'])
    y_b = fourier_mixer(u_b)
    q = u_q.reshape(bsz, t_len, N_HEADS, 2, HEAD_DIM)
    k = u_k.reshape(bsz, t_len, N_HEADS, 2, HEAD_DIM)
    v = u_v.reshape(bsz, t_len, N_HEADS, VALUE_DIM)
    if rope is not None:
        q = apply_axial_rope(q, rope[0], rope[1])
        k = apply_axial_rope(k, rope[0], rope[1])
    k_keys, v_keys = k, v
    if ctx_k is not None:
        ck = ctx_k.reshape(ctx_k.shape[0], ctx_k.shape[1], N_HEADS, 2, HEAD_DIM)
        k_keys = jnp.concatenate([k, ck], axis=1)
        v_keys = jnp.concatenate([v, ctx_v], axis=1)
    y_c = diff_attention(q, k_keys, v_keys, lp, lam_init)
    gates = jax.nn.sigmoid(h @ lp['w_gate'] + lp['b_gate'])
    g_a, g_b, g_c = jnp.split(gates, N_BRANCHES, axis=-1)
    merged = (g_a * (y_a @ lp['w_branch_a']) + g_b * (y_b @ lp['w_branch_b'])
              + g_c * (y_c @ lp['w_branch_c']))
    x = x + gate1 * (merged @ lp['w_out'])
    h2 = rmsnorm(x, lp['norm2_g']) * (1.0 + scale2) + shift2
    x = x + gate2 * expert_choice_moe(h2, lp)
    return x, k.reshape(bsz, t_len, N_HEADS, 2 * HEAD_DIM), v, fin_re, fin_im


def setup_inputs(seed: int = 0) -> dict:
    key = jax.random.key(seed)
    keys = jax.random.split(key, 48)
    counter = [0]

    def nk():
        counter[0] += 1
        return keys[counter[0] - 1]

    def nrm(shape, scale):
        return jax.random.normal(nk(), shape, jnp.float32) * scale

    D = D_MODEL
    n_idx = jnp.arange(S5_STATE, dtype=jnp.float32)
    s5_shape = (DEPTH, 2, S5_GROUPS, S5_STATE)
    return {
        'x_prompt': nrm((BATCH, SEQ, D), 1.0),
        'x_sample': nrm((DEC_BATCH, DEC_SEQ, D), 1.0),
        'cache_k': nrm((DEC_BATCH, DEPTH, PAST_LEN, N_HEADS, 2 * HEAD_DIM), 1.0),
        'cache_v': nrm((DEC_BATCH, DEPTH, PAST_LEN, N_HEADS, 2 * HEAD_DIM), 1.0),
        'state_s5_re': nrm((DEC_BATCH, DEPTH, 2, S5_GROUPS, S5_STATE), 0.1),
        'state_s5_im': nrm((DEC_BATCH, DEPTH, 2, S5_GROUPS, S5_STATE), 0.1),
        'c': nrm((DEC_BATCH, D), 1.0),
        'c_ctx': nrm((D,), 1.0),
        'norm1_g': 1.0 + nrm((DEPTH, D), 0.02),
        'norm2_g': 1.0 + nrm((DEPTH, D), 0.02),
        'final_norm_g': 1.0 + nrm((D,), 0.02),
        'w_ada': nrm((DEPTH, D, 6 * D), 0.2 * D ** -0.5),
        'b_ada': nrm((DEPTH, 6 * D), 0.02),
        'w_in': nrm((DEPTH, D, IN_WIDTH), D ** -0.5),
        's5_lam_re': -0.5 + nrm(s5_shape, 0.01),
        's5_lam_im': math.pi * n_idx + nrm(s5_shape, 0.01),
        's5_log_dt': jax.random.uniform(nk(), (DEPTH, 2, S5_GROUPS), jnp.float32, math.log(1e-3), math.log(1e-1)),
        's5_b_re': nrm((DEPTH, 2, S5_GROUPS, S5_STATE, S5_GROUP_CH), (2 * S5_GROUP_CH) ** -0.5),
        's5_b_im': nrm((DEPTH, 2, S5_GROUPS, S5_STATE, S5_GROUP_CH), (2 * S5_GROUP_CH) ** -0.5),
        's5_c_re': nrm((DEPTH, 2, S5_GROUPS, S5_GROUP_CH, S5_STATE), (2 * S5_STATE) ** -0.5),
        's5_c_im': nrm((DEPTH, 2, S5_GROUPS, S5_GROUP_CH, S5_STATE), (2 * S5_STATE) ** -0.5),
        's5_d': nrm((DEPTH, S5_GROUPS, S5_GROUP_CH), 1.0),
        'w_glu': nrm((DEPTH, S5_WIDTH, S5_WIDTH), S5_WIDTH ** -0.5),
        'b_glu': nrm((DEPTH, S5_WIDTH), 0.02),
        'lam_q1': nrm((DEPTH, HEAD_DIM), 0.1),
        'lam_k1': nrm((DEPTH, HEAD_DIM), 0.1),
        'lam_q2': nrm((DEPTH, HEAD_DIM), 0.1),
        'lam_k2': nrm((DEPTH, HEAD_DIM), 0.1),
        'subln_g': 1.0 + nrm((DEPTH, VALUE_DIM), 0.02),
        'w_branch_a': nrm((DEPTH, S5_WIDTH, D), S5_WIDTH ** -0.5),
        'w_branch_b': nrm((DEPTH, FNET_WIDTH, D), FNET_WIDTH ** -0.5),
        'w_branch_c': nrm((DEPTH, V_WIDTH, D), V_WIDTH ** -0.5),
        'w_gate': nrm((DEPTH, D, N_BRANCHES * D), D ** -0.5),
        'b_gate': nrm((DEPTH, N_BRANCHES * D), 0.02),
        'w_out': nrm((DEPTH, D, D), D ** -0.5),
        'w_router': nrm((DEPTH, D, N_EXPERTS), D ** -0.5),
        'moe_w1': nrm((DEPTH, N_EXPERTS, D, EXPERT_FF), D ** -0.5),
        'moe_w3': nrm((DEPTH, N_EXPERTS, D, EXPERT_FF), D ** -0.5),
        'moe_w2': nrm((DEPTH, N_EXPERTS, EXPERT_FF, D), EXPERT_FF ** -0.5),
    }


def reference(x_prompt, x_sample, cache_k, cache_v, state_s5_re, state_s5_im, c, c_ctx,
              norm1_g, norm2_g, final_norm_g, w_ada, b_ada, w_in,
              s5_lam_re, s5_lam_im, s5_log_dt, s5_b_re, s5_b_im, s5_c_re, s5_c_im, s5_d,
              w_glu, b_glu, lam_q1, lam_k1, lam_q2, lam_k2, subln_g,
              w_branch_a, w_branch_b, w_branch_c, w_gate, b_gate, w_out,
              w_router, moe_w1, moe_w3, moe_w2):
    params = dict(norm1_g=norm1_g, norm2_g=norm2_g, w_ada=w_ada, b_ada=b_ada, w_in=w_in,
                  s5_lam_re=s5_lam_re, s5_lam_im=s5_lam_im, s5_log_dt=s5_log_dt,
                  s5_b_re=s5_b_re, s5_b_im=s5_b_im, s5_c_re=s5_c_re, s5_c_im=s5_c_im, s5_d=s5_d,
                  w_glu=w_glu, b_glu=b_glu, lam_q1=lam_q1, lam_k1=lam_k1, lam_q2=lam_q2, lam_k2=lam_k2,
                  subln_g=subln_g, w_branch_a=w_branch_a, w_branch_b=w_branch_b, w_branch_c=w_branch_c,
                  w_gate=w_gate, b_gate=b_gate, w_out=w_out, w_router=w_router,
                  moe_w1=moe_w1, moe_w3=moe_w3, moe_w2=moe_w2)
    rope = axial_rope_tables(x_sample.shape[1])
    x_ctx, x_lat = x_prompt, x_sample
    ks, vs, s_res, s_ims = [], [], [], []
    for l in range(DEPTH):
        lp = {name: arr[l] for name, arr in params.items()}
        lam_init = 0.8 - 0.6 * math.exp(-0.3 * l)
        mod_ctx = (jax.nn.silu(c_ctx) @ lp['w_ada'] + lp['b_ada'])[None, None, :]
        x_ctx, k_c, v_c, s_re, s_im = trunk_layer(x_ctx, mod_ctx, lp, lam_init, None, None, None, None, None)
        ks.append(k_c)
        vs.append(v_c)
        s_res.append(s_re)
        s_ims.append(s_im)
        mod_lat = (jax.nn.silu(c) @ lp['w_ada'] + lp['b_ada'])[:, None, :]
        x_lat = trunk_layer(x_lat, mod_lat, lp, lam_init, rope, cache_k[:, l], cache_v[:, l],
                            state_s5_re[:, l], state_s5_im[:, l])[0]
    y_prompt = rmsnorm(x_ctx, final_norm_g)
    y_sample = rmsnorm(x_lat, final_norm_g)
    new_k = jnp.stack(ks, axis=1)
    new_v = jnp.stack(vs, axis=1)
    new_s5_re = jnp.stack(s_res, axis=1)
    new_s5_im = jnp.stack(s_ims, axis=1)
    return (y_prompt, y_sample, new_k, new_v, new_s5_re, new_s5_im)
```

```python
import functools
import math

import jax
import jax.numpy as jnp
from jax import lax
from jax.experimental import pallas as pl
from jax.experimental.pallas import tpu as pltpu

F32 = jnp.float32
BF16 = jnp.bfloat16

D = 2048
BATCH, SEQ = 32, 256
DEC_BATCH, DEC_SEQ = 8, 2048
DEPTH = 2
PAST_LEN = 256
GRID_W = 64
EPS = 1e-6
S5_GROUP_CH, S5_GROUPS, S5_STATE = 16, 32, 64
S5_WIDTH = S5_GROUPS * S5_GROUP_CH
FNET_GROUPS, FNET_GROUP_CH = 4, 128
FNET_WIDTH = FNET_GROUPS * FNET_GROUP_CH
N_HEADS, HEAD_DIM = 8, 64
VALUE_DIM = 2 * HEAD_DIM
QK_WIDTH = N_HEADS * 2 * HEAD_DIM
V_WIDTH = N_HEADS * VALUE_DIM
IN_WIDTH = S5_WIDTH + FNET_WIDTH + 2 * QK_WIDTH + V_WIDTH
ROPE_BASE = 10000.0
ROPE_AXIS_DIM = HEAD_DIM // 2
N_EXPERTS = 16
EXPERT_FF = 2048
CAPACITY_FACTOR = 2

M_CTX = BATCH * SEQ
M_LAT = DEC_BATCH * DEC_SEQ
M = M_CTX + M_LAT
N_MOD_ROWS = 16

LANES = 128
SUBLANES = 8
VMEM_LIMIT = 56 * 1024 * 1024

ROUTE_TILE = 2048
ROUTE_SLOTS = CAPACITY_FACTOR * ROUTE_TILE // N_EXPERTS
N_ROUTE_TILES = M // ROUTE_TILE
SLOTS_PER_EXPERT = N_ROUTE_TILES * ROUTE_SLOTS

S5_BLK = 4
S5_BLK_STATES = 512


def _cparams(sem):
    return pltpu.CompilerParams(dimension_semantics=sem, vmem_limit_bytes=VMEM_LIMIT)


def _mod_row(i, tm):
    nct = M_CTX // tm
    return jnp.where(i < nct, 0, 1 + (i - nct) // (DEC_SEQ // tm))


def _sigmoid(x):
    return 1.0 / (1.0 + jnp.exp(-x))


def _ada_kernel(c_ref, w_ref, b_ref, o_ref):
    c = c_ref[...]
    s = (c * _sigmoid(c)).astype(BF16)
    o_ref[...] = jnp.dot(s, w_ref[...].astype(BF16), preferred_element_type=F32) + b_ref[...]


def _ada(cc, w_ada, b_ada):
    tn = 1024
    n = 6 * D
    return pl.pallas_call(
        _ada_kernel,
        grid=(DEPTH, n // tn),
        in_specs=[
            pl.BlockSpec((N_MOD_ROWS, D), lambda l, j: (0, 0)),
            pl.BlockSpec((None, D, tn), lambda l, j: (l, 0, j)),
            pl.BlockSpec((None, 1, tn), lambda l, j: (l, 0, j)),
        ],
        out_specs=pl.BlockSpec((None, N_MOD_ROWS, tn), lambda l, j: (l, 0, j)),
        out_shape=jax.ShapeDtypeStruct((DEPTH, N_MOD_ROWS, n), F32),
        compiler_params=_cparams(("arbitrary", "arbitrary")),
        name="ada",
    )(cc, w_ada, b_ada.reshape(DEPTH, 1, n))


def _norm_proj_kernel(x_ref, g_ref, sc_ref, sh_ref, w_ref, h_ref, u_ref):
    @pl.when(pl.program_id(1) == 0)
    def _():
        x = x_ref[...]
        r = x * lax.rsqrt(jnp.mean(x * x, axis=-1, keepdims=True) + EPS)
        h = (r * g_ref[...]) * (1.0 + sc_ref[...]) + sh_ref[...]
        h_ref[...] = h.astype(BF16)

    u_ref[...] = jnp.dot(h_ref[...], w_ref[...], preferred_element_type=F32)


def _norm_proj(x, g, mod4, w_in):
    tm, tn = 1024, 1024
    return pl.pallas_call(
        _norm_proj_kernel,
        grid=(M // tm, IN_WIDTH // tn),
        in_specs=[
            pl.BlockSpec((tm, D), lambda i, j: (i, 0)),
            pl.BlockSpec((1, D), lambda i, j: (0, 0)),
            pl.BlockSpec((None, None, 1, D), lambda i, j: (_mod_row(i, tm), 1, 0, 0)),
            pl.BlockSpec((None, None, 1, D), lambda i, j: (_mod_row(i, tm), 0, 0, 0)),
            pl.BlockSpec((D, tn), lambda i, j: (0, j)),
        ],
        out_specs=[
            pl.BlockSpec((tm, D), lambda i, j: (i, 0)),
            pl.BlockSpec((tm, tn), lambda i, j: (i, j)),
        ],
        out_shape=[
            jax.ShapeDtypeStruct((M, D), BF16),
            jax.ShapeDtypeStruct((M, IN_WIDTH), F32),
        ],
        compiler_params=_cparams(("arbitrary", "arbitrary")),
        name="norm_proj",
    )(x, g.reshape(1, D), mod4, mod4, w_in)


def _s5_kernel(u_ref, wb_ref, wc_ref, are_ref, aim_ref, h0re_ref, h0im_ref,
               y_ref, finre_ref, finim_ref, xs_ref, stre_ref, stim_ref, *, chunk):
    d = pl.program_id(1)
    c = pl.program_id(2)
    nc = pl.num_programs(2)
    bs = S5_BLK_STATES

    @pl.when(c == 0)
    def _():
        stre_ref[...] = h0re_ref[...]
        stim_ref[...] = h0im_ref[...]

    u = u_ref[...].astype(BF16)
    for blk in range(S5_BLK):
        xs_ref[:, blk * 2 * bs:(blk + 1) * 2 * bs] = jnp.dot(
            u[:, blk * LANES:(blk + 1) * LANES], wb_ref[blk], preferred_element_type=F32)

    for blk in range(S5_BLK):
        ar = are_ref[:, blk * bs:(blk + 1) * bs]
        ai = aim_ref[:, blk * bs:(blk + 1) * bs]
        cre = slice(blk * 2 * bs, blk * 2 * bs + bs)
        cim = slice(blk * 2 * bs + bs, (blk + 1) * 2 * bs)

        def step(i, carry, ar=ar, ai=ai, cre=cre, cim=cim):
            sr, si = carry
            t = jnp.where(d == 0, i, chunk - 1 - i)
            rows = pl.ds(pl.multiple_of(t * SUBLANES, SUBLANES), SUBLANES)
            nr = ar * sr - ai * si + xs_ref[rows, cre]
            ni = ar * si + ai * sr + xs_ref[rows, cim]
            xs_ref[rows, cre] = nr
            xs_ref[rows, cim] = ni
            return nr, ni

        sr, si = lax.fori_loop(
            0, chunk, step,
            (stre_ref[:, blk * bs:(blk + 1) * bs], stim_ref[:, blk * bs:(blk + 1) * bs]),
            unroll=2)
        stre_ref[:, blk * bs:(blk + 1) * bs] = sr
        stim_ref[:, blk * bs:(blk + 1) * bs] = si

    for blk in range(S5_BLK):
        xb = xs_ref[:, blk * 2 * bs:(blk + 1) * 2 * bs].astype(BF16)
        y_ref[:, blk * LANES:(blk + 1) * LANES] = jnp.dot(xb, wc_ref[blk], preferred_element_type=F32)

    @pl.when(c == nc - 1)
    def _():
        finre_ref[...] = stre_ref[...]
        finim_ref[...] = stim_ref[...]


def _s5(u_t, wb, wc, a_re, a_im, h0_re, h0_im, t_len, chunk=64):
    ng = u_t.shape[0]
    nc = t_len // chunk
    rows = chunk * SUBLANES
    ns = S5_BLK * S5_BLK_STATES

    def ceff(d, c):
        return jnp.where(d == 0, c, nc - 1 - c)

    return pl.pallas_call(
        functools.partial(_s5_kernel, chunk=chunk),
        grid=(ng, 2, nc),
        in_specs=[
            pl.BlockSpec((None, rows, S5_WIDTH), lambda g, d, c: (g, ceff(d, c), 0)),
            pl.BlockSpec((None, S5_BLK, LANES, 2 * S5_BLK_STATES), lambda g, d, c: (d, 0, 0, 0)),
            pl.BlockSpec((None, S5_BLK, 2 * S5_BLK_STATES, LANES), lambda g, d, c: (d, 0, 0, 0)),
            pl.BlockSpec((None, SUBLANES, ns), lambda g, d, c: (d, 0, 0)),
            pl.BlockSpec((None, SUBLANES, ns), lambda g, d, c: (d, 0, 0)),
            pl.BlockSpec((None, None, SUBLANES, ns), lambda g, d, c: (g, d, 0, 0)),
            pl.BlockSpec((None, None, SUBLANES, ns), lambda g, d, c: (g, d, 0, 0)),
        ],
        out_specs=[
            pl.BlockSpec((None, None, rows, S5_WIDTH), lambda g, d, c: (d, g, ceff(d, c), 0)),
            pl.BlockSpec((None, None, SUBLANES, ns), lambda g, d, c: (g, d, 0, 0)),
            pl.BlockSpec((None, None, SUBLANES, ns), lambda g, d, c: (g, d, 0, 0)),
        ],
        out_shape=[
            jax.ShapeDtypeStruct((2, ng, t_len * SUBLANES, S5_WIDTH), F32),
            jax.ShapeDtypeStruct((ng, 2, SUBLANES, ns), F32),
            jax.ShapeDtypeStruct((ng, 2, SUBLANES, ns), F32),
        ],
        scratch_shapes=[
            pltpu.VMEM((rows, 2 * ns), F32),
            pltpu.VMEM((SUBLANES, ns), F32),
            pltpu.VMEM((SUBLANES, ns), F32),
        ],
        compiler_params=_cparams(("arbitrary", "arbitrary", "arbitrary")),
        name="s5_scan",
    )(u_t, wb, wc, a_re, a_im, h0_re, h0_im)


def _s5_params(lp):
    lam_re = lp['s5_lam_re'].astype(F32)
    lam_im = lp['s5_lam_im'].astype(F32)
    dt = jnp.exp(lp['s5_log_dt'].astype(F32))[..., None]
    b_re = lp['s5_b_re'].astype(F32)
    b_im = lp['s5_b_im'].astype(F32)
    c_re = lp['s5_c_re'].astype(F32)
    c_im = lp['s5_c_im'].astype(F32)
    mag = jnp.exp(lam_re * dt)
    ab_re = mag * jnp.cos(lam_im * dt)
    ab_im = mag * jnp.sin(lam_im * dt)
    den = lam_re * lam_re + lam_im * lam_im
    f_re = ((ab_re - 1.0) * lam_re + ab_im * lam_im) / den
    f_im = (ab_im * lam_re - (ab_re - 1.0) * lam_im) / den
    bb_re = f_re[..., None] * b_re - f_im[..., None] * b_im
    bb_im = f_re[..., None] * b_im + f_im[..., None] * b_re
    gpb = S5_GROUPS // S5_BLK
    eye = jnp.eye(gpb, dtype=F32)

    def pack_b(bb):
        bb = bb.reshape(2, S5_BLK, gpb, S5_STATE, S5_GROUP_CH)
        w = jnp.einsum('dbgph,gk->dbghkp', bb, eye)
        return w.reshape(2, S5_BLK, gpb * S5_GROUP_CH, gpb * S5_STATE)

    def pack_c(cc):
        cc = cc.reshape(2, S5_BLK, gpb, S5_GROUP_CH, S5_STATE)
        w = jnp.einsum('dbghp,gk->dbkpgh', cc, eye)
        return w.reshape(2, S5_BLK, gpb * S5_STATE, gpb * S5_GROUP_CH)

    wb = jnp.concatenate([pack_b(bb_re), pack_b(bb_im)], axis=-1).astype(BF16)
    wc = jnp.concatenate([pack_c(c_re), -pack_c(c_im)], axis=-2).astype(BF16)
    ns = S5_GROUPS * S5_STATE
    a_re = jnp.broadcast_to(ab_re.reshape(2, 1, ns), (2, SUBLANES, ns))
    a_im = jnp.broadcast_to(ab_im.reshape(2, 1, ns), (2, SUBLANES, ns))
    return wb, wc, a_re, a_im


def _gelu_tanh(x):
    return 0.5 * x * (1.0 + jnp.tanh(math.sqrt(2.0 / math.pi) * (x + 0.044715 * (x * x * x))))


def _glu_kernel(y_ref, u_ref, d_ref, w_ref, b_ref, o_ref):
    y = y_ref[0] + y_ref[1] + d_ref[...] * u_ref[...]
    y = _gelu_tanh(y)
    z = jnp.dot(y.astype(BF16), w_ref[...], preferred_element_type=F32) + b_ref[...]
    o_ref[...] = (y * _sigmoid(z)).astype(BF16)


def _glu(y_dirs, u, d_row, w_glu, b_glu, t_len, row_off, tt):
    ng = y_dirs.shape[1]
    nb = ng * SUBLANES
    ntt = t_len // tt
    off = row_off // tt
    return pl.pallas_call(
        _glu_kernel,
        grid=(nb, ntt),
        in_specs=[
            pl.BlockSpec((2, None, tt, S5_WIDTH), lambda b, t: (0, b // SUBLANES, t, b % SUBLANES)),
            pl.BlockSpec((tt, S5_WIDTH), lambda b, t: (off + b * ntt + t, 0)),
            pl.BlockSpec((1, S5_WIDTH), lambda b, t: (0, 0)),
            pl.BlockSpec((S5_WIDTH, S5_WIDTH), lambda b, t: (0, 0)),
            pl.BlockSpec((1, S5_WIDTH), lambda b, t: (0, 0)),
        ],
        out_specs=pl.BlockSpec((tt, S5_WIDTH), lambda b, t: (b * ntt + t, 0)),
        out_shape=jax.ShapeDtypeStruct((nb * t_len, S5_WIDTH), BF16),
        compiler_params=_cparams(("arbitrary", "arbitrary")),
        name="s5_glu",
    )(y_dirs, u, d_row, w_glu, b_glu)


def _fnet_kernel(x_ref, cs_ref, dft_ref, o_ref, z_ref, *, t_len):
    @pl.when(pl.program_id(1) == 0)
    def _():
        x = x_ref[...].astype(BF16)
        for g in range(FNET_GROUPS):
            z = jnp.dot(x[:, g * LANES:(g + 1) * LANES], cs_ref[...], preferred_element_type=F32)
            z_ref[0:t_len, g * LANES:(g + 1) * LANES] = z[:, :LANES].astype(BF16)
            z_ref[t_len:2 * t_len, g * LANES:(g + 1) * LANES] = z[:, LANES:].astype(BF16)

    scale = 1.0 / math.sqrt(t_len * FNET_GROUP_CH)
    o_ref[...] = (jnp.dot(dft_ref[...], z_ref[...], preferred_element_type=F32) * scale).astype(BF16)


def _dft_tables(n):
    j = lax.broadcasted_iota(jnp.int32, (n, n), 0)
    k = lax.broadcasted_iota(jnp.int32, (n, n), 1)
    ang = ((j * k) % n).astype(F32) * (2.0 * math.pi / n)
    return jnp.cos(ang), jnp.sin(ang)


def _fnet(u, t_len, nb, row_off, tr):
    c_t, s_t = _dft_tables(t_len)
    dft = jnp.concatenate([c_t, -s_t], axis=1).astype(BF16)
    c_c, s_c = _dft_tables(FNET_GROUP_CH)
    cs = jnp.concatenate([c_c, s_c], axis=1).astype(BF16)
    ntr = t_len // tr
    off = row_off // t_len
    col = S5_WIDTH // FNET_WIDTH
    return pl.pallas_call(
        functools.partial(_fnet_kernel, t_len=t_len),
        grid=(nb, ntr),
        in_specs=[
            pl.BlockSpec((t_len, FNET_WIDTH), lambda b, r: (off + b, col)),
            pl.BlockSpec((FNET_GROUP_CH, 2 * FNET_GROUP_CH), lambda b, r: (0, 0)),
            pl.BlockSpec((tr, 2 * t_len), lambda b, r: (r, 0)),
        ],
        out_specs=pl.BlockSpec((tr, FNET_WIDTH), lambda b, r: (b * ntr + r, 0)),
        out_shape=jax.ShapeDtypeStruct((nb * t_len, FNET_WIDTH), BF16),
        scratch_shapes=[pltpu.VMEM((2 * t_len, FNET_WIDTH), BF16)],
        compiler_params=_cparams(("arbitrary", "arbitrary")),
        name="fnet",
    )(u, cs, dft)


def _rope(x, cos, sin):
    half = ROPE_AXIS_DIM // 2
    lane = lax.broadcasted_iota(jnp.int32, x.shape, 1)
    up = pltpu.roll(x, LANES - half, 1)
    dn = pltpu.roll(x, half, 1)
    rot = jnp.where((lane % ROPE_AXIS_DIM) < half, -up, dn)
    return x * cos + rot * sin


def _attn_kernel(lam_ref, q_ref, k_ref, v_ref, *rest, tq, t_len, has_ctx, lam_init):
    if has_ctx:
        ck_ref, cv_ref, cos_ref, sin_ref, g_ref, o_ref, kk_ref, vv_ref = rest
    else:
        g_ref, o_ref, kk_ref, vv_ref = rest
    qi = pl.program_id(2)

    @pl.when(qi == 0)
    def _():
        k = k_ref[...]
        if has_ctx:
            k = _rope(k, cos_ref[...], sin_ref[...])
            kk_ref[t_len:, :] = ck_ref[...].astype(BF16)
            vv_ref[t_len:, :] = cv_ref[...].astype(BF16)
        kk_ref[0:t_len, :] = k.astype(BF16)
        vv_ref[0:t_len, :] = v_ref[...].astype(BF16)

    q = q_ref[...]
    if has_ctx:
        r0 = pl.multiple_of(qi * tq, tq)
        q = _rope(q, cos_ref[pl.ds(r0, tq), :], sin_ref[pl.ds(r0, tq), :])
    q = (q * (HEAD_DIM ** -0.5)).astype(BF16)
    lam = lam_ref[0]
    kk = kk_ref[...]
    dn = (((1,), (1,)), ((), ()))
    w = None
    for n in range(2):
        s = lax.dot_general(q[:, n * HEAD_DIM:(n + 1) * HEAD_DIM], kk[:, n * HEAD_DIM:(n + 1) * HEAD_DIM],
                            dn, preferred_element_type=F32)
        p = jnp.exp(s - jnp.max(s, axis=-1, keepdims=True))
        a = p / jnp.sum(p, axis=-1, keepdims=True)
        w = a if n == 0 else w - lam * a
    o = jnp.dot(w.astype(BF16), vv_ref[...], preferred_element_type=F32)
    o = o * lax.rsqrt(jnp.mean(o * o, axis=-1, keepdims=True) + EPS)
    o_ref[...] = ((o * g_ref[...]) * (1.0 - lam_init)).astype(BF16)


def _attention(u, lam, subln_g, lam_init, t_len, nb, row_off, tq, ctx=None):
    nq = t_len // tq
    off = row_off // t_len
    offq = row_off // tq
    qcol = (S5_WIDTH + FNET_WIDTH) // LANES
    kcol = qcol + QK_WIDTH // LANES
    vcol = kcol + QK_WIDTH // LANES
    has_ctx = ctx is not None
    n_keys = t_len + (PAST_LEN if has_ctx else 0)
    in_specs = [
        pl.BlockSpec(memory_space=pltpu.SMEM),
        pl.BlockSpec((tq, LANES), lambda b, h, i: (offq + b * nq + i, qcol + h)),
        pl.BlockSpec((t_len, LANES), lambda b, h, i: (off + b, kcol + h)),
        pl.BlockSpec((t_len, LANES), lambda b, h, i: (off + b, vcol + h)),
    ]
    args = [lam, u, u, u]
    if has_ctx:
        ck, cv, layer, cos, sin = ctx
        in_specs += [
            pl.BlockSpec((None, None, PAST_LEN, LANES), lambda b, h, i: (b, layer, 0, h)),
            pl.BlockSpec((None, None, PAST_LEN, LANES), lambda b, h, i: (b, layer, 0, h)),
            pl.BlockSpec((t_len, LANES), lambda b, h, i: (0, 0)),
            pl.BlockSpec((t_len, LANES), lambda b, h, i: (0, 0)),
        ]
        args += [ck, cv, cos, sin]
    in_specs.append(pl.BlockSpec((1, LANES), lambda b, h, i: (0, 0)))
    args.append(subln_g)
    return pl.pallas_call(
        functools.partial(_attn_kernel, tq=tq, t_len=t_len, has_ctx=has_ctx, lam_init=lam_init),
        grid=(nb, N_HEADS, nq),
        in_specs=in_specs,
        out_specs=pl.BlockSpec((tq, LANES), lambda b, h, i: (b * nq + i, h)),
        out_shape=jax.ShapeDtypeStruct((nb * t_len, V_WIDTH), BF16),
        scratch_shapes=[pltpu.VMEM((n_keys, LANES), BF16), pltpu.VMEM((n_keys, LANES), BF16)],
        compiler_params=_cparams(("arbitrary", "arbitrary", "arbitrary")),
        name="diff_attn",
    )(*args)


def _rope_tables(t_len):
    rows = t_len // GRID_W
    pos_row = jnp.broadcast_to(jnp.arange(rows, dtype=F32)[:, None], (rows, GRID_W)).reshape(-1)
    pos_col = jnp.broadcast_to(jnp.arange(GRID_W, dtype=F32)[None, :], (rows, GRID_W)).reshape(-1)
    inv = ROPE_BASE ** (-jnp.arange(0, ROPE_AXIS_DIM, 2, dtype=F32) / ROPE_AXIS_DIM)
    ang_r = pos_row[:, None] * inv
    ang_c = pos_col[:, None] * inv
    cos = jnp.concatenate([jnp.cos(ang_r), jnp.cos(ang_r), jnp.cos(ang_c), jnp.cos(ang_c)], axis=-1)
    sin = jnp.concatenate([jnp.sin(ang_r), jnp.sin(ang_r), jnp.sin(ang_c), jnp.sin(ang_c)], axis=-1)
    return jnp.concatenate([cos, cos], axis=-1), jnp.concatenate([sin, sin], axis=-1)


def _merge_kernel(h_ref, ya_ref, yb_ref, yc_ref, wga_ref, wgb_ref, wgc_ref, bga_ref, bgb_ref, bgc_ref,
                  wa_ref, wb_ref, wc_ref, o_ref):
    h = h_ref[...]
    acc = None
    for wg, bg, y, w in ((wga_ref, bga_ref, ya_ref, wa_ref), (wgb_ref, bgb_ref, yb_ref, wb_ref),
                         (wgc_ref, bgc_ref, yc_ref, wc_ref)):
        gate = _sigmoid(jnp.dot(h, wg[...], preferred_element_type=F32) + bg[...])
        term = gate * jnp.dot(y[...], w[...], preferred_element_type=F32)
        acc = term if acc is None else acc + term
    o_ref[...] = acc.astype(BF16)


def _merge(h, ya, yb, yc, w_gate, b_gate, w_a, w_b, w_c):
    tm, tn = 1024, 512
    nj = D // tn
    row = lambda i, j: (i, 0)
    wspecs = [pl.BlockSpec((D, tn), lambda i, j, k=k: (0, k * nj + j)) for k in range(3)]
    bspecs = [pl.BlockSpec((1, tn), lambda i, j, k=k: (0, k * nj + j)) for k in range(3)]
    return pl.pallas_call(
        _merge_kernel,
        grid=(M // tm, nj),
        in_specs=[
            pl.BlockSpec((tm, D), row),
            pl.BlockSpec((tm, S5_WIDTH), row),
            pl.BlockSpec((tm, FNET_WIDTH), row),
            pl.BlockSpec((tm, V_WIDTH), row),
            *wspecs, *bspecs,
            pl.BlockSpec((S5_WIDTH, tn), lambda i, j: (0, j)),
            pl.BlockSpec((FNET_WIDTH, tn), lambda i, j: (0, j)),
            pl.BlockSpec((V_WIDTH, tn), lambda i, j: (0, j)),
        ],
        out_specs=pl.BlockSpec((tm, tn), lambda i, j: (i, j)),
        out_shape=jax.ShapeDtypeStruct((M, D), BF16),
        compiler_params=_cparams(("arbitrary", "arbitrary")),
        name="merge",
    )(h, ya, yb, yc, w_gate, w_gate, w_gate, b_gate, b_gate, b_gate, w_a, w_b, w_c)


def _out_proj_kernel(m_ref, w_ref, x_ref, g1_ref, n2_ref, sc_ref, sh_ref, x1_ref, h2_ref):
    x1 = x_ref[...] + g1_ref[...] * jnp.dot(m_ref[...], w_ref[...], preferred_element_type=F32)
    x1_ref[...] = x1
    r = x1 * lax.rsqrt(jnp.mean(x1 * x1, axis=-1, keepdims=True) + EPS)
    h2_ref[...] = ((r * n2_ref[...]) * (1.0 + sc_ref[...]) + sh_ref[...]).astype(BF16)


def _out_proj(merged, w_out, x, norm2_g, mod4):
    tm = 512
    row = lambda i: (i, 0)
    mspec = lambda k: pl.BlockSpec((None, None, 1, D), lambda i, k=k: (_mod_row(i, tm), k, 0, 0))
    return pl.pallas_call(
        _out_proj_kernel,
        grid=(M // tm,),
        in_specs=[
            pl.BlockSpec((tm, D), row),
            pl.BlockSpec((D, D), lambda i: (0, 0)),
            pl.BlockSpec((tm, D), row),
            mspec(2),
            pl.BlockSpec((1, D), lambda i: (0, 0)),
            mspec(4),
            mspec(3),
        ],
        out_specs=[pl.BlockSpec((tm, D), row), pl.BlockSpec((tm, D), row)],
        out_shape=[jax.ShapeDtypeStruct((M, D), F32), jax.ShapeDtypeStruct((M, D), BF16)],
        compiler_params=_cparams(("arbitrary",)),
        name="out_proj",
    )(merged, w_out, x, mod4, norm2_g.reshape(1, D), mod4, mod4)


def _segment_sum(x, seg):
    parts = []
    for s in range(ROUTE_TILE // seg):
        tot = jnp.sum(x[:, s * seg:(s + 1) * seg], axis=1, keepdims=True)
        parts.append(jnp.broadcast_to(tot, (x.shape[0], seg)))
    return parts[0] if len(parts) == 1 else jnp.concatenate(parts, axis=1)


def _segment_cumsum(x, seg):
    pos = lax.broadcasted_iota(jnp.int32, x.shape, 1) % seg
    sh = 1
    while sh < seg:
        x = x + jnp.where(pos >= sh, pltpu.roll(x, sh, 1), 0.0)
        sh *= 2
    return x


def _router_kernel(h_ref, wr_ref, slot_ref, gcol_ref, *, seg):
    cap = CAPACITY_FACTOR * seg // N_EXPERTS
    logits = lax.dot_general(wr_ref[...], h_ref[...], (((1,), (1,)), ((), ())),
                             preferred_element_type=F32)
    p = jnp.exp(logits - jnp.max(logits, axis=0, keepdims=True))
    aff = p / jnp.sum(p, axis=0, keepdims=True)
    bits = pltpu.bitcast(aff, jnp.int32)

    def bis(i, thr):
        cand = thr | jnp.left_shift(jnp.int32(1), 29 - i)
        cnt = _segment_sum(jnp.where(bits >= cand, 1.0, 0.0), seg)
        return jnp.where(cnt >= cap, cand, thr)

    thr = lax.fori_loop(0, 30, bis, jnp.zeros(bits.shape, jnp.int32))
    gt = jnp.where(bits > thr, 1.0, 0.0)
    eq = jnp.where(bits == thr, 1.0, 0.0)
    need = cap - _segment_sum(gt, seg)
    eq_rank = _segment_cumsum(eq, seg)
    sel = gt + eq * jnp.where(eq_rank <= need, 1.0, 0.0)
    rank = _segment_cumsum(sel, seg)
    seg_id = lax.broadcasted_iota(jnp.int32, bits.shape, 1) // seg
    slot = seg_id * cap + rank.astype(jnp.int32) - 1
    slot = jnp.where(sel > 0.0, slot, -1)
    slot_ref[...] = slot

    iota_c = lax.broadcasted_iota(jnp.int32, (ROUTE_SLOTS, ROUTE_TILE), 0)
    for e in range(N_EXPERTS):
        hit = iota_c == slot[e:e + 1, :]
        gcol_ref[e] = jnp.sum(jnp.where(hit, aff[e:e + 1, :], 0.0), axis=1, keepdims=True)


def _router(h2, w_router_t):
    def call(seg, tile0, ntiles):
        return pl.pallas_call(
            functools.partial(_router_kernel, seg=seg),
            grid=(ntiles,),
            in_specs=[
                pl.BlockSpec((ROUTE_TILE, D), lambda i: (tile0 + i, 0)),
                pl.BlockSpec((N_EXPERTS, D), lambda i: (0, 0)),
            ],
            out_specs=[
                pl.BlockSpec((None, N_EXPERTS, ROUTE_TILE), lambda i: (i, 0, 0)),
                pl.BlockSpec((N_EXPERTS, None, ROUTE_SLOTS, 1), lambda i: (0, i, 0, 0)),
            ],
            out_shape=[
                jax.ShapeDtypeStruct((ntiles, N_EXPERTS, ROUTE_TILE), jnp.int32),
                jax.ShapeDtypeStruct((N_EXPERTS, ntiles, ROUTE_SLOTS, 1), F32),
            ],
            compiler_params=_cparams(("arbitrary",)),
            name="router",
        )(h2, w_router_t)

    nct = M_CTX // ROUTE_TILE
    slot_c, g_c = call(SEQ, 0, nct)
    slot_l, g_l = call(DEC_SEQ, nct, N_ROUTE_TILES - nct)
    return jnp.concatenate([slot_c, slot_l], axis=0), jnp.concatenate([g_c, g_l], axis=1)


def _dispatch_kernel(h_ref, slot_ref, o_ref):
    e = pl.program_id(1)
    srow = slot_ref[pl.ds(e, 1), :]
    iota_c = lax.broadcasted_iota(jnp.int32, (ROUTE_SLOTS, ROUTE_TILE), 0)
    onehot = jnp.where(iota_c == srow, 1.0, 0.0).astype(BF16)
    o_ref[...] = jnp.dot(onehot, h_ref[...], preferred_element_type=F32).astype(BF16)


def _dispatch(h2, slot_et):
    return pl.pallas_call(
        _dispatch_kernel,
        grid=(N_ROUTE_TILES, N_EXPERTS),
        in_specs=[
            pl.BlockSpec((ROUTE_TILE, D), lambda i, e: (i, 0)),
            pl.BlockSpec((None, N_EXPERTS, ROUTE_TILE), lambda i, e: (i, 0, 0)),
        ],
        out_specs=pl.BlockSpec((None, ROUTE_SLOTS, D), lambda i, e: (e, i, 0)),
        out_shape=jax.ShapeDtypeStruct((N_EXPERTS, SLOTS_PER_EXPERT, D), BF16),
        compiler_params=_cparams(("arbitrary", "arbitrary")),
        name="moe_dispatch",
    )(h2, slot_et)


def _ffn_kernel(x_ref, w1_ref, w3_ref, w2_ref, g_ref, o_ref, acc_ref, *, tn):
    f = pl.program_id(2)
    nf = pl.num_programs(2)

    @pl.when(f == 0)
    def _():
        acc_ref[...] = jnp.zeros(acc_ref.shape, F32)

    x = x_ref[...]
    a = jnp.dot(x, w1_ref[...].astype(BF16), preferred_element_type=F32)
    b = jnp.dot(x, w3_ref[...].astype(BF16), preferred_element_type=F32)
    hmid = ((a * _sigmoid(a)) * b).astype(BF16)
    w2 = w2_ref[...].astype(BF16)
    for n in range(D // tn):
        acc_ref[:, n * tn:(n + 1) * tn] += jnp.dot(hmid, w2[:, n * tn:(n + 1) * tn],
                                                  preferred_element_type=F32)

    @pl.when(f == nf - 1)
    def _():
        o_ref[...] = (acc_ref[...] * g_ref[...]).astype(BF16)


def _ffn(xg, w1, w3, w2, gcol, layer):
    tr, tf, tn = 1024, 256, 512
    nr = SLOTS_PER_EXPERT // tr
    return pl.pallas_call(
        functools.partial(_ffn_kernel, tn=tn),
        grid=(N_EXPERTS, nr, EXPERT_FF // tf),
        in_specs=[
            pl.BlockSpec((None, tr, D), lambda e, r, f: (e, r, 0)),
            pl.BlockSpec((None, None, D, tf), lambda e, r, f: (layer, e, 0, f)),
            pl.BlockSpec((None, None, D, tf), lambda e, r, f: (layer, e, 0, f)),
            pl.BlockSpec((None, None, tf, D), lambda e, r, f: (layer, e, f, 0)),
            pl.BlockSpec((None, tr, 1), lambda e, r, f: (e, r, 0)),
        ],
        out_specs=pl.BlockSpec((None, tr, D), lambda e, r, f: (e, r, 0)),
        out_shape=jax.ShapeDtypeStruct((N_EXPERTS, SLOTS_PER_EXPERT, D), BF16),
        scratch_shapes=[pltpu.VMEM((tr, D), F32)],
        compiler_params=_cparams(("arbitrary", "arbitrary", "arbitrary")),
        name="moe_ffn",
    )(xg, w1, w3, w2, gcol)


def _combine_kernel(slot_ref, y_ref, x_ref, g2_ref, o_ref, *, tr):
    r = pl.program_id(2)
    slot = slot_ref[pl.ds(pl.multiple_of(r * tr, tr), tr), :]
    iota_c = lax.broadcasted_iota(jnp.int32, (tr, ROUTE_SLOTS), 1)
    onehot = jnp.concatenate(
        [jnp.where(iota_c == slot[:, e:e + 1], 1.0, 0.0).astype(BF16) for e in range(N_EXPERTS)], axis=1)
    y = y_ref[...].reshape(N_EXPERTS * ROUTE_SLOTS, y_ref.shape[-1])
    moe = jnp.dot(onehot, y, preferred_element_type=F32)
    o_ref[...] = x_ref[...] + g2_ref[...] * moe


def _combine(slot_te, yg, x1, mod4):
    tr, tn = 512, 1024
    nr = ROUTE_TILE // tr
    yg4 = yg.reshape(N_EXPERTS, N_ROUTE_TILES, ROUTE_SLOTS, D)
    return pl.pallas_call(
        functools.partial(_combine_kernel, tr=tr),
        grid=(N_ROUTE_TILES, D // tn, nr),
        in_specs=[
            pl.BlockSpec((None, ROUTE_TILE, N_EXPERTS), lambda i, j, r: (i, 0, 0)),
            pl.BlockSpec((N_EXPERTS, None, ROUTE_SLOTS, tn), lambda i, j, r: (0, i, 0, j)),
            pl.BlockSpec((tr, tn), lambda i, j, r: (i * nr + r, j)),
            pl.BlockSpec((None, None, 1, tn), lambda i, j, r: (_mod_row(i, ROUTE_TILE), 5, 0, j)),
        ],
        out_specs=pl.BlockSpec((tr, tn), lambda i, j, r: (i * nr + r, j)),
        out_shape=jax.ShapeDtypeStruct((M, D), F32),
        compiler_params=_cparams(("arbitrary", "arbitrary", "arbitrary")),
        name="moe_combine",
    )(slot_te, yg4, x1, mod4)


def _final_norm_kernel(x_ref, g_ref, o_ref):
    x = x_ref[...]
    o_ref[...] = (x * lax.rsqrt(jnp.mean(x * x, axis=-1, keepdims=True) + EPS)) * g_ref[...]


def _final_norm(x, g, row_off, rows):
    tm = 512
    off = row_off // tm
    return pl.pallas_call(
        _final_norm_kernel,
        grid=(rows // tm,),
        in_specs=[pl.BlockSpec((tm, D), lambda i: (off + i, 0)), pl.BlockSpec((1, D), lambda i: (0, 0))],
        out_specs=pl.BlockSpec((tm, D), lambda i: (i, 0)),
        out_shape=jax.ShapeDtypeStruct((rows, D), F32),
        compiler_params=_cparams(("arbitrary",)),
        name="final_norm",
    )(x, g.reshape(1, D))


def _to_position_major(ua, nb, t_len):
    ng = nb // SUBLANES
    c = ua.shape[-1]
    return ua.reshape(ng, SUBLANES, t_len, c).transpose(0, 2, 1, 3).reshape(ng, t_len * SUBLANES, c)


def _layer(x, mod, lp, layer, lam_init, rope, cache_k4, cache_v4, st_re, st_im):
    mod4 = mod.reshape(N_MOD_ROWS, 6, 1, D)
    h, u = _norm_proj(x, lp['norm1_g'], mod4, lp['w_in'].astype(BF16))

    wb, wc, a_re, a_im = _s5_params(lp)
    ns = S5_GROUPS * S5_STATE
    ua = u[:, :S5_WIDTH]
    u_ctx = _to_position_major(ua[:M_CTX], BATCH, SEQ)
    u_lat = _to_position_major(ua[M_CTX:], DEC_BATCH, DEC_SEQ)
    zeros = jnp.zeros((BATCH // SUBLANES, 2, SUBLANES, ns), F32)
    y_ctx, fin_re, fin_im = _s5(u_ctx, wb, wc, a_re, a_im, zeros, zeros, SEQ)
    h0_re = st_re.reshape(DEC_BATCH, 2, ns).transpose(1, 0, 2)[None]
    h0_im = st_im.reshape(DEC_BATCH, 2, ns).transpose(1, 0, 2)[None]
    y_lat, _, _ = _s5(u_lat, wb, wc, a_re, a_im, h0_re, h0_im, DEC_SEQ)
    d_row = lp['s5_d'].reshape(1, S5_WIDTH)
    w_glu = lp['w_glu'].astype(BF16)
    b_glu = lp['b_glu'].reshape(1, S5_WIDTH)
    ya = jnp.concatenate([
        _glu(y_ctx.reshape(2, -1, SEQ, SUBLANES * S5_WIDTH), u, d_row, w_glu, b_glu, SEQ, 0, SEQ),
        _glu(y_lat.reshape(2, -1, DEC_SEQ, SUBLANES * S5_WIDTH), u, d_row, w_glu, b_glu, DEC_SEQ, M_CTX, 512),
    ], axis=0)

    yb = jnp.concatenate([
        _fnet(u, SEQ, BATCH, 0, SEQ),
        _fnet(u, DEC_SEQ, DEC_BATCH, M_CTX, 512),
    ], axis=0)

    lam = (jnp.exp(jnp.sum(lp['lam_q1'].astype(F32) * lp['lam_k1'].astype(F32)))
           - jnp.exp(jnp.sum(lp['lam_q2'].astype(F32) * lp['lam_k2'].astype(F32))) + lam_init).reshape(1)
    subln = lp['subln_g'].reshape(1, VALUE_DIM)
    yc = jnp.concatenate([
        _attention(u, lam, subln, lam_init, SEQ, BATCH, 0, SEQ),
        _attention(u, lam, subln, lam_init, DEC_SEQ, DEC_BATCH, M_CTX, 256,
                   ctx=(cache_k4, cache_v4, layer, rope[0], rope[1])),
    ], axis=0)

    merged = _merge(h, ya, yb, yc, lp['w_gate'].astype(BF16), lp['b_gate'].reshape(1, 3 * D),
                    lp['w_branch_a'].astype(BF16), lp['w_branch_b'].astype(BF16),
                    lp['w_branch_c'].astype(BF16))
    x1, h2 = _out_proj(merged, lp['w_out'].astype(BF16), x, lp['norm2_g'], mod4)

    slot_et, gcol = _router(h2, lp['w_router'].T.astype(BF16))
    xg = _dispatch(h2, slot_et)
    yg = _ffn(xg, lp['moe_w1_all'], lp['moe_w3_all'], lp['moe_w2_all'],
              gcol.reshape(N_EXPERTS, SLOTS_PER_EXPERT, 1), layer)
    x2 = _combine(slot_et.transpose(0, 2, 1), yg, x1, mod4)

    k_ctx = u[:M_CTX, S5_WIDTH + FNET_WIDTH + QK_WIDTH:S5_WIDTH + FNET_WIDTH + 2 * QK_WIDTH]
    v_ctx = u[:M_CTX, S5_WIDTH + FNET_WIDTH + 2 * QK_WIDTH:]
    fin_shape = (BATCH // SUBLANES, 2, SUBLANES, S5_GROUPS, S5_STATE)
    s_re = fin_re.reshape(fin_shape).transpose(0, 2, 1, 3, 4).reshape(BATCH, 2, S5_GROUPS, S5_STATE)
    s_im = fin_im.reshape(fin_shape).transpose(0, 2, 1, 3, 4).reshape(BATCH, 2, S5_GROUPS, S5_STATE)
    return (x2, k_ctx.reshape(BATCH, SEQ, N_HEADS, VALUE_DIM), v_ctx.reshape(BATCH, SEQ, N_HEADS, VALUE_DIM),
            s_re, s_im)


def kernel(x_prompt, x_sample, cache_k, cache_v, state_s5_re, state_s5_im, c, c_ctx, norm1_g, norm2_g, final_norm_g, w_ada, b_ada, w_in, s5_lam_re, s5_lam_im, s5_log_dt, s5_b_re, s5_b_im, s5_c_re, s5_c_im, s5_d, w_glu, b_glu, lam_q1, lam_k1, lam_q2, lam_k2, subln_g, w_branch_a, w_branch_b, w_branch_c, w_gate, b_gate, w_out, w_router, moe_w1, moe_w3, moe_w2):
    params = dict(norm1_g=norm1_g, norm2_g=norm2_g, w_in=w_in,
                  s5_lam_re=s5_lam_re, s5_lam_im=s5_lam_im, s5_log_dt=s5_log_dt,
                  s5_b_re=s5_b_re, s5_b_im=s5_b_im, s5_c_re=s5_c_re, s5_c_im=s5_c_im, s5_d=s5_d,
                  w_glu=w_glu, b_glu=b_glu, lam_q1=lam_q1, lam_k1=lam_k1, lam_q2=lam_q2, lam_k2=lam_k2,
                  subln_g=subln_g, w_branch_a=w_branch_a, w_branch_b=w_branch_b, w_branch_c=w_branch_c,
                  w_gate=w_gate, b_gate=b_gate, w_out=w_out, w_router=w_router)
    x = jnp.concatenate([x_prompt.reshape(M_CTX, D), x_sample.reshape(M_LAT, D)], axis=0)
    cc = jnp.zeros((N_MOD_ROWS, D), F32).at[0].set(c_ctx).at[1:1 + DEC_BATCH].set(c)
    mods = _ada(cc, w_ada, b_ada)
    rope = _rope_tables(DEC_SEQ)
    cache_k4 = cache_k.reshape(DEC_BATCH, DEPTH, PAST_LEN, QK_WIDTH)
    cache_v4 = cache_v.reshape(DEC_BATCH, DEPTH, PAST_LEN, V_WIDTH)
    ks, vs, s_res, s_ims = [], [], [], []
    for l in range(DEPTH):
        lp = {name: arr[l] for name, arr in params.items()}
        lp['moe_w1_all'], lp['moe_w3_all'], lp['moe_w2_all'] = moe_w1, moe_w3, moe_w2
        lam_init = 0.8 - 0.6 * math.exp(-0.3 * l)
        x, k_c, v_c, s_re, s_im = _layer(x, mods[l], lp, l, lam_init, rope, cache_k4, cache_v4,
                                         state_s5_re[:, l], state_s5_im[:, l])
        ks.append(k_c)
        vs.append(v_c)
        s_res.append(s_re)
        s_ims.append(s_im)
    y_prompt = _final_norm(x, final_norm_g, 0, M_CTX).reshape(BATCH, SEQ, D)
    y_sample = _final_norm(x, final_norm_g, M_CTX, M_LAT).reshape(DEC_BATCH, DEC_SEQ, D)
    return (y_prompt, y_sample, jnp.stack(ks, axis=1), jnp.stack(vs, axis=1),
            jnp.stack(s_res, axis=1), jnp.stack(s_ims, axis=1))
```

```python
import functools
import math

import jax
import jax.numpy as jnp
from jax import lax
from jax.experimental import pallas as pl
from jax.experimental.pallas import tpu as pltpu

F32 = jnp.float32
BF16 = jnp.bfloat16

D = 2048
BATCH, SEQ = 32, 256
DEC_BATCH, DEC_SEQ = 8, 2048
DEPTH = 2
PAST_LEN = 256
GRID_W = 64
EPS = 1e-6
S5_GROUP_CH, S5_GROUPS, S5_STATE = 16, 32, 64
S5_WIDTH = S5_GROUPS * S5_GROUP_CH
FNET_GROUPS, FNET_GROUP_CH = 4, 128
FNET_WIDTH = FNET_GROUPS * FNET_GROUP_CH
N_HEADS, HEAD_DIM = 8, 64
VALUE_DIM = 2 * HEAD_DIM
QK_WIDTH = N_HEADS * 2 * HEAD_DIM
V_WIDTH = N_HEADS * VALUE_DIM
IN_WIDTH = S5_WIDTH + FNET_WIDTH + 2 * QK_WIDTH + V_WIDTH
ROPE_BASE = 10000.0
ROPE_AXIS_DIM = HEAD_DIM // 2
N_EXPERTS = 16
EXPERT_FF = 2048
CAPACITY_FACTOR = 2

M_CTX = BATCH * SEQ
M_LAT = DEC_BATCH * DEC_SEQ
M = M_CTX + M_LAT
N_MOD_ROWS = 16

LANES = 128
SUBLANES = 8
VMEM_LIMIT = 56 * 1024 * 1024

ROUTE_TILE = 2048
ROUTE_SLOTS = CAPACITY_FACTOR * ROUTE_TILE // N_EXPERTS
N_ROUTE_TILES = M // ROUTE_TILE
SLOTS_PER_EXPERT = N_ROUTE_TILES * ROUTE_SLOTS

S5_BLK = 4
S5_BLK_STATES = 512

ANY_SPEC = pl.BlockSpec(memory_space=pl.ANY)


def _cparams(sem):
    return pltpu.CompilerParams(dimension_semantics=sem, vmem_limit_bytes=VMEM_LIMIT)


def _sigmoid(x):
    return 1.0 / (1.0 + jnp.exp(-x))


def _part(is_ctx):
    return (M_CTX, 0) if is_ctx else (M_LAT, M_CTX)


def _mod_row_fn(is_ctx, tm):
    if is_ctx:
        return lambda i: 0
    return lambda i: 1 + i // (DEC_SEQ // tm)


def _ada_kernel(c_ref, w_ref, b_ref, o_ref):
    c = c_ref[...]
    s = (c * _sigmoid(c)).astype(BF16)
    o_ref[...] = jnp.dot(s, w_ref[...].astype(BF16), preferred_element_type=F32) + b_ref[...]


def _ada(cc, w_ada, b_ada):
    tn = 1024
    n = 6 * D
    return pl.pallas_call(
        _ada_kernel,
        grid=(DEPTH, n // tn),
        in_specs=[
            pl.BlockSpec((N_MOD_ROWS, D), lambda l, j: (0, 0)),
            pl.BlockSpec((None, D, tn), lambda l, j: (l, 0, j)),
            pl.BlockSpec((None, 1, tn), lambda l, j: (l, 0, j)),
        ],
        out_specs=pl.BlockSpec((None, N_MOD_ROWS, tn), lambda l, j: (l, 0, j)),
        out_shape=jax.ShapeDtypeStruct((DEPTH, N_MOD_ROWS, n), F32),
        compiler_params=_cparams(("arbitrary", "arbitrary")),
        name="ada",
    )(cc, w_ada, b_ada.reshape(DEPTH, 1, n))


NP_TM_CTX, NP_TM_LAT, NP_TN = 512, 1024, 1024


def _norm_proj_kernel(x_ref, g_ref, sc_ref, sh_ref, w_ref, *rest, is_ctx, n_alias):
    rest = rest[n_alias:]
    if is_ctx:
        h_ref, u_ref, ua_ref, k_ref, v_ref = rest
    else:
        h_ref, u_ref, ua_ref = rest
    j = pl.program_id(1)

    @pl.when(j == 0)
    def _():
        x = x_ref[...]
        r = x * lax.rsqrt(jnp.mean(x * x, axis=-1, keepdims=True) + EPS)
        h = (r * g_ref[...]) * (1.0 + sc_ref[...]) + sh_ref[...]
        h_ref[...] = h.astype(BF16)

    u_ref[...] = jnp.dot(h_ref[...], w_ref[...], preferred_element_type=F32)

    @pl.when(j == 0)
    def _():
        if is_ctx:
            for b in range(NP_TM_CTX // SEQ):
                ua_ref[:, b * S5_WIDTH:(b + 1) * S5_WIDTH] = u_ref[b * SEQ:(b + 1) * SEQ, :S5_WIDTH]
        else:
            ua_ref[...] = u_ref[:, :S5_WIDTH]

    if is_ctx:
        @pl.when(j == 2)
        def _():
            for b in range(NP_TM_CTX // SEQ):
                k_ref[b] = u_ref[b * SEQ:(b + 1) * SEQ, :]

        @pl.when(j == 3)
        def _():
            for b in range(NP_TM_CTX // SEQ):
                v_ref[b] = u_ref[b * SEQ:(b + 1) * SEQ, :]


def _norm_proj(x, x_row0, g, mod4, w_in, layer, is_ctx, prev):
    tm, tn = (NP_TM_CTX if is_ctx else NP_TM_LAT), NP_TN
    assert IN_WIDTH // tn == 4 and QK_WIDTH == tn and V_WIDTH == tn
    rows, row0 = _part(is_ctx)
    xoff, ooff = x_row0 // tm, row0 // tm
    mrow = _mod_row_fn(is_ctx, tm)
    prev = list(prev) if prev is not None else []
    in_specs = [
        pl.BlockSpec((tm, D), lambda i, j: (xoff + i, 0)),
        pl.BlockSpec((1, D), lambda i, j: (0, 0)),
        pl.BlockSpec((None, None, 1, D), lambda i, j: (mrow(i), 1, 0, 0)),
        pl.BlockSpec((None, None, 1, D), lambda i, j: (mrow(i), 0, 0, 0)),
        pl.BlockSpec((D, tn), lambda i, j: (0, j)),
    ] + [ANY_SPEC] * len(prev)
    out_specs = [
        pl.BlockSpec((tm, D), lambda i, j: (ooff + i, 0)),
        pl.BlockSpec((tm, tn), lambda i, j: (ooff + i, j)),
    ]
    out_shape = [jax.ShapeDtypeStruct((M, D), BF16), jax.ShapeDtypeStruct((M, IN_WIDTH), F32)]
    if is_ctx:
        nbt = tm // SEQ
        out_specs += [
            pl.BlockSpec((SEQ, nbt * S5_WIDTH), lambda i, j: (0, i)),
            pl.BlockSpec((nbt, None, SEQ, QK_WIDTH), lambda i, j: (i, layer, 0, 0)),
            pl.BlockSpec((nbt, None, SEQ, V_WIDTH), lambda i, j: (i, layer, 0, 0)),
        ]
        out_shape += [
            jax.ShapeDtypeStruct((SEQ, BATCH * S5_WIDTH), F32),
            jax.ShapeDtypeStruct((BATCH, DEPTH, SEQ, QK_WIDTH), F32),
            jax.ShapeDtypeStruct((BATCH, DEPTH, SEQ, V_WIDTH), F32),
        ]
        aliases = {5: 3, 6: 4} if prev else {}
    else:
        tpb = DEC_SEQ // tm
        out_specs.append(pl.BlockSpec((tm, S5_WIDTH), lambda i, j: (i % tpb, i // tpb)))
        out_shape.append(jax.ShapeDtypeStruct((DEC_SEQ, DEC_BATCH * S5_WIDTH), F32))
        aliases = {5: 0, 6: 1}
    return pl.pallas_call(
        functools.partial(_norm_proj_kernel, is_ctx=is_ctx, n_alias=len(prev)),
        grid=(rows // tm, IN_WIDTH // tn),
        in_specs=in_specs,
        out_specs=out_specs,
        out_shape=out_shape,
        input_output_aliases=aliases,
        compiler_params=_cparams(("arbitrary", "arbitrary")),
        name="norm_proj",
    )(x, g.reshape(1, D), mod4, mod4, w_in, *prev)


def _s5_kernel(u_ref, wb_ref, wc_ref, are_ref, aim_ref, h0re_ref, h0im_ref,
               y_ref, finre_ref, finim_ref, xs_ref, stre_ref, stim_ref, *, chunk):
    d = pl.program_id(1)
    c = pl.program_id(2)
    nc = pl.num_programs(2)
    bs = S5_BLK_STATES
    rows = chunk * SUBLANES

    @pl.when(c == 0)
    def _():
        stre_ref[...] = h0re_ref[...]
        stim_ref[...] = h0im_ref[...]

    u = u_ref[...].reshape(rows, S5_WIDTH).astype(BF16)
    for blk in range(S5_BLK):
        xs_ref[:, blk * 2 * bs:(blk + 1) * 2 * bs] = jnp.dot(
            u[:, blk * LANES:(blk + 1) * LANES], wb_ref[blk], preferred_element_type=F32)

    for blk in range(S5_BLK):
        ar = are_ref[:, blk * bs:(blk + 1) * bs]
        ai = aim_ref[:, blk * bs:(blk + 1) * bs]
        cre = slice(blk * 2 * bs, blk * 2 * bs + bs)
        cim = slice(blk * 2 * bs + bs, (blk + 1) * 2 * bs)

        def step(i, carry, ar=ar, ai=ai, cre=cre, cim=cim):
            sr, si = carry
            t = jnp.where(d == 0, i, chunk - 1 - i)
            r8 = pl.ds(pl.multiple_of(t * SUBLANES, SUBLANES), SUBLANES)
            nr = ar * sr - ai * si + xs_ref[r8, cre]
            ni = ar * si + ai * sr + xs_ref[r8, cim]
            xs_ref[r8, cre] = nr
            xs_ref[r8, cim] = ni
            return nr, ni

        sr, si = lax.fori_loop(
            0, chunk, step,
            (stre_ref[:, blk * bs:(blk + 1) * bs], stim_ref[:, blk * bs:(blk + 1) * bs]),
            unroll=2)
        stre_ref[:, blk * bs:(blk + 1) * bs] = sr
        stim_ref[:, blk * bs:(blk + 1) * bs] = si

    for blk in range(S5_BLK):
        xb = xs_ref[:, blk * 2 * bs:(blk + 1) * 2 * bs].astype(BF16)
        y_ref[:, blk * LANES:(blk + 1) * LANES] = jnp.dot(xb, wc_ref[blk], preferred_element_type=F32)

    @pl.when(c == nc - 1)
    def _():
        finre_ref[...] = stre_ref[...]
        finim_ref[...] = stim_ref[...]


def _s5(u_pm, wb, wc, a_re, a_im, h0_re, h0_im, chunk=64):
    t_len, ng = u_pm.shape[:2]
    nc = t_len // chunk
    rows = chunk * SUBLANES
    ns = S5_BLK * S5_BLK_STATES

    def ceff(d, c):
        return jnp.where(d == 0, c, nc - 1 - c)

    return pl.pallas_call(
        functools.partial(_s5_kernel, chunk=chunk),
        grid=(ng, 2, nc),
        in_specs=[
            pl.BlockSpec((chunk, None, SUBLANES, S5_WIDTH), lambda g, d, c: (ceff(d, c), g, 0, 0)),
            pl.BlockSpec((None, S5_BLK, LANES, 2 * S5_BLK_STATES), lambda g, d, c: (d, 0, 0, 0)),
            pl.BlockSpec((None, S5_BLK, 2 * S5_BLK_STATES, LANES), lambda g, d, c: (d, 0, 0, 0)),
            pl.BlockSpec((None, SUBLANES, ns), lambda g, d, c: (d, 0, 0)),
            pl.BlockSpec((None, SUBLANES, ns), lambda g, d, c: (d, 0, 0)),
            pl.BlockSpec((None, None, SUBLANES, ns), lambda g, d, c: (g, d, 0, 0)),
            pl.BlockSpec((None, None, SUBLANES, ns), lambda g, d, c: (g, d, 0, 0)),
        ],
        out_specs=[
            pl.BlockSpec((None, None, rows, S5_WIDTH), lambda g, d, c: (d, g, ceff(d, c), 0)),
            pl.BlockSpec((None, None, SUBLANES, ns), lambda g, d, c: (g, d, 0, 0)),
            pl.BlockSpec((None, None, SUBLANES, ns), lambda g, d, c: (g, d, 0, 0)),
        ],
        out_shape=[
            jax.ShapeDtypeStruct((2, ng, t_len * SUBLANES, S5_WIDTH), F32),
            jax.ShapeDtypeStruct((ng, 2, SUBLANES, ns), F32),
            jax.ShapeDtypeStruct((ng, 2, SUBLANES, ns), F32),
        ],
        scratch_shapes=[
            pltpu.VMEM((rows, 2 * ns), F32),
            pltpu.VMEM((SUBLANES, ns), F32),
            pltpu.VMEM((SUBLANES, ns), F32),
        ],
        compiler_params=_cparams(("arbitrary", "arbitrary", "arbitrary")),
        name="s5_scan",
    )(u_pm, wb, wc, a_re, a_im, h0_re, h0_im)


def _s5_params(lp):
    lam_re = lp['s5_lam_re'].astype(F32)
    lam_im = lp['s5_lam_im'].astype(F32)
    dt = jnp.exp(lp['s5_log_dt'].astype(F32))[..., None]
    b_re = lp['s5_b_re'].astype(F32)
    b_im = lp['s5_b_im'].astype(F32)
    c_re = lp['s5_c_re'].astype(F32)
    c_im = lp['s5_c_im'].astype(F32)
    mag = jnp.exp(lam_re * dt)
    ab_re = mag * jnp.cos(lam_im * dt)
    ab_im = mag * jnp.sin(lam_im * dt)
    den = lam_re * lam_re + lam_im * lam_im
    f_re = ((ab_re - 1.0) * lam_re + ab_im * lam_im) / den
    f_im = (ab_im * lam_re - (ab_re - 1.0) * lam_im) / den
    bb_re = f_re[..., None] * b_re - f_im[..., None] * b_im
    bb_im = f_re[..., None] * b_im + f_im[..., None] * b_re
    gpb = S5_GROUPS // S5_BLK
    eye = jnp.eye(gpb, dtype=F32)

    def pack_b(bb):
        bb = bb.reshape(2, S5_BLK, gpb, S5_STATE, S5_GROUP_CH)
        w = jnp.einsum('dbgph,gk->dbghkp', bb, eye)
        return w.reshape(2, S5_BLK, gpb * S5_GROUP_CH, gpb * S5_STATE)

    def pack_c(cc):
        cc = cc.reshape(2, S5_BLK, gpb, S5_GROUP_CH, S5_STATE)
        w = jnp.einsum('dbghp,gk->dbkpgh', cc, eye)
        return w.reshape(2, S5_BLK, gpb * S5_STATE, gpb * S5_GROUP_CH)

    wb = jnp.concatenate([pack_b(bb_re), pack_b(bb_im)], axis=-1).astype(BF16)
    wc = jnp.concatenate([pack_c(c_re), -pack_c(c_im)], axis=-2).astype(BF16)
    ns = S5_GROUPS * S5_STATE
    a_re = jnp.broadcast_to(ab_re.reshape(2, 1, ns), (2, SUBLANES, ns))
    a_im = jnp.broadcast_to(ab_im.reshape(2, 1, ns), (2, SUBLANES, ns))
    return wb, wc, a_re, a_im


def _gelu_tanh(x):
    return 0.5 * x * (1.0 + jnp.tanh(math.sqrt(2.0 / math.pi) * (x + 0.044715 * (x * x * x))))


def _glu_kernel(y_ref, u_ref, d_ref, w_ref, b_ref, *rest):
    o_ref = rest[-1]
    y = y_ref[0] + y_ref[1] + d_ref[...] * u_ref[...]
    y = _gelu_tanh(y)
    z = jnp.dot(y.astype(BF16), w_ref[...], preferred_element_type=F32) + b_ref[...]
    o_ref[...] = (y * _sigmoid(z)).astype(BF16)


def _glu(y_dirs, u, d_row, w_glu, b_glu, is_ctx, tt, prev):
    ng, t_len = y_dirs.shape[1:3]
    nb = ng * SUBLANES
    ntt = t_len // tt
    off = _part(is_ctx)[1] // tt
    prev = [prev] if prev is not None else []
    return pl.pallas_call(
        _glu_kernel,
        grid=(nb, ntt),
        in_specs=[
            pl.BlockSpec((2, None, tt, S5_WIDTH), lambda b, t: (0, b // SUBLANES, t, b % SUBLANES)),
            pl.BlockSpec((tt, S5_WIDTH), lambda b, t: (off + b * ntt + t, 0)),
            pl.BlockSpec((1, S5_WIDTH), lambda b, t: (0, 0)),
            pl.BlockSpec((S5_WIDTH, S5_WIDTH), lambda b, t: (0, 0)),
            pl.BlockSpec((1, S5_WIDTH), lambda b, t: (0, 0)),
        ] + [ANY_SPEC] * len(prev),
        out_specs=pl.BlockSpec((tt, S5_WIDTH), lambda b, t: (off + b * ntt + t, 0)),
        out_shape=jax.ShapeDtypeStruct((M, S5_WIDTH), BF16),
        input_output_aliases={5: 0} if prev else {},
        compiler_params=_cparams(("arbitrary", "arbitrary")),
        name="s5_glu",
    )(y_dirs, u, d_row, w_glu, b_glu, *prev)


def _fnet_kernel(x_ref, cs_ref, dft_ref, *rest, t_len):
    o_ref, z_ref = rest[-2:]

    @pl.when(pl.program_id(1) == 0)
    def _():
        x = x_ref[...].astype(BF16)
        for g in range(FNET_GROUPS):
            z = jnp.dot(x[:, g * LANES:(g + 1) * LANES], cs_ref[...], preferred_element_type=F32)
            z_ref[0:t_len, g * LANES:(g + 1) * LANES] = z[:, :LANES].astype(BF16)
            z_ref[t_len:2 * t_len, g * LANES:(g + 1) * LANES] = z[:, LANES:].astype(BF16)

    scale = 1.0 / math.sqrt(t_len * FNET_GROUP_CH)
    o_ref[...] = (jnp.dot(dft_ref[...], z_ref[...], preferred_element_type=F32) * scale).astype(BF16)


def _dft_tables(n):
    j = lax.broadcasted_iota(jnp.int32, (n, n), 0)
    k = lax.broadcasted_iota(jnp.int32, (n, n), 1)
    ang = ((j * k) % n).astype(F32) * (2.0 * math.pi / n)
    return jnp.cos(ang), jnp.sin(ang)


def _fnet_tables(t_len):
    c_t, s_t = _dft_tables(t_len)
    return jnp.concatenate([c_t, -s_t], axis=1).astype(BF16)


def _fnet(u, dft, cs, is_ctx, tr, prev):
    t_len = dft.shape[0]
    rows, row0 = _part(is_ctx)
    nb = rows // t_len
    ntr = t_len // tr
    off = row0 // t_len
    offr = row0 // tr
    col = S5_WIDTH // FNET_WIDTH
    prev = [prev] if prev is not None else []
    return pl.pallas_call(
        functools.partial(_fnet_kernel, t_len=t_len),
        grid=(nb, ntr),
        in_specs=[
            pl.BlockSpec((t_len, FNET_WIDTH), lambda b, r: (off + b, col)),
            pl.BlockSpec((FNET_GROUP_CH, 2 * FNET_GROUP_CH), lambda b, r: (0, 0)),
            pl.BlockSpec((tr, 2 * t_len), lambda b, r: (r, 0)),
        ] + [ANY_SPEC] * len(prev),
        out_specs=pl.BlockSpec((tr, FNET_WIDTH), lambda b, r: (offr + b * ntr + r, 0)),
        out_shape=jax.ShapeDtypeStruct((M, FNET_WIDTH), BF16),
        scratch_shapes=[pltpu.VMEM((2 * t_len, FNET_WIDTH), BF16)],
        input_output_aliases={3: 0} if prev else {},
        compiler_params=_cparams(("arbitrary", "arbitrary")),
        name="fnet",
    )(u, cs, dft, *prev)


def _rope(x, cos, sin):
    half = ROPE_AXIS_DIM // 2
    lane = lax.broadcasted_iota(jnp.int32, x.shape, 1)
    up = pltpu.roll(x, LANES - half, 1)
    dn = pltpu.roll(x, half, 1)
    rot = jnp.where((lane % ROPE_AXIS_DIM) < half, -up, dn)
    return x * cos + rot * sin


def _attn_kernel(lam_ref, q_ref, k_ref, v_ref, *rest, tq, t_len, has_ctx, lam_init):
    g_ref = rest[-4]
    o_ref, kk_ref, vv_ref = rest[-3:]
    if has_ctx:
        ck_ref, cv_ref, cos_ref, sin_ref = rest[:4]
    qi = pl.program_id(2)

    @pl.when(qi == 0)
    def _():
        k = k_ref[...]
        if has_ctx:
            k = _rope(k, cos_ref[...], sin_ref[...])
            kk_ref[t_len:, :] = ck_ref[...].astype(BF16)
            vv_ref[t_len:, 0:VALUE_DIM] = cv_ref[...].astype(BF16)
        kk_ref[0:t_len, :] = k.astype(BF16)
        vv_ref[0:t_len, 0:VALUE_DIM] = v_ref[...].astype(BF16)
        vv_ref[:, VALUE_DIM:] = jnp.ones((vv_ref.shape[0], VALUE_DIM), BF16)

    q = q_ref[...]
    if has_ctx:
        r0 = pl.multiple_of(qi * tq, tq)
        q = _rope(q, cos_ref[pl.ds(r0, tq), :], sin_ref[pl.ds(r0, tq), :])
    q = (q * (HEAD_DIM ** -0.5)).astype(BF16)
    lam = lam_ref[0]
    kk = kk_ref[...]
    vv = vv_ref[...]
    dn = (((1,), (1,)), ((), ()))
    o = None
    for n in range(2):
        s = lax.dot_general(q[:, n * HEAD_DIM:(n + 1) * HEAD_DIM], kk[:, n * HEAD_DIM:(n + 1) * HEAD_DIM],
                            dn, preferred_element_type=F32)
        p = jnp.exp(s - jnp.max(s, axis=-1, keepdims=True)).astype(BF16)
        oa = jnp.dot(p, vv, preferred_element_type=F32)
        on = oa[:, :VALUE_DIM] * (1.0 / oa[:, VALUE_DIM:VALUE_DIM + 1])
        o = on if n == 0 else o - lam * on
    o = o * lax.rsqrt(jnp.mean(o * o, axis=-1, keepdims=True) + EPS)
    o_ref[...] = ((o * g_ref[...]) * (1.0 - lam_init)).astype(BF16)


def _attention(u, lam, subln_g, lam_init, is_ctx, tq, prev, ctx=None):
    rows, row0 = _part(is_ctx)
    t_len = SEQ if is_ctx else DEC_SEQ
    nb = rows // t_len
    nq = t_len // tq
    off = row0 // t_len
    offq = row0 // tq
    qcol = (S5_WIDTH + FNET_WIDTH) // LANES
    kcol = qcol + QK_WIDTH // LANES
    vcol = kcol + QK_WIDTH // LANES
    has_ctx = ctx is not None
    n_keys = t_len + (PAST_LEN if has_ctx else 0)
    in_specs = [
        pl.BlockSpec(memory_space=pltpu.SMEM),
        pl.BlockSpec((tq, LANES), lambda b, h, i: (offq + b * nq + i, qcol + h)),
        pl.BlockSpec((t_len, LANES), lambda b, h, i: (off + b, kcol + h)),
        pl.BlockSpec((t_len, LANES), lambda b, h, i: (off + b, vcol + h)),
    ]
    args = [lam, u, u, u]
    if has_ctx:
        ck, cv, layer, cos, sin = ctx
        in_specs += [
            pl.BlockSpec((None, None, PAST_LEN, LANES), lambda b, h, i: (b, layer, 0, h)),
            pl.BlockSpec((None, None, PAST_LEN, LANES), lambda b, h, i: (b, layer, 0, h)),
            pl.BlockSpec((t_len, LANES), lambda b, h, i: (0, 0)),
            pl.BlockSpec((t_len, LANES), lambda b, h, i: (0, 0)),
        ]
        args += [ck, cv, cos, sin]
    in_specs.append(pl.BlockSpec((1, LANES), lambda b, h, i: (0, 0)))
    args.append(subln_g)
    aliases = {}
    if prev is not None:
        aliases = {len(args): 0}
        in_specs.append(ANY_SPEC)
        args.append(prev)
    kern = functools.partial(_attn_kernel, tq=tq, t_len=t_len, has_ctx=has_ctx, lam_init=lam_init)
    if prev is not None:
        kern = functools.partial(_skip_ref, kern, len(args) - 1)
    return pl.pallas_call(
        kern,
        grid=(nb, N_HEADS, nq),
        in_specs=in_specs,
        out_specs=pl.BlockSpec((tq, LANES), lambda b, h, i: (offq + b * nq + i, h)),
        out_shape=jax.ShapeDtypeStruct((M, V_WIDTH), BF16),
        scratch_shapes=[pltpu.VMEM((n_keys, LANES), BF16), pltpu.VMEM((n_keys, 2 * VALUE_DIM), BF16)],
        input_output_aliases=aliases,
        compiler_params=_cparams(("arbitrary", "arbitrary", "arbitrary")),
        name="diff_attn",
    )(*args)


def _skip_ref(kern, idx, *refs):
    return kern(*refs[:idx], *refs[idx + 1:])


def _rope_tables(t_len):
    rows = t_len // GRID_W
    pos_row = jnp.broadcast_to(jnp.arange(rows, dtype=F32)[:, None], (rows, GRID_W)).reshape(-1)
    pos_col = jnp.broadcast_to(jnp.arange(GRID_W, dtype=F32)[None, :], (rows, GRID_W)).reshape(-1)
    inv = ROPE_BASE ** (-jnp.arange(0, ROPE_AXIS_DIM, 2, dtype=F32) / ROPE_AXIS_DIM)
    ang_r = pos_row[:, None] * inv
    ang_c = pos_col[:, None] * inv
    cos = jnp.concatenate([jnp.cos(ang_r), jnp.cos(ang_r), jnp.cos(ang_c), jnp.cos(ang_c)], axis=-1)
    sin = jnp.concatenate([jnp.sin(ang_r), jnp.sin(ang_r), jnp.sin(ang_c), jnp.sin(ang_c)], axis=-1)
    return jnp.concatenate([cos, cos], axis=-1), jnp.concatenate([sin, sin], axis=-1)


def _merge_kernel(h_ref, ya_ref, yb_ref, yc_ref, wga_ref, wgb_ref, wgc_ref, bga_ref, bgb_ref, bgc_ref,
                  wa_ref, wb_ref, wc_ref, o_ref):
    h = h_ref[...]
    acc = None
    for wg, bg, y, w in ((wga_ref, bga_ref, ya_ref, wa_ref), (wgb_ref, bgb_ref, yb_ref, wb_ref),
                         (wgc_ref, bgc_ref, yc_ref, wc_ref)):
        gate = _sigmoid(jnp.dot(h, wg[...], preferred_element_type=F32) + bg[...])
        term = gate * jnp.dot(y[...], w[...], preferred_element_type=F32)
        acc = term if acc is None else acc + term
    o_ref[...] = acc.astype(BF16)


def _merge(h, ya, yb, yc, w_gate, b_gate, w_a, w_b, w_c):
    tm, tn = 1024, 512
    nj = D // tn
    row = lambda i, j: (i, 0)
    wspecs = [pl.BlockSpec((D, tn), lambda i, j, k=k: (0, k * nj + j)) for k in range(3)]
    bspecs = [pl.BlockSpec((1, tn), lambda i, j, k=k: (0, k * nj + j)) for k in range(3)]
    return pl.pallas_call(
        _merge_kernel,
        grid=(M // tm, nj),
        in_specs=[
            pl.BlockSpec((tm, D), row),
            pl.BlockSpec((tm, S5_WIDTH), row),
            pl.BlockSpec((tm, FNET_WIDTH), row),
            pl.BlockSpec((tm, V_WIDTH), row),
            *wspecs, *bspecs,
            pl.BlockSpec((S5_WIDTH, tn), lambda i, j: (0, j)),
            pl.BlockSpec((FNET_WIDTH, tn), lambda i, j: (0, j)),
            pl.BlockSpec((V_WIDTH, tn), lambda i, j: (0, j)),
        ],
        out_specs=pl.BlockSpec((tm, tn), lambda i, j: (i, j)),
        out_shape=jax.ShapeDtypeStruct((M, D), BF16),
        compiler_params=_cparams(("arbitrary", "arbitrary")),
        name="merge",
    )(h, ya, yb, yc, w_gate, w_gate, w_gate, b_gate, b_gate, b_gate, w_a, w_b, w_c)


def _out_proj_kernel(m_ref, w_ref, x_ref, g1_ref, n2_ref, sc_ref, sh_ref, *rest):
    x1_ref, h2_ref = rest[-2:]
    x1 = x_ref[...] + g1_ref[...] * jnp.dot(m_ref[...], w_ref[...], preferred_element_type=F32)
    x1_ref[...] = x1
    r = x1 * lax.rsqrt(jnp.mean(x1 * x1, axis=-1, keepdims=True) + EPS)
    h2_ref[...] = ((r * n2_ref[...]) * (1.0 + sc_ref[...]) + sh_ref[...]).astype(BF16)


def _out_proj(merged, w_out, x, x_row0, norm2_g, mod4, is_ctx, prev):
    tm = 512
    rows, row0 = _part(is_ctx)
    xoff, ooff = x_row0 // tm, row0 // tm
    mrow = _mod_row_fn(is_ctx, tm)
    orow = lambda i: (ooff + i, 0)
    mspec = lambda k: pl.BlockSpec((None, None, 1, D), lambda i, k=k: (mrow(i), k, 0, 0))
    prev = list(prev) if prev is not None else []
    return pl.pallas_call(
        _out_proj_kernel,
        grid=(rows // tm,),
        in_specs=[
            pl.BlockSpec((tm, D), orow),
            pl.BlockSpec((D, D), lambda i: (0, 0)),
            pl.BlockSpec((tm, D), lambda i: (xoff + i, 0)),
            mspec(2),
            pl.BlockSpec((1, D), lambda i: (0, 0)),
            mspec(4),
            mspec(3),
        ] + [ANY_SPEC] * len(prev),
        out_specs=[pl.BlockSpec((tm, D), orow), pl.BlockSpec((tm, D), orow)],
        out_shape=[jax.ShapeDtypeStruct((M, D), F32), jax.ShapeDtypeStruct((M, D), BF16)],
        input_output_aliases={7: 0, 8: 1} if prev else {},
        compiler_params=_cparams(("arbitrary",)),
        name="out_proj",
    )(merged, w_out, x, mod4, norm2_g.reshape(1, D), mod4, mod4, *prev)


def _segment_sum(x, seg):
    parts = []
    for s in range(ROUTE_TILE // seg):
        tot = jnp.sum(x[:, s * seg:(s + 1) * seg], axis=1, keepdims=True)
        parts.append(jnp.broadcast_to(tot, (x.shape[0], seg)))
    return parts[0] if len(parts) == 1 else jnp.concatenate(parts, axis=1)


def _segment_cumsum(x, seg):
    pos = lax.broadcasted_iota(jnp.int32, x.shape, 1) % seg
    sh = 1
    while sh < seg:
        x = x + jnp.where(pos >= sh, pltpu.roll(x, sh, 1), 0.0)
        sh *= 2
    return x


def _router_kernel(h_ref, wr_ref, slot_ref, gcol_ref, *, seg):
    cap = CAPACITY_FACTOR * seg // N_EXPERTS
    logits = lax.dot_general(wr_ref[...], h_ref[...], (((1,), (1,)), ((), ())),
                             preferred_element_type=F32)
    p = jnp.exp(logits - jnp.max(logits, axis=0, keepdims=True))
    aff = p / jnp.sum(p, axis=0, keepdims=True)
    bits = pltpu.bitcast(aff, jnp.int32)

    def bis(i, thr):
        cand = thr | jnp.left_shift(jnp.int32(1), 29 - i)
        cnt = _segment_sum(jnp.where(bits >= cand, 1.0, 0.0), seg)
        return jnp.where(cnt >= cap, cand, thr)

    thr = lax.fori_loop(0, 30, bis, jnp.zeros(bits.shape, jnp.int32))
    gt = jnp.where(bits > thr, 1.0, 0.0)
    eq = jnp.where(bits == thr, 1.0, 0.0)
    need = cap - _segment_sum(gt, seg)
    eq_rank = _segment_cumsum(eq, seg)
    sel = gt + eq * jnp.where(eq_rank <= need, 1.0, 0.0)
    rank = _segment_cumsum(sel, seg)
    seg_id = lax.broadcasted_iota(jnp.int32, bits.shape, 1) // seg
    slot = seg_id * cap + rank.astype(jnp.int32) - 1
    slot = jnp.where(sel > 0.0, slot, -1)
    slot_ref[...] = slot

    iota_c = lax.broadcasted_iota(jnp.int32, (ROUTE_SLOTS, ROUTE_TILE), 0)
    for e in range(N_EXPERTS):
        hit = iota_c == slot[e:e + 1, :]
        gcol_ref[e] = jnp.sum(jnp.where(hit, aff[e:e + 1, :], 0.0), axis=1, keepdims=True)


def _router(h2, w_router_t):
    def call(seg, tile0, ntiles):
        return pl.pallas_call(
            functools.partial(_router_kernel, seg=seg),
            grid=(ntiles,),
            in_specs=[
                pl.BlockSpec((ROUTE_TILE, D), lambda i: (tile0 + i, 0)),
                pl.BlockSpec((N_EXPERTS, D), lambda i: (0, 0)),
            ],
            out_specs=[
                pl.BlockSpec((None, N_EXPERTS, ROUTE_TILE), lambda i: (i, 0, 0)),
                pl.BlockSpec((N_EXPERTS, None, ROUTE_SLOTS, 1), lambda i: (0, i, 0, 0)),
            ],
            out_shape=[
                jax.ShapeDtypeStruct((ntiles, N_EXPERTS, ROUTE_TILE), jnp.int32),
                jax.ShapeDtypeStruct((N_EXPERTS, ntiles, ROUTE_SLOTS, 1), F32),
            ],
            compiler_params=_cparams(("arbitrary",)),
            name="router",
        )(h2, w_router_t)

    nct = M_CTX // ROUTE_TILE
    slot_c, g_c = call(SEQ, 0, nct)
    slot_l, g_l = call(DEC_SEQ, nct, N_ROUTE_TILES - nct)
    return jnp.concatenate([slot_c, slot_l], axis=0), jnp.concatenate([g_c, g_l], axis=1)


def _dispatch_kernel(h_ref, slot_ref, o_ref):
    e = pl.program_id(1)
    srow = slot_ref[pl.ds(e, 1), :]
    iota_c = lax.broadcasted_iota(jnp.int32, (ROUTE_SLOTS, ROUTE_TILE), 0)
    onehot = jnp.where(iota_c == srow, 1.0, 0.0).astype(BF16)
    o_ref[...] = jnp.dot(onehot, h_ref[...], preferred_element_type=F32).astype(BF16)


def _dispatch(h2, slot_et):
    return pl.pallas_call(
        _dispatch_kernel,
        grid=(N_ROUTE_TILES, N_EXPERTS),
        in_specs=[
            pl.BlockSpec((ROUTE_TILE, D), lambda i, e: (i, 0)),
            pl.BlockSpec((None, N_EXPERTS, ROUTE_TILE), lambda i, e: (i, 0, 0)),
        ],
        out_specs=pl.BlockSpec((None, ROUTE_SLOTS, D), lambda i, e: (e, i, 0)),
        out_shape=jax.ShapeDtypeStruct((N_EXPERTS, SLOTS_PER_EXPERT, D), BF16),
        compiler_params=_cparams(("arbitrary", "arbitrary")),
        name="moe_dispatch",
    )(h2, slot_et)


def _ffn_kernel(x_ref, w1_ref, w3_ref, w2_ref, g_ref, o_ref, acc_ref, *, tn):
    f = pl.program_id(2)
    nf = pl.num_programs(2)

    @pl.when(f == 0)
    def _():
        acc_ref[...] = jnp.zeros(acc_ref.shape, F32)

    x = x_ref[...]
    a = jnp.dot(x, w1_ref[...].astype(BF16), preferred_element_type=F32)
    b = jnp.dot(x, w3_ref[...].astype(BF16), preferred_element_type=F32)
    hmid = ((a * _sigmoid(a)) * b).astype(BF16)
    w2 = w2_ref[...].astype(BF16)
    for n in range(D // tn):
        acc_ref[:, n * tn:(n + 1) * tn] += jnp.dot(hmid, w2[:, n * tn:(n + 1) * tn],
                                                  preferred_element_type=F32)

    @pl.when(f == nf - 1)
    def _():
        o_ref[...] = (acc_ref[...] * g_ref[...]).astype(BF16)


def _ffn(xg, w1, w3, w2, gcol, layer):
    tr, tf, tn = 1024, 256, 512
    nr = SLOTS_PER_EXPERT // tr
    return pl.pallas_call(
        functools.partial(_ffn_kernel, tn=tn),
        grid=(N_EXPERTS, nr, EXPERT_FF // tf),
        in_specs=[
            pl.BlockSpec((None, tr, D), lambda e, r, f: (e, r, 0)),
            pl.BlockSpec((None, None, D, tf), lambda e, r, f: (layer, e, 0, f)),
            pl.BlockSpec((None, None, D, tf), lambda e, r, f: (layer, e, 0, f)),
            pl.BlockSpec((None, None, tf, D), lambda e, r, f: (layer, e, f, 0)),
            pl.BlockSpec((None, tr, 1), lambda e, r, f: (e, r, 0)),
        ],
        out_specs=pl.BlockSpec((None, tr, D), lambda e, r, f: (e, r, 0)),
        out_shape=jax.ShapeDtypeStruct((N_EXPERTS, SLOTS_PER_EXPERT, D), BF16),
        scratch_shapes=[pltpu.VMEM((tr, D), F32)],
        compiler_params=_cparams(("arbitrary", "arbitrary", "arbitrary")),
        name="moe_ffn",
    )(xg, w1, w3, w2, gcol)


def _combine_kernel(slot_ref, y_ref, x_ref, g2_ref, o_ref, *, tr):
    r = pl.program_id(2)
    slot = slot_ref[pl.ds(pl.multiple_of(r * tr, tr), tr), :]
    iota_c = lax.broadcasted_iota(jnp.int32, (tr, ROUTE_SLOTS), 1)
    onehot = jnp.concatenate(
        [jnp.where(iota_c == slot[:, e:e + 1], 1.0, 0.0).astype(BF16) for e in range(N_EXPERTS)], axis=1)
    y = y_ref[...].reshape(N_EXPERTS * ROUTE_SLOTS, y_ref.shape[-1])
    moe = jnp.dot(onehot, y, preferred_element_type=F32)
    o_ref[...] = x_ref[...] + g2_ref[...] * moe


def _combine(slot_te, yg, x1, mod4):
    tr, tn = 512, 1024
    nr = ROUTE_TILE // tr
    nct = M_CTX // ROUTE_TILE
    mrow = lambda i: jnp.where(i < nct, 0, 1 + i - nct)
    yg4 = yg.reshape(N_EXPERTS, N_ROUTE_TILES, ROUTE_SLOTS, D)
    return pl.pallas_call(
        functools.partial(_combine_kernel, tr=tr),
        grid=(N_ROUTE_TILES, D // tn, nr),
        in_specs=[
            pl.BlockSpec((None, ROUTE_TILE, N_EXPERTS), lambda i, j, r: (i, 0, 0)),
            pl.BlockSpec((N_EXPERTS, None, ROUTE_SLOTS, tn), lambda i, j, r: (0, i, 0, j)),
            pl.BlockSpec((tr, tn), lambda i, j, r: (i * nr + r, j)),
            pl.BlockSpec((None, None, 1, tn), lambda i, j, r: (mrow(i), 5, 0, j)),
        ],
        out_specs=pl.BlockSpec((tr, tn), lambda i, j, r: (i * nr + r, j)),
        out_shape=jax.ShapeDtypeStruct((M, D), F32),
        compiler_params=_cparams(("arbitrary", "arbitrary", "arbitrary")),
        name="moe_combine",
    )(slot_te, yg4, x1, mod4)


def _final_norm_kernel(x_ref, g_ref, o_ref):
    x = x_ref[...]
    o_ref[...] = (x * lax.rsqrt(jnp.mean(x * x, axis=-1, keepdims=True) + EPS)) * g_ref[...]


def _final_norm(x, g, is_ctx):
    tm = 512
    rows, row0 = _part(is_ctx)
    off = row0 // tm
    return pl.pallas_call(
        _final_norm_kernel,
        grid=(rows // tm,),
        in_specs=[pl.BlockSpec((tm, D), lambda i: (off + i, 0)), pl.BlockSpec((1, D), lambda i: (0, 0))],
        out_specs=pl.BlockSpec((tm, D), lambda i: (i, 0)),
        out_shape=jax.ShapeDtypeStruct((rows, D), F32),
        compiler_params=_cparams(("arbitrary",)),
        name="final_norm",
    )(x, g.reshape(1, D))


def _layer(x_parts, mod, lp, layer, lam_init, tables, cache_k4, cache_v4, st_re, st_im, new_kv):
    (xc, xc_row0), (xl, xl_row0) = x_parts
    rope_cos, rope_sin, dft_ctx, dft_lat, cs = tables
    mod4 = mod.reshape(N_MOD_ROWS, 6, 1, D)
    w_in = lp['w_in'].astype(BF16)
    h, u, ua_ctx, new_k, new_v = _norm_proj(xc, xc_row0, lp['norm1_g'], mod4, w_in, layer, True, new_kv)
    h, u, ua_lat = _norm_proj(xl, xl_row0, lp['norm1_g'], mod4, w_in, layer, False, (h, u))

    wb, wc, a_re, a_im = _s5_params(lp)
    ns = S5_GROUPS * S5_STATE
    ngc = BATCH // SUBLANES
    zeros = jnp.zeros((ngc, 2, SUBLANES, ns), F32)
    y_ctx, fin_re, fin_im = _s5(ua_ctx.reshape(SEQ, ngc, SUBLANES, S5_WIDTH), wb, wc, a_re, a_im, zeros, zeros)
    h0_re = st_re.reshape(DEC_BATCH, 2, ns).transpose(1, 0, 2)[None]
    h0_im = st_im.reshape(DEC_BATCH, 2, ns).transpose(1, 0, 2)[None]
    y_lat, _, _ = _s5(ua_lat.reshape(DEC_SEQ, 1, SUBLANES, S5_WIDTH), wb, wc, a_re, a_im, h0_re, h0_im)
    d_row = lp['s5_d'].reshape(1, S5_WIDTH)
    w_glu = lp['w_glu'].astype(BF16)
    b_glu = lp['b_glu'].reshape(1, S5_WIDTH)
    ya = _glu(y_ctx.reshape(2, ngc, SEQ, SUBLANES * S5_WIDTH), u, d_row, w_glu, b_glu, True, SEQ, None)
    ya = _glu(y_lat.reshape(2, 1, DEC_SEQ, SUBLANES * S5_WIDTH), u, d_row, w_glu, b_glu, False, 512, ya)

    yb = _fnet(u, dft_ctx, cs, True, SEQ, None)
    yb = _fnet(u, dft_lat, cs, False, 512, yb)

    lam = (jnp.exp(jnp.sum(lp['lam_q1'].astype(F32) * lp['lam_k1'].astype(F32)))
           - jnp.exp(jnp.sum(lp['lam_q2'].astype(F32) * lp['lam_k2'].astype(F32))) + lam_init).reshape(1)
    subln = lp['subln_g'].reshape(1, VALUE_DIM)
    yc = _attention(u, lam, subln, lam_init, True, SEQ, None)
    yc = _attention(u, lam, subln, lam_init, False, 512, yc,
                    ctx=(cache_k4, cache_v4, layer, rope_cos, rope_sin))

    merged = _merge(h, ya, yb, yc, lp['w_gate'].astype(BF16), lp['b_gate'].reshape(1, 3 * D),
                    lp['w_branch_a'].astype(BF16), lp['w_branch_b'].astype(BF16),
                    lp['w_branch_c'].astype(BF16))
    w_out = lp['w_out'].astype(BF16)
    x1, h2 = _out_proj(merged, w_out, xc, xc_row0, lp['norm2_g'], mod4, True, None)
    x1, h2 = _out_proj(merged, w_out, xl, xl_row0, lp['norm2_g'], mod4, False, (x1, h2))

    slot_et, gcol = _router(h2, lp['w_router'].T.astype(BF16))
    xg = _dispatch(h2, slot_et)
    yg = _ffn(xg, lp['moe_w1_all'], lp['moe_w3_all'], lp['moe_w2_all'],
              gcol.reshape(N_EXPERTS, SLOTS_PER_EXPERT, 1), layer)
    x2 = _combine(slot_et.transpose(0, 2, 1), yg, x1, mod4)

    fin_shape = (ngc, 2, SUBLANES, S5_GROUPS, S5_STATE)
    s_re = fin_re.reshape(fin_shape).transpose(0, 2, 1, 3, 4).reshape(BATCH, 2, S5_GROUPS, S5_STATE)
    s_im = fin_im.reshape(fin_shape).transpose(0, 2, 1, 3, 4).reshape(BATCH, 2, S5_GROUPS, S5_STATE)
    return x2, (new_k, new_v), s_re, s_im


def kernel(x_prompt, x_sample, cache_k, cache_v, state_s5_re, state_s5_im, c, c_ctx, norm1_g, norm2_g, final_norm_g, w_ada, b_ada, w_in, s5_lam_re, s5_lam_im, s5_log_dt, s5_b_re, s5_b_im, s5_c_re, s5_c_im, s5_d, w_glu, b_glu, lam_q1, lam_k1, lam_q2, lam_k2, subln_g, w_branch_a, w_branch_b, w_branch_c, w_gate, b_gate, w_out, w_router, moe_w1, moe_w3, moe_w2):
    params = dict(norm1_g=norm1_g, norm2_g=norm2_g, w_in=w_in,
                  s5_lam_re=s5_lam_re, s5_lam_im=s5_lam_im, s5_log_dt=s5_log_dt,
                  s5_b_re=s5_b_re, s5_b_im=s5_b_im, s5_c_re=s5_c_re, s5_c_im=s5_c_im, s5_d=s5_d,
                  w_glu=w_glu, b_glu=b_glu, lam_q1=lam_q1, lam_k1=lam_k1, lam_q2=lam_q2, lam_k2=lam_k2,
                  subln_g=subln_g, w_branch_a=w_branch_a, w_branch_b=w_branch_b, w_branch_c=w_branch_c,
                  w_gate=w_gate, b_gate=b_gate, w_out=w_out, w_router=w_router)
    cc = jnp.zeros((N_MOD_ROWS, D), F32).at[0].set(c_ctx).at[1:1 + DEC_BATCH].set(c)
    mods = _ada(cc, w_ada, b_ada)
    rope_cos, rope_sin = _rope_tables(DEC_SEQ)
    c_c, s_c = _dft_tables(FNET_GROUP_CH)
    cs = jnp.concatenate([c_c, s_c], axis=1).astype(BF16)
    tables = (rope_cos, rope_sin, _fnet_tables(SEQ), _fnet_tables(DEC_SEQ), cs)
    cache_k4 = cache_k.reshape(DEC_BATCH, DEPTH, PAST_LEN, QK_WIDTH)
    cache_v4 = cache_v.reshape(DEC_BATCH, DEPTH, PAST_LEN, V_WIDTH)
    x_parts = ((x_prompt.reshape(M_CTX, D), 0), (x_sample.reshape(M_LAT, D), 0))
    new_kv = None
    s_res, s_ims = [], []
    for l in range(DEPTH):
        lp = {name: arr[l] for name, arr in params.items()}
        lp['moe_w1_all'], lp['moe_w3_all'], lp['moe_w2_all'] = moe_w1, moe_w3, moe_w2
        lam_init = 0.8 - 0.6 * math.exp(-0.3 * l)
        x, new_kv, s_re, s_im = _layer(x_parts, mods[l], lp, l, lam_init, tables, cache_k4, cache_v4,
                                       state_s5_re[:, l], state_s5_im[:, l], new_kv)
        x_parts = ((x, 0), (x, M_CTX))
        s_res.append(s_re)
        s_ims.append(s_im)
    y_prompt = _final_norm(x, final_norm_g, True).reshape(BATCH, SEQ, D)
    y_sample = _final_norm(x, final_norm_g, False).reshape(DEC_BATCH, DEC_SEQ, D)
    kv_shape = (BATCH, DEPTH, SEQ, N_HEADS, VALUE_DIM)
    return (y_prompt, y_sample, new_kv[0].reshape(kv_shape), new_kv[1].reshape(kv_shape),
            jnp.stack(s_res, axis=1), jnp.stack(s_ims, axis=1))
```

```python
import functools
import math

import jax
import jax.numpy as jnp
from jax import lax
from jax.experimental import pallas as pl
from jax.experimental.pallas import tpu as pltpu

F32 = jnp.float32
BF16 = jnp.bfloat16

D = 2048
BATCH, SEQ = 32, 256
DEC_BATCH, DEC_SEQ = 8, 2048
DEPTH = 2
PAST_LEN = 256
GRID_W = 64
EPS = 1e-6
S5_GROUP_CH, S5_GROUPS, S5_STATE = 16, 32, 64
S5_WIDTH = S5_GROUPS * S5_GROUP_CH
FNET_GROUPS, FNET_GROUP_CH = 4, 128
FNET_WIDTH = FNET_GROUPS * FNET_GROUP_CH
N_HEADS, HEAD_DIM = 8, 64
VALUE_DIM = 2 * HEAD_DIM
QK_WIDTH = N_HEADS * 2 * HEAD_DIM
V_WIDTH = N_HEADS * VALUE_DIM
IN_WIDTH = S5_WIDTH + FNET_WIDTH + 2 * QK_WIDTH + V_WIDTH
ROPE_BASE = 10000.0
ROPE_AXIS_DIM = HEAD_DIM // 2
N_EXPERTS = 16
EXPERT_FF = 2048
CAPACITY_FACTOR = 2

M_CTX = BATCH * SEQ
M_LAT = DEC_BATCH * DEC_SEQ
M = M_CTX + M_LAT
N_MOD_ROWS = 16

LANES = 128
SUBLANES = 8
VMEM_LIMIT = 56 * 1024 * 1024

ROUTE_TILE = 2048
ROUTE_SLOTS = CAPACITY_FACTOR * ROUTE_TILE // N_EXPERTS
N_ROUTE_TILES = M // ROUTE_TILE
SLOTS_PER_EXPERT = N_ROUTE_TILES * ROUTE_SLOTS

S5_BLK = 4
S5_BLK_STATES = 512

ANY_SPEC = pl.BlockSpec(memory_space=pl.ANY)


def _cparams(sem):
    return pltpu.CompilerParams(dimension_semantics=sem, vmem_limit_bytes=VMEM_LIMIT)


def _sigmoid(x):
    return 1.0 / (1.0 + jnp.exp(-x))


def _part(is_ctx):
    return (M_CTX, 0) if is_ctx else (M_LAT, M_CTX)


def _mod_row_fn(is_ctx, tm):
    if is_ctx:
        return lambda i: 0
    return lambda i: 1 + i // (DEC_SEQ // tm)


def _ada_kernel(c_ref, w_ref, b_ref, o_ref):
    c = c_ref[...]
    s = (c * _sigmoid(c)).astype(BF16)
    o_ref[...] = jnp.dot(s, w_ref[...].astype(BF16), preferred_element_type=F32) + b_ref[...]


def _ada(cc, w_ada, b_ada):
    tn = 1024
    n = 6 * D
    return pl.pallas_call(
        _ada_kernel,
        grid=(DEPTH, n // tn),
        in_specs=[
            pl.BlockSpec((N_MOD_ROWS, D), lambda l, j: (0, 0)),
            pl.BlockSpec((None, D, tn), lambda l, j: (l, 0, j)),
            pl.BlockSpec((None, 1, tn), lambda l, j: (l, 0, j)),
        ],
        out_specs=pl.BlockSpec((None, N_MOD_ROWS, tn), lambda l, j: (l, 0, j)),
        out_shape=jax.ShapeDtypeStruct((DEPTH, N_MOD_ROWS, n), F32),
        compiler_params=_cparams(("arbitrary", "arbitrary")),
        name="ada",
    )(cc, w_ada, b_ada.reshape(DEPTH, 1, n))


NP_TM_CTX, NP_TM_LAT, NP_TN = 512, 1024, 1024


def _norm_proj_kernel(x_ref, g_ref, sc_ref, sh_ref, w_ref, *rest, is_ctx, n_alias):
    rest = rest[n_alias:]
    if is_ctx:
        h_ref, u_ref, ua_ref, k_ref, v_ref = rest
    else:
        h_ref, u_ref, ua_ref = rest
    j = pl.program_id(1)

    @pl.when(j == 0)
    def _():
        x = x_ref[...]
        r = x * lax.rsqrt(jnp.mean(x * x, axis=-1, keepdims=True) + EPS)
        h = (r * g_ref[...]) * (1.0 + sc_ref[...]) + sh_ref[...]
        h_ref[...] = h.astype(BF16)

    u_ref[...] = jnp.dot(h_ref[...], w_ref[...], preferred_element_type=F32)

    @pl.when(j == 0)
    def _():
        if is_ctx:
            for b in range(NP_TM_CTX // SEQ):
                ua_ref[:, b * S5_WIDTH:(b + 1) * S5_WIDTH] = u_ref[b * SEQ:(b + 1) * SEQ, :S5_WIDTH]
        else:
            ua_ref[...] = u_ref[:, :S5_WIDTH]

    if is_ctx:
        @pl.when(j == 2)
        def _():
            for b in range(NP_TM_CTX // SEQ):
                k_ref[b] = u_ref[b * SEQ:(b + 1) * SEQ, :]

        @pl.when(j == 3)
        def _():
            for b in range(NP_TM_CTX // SEQ):
                v_ref[b] = u_ref[b * SEQ:(b + 1) * SEQ, :]


def _norm_proj(x, x_row0, g, mod4, w_in, layer, is_ctx, prev):
    tm, tn = (NP_TM_CTX if is_ctx else NP_TM_LAT), NP_TN
    assert IN_WIDTH // tn == 4 and QK_WIDTH == tn and V_WIDTH == tn
    rows, row0 = _part(is_ctx)
    xoff, ooff = x_row0 // tm, row0 // tm
    mrow = _mod_row_fn(is_ctx, tm)
    prev = list(prev) if prev is not None else []
    in_specs = [
        pl.BlockSpec((tm, D), lambda i, j: (xoff + i, 0)),
        pl.BlockSpec((1, D), lambda i, j: (0, 0)),
        pl.BlockSpec((None, None, 1, D), lambda i, j: (mrow(i), 1, 0, 0)),
        pl.BlockSpec((None, None, 1, D), lambda i, j: (mrow(i), 0, 0, 0)),
        pl.BlockSpec((D, tn), lambda i, j: (0, j)),
    ] + [ANY_SPEC] * len(prev)
    out_specs = [
        pl.BlockSpec((tm, D), lambda i, j: (ooff + i, 0)),
        pl.BlockSpec((tm, tn), lambda i, j: (ooff + i, j)),
    ]
    out_shape = [jax.ShapeDtypeStruct((M, D), BF16), jax.ShapeDtypeStruct((M, IN_WIDTH), F32)]
    if is_ctx:
        nbt = tm // SEQ
        out_specs += [
            pl.BlockSpec((SEQ, nbt * S5_WIDTH), lambda i, j: (0, i)),
            pl.BlockSpec((nbt, None, SEQ, QK_WIDTH), lambda i, j: (i, layer, 0, 0)),
            pl.BlockSpec((nbt, None, SEQ, V_WIDTH), lambda i, j: (i, layer, 0, 0)),
        ]
        out_shape += [
            jax.ShapeDtypeStruct((SEQ, BATCH * S5_WIDTH), F32),
            jax.ShapeDtypeStruct((BATCH, DEPTH, SEQ, QK_WIDTH), F32),
            jax.ShapeDtypeStruct((BATCH, DEPTH, SEQ, V_WIDTH), F32),
        ]
        aliases = {5: 3, 6: 4} if prev else {}
    else:
        tpb = DEC_SEQ // tm
        out_specs.append(pl.BlockSpec((tm, S5_WIDTH), lambda i, j: (i % tpb, i // tpb)))
        out_shape.append(jax.ShapeDtypeStruct((DEC_SEQ, DEC_BATCH * S5_WIDTH), F32))
        aliases = {5: 0, 6: 1}
    return pl.pallas_call(
        functools.partial(_norm_proj_kernel, is_ctx=is_ctx, n_alias=len(prev)),
        grid=(rows // tm, IN_WIDTH // tn),
        in_specs=in_specs,
        out_specs=out_specs,
        out_shape=out_shape,
        input_output_aliases=aliases,
        compiler_params=_cparams(("arbitrary", "arbitrary")),
        name="norm_proj",
    )(x, g.reshape(1, D), mod4, mod4, w_in, *prev)


def _s5_kernel(u_ref, wb_ref, wc_ref, are_ref, aim_ref, h0re_ref, h0im_ref,
               y_ref, finre_ref, finim_ref, xs_ref, stre_ref, stim_ref, *, chunk):
    d = pl.program_id(1)
    c = pl.program_id(2)
    nc = pl.num_programs(2)
    bs = S5_BLK_STATES
    rows = chunk * SUBLANES

    @pl.when(c == 0)
    def _():
        stre_ref[...] = h0re_ref[...]
        stim_ref[...] = h0im_ref[...]

    u = u_ref[...].reshape(rows, S5_WIDTH).astype(BF16)
    for blk in range(S5_BLK):
        xs_ref[:, blk * 2 * bs:(blk + 1) * 2 * bs] = jnp.dot(
            u[:, blk * LANES:(blk + 1) * LANES], wb_ref[blk], preferred_element_type=F32)

    for blk in range(S5_BLK):
        ar = are_ref[:, blk * bs:(blk + 1) * bs]
        ai = aim_ref[:, blk * bs:(blk + 1) * bs]
        cre = slice(blk * 2 * bs, blk * 2 * bs + bs)
        cim = slice(blk * 2 * bs + bs, (blk + 1) * 2 * bs)

        def step(i, carry, ar=ar, ai=ai, cre=cre, cim=cim):
            sr, si = carry
            t = jnp.where(d == 0, i, chunk - 1 - i)
            r8 = pl.ds(pl.multiple_of(t * SUBLANES, SUBLANES), SUBLANES)
            nr = ar * sr - ai * si + xs_ref[r8, cre]
            ni = ar * si + ai * sr + xs_ref[r8, cim]
            xs_ref[r8, cre] = nr
            xs_ref[r8, cim] = ni
            return nr, ni

        sr, si = lax.fori_loop(
            0, chunk, step,
            (stre_ref[:, blk * bs:(blk + 1) * bs], stim_ref[:, blk * bs:(blk + 1) * bs]),
            unroll=2)
        stre_ref[:, blk * bs:(blk + 1) * bs] = sr
        stim_ref[:, blk * bs:(blk + 1) * bs] = si

    for blk in range(S5_BLK):
        xb = xs_ref[:, blk * 2 * bs:(blk + 1) * 2 * bs].astype(BF16)
        y_ref[:, blk * LANES:(blk + 1) * LANES] = jnp.dot(xb, wc_ref[blk], preferred_element_type=F32)

    @pl.when(c == nc - 1)
    def _():
        finre_ref[...] = stre_ref[...]
        finim_ref[...] = stim_ref[...]


def _s5(u_pm, wb, wc, a_re, a_im, h0_re, h0_im, chunk=64):
    t_len, ng = u_pm.shape[:2]
    nc = t_len // chunk
    rows = chunk * SUBLANES
    ns = S5_BLK * S5_BLK_STATES

    def ceff(d, c):
        return jnp.where(d == 0, c, nc - 1 - c)

    return pl.pallas_call(
        functools.partial(_s5_kernel, chunk=chunk),
        grid=(ng, 2, nc),
        in_specs=[
            pl.BlockSpec((chunk, None, SUBLANES, S5_WIDTH), lambda g, d, c: (ceff(d, c), g, 0, 0)),
            pl.BlockSpec((None, S5_BLK, LANES, 2 * S5_BLK_STATES), lambda g, d, c: (d, 0, 0, 0)),
            pl.BlockSpec((None, S5_BLK, 2 * S5_BLK_STATES, LANES), lambda g, d, c: (d, 0, 0, 0)),
            pl.BlockSpec((None, SUBLANES, ns), lambda g, d, c: (d, 0, 0)),
            pl.BlockSpec((None, SUBLANES, ns), lambda g, d, c: (d, 0, 0)),
            pl.BlockSpec((None, None, SUBLANES, ns), lambda g, d, c: (g, d, 0, 0)),
            pl.BlockSpec((None, None, SUBLANES, ns), lambda g, d, c: (g, d, 0, 0)),
        ],
        out_specs=[
            pl.BlockSpec((None, None, rows, S5_WIDTH), lambda g, d, c: (d, g, ceff(d, c), 0)),
            pl.BlockSpec((None, None, SUBLANES, ns), lambda g, d, c: (g, d, 0, 0)),
            pl.BlockSpec((None, None, SUBLANES, ns), lambda g, d, c: (g, d, 0, 0)),
        ],
        out_shape=[
            jax.ShapeDtypeStruct((2, ng, t_len * SUBLANES, S5_WIDTH), F32),
            jax.ShapeDtypeStruct((ng, 2, SUBLANES, ns), F32),
            jax.ShapeDtypeStruct((ng, 2, SUBLANES, ns), F32),
        ],
        scratch_shapes=[
            pltpu.VMEM((rows, 2 * ns), F32),
            pltpu.VMEM((SUBLANES, ns), F32),
            pltpu.VMEM((SUBLANES, ns), F32),
        ],
        compiler_params=_cparams(("arbitrary", "arbitrary", "arbitrary")),
        name="s5_scan",
    )(u_pm, wb, wc, a_re, a_im, h0_re, h0_im)


def _s5_params(lp):
    lam_re = lp['s5_lam_re'].astype(F32)
    lam_im = lp['s5_lam_im'].astype(F32)
    dt = jnp.exp(lp['s5_log_dt'].astype(F32))[..., None]
    b_re = lp['s5_b_re'].astype(F32)
    b_im = lp['s5_b_im'].astype(F32)
    c_re = lp['s5_c_re'].astype(F32)
    c_im = lp['s5_c_im'].astype(F32)
    mag = jnp.exp(lam_re * dt)
    ab_re = mag * jnp.cos(lam_im * dt)
    ab_im = mag * jnp.sin(lam_im * dt)
    den = lam_re * lam_re + lam_im * lam_im
    f_re = ((ab_re - 1.0) * lam_re + ab_im * lam_im) / den
    f_im = (ab_im * lam_re - (ab_re - 1.0) * lam_im) / den
    bb_re = f_re[..., None] * b_re - f_im[..., None] * b_im
    bb_im = f_re[..., None] * b_im + f_im[..., None] * b_re
    gpb = S5_GROUPS // S5_BLK
    eye = jnp.eye(gpb, dtype=F32)

    def pack_b(bb):
        bb = bb.reshape(2, S5_BLK, gpb, S5_STATE, S5_GROUP_CH)
        w = jnp.einsum('dbgph,gk->dbghkp', bb, eye)
        return w.reshape(2, S5_BLK, gpb * S5_GROUP_CH, gpb * S5_STATE)

    def pack_c(cc):
        cc = cc.reshape(2, S5_BLK, gpb, S5_GROUP_CH, S5_STATE)
        w = jnp.einsum('dbghp,gk->dbkpgh', cc, eye)
        return w.reshape(2, S5_BLK, gpb * S5_STATE, gpb * S5_GROUP_CH)

    wb = jnp.concatenate([pack_b(bb_re), pack_b(bb_im)], axis=-1).astype(BF16)
    wc = jnp.concatenate([pack_c(c_re), -pack_c(c_im)], axis=-2).astype(BF16)
    ns = S5_GROUPS * S5_STATE
    a_re = jnp.broadcast_to(ab_re.reshape(2, 1, ns), (2, SUBLANES, ns))
    a_im = jnp.broadcast_to(ab_im.reshape(2, 1, ns), (2, SUBLANES, ns))
    return wb, wc, a_re, a_im


def _gelu_tanh(x):
    return 0.5 * x * (1.0 + jnp.tanh(math.sqrt(2.0 / math.pi) * (x + 0.044715 * (x * x * x))))


def _glu_kernel(y_ref, u_ref, d_ref, w_ref, b_ref, *rest):
    o_ref = rest[-1]
    y = y_ref[0] + y_ref[1] + d_ref[...] * u_ref[...]
    y = _gelu_tanh(y)
    z = jnp.dot(y.astype(BF16), w_ref[...], preferred_element_type=F32) + b_ref[...]
    o_ref[...] = (y * _sigmoid(z)).astype(BF16)


def _glu(y_dirs, u, d_row, w_glu, b_glu, is_ctx, tt, prev):
    ng, t_len = y_dirs.shape[1:3]
    nb = ng * SUBLANES
    ntt = t_len // tt
    off = _part(is_ctx)[1] // tt
    prev = [prev] if prev is not None else []
    return pl.pallas_call(
        _glu_kernel,
        grid=(nb, ntt),
        in_specs=[
            pl.BlockSpec((2, None, tt, S5_WIDTH), lambda b, t: (0, b // SUBLANES, t, b % SUBLANES)),
            pl.BlockSpec((tt, S5_WIDTH), lambda b, t: (off + b * ntt + t, 0)),
            pl.BlockSpec((1, S5_WIDTH), lambda b, t: (0, 0)),
            pl.BlockSpec((S5_WIDTH, S5_WIDTH), lambda b, t: (0, 0)),
            pl.BlockSpec((1, S5_WIDTH), lambda b, t: (0, 0)),
        ] + [ANY_SPEC] * len(prev),
        out_specs=pl.BlockSpec((tt, S5_WIDTH), lambda b, t: (off + b * ntt + t, 0)),
        out_shape=jax.ShapeDtypeStruct((M, S5_WIDTH), BF16),
        input_output_aliases={5: 0} if prev else {},
        compiler_params=_cparams(("arbitrary", "arbitrary")),
        name="s5_glu",
    )(y_dirs, u, d_row, w_glu, b_glu, *prev)


def _fnet_kernel(x_ref, cs_ref, dft_ref, *rest, t_len):
    o_ref, z_ref = rest[-2:]

    @pl.when(pl.program_id(1) == 0)
    def _():
        x = x_ref[...].astype(BF16)
        for g in range(FNET_GROUPS):
            z = jnp.dot(x[:, g * LANES:(g + 1) * LANES], cs_ref[...], preferred_element_type=F32)
            z_ref[0:t_len, g * LANES:(g + 1) * LANES] = z[:, :LANES].astype(BF16)
            z_ref[t_len:2 * t_len, g * LANES:(g + 1) * LANES] = z[:, LANES:].astype(BF16)

    scale = 1.0 / math.sqrt(t_len * FNET_GROUP_CH)
    o_ref[...] = (jnp.dot(dft_ref[...], z_ref[...], preferred_element_type=F32) * scale).astype(BF16)


def _dft_tables(n):
    j = lax.broadcasted_iota(jnp.int32, (n, n), 0)
    k = lax.broadcasted_iota(jnp.int32, (n, n), 1)
    ang = ((j * k) % n).astype(F32) * (2.0 * math.pi / n)
    return jnp.cos(ang), jnp.sin(ang)


def _fnet_tables(t_len):
    c_t, s_t = _dft_tables(t_len)
    return jnp.concatenate([c_t, -s_t], axis=1).astype(BF16)


def _fnet(u, dft, cs, is_ctx, tr, prev):
    t_len = dft.shape[0]
    rows, row0 = _part(is_ctx)
    nb = rows // t_len
    ntr = t_len // tr
    off = row0 // t_len
    offr = row0 // tr
    col = S5_WIDTH // FNET_WIDTH
    prev = [prev] if prev is not None else []
    return pl.pallas_call(
        functools.partial(_fnet_kernel, t_len=t_len),
        grid=(nb, ntr),
        in_specs=[
            pl.BlockSpec((t_len, FNET_WIDTH), lambda b, r: (off + b, col)),
            pl.BlockSpec((FNET_GROUP_CH, 2 * FNET_GROUP_CH), lambda b, r: (0, 0)),
            pl.BlockSpec((tr, 2 * t_len), lambda b, r: (r, 0)),
        ] + [ANY_SPEC] * len(prev),
        out_specs=pl.BlockSpec((tr, FNET_WIDTH), lambda b, r: (offr + b * ntr + r, 0)),
        out_shape=jax.ShapeDtypeStruct((M, FNET_WIDTH), BF16),
        scratch_shapes=[pltpu.VMEM((2 * t_len, FNET_WIDTH), BF16)],
        input_output_aliases={3: 0} if prev else {},
        compiler_params=_cparams(("arbitrary", "arbitrary")),
        name="fnet",
    )(u, cs, dft, *prev)


def _rope(x, cos, sin):
    half = ROPE_AXIS_DIM // 2
    lane = lax.broadcasted_iota(jnp.int32, x.shape, 1)
    up = pltpu.roll(x, LANES - half, 1)
    dn = pltpu.roll(x, half, 1)
    rot = jnp.where((lane % ROPE_AXIS_DIM) < half, -up, dn)
    return x * cos + rot * sin


ATTN_KEY_CHUNK = 512
ATTN_ROW_BLOCK = 256


def _attn_kernel(lam_ref, q_ref, k_ref, v_ref, *rest, tq, t_len, hb, has_ctx, lam_init):
    g_ref = rest[-6]
    o_ref, kk_ref, vv_ref, qs_ref, s_ref = rest[-5:]
    if has_ctx:
        ck_ref, cv_ref, cos_ref, sin_ref = rest[:4]
    qi = pl.program_id(2)
    n_keys = kk_ref.shape[1]
    kc, rb = ATTN_KEY_CHUNK, ATTN_ROW_BLOCK
    chunks = [(c0, min(c0 + kc, n_keys)) for c0 in range(0, n_keys, kc)]

    @pl.when(qi == 0)
    def _():
        for h in range(hb):
            hs = slice(h * LANES, (h + 1) * LANES)
            k = k_ref[:, hs]
            if has_ctx:
                k = _rope(k, cos_ref[...], sin_ref[...])
                kk_ref[h, t_len:, :] = ck_ref[:, hs].astype(BF16)
                vv_ref[h, t_len:, 0:VALUE_DIM] = cv_ref[:, hs].astype(BF16)
            kk_ref[h, 0:t_len, :] = k.astype(BF16)
            vv_ref[h, 0:t_len, 0:VALUE_DIM] = v_ref[:, hs].astype(BF16)
            vv_ref[h, :, VALUE_DIM:] = jnp.ones((n_keys, VALUE_DIM), BF16)

    lane = lax.broadcasted_iota(jnp.int32, (tq, LANES), 1)
    for h in range(hb):
        hs = slice(h * LANES, (h + 1) * LANES)
        q = q_ref[:, hs]
        if has_ctx:
            r0 = pl.multiple_of(qi * tq, tq)
            q = _rope(q, cos_ref[pl.ds(r0, tq), :], sin_ref[pl.ds(r0, tq), :])
        q = q * (HEAD_DIM ** -0.5)
        qs_ref[0, :, hs] = jnp.where(lane < HEAD_DIM, q, 0.0).astype(BF16)
        qs_ref[1, :, hs] = jnp.where(lane >= HEAD_DIM, q, 0.0).astype(BF16)

    lam = lam_ref[0]
    gain = g_ref[...]
    dn = (((1,), (1,)), ((), ()))

    unit = 0
    for r in range(tq // rb):
        rows = slice(r * rb, (r + 1) * rb)
        for h in range(hb):
            hs = slice(h * LANES, (h + 1) * LANES)
            o = None
            for n in range(2):
                slot = unit % 2
                unit += 1
                qn = qs_ref[n, rows, hs]
                mrun = None
                for c0, c1 in chunks:
                    s_c = lax.dot_general(qn, kk_ref[h, c0:c1, :], dn, preferred_element_type=F32)
                    s_ref[slot, :, c0:c1] = s_c
                    for l0 in range(0, c1 - c0, LANES):
                        mc = s_c[:, l0:l0 + LANES]
                        mrun = mc if mrun is None else jnp.maximum(mrun, mc)
                m_b = jnp.broadcast_to(jnp.max(mrun, axis=-1, keepdims=True), (rb, LANES))
                oa = None
                for c0, c1 in chunks:
                    p = jnp.concatenate(
                        [jnp.exp(s_ref[slot, :, l0:l0 + LANES] - m_b) for l0 in range(c0, c1, LANES)],
                        axis=1).astype(BF16)
                    part = jnp.dot(p, vv_ref[h, c0:c1, :], preferred_element_type=F32)
                    oa = part if oa is None else oa + part
                on = oa[:, :VALUE_DIM] * (1.0 / oa[:, VALUE_DIM:VALUE_DIM + 1])
                o = on if n == 0 else o - lam * on
            o = o * lax.rsqrt(jnp.mean(o * o, axis=-1, keepdims=True) + EPS)
            o_ref[rows, hs] = ((o * gain) * (1.0 - lam_init)).astype(BF16)


def _attention(u, lam, subln_g, lam_init, is_ctx, tq, hb, prev, ctx=None):
    rows, row0 = _part(is_ctx)
    t_len = SEQ if is_ctx else DEC_SEQ
    nb = rows // t_len
    nq = t_len // tq
    off = row0 // t_len
    offq = row0 // tq
    width = hb * LANES
    qcol = (S5_WIDTH + FNET_WIDTH) // width
    kcol = qcol + QK_WIDTH // width
    vcol = kcol + QK_WIDTH // width
    has_ctx = ctx is not None
    n_keys = t_len + (PAST_LEN if has_ctx else 0)
    assert tq % ATTN_ROW_BLOCK == 0 and n_keys % (2 * LANES) == 0
    in_specs = [
        pl.BlockSpec(memory_space=pltpu.SMEM),
        pl.BlockSpec((tq, width), lambda b, h, i: (offq + b * nq + i, qcol + h)),
        pl.BlockSpec((t_len, width), lambda b, h, i: (off + b, kcol + h)),
        pl.BlockSpec((t_len, width), lambda b, h, i: (off + b, vcol + h)),
    ]
    args = [lam, u, u, u]
    if has_ctx:
        ck, cv, layer, cos, sin = ctx
        in_specs += [
            pl.BlockSpec((None, None, PAST_LEN, width), lambda b, h, i: (b, layer, 0, h)),
            pl.BlockSpec((None, None, PAST_LEN, width), lambda b, h, i: (b, layer, 0, h)),
            pl.BlockSpec((t_len, LANES), lambda b, h, i: (0, 0)),
            pl.BlockSpec((t_len, LANES), lambda b, h, i: (0, 0)),
        ]
        args += [ck, cv, cos, sin]
    in_specs.append(pl.BlockSpec((1, LANES), lambda b, h, i: (0, 0)))
    args.append(subln_g)
    aliases = {}
    if prev is not None:
        aliases = {len(args): 0}
        in_specs.append(ANY_SPEC)
        args.append(prev)
    kern = functools.partial(_attn_kernel, tq=tq, t_len=t_len, hb=hb, has_ctx=has_ctx, lam_init=lam_init)
    if prev is not None:
        kern = functools.partial(_skip_ref, kern, len(args) - 1)
    return pl.pallas_call(
        kern,
        grid=(nb, N_HEADS // hb, nq),
        in_specs=in_specs,
        out_specs=pl.BlockSpec((tq, width), lambda b, h, i: (offq + b * nq + i, h)),
        out_shape=jax.ShapeDtypeStruct((M, V_WIDTH), BF16),
        scratch_shapes=[
            pltpu.VMEM((hb, n_keys, LANES), BF16),
            pltpu.VMEM((hb, n_keys, 2 * VALUE_DIM), BF16),
            pltpu.VMEM((2, tq, width), BF16),
            pltpu.VMEM((2, ATTN_ROW_BLOCK, n_keys), F32),
        ],
        input_output_aliases=aliases,
        compiler_params=_cparams(("arbitrary", "arbitrary", "arbitrary")),
        name="diff_attn",
    )(*args)


def _skip_ref(kern, idx, *refs):
    return kern(*refs[:idx], *refs[idx + 1:])


def _rope_tables(t_len):
    rows = t_len // GRID_W
    pos_row = jnp.broadcast_to(jnp.arange(rows, dtype=F32)[:, None], (rows, GRID_W)).reshape(-1)
    pos_col = jnp.broadcast_to(jnp.arange(GRID_W, dtype=F32)[None, :], (rows, GRID_W)).reshape(-1)
    inv = ROPE_BASE ** (-jnp.arange(0, ROPE_AXIS_DIM, 2, dtype=F32) / ROPE_AXIS_DIM)
    ang_r = pos_row[:, None] * inv
    ang_c = pos_col[:, None] * inv
    cos = jnp.concatenate([jnp.cos(ang_r), jnp.cos(ang_r), jnp.cos(ang_c), jnp.cos(ang_c)], axis=-1)
    sin = jnp.concatenate([jnp.sin(ang_r), jnp.sin(ang_r), jnp.sin(ang_c), jnp.sin(ang_c)], axis=-1)
    return jnp.concatenate([cos, cos], axis=-1), jnp.concatenate([sin, sin], axis=-1)


def _merge_kernel(h_ref, ya_ref, yb_ref, yc_ref, wga_ref, wgb_ref, wgc_ref, bga_ref, bgb_ref, bgc_ref,
                  wa_ref, wb_ref, wc_ref, o_ref):
    h = h_ref[...]
    acc = None
    for wg, bg, y, w in ((wga_ref, bga_ref, ya_ref, wa_ref), (wgb_ref, bgb_ref, yb_ref, wb_ref),
                         (wgc_ref, bgc_ref, yc_ref, wc_ref)):
        gate = _sigmoid(jnp.dot(h, wg[...], preferred_element_type=F32) + bg[...])
        term = gate * jnp.dot(y[...], w[...], preferred_element_type=F32)
        acc = term if acc is None else acc + term
    o_ref[...] = acc.astype(BF16)


def _merge(h, ya, yb, yc, w_gate, b_gate, w_a, w_b, w_c):
    tm, tn = 1024, 512
    nj = D // tn
    row = lambda i, j: (i, 0)
    wspecs = [pl.BlockSpec((D, tn), lambda i, j, k=k: (0, k * nj + j)) for k in range(3)]
    bspecs = [pl.BlockSpec((1, tn), lambda i, j, k=k: (0, k * nj + j)) for k in range(3)]
    return pl.pallas_call(
        _merge_kernel,
        grid=(M // tm, nj),
        in_specs=[
            pl.BlockSpec((tm, D), row),
            pl.BlockSpec((tm, S5_WIDTH), row),
            pl.BlockSpec((tm, FNET_WIDTH), row),
            pl.BlockSpec((tm, V_WIDTH), row),
            *wspecs, *bspecs,
            pl.BlockSpec((S5_WIDTH, tn), lambda i, j: (0, j)),
            pl.BlockSpec((FNET_WIDTH, tn), lambda i, j: (0, j)),
            pl.BlockSpec((V_WIDTH, tn), lambda i, j: (0, j)),
        ],
        out_specs=pl.BlockSpec((tm, tn), lambda i, j: (i, j)),
        out_shape=jax.ShapeDtypeStruct((M, D), BF16),
        compiler_params=_cparams(("arbitrary", "arbitrary")),
        name="merge",
    )(h, ya, yb, yc, w_gate, w_gate, w_gate, b_gate, b_gate, b_gate, w_a, w_b, w_c)


def _out_proj_kernel(m_ref, w_ref, x_ref, g1_ref, n2_ref, sc_ref, sh_ref, *rest):
    x1_ref, h2_ref = rest[-2:]
    x1 = x_ref[...] + g1_ref[...] * jnp.dot(m_ref[...], w_ref[...], preferred_element_type=F32)
    x1_ref[...] = x1
    r = x1 * lax.rsqrt(jnp.mean(x1 * x1, axis=-1, keepdims=True) + EPS)
    h2_ref[...] = ((r * n2_ref[...]) * (1.0 + sc_ref[...]) + sh_ref[...]).astype(BF16)


def _out_proj(merged, w_out, x, x_row0, norm2_g, mod4, is_ctx, prev):
    tm = 512
    rows, row0 = _part(is_ctx)
    xoff, ooff = x_row0 // tm, row0 // tm
    mrow = _mod_row_fn(is_ctx, tm)
    orow = lambda i: (ooff + i, 0)
    mspec = lambda k: pl.BlockSpec((None, None, 1, D), lambda i, k=k: (mrow(i), k, 0, 0))
    prev = list(prev) if prev is not None else []
    return pl.pallas_call(
        _out_proj_kernel,
        grid=(rows // tm,),
        in_specs=[
            pl.BlockSpec((tm, D), orow),
            pl.BlockSpec((D, D), lambda i: (0, 0)),
            pl.BlockSpec((tm, D), lambda i: (xoff + i, 0)),
            mspec(2),
            pl.BlockSpec((1, D), lambda i: (0, 0)),
            mspec(4),
            mspec(3),
        ] + [ANY_SPEC] * len(prev),
        out_specs=[pl.BlockSpec((tm, D), orow), pl.BlockSpec((tm, D), orow)],
        out_shape=[jax.ShapeDtypeStruct((M, D), F32), jax.ShapeDtypeStruct((M, D), BF16)],
        input_output_aliases={7: 0, 8: 1} if prev else {},
        compiler_params=_cparams(("arbitrary",)),
        name="out_proj",
    )(merged, w_out, x, mod4, norm2_g.reshape(1, D), mod4, mod4, *prev)


def _segment_sum(x, seg):
    parts = []
    for s in range(ROUTE_TILE // seg):
        tot = jnp.sum(x[:, s * seg:(s + 1) * seg], axis=1, keepdims=True)
        parts.append(jnp.broadcast_to(tot, (x.shape[0], seg)))
    return parts[0] if len(parts) == 1 else jnp.concatenate(parts, axis=1)


def _segment_cumsum(x, seg):
    pos = lax.broadcasted_iota(jnp.int32, x.shape, 1) % seg
    sh = 1
    while sh < seg:
        x = x + jnp.where(pos >= sh, pltpu.roll(x, sh, 1), 0.0)
        sh *= 2
    return x


def _router_kernel(h_ref, wr_ref, slot_ref, gcol_ref, *, seg):
    cap = CAPACITY_FACTOR * seg // N_EXPERTS
    logits = lax.dot_general(wr_ref[...], h_ref[...], (((1,), (1,)), ((), ())),
                             preferred_element_type=F32)
    p = jnp.exp(logits - jnp.max(logits, axis=0, keepdims=True))
    aff = p / jnp.sum(p, axis=0, keepdims=True)
    bits = pltpu.bitcast(aff, jnp.int32)

    def bis(i, thr):
        cand = thr | jnp.left_shift(jnp.int32(1), 29 - i)
        cnt = _segment_sum(jnp.where(bits >= cand, 1.0, 0.0), seg)
        return jnp.where(cnt >= cap, cand, thr)

    thr = lax.fori_loop(0, 30, bis, jnp.zeros(bits.shape, jnp.int32))
    gt = jnp.where(bits > thr, 1.0, 0.0)
    eq = jnp.where(bits == thr, 1.0, 0.0)
    need = cap - _segment_sum(gt, seg)
    eq_rank = _segment_cumsum(eq, seg)
    sel = gt + eq * jnp.where(eq_rank <= need, 1.0, 0.0)
    rank = _segment_cumsum(sel, seg)
    seg_id = lax.broadcasted_iota(jnp.int32, bits.shape, 1) // seg
    slot = seg_id * cap + rank.astype(jnp.int32) - 1
    slot = jnp.where(sel > 0.0, slot, -1)
    slot_ref[...] = slot

    iota_c = lax.broadcasted_iota(jnp.int32, (ROUTE_SLOTS, ROUTE_TILE), 0)
    for e in range(N_EXPERTS):
        hit = iota_c == slot[e:e + 1, :]
        gcol_ref[e] = jnp.sum(jnp.where(hit, aff[e:e + 1, :], 0.0), axis=1, keepdims=True)


def _router(h2, w_router_t):
    def call(seg, tile0, ntiles):
        return pl.pallas_call(
            functools.partial(_router_kernel, seg=seg),
            grid=(ntiles,),
            in_specs=[
                pl.BlockSpec((ROUTE_TILE, D), lambda i: (tile0 + i, 0)),
                pl.BlockSpec((N_EXPERTS, D), lambda i: (0, 0)),
            ],
            out_specs=[
                pl.BlockSpec((None, N_EXPERTS, ROUTE_TILE), lambda i: (i, 0, 0)),
                pl.BlockSpec((N_EXPERTS, None, ROUTE_SLOTS, 1), lambda i: (0, i, 0, 0)),
            ],
            out_shape=[
                jax.ShapeDtypeStruct((ntiles, N_EXPERTS, ROUTE_TILE), jnp.int32),
                jax.ShapeDtypeStruct((N_EXPERTS, ntiles, ROUTE_SLOTS, 1), F32),
            ],
            compiler_params=_cparams(("arbitrary",)),
            name="router",
        )(h2, w_router_t)

    nct = M_CTX // ROUTE_TILE
    slot_c, g_c = call(SEQ, 0, nct)
    slot_l, g_l = call(DEC_SEQ, nct, N_ROUTE_TILES - nct)
    return jnp.concatenate([slot_c, slot_l], axis=0), jnp.concatenate([g_c, g_l], axis=1)


def _dispatch_kernel(h_ref, slot_ref, o_ref):
    e = pl.program_id(1)
    srow = slot_ref[pl.ds(e, 1), :]
    iota_c = lax.broadcasted_iota(jnp.int32, (ROUTE_SLOTS, ROUTE_TILE), 0)
    onehot = jnp.where(iota_c == srow, 1.0, 0.0).astype(BF16)
    o_ref[...] = jnp.dot(onehot, h_ref[...], preferred_element_type=F32).astype(BF16)


def _dispatch(h2, slot_et):
    return pl.pallas_call(
        _dispatch_kernel,
        grid=(N_ROUTE_TILES, N_EXPERTS),
        in_specs=[
            pl.BlockSpec((ROUTE_TILE, D), lambda i, e: (i, 0)),
            pl.BlockSpec((None, N_EXPERTS, ROUTE_TILE), lambda i, e: (i, 0, 0)),
        ],
        out_specs=pl.BlockSpec((None, ROUTE_SLOTS, D), lambda i, e: (e, i, 0)),
        out_shape=jax.ShapeDtypeStruct((N_EXPERTS, SLOTS_PER_EXPERT, D), BF16),
        compiler_params=_cparams(("arbitrary", "arbitrary")),
        name="moe_dispatch",
    )(h2, slot_et)


def _ffn_kernel(x_ref, w1_ref, w3_ref, w2_ref, g_ref, o_ref, acc_ref, *, tn):
    f = pl.program_id(2)
    nf = pl.num_programs(2)

    @pl.when(f == 0)
    def _():
        acc_ref[...] = jnp.zeros(acc_ref.shape, F32)

    x = x_ref[...]
    a = jnp.dot(x, w1_ref[...].astype(BF16), preferred_element_type=F32)
    b = jnp.dot(x, w3_ref[...].astype(BF16), preferred_element_type=F32)
    hmid = ((a * _sigmoid(a)) * b).astype(BF16)
    w2 = w2_ref[...].astype(BF16)
    for n in range(D // tn):
        acc_ref[:, n * tn:(n + 1) * tn] += jnp.dot(hmid, w2[:, n * tn:(n + 1) * tn],
                                                  preferred_element_type=F32)

    @pl.when(f == nf - 1)
    def _():
        o_ref[...] = (acc_ref[...] * g_ref[...]).astype(BF16)


def _ffn(xg, w1, w3, w2, gcol, layer):
    tr, tf, tn = 1024, 256, 512
    nr = SLOTS_PER_EXPERT // tr
    return pl.pallas_call(
        functools.partial(_ffn_kernel, tn=tn),
        grid=(N_EXPERTS, nr, EXPERT_FF // tf),
        in_specs=[
            pl.BlockSpec((None, tr, D), lambda e, r, f: (e, r, 0)),
            pl.BlockSpec((None, None, D, tf), lambda e, r, f: (layer, e, 0, f)),
            pl.BlockSpec((None, None, D, tf), lambda e, r, f: (layer, e, 0, f)),
            pl.BlockSpec((None, None, tf, D), lambda e, r, f: (layer, e, f, 0)),
            pl.BlockSpec((None, tr, 1), lambda e, r, f: (e, r, 0)),
        ],
        out_specs=pl.BlockSpec((None, tr, D), lambda e, r, f: (e, r, 0)),
        out_shape=jax.ShapeDtypeStruct((N_EXPERTS, SLOTS_PER_EXPERT, D), BF16),
        scratch_shapes=[pltpu.VMEM((tr, D), F32)],
        compiler_params=_cparams(("arbitrary", "arbitrary", "arbitrary")),
        name="moe_ffn",
    )(xg, w1, w3, w2, gcol)


def _combine_kernel(slot_ref, y_ref, x_ref, g2_ref, o_ref, *, tr):
    r = pl.program_id(2)
    slot = slot_ref[pl.ds(pl.multiple_of(r * tr, tr), tr), :]
    iota_c = lax.broadcasted_iota(jnp.int32, (tr, ROUTE_SLOTS), 1)
    onehot = jnp.concatenate(
        [jnp.where(iota_c == slot[:, e:e + 1], 1.0, 0.0).astype(BF16) for e in range(N_EXPERTS)], axis=1)
    y = y_ref[...].reshape(N_EXPERTS * ROUTE_SLOTS, y_ref.shape[-1])
    moe = jnp.dot(onehot, y, preferred_element_type=F32)
    o_ref[...] = x_ref[...] + g2_ref[...] * moe


def _combine(slot_te, yg, x1, mod4):
    tr, tn = 512, 1024
    nr = ROUTE_TILE // tr
    nct = M_CTX // ROUTE_TILE
    mrow = lambda i: jnp.where(i < nct, 0, 1 + i - nct)
    yg4 = yg.reshape(N_EXPERTS, N_ROUTE_TILES, ROUTE_SLOTS, D)
    return pl.pallas_call(
        functools.partial(_combine_kernel, tr=tr),
        grid=(N_ROUTE_TILES, D // tn, nr),
        in_specs=[
            pl.BlockSpec((None, ROUTE_TILE, N_EXPERTS), lambda i, j, r: (i, 0, 0)),
            pl.BlockSpec((N_EXPERTS, None, ROUTE_SLOTS, tn), lambda i, j, r: (0, i, 0, j)),
            pl.BlockSpec((tr, tn), lambda i, j, r: (i * nr + r, j)),
            pl.BlockSpec((None, None, 1, tn), lambda i, j, r: (mrow(i), 5, 0, j)),
        ],
        out_specs=pl.BlockSpec((tr, tn), lambda i, j, r: (i * nr + r, j)),
        out_shape=jax.ShapeDtypeStruct((M, D), F32),
        compiler_params=_cparams(("arbitrary", "arbitrary", "arbitrary")),
        name="moe_combine",
    )(slot_te, yg4, x1, mod4)


def _final_norm_kernel(x_ref, g_ref, o_ref):
    x = x_ref[...]
    o_ref[...] = (x * lax.rsqrt(jnp.mean(x * x, axis=-1, keepdims=True) + EPS)) * g_ref[...]


def _final_norm(x, g, is_ctx):
    tm = 512
    rows, row0 = _part(is_ctx)
    off = row0 // tm
    return pl.pallas_call(
        _final_norm_kernel,
        grid=(rows // tm,),
        in_specs=[pl.BlockSpec((tm, D), lambda i: (off + i, 0)), pl.BlockSpec((1, D), lambda i: (0, 0))],
        out_specs=pl.BlockSpec((tm, D), lambda i: (i, 0)),
        out_shape=jax.ShapeDtypeStruct((rows, D), F32),
        compiler_params=_cparams(("arbitrary",)),
        name="final_norm",
    )(x, g.reshape(1, D))


def _layer(x_parts, mod, lp, layer, lam_init, tables, cache_k4, cache_v4, st_re, st_im, new_kv):
    (xc, xc_row0), (xl, xl_row0) = x_parts
    rope_cos, rope_sin, dft_ctx, dft_lat, cs = tables
    mod4 = mod.reshape(N_MOD_ROWS, 6, 1, D)
    w_in = lp['w_in'].astype(BF16)
    h, u, ua_ctx, new_k, new_v = _norm_proj(xc, xc_row0, lp['norm1_g'], mod4, w_in, layer, True, new_kv)
    h, u, ua_lat = _norm_proj(xl, xl_row0, lp['norm1_g'], mod4, w_in, layer, False, (h, u))

    wb, wc, a_re, a_im = _s5_params(lp)
    ns = S5_GROUPS * S5_STATE
    ngc = BATCH // SUBLANES
    zeros = jnp.zeros((ngc, 2, SUBLANES, ns), F32)
    y_ctx, fin_re, fin_im = _s5(ua_ctx.reshape(SEQ, ngc, SUBLANES, S5_WIDTH), wb, wc, a_re, a_im, zeros, zeros)
    h0_re = st_re.reshape(DEC_BATCH, 2, ns).transpose(1, 0, 2)[None]
    h0_im = st_im.reshape(DEC_BATCH, 2, ns).transpose(1, 0, 2)[None]
    y_lat, _, _ = _s5(ua_lat.reshape(DEC_SEQ, 1, SUBLANES, S5_WIDTH), wb, wc, a_re, a_im, h0_re, h0_im)
    d_row = lp['s5_d'].reshape(1, S5_WIDTH)
    w_glu = lp['w_glu'].astype(BF16)
    b_glu = lp['b_glu'].reshape(1, S5_WIDTH)
    ya = _glu(y_ctx.reshape(2, ngc, SEQ, SUBLANES * S5_WIDTH), u, d_row, w_glu, b_glu, True, SEQ, None)
    ya = _glu(y_lat.reshape(2, 1, DEC_SEQ, SUBLANES * S5_WIDTH), u, d_row, w_glu, b_glu, False, 512, ya)

    yb = _fnet(u, dft_ctx, cs, True, SEQ, None)
    yb = _fnet(u, dft_lat, cs, False, 512, yb)

    lam = (jnp.exp(jnp.sum(lp['lam_q1'].astype(F32) * lp['lam_k1'].astype(F32)))
           - jnp.exp(jnp.sum(lp['lam_q2'].astype(F32) * lp['lam_k2'].astype(F32))) + lam_init).reshape(1)
    subln = lp['subln_g'].reshape(1, VALUE_DIM)
    yc = _attention(u, lam, subln, lam_init, True, SEQ, N_HEADS, None)
    yc = _attention(u, lam, subln, lam_init, False, 1024, 1, yc,
                    ctx=(cache_k4, cache_v4, layer, rope_cos, rope_sin))

    merged = _merge(h, ya, yb, yc, lp['w_gate'].astype(BF16), lp['b_gate'].reshape(1, 3 * D),
                    lp['w_branch_a'].astype(BF16), lp['w_branch_b'].astype(BF16),
                    lp['w_branch_c'].astype(BF16))
    w_out = lp['w_out'].astype(BF16)
    x1, h2 = _out_proj(merged, w_out, xc, xc_row0, lp['norm2_g'], mod4, True, None)
    x1, h2 = _out_proj(merged, w_out, xl, xl_row0, lp['norm2_g'], mod4, False, (x1, h2))

    slot_et, gcol = _router(h2, lp['w_router'].T.astype(BF16))
    xg = _dispatch(h2, slot_et)
    yg = _ffn(xg, lp['moe_w1_all'], lp['moe_w3_all'], lp['moe_w2_all'],
              gcol.reshape(N_EXPERTS, SLOTS_PER_EXPERT, 1), layer)
    x2 = _combine(slot_et.transpose(0, 2, 1), yg, x1, mod4)

    fin_shape = (ngc, 2, SUBLANES, S5_GROUPS, S5_STATE)
    s_re = fin_re.reshape(fin_shape).transpose(0, 2, 1, 3, 4).reshape(BATCH, 2, S5_GROUPS, S5_STATE)
    s_im = fin_im.reshape(fin_shape).transpose(0, 2, 1, 3, 4).reshape(BATCH, 2, S5_GROUPS, S5_STATE)
    return x2, (new_k, new_v), s_re, s_im


def kernel(x_prompt, x_sample, cache_k, cache_v, state_s5_re, state_s5_im, c, c_ctx, norm1_g, norm2_g, final_norm_g, w_ada, b_ada, w_in, s5_lam_re, s5_lam_im, s5_log_dt, s5_b_re, s5_b_im, s5_c_re, s5_c_im, s5_d, w_glu, b_glu, lam_q1, lam_k1, lam_q2, lam_k2, subln_g, w_branch_a, w_branch_b, w_branch_c, w_gate, b_gate, w_out, w_router, moe_w1, moe_w3, moe_w2):
    params = dict(norm1_g=norm1_g, norm2_g=norm2_g, w_in=w_in,
                  s5_lam_re=s5_lam_re, s5_lam_im=s5_lam_im, s5_log_dt=s5_log_dt,
                  s5_b_re=s5_b_re, s5_b_im=s5_b_im, s5_c_re=s5_c_re, s5_c_im=s5_c_im, s5_d=s5_d,
                  w_glu=w_glu, b_glu=b_glu, lam_q1=lam_q1, lam_k1=lam_k1, lam_q2=lam_q2, lam_k2=lam_k2,
                  subln_g=subln_g, w_branch_a=w_branch_a, w_branch_b=w_branch_b, w_branch_c=w_branch_c,
                  w_gate=w_gate, b_gate=b_gate, w_out=w_out, w_router=w_router)
    cc = jnp.zeros((N_MOD_ROWS, D), F32).at[0].set(c_ctx).at[1:1 + DEC_BATCH].set(c)
    mods = _ada(cc, w_ada, b_ada)
    rope_cos, rope_sin = _rope_tables(DEC_SEQ)
    c_c, s_c = _dft_tables(FNET_GROUP_CH)
    cs = jnp.concatenate([c_c, s_c], axis=1).astype(BF16)
    tables = (rope_cos, rope_sin, _fnet_tables(SEQ), _fnet_tables(DEC_SEQ), cs)
    cache_k4 = cache_k.reshape(DEC_BATCH, DEPTH, PAST_LEN, QK_WIDTH)
    cache_v4 = cache_v.reshape(DEC_BATCH, DEPTH, PAST_LEN, V_WIDTH)
    x_parts = ((x_prompt.reshape(M_CTX, D), 0), (x_sample.reshape(M_LAT, D), 0))
    new_kv = None
    s_res, s_ims = [], []
    for l in range(DEPTH):
        lp = {name: arr[l] for name, arr in params.items()}
        lp['moe_w1_all'], lp['moe_w3_all'], lp['moe_w2_all'] = moe_w1, moe_w3, moe_w2
        lam_init = 0.8 - 0.6 * math.exp(-0.3 * l)
        x, new_kv, s_re, s_im = _layer(x_parts, mods[l], lp, l, lam_init, tables, cache_k4, cache_v4,
                                       state_s5_re[:, l], state_s5_im[:, l], new_kv)
        x_parts = ((x, 0), (x, M_CTX))
        s_res.append(s_re)
        s_ims.append(s_im)
    y_prompt = _final_norm(x, final_norm_g, True).reshape(BATCH, SEQ, D)
    y_sample = _final_norm(x, final_norm_g, False).reshape(DEC_BATCH, DEC_SEQ, D)
    kv_shape = (BATCH, DEPTH, SEQ, N_HEADS, VALUE_DIM)
    return (y_prompt, y_sample, new_kv[0].reshape(kv_shape), new_kv[1].reshape(kv_shape),
            jnp.stack(s_res, axis=1), jnp.stack(s_ims, axis=1))
```

```python
import functools
import math

import jax
import jax.numpy as jnp
from jax import lax
from jax.experimental import pallas as pl
from jax.experimental.pallas import tpu as pltpu

F32 = jnp.float32
BF16 = jnp.bfloat16

D = 2048
BATCH, SEQ = 32, 256
DEC_BATCH, DEC_SEQ = 8, 2048
DEPTH = 2
PAST_LEN = 256
GRID_W = 64
EPS = 1e-6
S5_GROUP_CH, S5_GROUPS, S5_STATE = 16, 32, 64
S5_WIDTH = S5_GROUPS * S5_GROUP_CH
FNET_GROUPS, FNET_GROUP_CH = 4, 128
FNET_WIDTH = FNET_GROUPS * FNET_GROUP_CH
N_HEADS, HEAD_DIM = 8, 64
VALUE_DIM = 2 * HEAD_DIM
QK_WIDTH = N_HEADS * 2 * HEAD_DIM
V_WIDTH = N_HEADS * VALUE_DIM
IN_WIDTH = S5_WIDTH + FNET_WIDTH + 2 * QK_WIDTH + V_WIDTH
ROPE_BASE = 10000.0
ROPE_AXIS_DIM = HEAD_DIM // 2
N_EXPERTS = 16
EXPERT_FF = 2048
CAPACITY_FACTOR = 2

M_CTX = BATCH * SEQ
M_LAT = DEC_BATCH * DEC_SEQ
M = M_CTX + M_LAT
N_MOD_ROWS = 16

LANES = 128
SUBLANES = 8
VMEM_LIMIT = 56 * 1024 * 1024

ROUTE_TILE = 2048
ROUTE_SLOTS = CAPACITY_FACTOR * ROUTE_TILE // N_EXPERTS
N_ROUTE_TILES = M // ROUTE_TILE
SLOTS_PER_EXPERT = N_ROUTE_TILES * ROUTE_SLOTS

S5_BLK = 4
S5_BLK_STATES = 512

ANY_SPEC = pl.BlockSpec(memory_space=pl.ANY)


def _cparams(sem):
    return pltpu.CompilerParams(dimension_semantics=sem, vmem_limit_bytes=VMEM_LIMIT)


def _sigmoid(x):
    return 1.0 / (1.0 + jnp.exp(-x))


def _part(is_ctx):
    return (M_CTX, 0) if is_ctx else (M_LAT, M_CTX)


def _mod_row_fn(is_ctx, tm):
    if is_ctx:
        return lambda i: 0
    return lambda i: 1 + i // (DEC_SEQ // tm)


def _ada_kernel(c_ref, w_ref, b_ref, o_ref):
    c = c_ref[...]
    s = (c * _sigmoid(c)).astype(BF16)
    o_ref[...] = jnp.dot(s, w_ref[...].astype(BF16), preferred_element_type=F32) + b_ref[...]


def _ada(cc, w_ada, b_ada):
    tn = 1024
    n = 6 * D
    return pl.pallas_call(
        _ada_kernel,
        grid=(DEPTH, n // tn),
        in_specs=[
            pl.BlockSpec((N_MOD_ROWS, D), lambda l, j: (0, 0)),
            pl.BlockSpec((None, D, tn), lambda l, j: (l, 0, j)),
            pl.BlockSpec((None, 1, tn), lambda l, j: (l, 0, j)),
        ],
        out_specs=pl.BlockSpec((None, N_MOD_ROWS, tn), lambda l, j: (l, 0, j)),
        out_shape=jax.ShapeDtypeStruct((DEPTH, N_MOD_ROWS, n), F32),
        compiler_params=_cparams(("arbitrary", "arbitrary")),
        name="ada",
    )(cc, w_ada, b_ada.reshape(DEPTH, 1, n))


NP_TM_CTX, NP_TM_LAT, NP_TN = 512, 1024, 1024


def _norm_proj_kernel(x_ref, g_ref, sc_ref, sh_ref, w_ref, *rest, is_ctx, n_alias):
    rest = rest[n_alias:]
    if is_ctx:
        h_ref, u_ref, ua_ref, k_ref, v_ref = rest
    else:
        h_ref, u_ref, ua_ref = rest
    j = pl.program_id(1)

    @pl.when(j == 0)
    def _():
        x = x_ref[...]
        r = x * lax.rsqrt(jnp.mean(x * x, axis=-1, keepdims=True) + EPS)
        h = (r * g_ref[...]) * (1.0 + sc_ref[...]) + sh_ref[...]
        h_ref[...] = h.astype(BF16)

    u_ref[...] = jnp.dot(h_ref[...], w_ref[...], preferred_element_type=F32)

    @pl.when(j == 0)
    def _():
        if is_ctx:
            for b in range(NP_TM_CTX // SEQ):
                ua_ref[:, b * S5_WIDTH:(b + 1) * S5_WIDTH] = u_ref[b * SEQ:(b + 1) * SEQ, :S5_WIDTH]
        else:
            ua_ref[...] = u_ref[:, :S5_WIDTH]

    if is_ctx:
        @pl.when(j == 2)
        def _():
            for b in range(NP_TM_CTX // SEQ):
                k_ref[b] = u_ref[b * SEQ:(b + 1) * SEQ, :]

        @pl.when(j == 3)
        def _():
            for b in range(NP_TM_CTX // SEQ):
                v_ref[b] = u_ref[b * SEQ:(b + 1) * SEQ, :]


def _norm_proj(x, x_row0, g, mod4, w_in, layer, is_ctx, prev):
    tm, tn = (NP_TM_CTX if is_ctx else NP_TM_LAT), NP_TN
    assert IN_WIDTH // tn == 4 and QK_WIDTH == tn and V_WIDTH == tn
    rows, row0 = _part(is_ctx)
    xoff, ooff = x_row0 // tm, row0 // tm
    mrow = _mod_row_fn(is_ctx, tm)
    prev = list(prev) if prev is not None else []
    in_specs = [
        pl.BlockSpec((tm, D), lambda i, j: (xoff + i, 0)),
        pl.BlockSpec((1, D), lambda i, j: (0, 0)),
        pl.BlockSpec((None, None, 1, D), lambda i, j: (mrow(i), 1, 0, 0)),
        pl.BlockSpec((None, None, 1, D), lambda i, j: (mrow(i), 0, 0, 0)),
        pl.BlockSpec((D, tn), lambda i, j: (0, j)),
    ] + [ANY_SPEC] * len(prev)
    out_specs = [
        pl.BlockSpec((tm, D), lambda i, j: (ooff + i, 0)),
        pl.BlockSpec((tm, tn), lambda i, j: (ooff + i, j)),
    ]
    out_shape = [jax.ShapeDtypeStruct((M, D), BF16), jax.ShapeDtypeStruct((M, IN_WIDTH), F32)]
    if is_ctx:
        nbt = tm // SEQ
        out_specs += [
            pl.BlockSpec((SEQ, nbt * S5_WIDTH), lambda i, j: (0, i)),
            pl.BlockSpec((nbt, None, SEQ, QK_WIDTH), lambda i, j: (i, layer, 0, 0)),
            pl.BlockSpec((nbt, None, SEQ, V_WIDTH), lambda i, j: (i, layer, 0, 0)),
        ]
        out_shape += [
            jax.ShapeDtypeStruct((SEQ, BATCH * S5_WIDTH), F32),
            jax.ShapeDtypeStruct((BATCH, DEPTH, SEQ, QK_WIDTH), F32),
            jax.ShapeDtypeStruct((BATCH, DEPTH, SEQ, V_WIDTH), F32),
        ]
        aliases = {5: 3, 6: 4} if prev else {}
    else:
        tpb = DEC_SEQ // tm
        out_specs.append(pl.BlockSpec((tm, S5_WIDTH), lambda i, j: (i % tpb, i // tpb)))
        out_shape.append(jax.ShapeDtypeStruct((DEC_SEQ, DEC_BATCH * S5_WIDTH), F32))
        aliases = {5: 0, 6: 1}
    return pl.pallas_call(
        functools.partial(_norm_proj_kernel, is_ctx=is_ctx, n_alias=len(prev)),
        grid=(rows // tm, IN_WIDTH // tn),
        in_specs=in_specs,
        out_specs=out_specs,
        out_shape=out_shape,
        input_output_aliases=aliases,
        compiler_params=_cparams(("arbitrary", "arbitrary")),
        name="norm_proj",
    )(x, g.reshape(1, D), mod4, mod4, w_in, *prev)


S5_CHUNK = 64


def _s5_kernel(u_ref, pin_ref, pout_ref, wb_ref, wc_ref, are_ref, aim_ref, h0re_ref, h0im_ref,
               y_ref, finre_ref, finim_ref, xs0_ref, xs1_ref, xs2_ref, xs3_ref, stre_ref, stim_ref):
    c = pl.program_id(2)
    nc = pl.num_programs(2)
    bs = S5_BLK_STATES
    chunk = S5_CHUNK
    xs_refs = (xs0_ref, xs1_ref, xs2_ref, xs3_ref)

    @pl.when(c == 0)
    def _():
        stre_ref[...] = h0re_ref[...]
        stim_ref[...] = h0im_ref[...]

    ustack = jnp.concatenate(
        [u_ref[:, j * S5_WIDTH:(j + 1) * S5_WIDTH] for j in range(SUBLANES)], axis=0).astype(BF16)
    up = jnp.dot(pin_ref[...], ustack, preferred_element_type=F32).astype(BF16)

    for blk in range(S5_BLK):
        xs_refs[blk][...] = jnp.dot(up[:, blk * LANES:(blk + 1) * LANES], wb_ref[blk],
                                    preferred_element_type=F32)

    for blk in range(S5_BLK):
        xs_ref = xs_refs[blk]
        ar = are_ref[:, blk * bs:(blk + 1) * bs]
        ai = aim_ref[:, blk * bs:(blk + 1) * bs]
        sr = stre_ref[:, blk * bs:(blk + 1) * bs]
        si = stim_ref[:, blk * bs:(blk + 1) * bs]
        for i in range(chunk):
            r8 = slice(i * SUBLANES, (i + 1) * SUBLANES)
            nr = ar * sr - ai * si + xs_ref[r8, 0:bs]
            ni = ar * si + ai * sr + xs_ref[r8, bs:2 * bs]
            xs_ref[r8, 0:bs] = nr
            xs_ref[r8, bs:2 * bs] = ni
            sr, si = nr, ni
        stre_ref[:, blk * bs:(blk + 1) * bs] = sr
        stim_ref[:, blk * bs:(blk + 1) * bs] = si

    y = jnp.concatenate(
        [jnp.dot(xs_refs[blk][...].astype(BF16), wc_ref[blk], preferred_element_type=F32)
         for blk in range(S5_BLK)], axis=1)
    y_hi = y.astype(BF16)
    y_lo = (y - y_hi.astype(F32)).astype(BF16)
    ys = (jnp.dot(pout_ref[...], y_hi, preferred_element_type=F32)
          + jnp.dot(pout_ref[...], y_lo, preferred_element_type=F32))
    for j in range(SUBLANES):
        y_ref[:, j * S5_WIDTH:(j + 1) * S5_WIDTH] = ys[j * chunk:(j + 1) * chunk, :]

    @pl.when(c == nc - 1)
    def _():
        finre_ref[...] = stre_ref[...]
        finim_ref[...] = stim_ref[...]


def _s5_permutations():
    chunk = S5_CHUNK
    rows = chunk * SUBLANES
    r_out = lax.broadcasted_iota(jnp.int32, (rows, rows), 0)
    r_in = lax.broadcasted_iota(jnp.int32, (rows, rows), 1)
    step, seq = r_out // SUBLANES, r_out % SUBLANES
    fwd = r_in == seq * chunk + step
    bwd = r_in == seq * chunk + (chunk - 1 - step)
    p = jnp.stack([fwd, bwd]).astype(BF16)
    return p, p.transpose(0, 2, 1)


def _s5(u2d, perms, wb, wc, a_re, a_im, h0_re, h0_im):
    t_len = u2d.shape[0]
    ng = u2d.shape[1] // (SUBLANES * S5_WIDTH)
    chunk = S5_CHUNK
    nc = t_len // chunk
    rows = chunk * SUBLANES
    ns = S5_BLK * S5_BLK_STATES
    pin, pout = perms

    def ceff(d, c):
        return jnp.where(d == 0, c, nc - 1 - c)

    return pl.pallas_call(
        _s5_kernel,
        grid=(ng, 2, nc),
        in_specs=[
            pl.BlockSpec((chunk, SUBLANES * S5_WIDTH), lambda g, d, c: (ceff(d, c), g)),
            pl.BlockSpec((None, rows, rows), lambda g, d, c: (d, 0, 0)),
            pl.BlockSpec((None, rows, rows), lambda g, d, c: (d, 0, 0)),
            pl.BlockSpec((None, S5_BLK, LANES, 2 * S5_BLK_STATES), lambda g, d, c: (d, 0, 0, 0)),
            pl.BlockSpec((None, S5_BLK, 2 * S5_BLK_STATES, LANES), lambda g, d, c: (d, 0, 0, 0)),
            pl.BlockSpec((None, SUBLANES, ns), lambda g, d, c: (d, 0, 0)),
            pl.BlockSpec((None, SUBLANES, ns), lambda g, d, c: (d, 0, 0)),
            pl.BlockSpec((None, None, SUBLANES, ns), lambda g, d, c: (g, d, 0, 0)),
            pl.BlockSpec((None, None, SUBLANES, ns), lambda g, d, c: (g, d, 0, 0)),
        ],
        out_specs=[
            pl.BlockSpec((None, chunk, SUBLANES * S5_WIDTH), lambda g, d, c: (d, ceff(d, c), g)),
            pl.BlockSpec((None, None, SUBLANES, ns), lambda g, d, c: (g, d, 0, 0)),
            pl.BlockSpec((None, None, SUBLANES, ns), lambda g, d, c: (g, d, 0, 0)),
        ],
        out_shape=[
            jax.ShapeDtypeStruct((2,) + u2d.shape, F32),
            jax.ShapeDtypeStruct((ng, 2, SUBLANES, ns), F32),
            jax.ShapeDtypeStruct((ng, 2, SUBLANES, ns), F32),
        ],
        scratch_shapes=[pltpu.VMEM((rows, 2 * S5_BLK_STATES), F32)] * S5_BLK + [
            pltpu.VMEM((SUBLANES, ns), F32),
            pltpu.VMEM((SUBLANES, ns), F32),
        ],
        compiler_params=_cparams(("arbitrary", "arbitrary", "arbitrary")),
        name="s5_scan",
    )(u2d, pin, pout, wb, wc, a_re, a_im, h0_re, h0_im)


def _s5_params(lp):
    lam_re = lp['s5_lam_re'].astype(F32)
    lam_im = lp['s5_lam_im'].astype(F32)
    dt = jnp.exp(lp['s5_log_dt'].astype(F32))[..., None]
    b_re = lp['s5_b_re'].astype(F32)
    b_im = lp['s5_b_im'].astype(F32)
    c_re = lp['s5_c_re'].astype(F32)
    c_im = lp['s5_c_im'].astype(F32)
    mag = jnp.exp(lam_re * dt)
    ab_re = mag * jnp.cos(lam_im * dt)
    ab_im = mag * jnp.sin(lam_im * dt)
    den = lam_re * lam_re + lam_im * lam_im
    f_re = ((ab_re - 1.0) * lam_re + ab_im * lam_im) / den
    f_im = (ab_im * lam_re - (ab_re - 1.0) * lam_im) / den
    bb_re = f_re[..., None] * b_re - f_im[..., None] * b_im
    bb_im = f_re[..., None] * b_im + f_im[..., None] * b_re
    gpb = S5_GROUPS // S5_BLK
    eye = jnp.eye(gpb, dtype=F32)

    def pack_b(bb):
        bb = bb.reshape(2, S5_BLK, gpb, S5_STATE, S5_GROUP_CH)
        w = jnp.einsum('dbgph,gk->dbghkp', bb, eye)
        return w.reshape(2, S5_BLK, gpb * S5_GROUP_CH, gpb * S5_STATE)

    def pack_c(cc):
        cc = cc.reshape(2, S5_BLK, gpb, S5_GROUP_CH, S5_STATE)
        w = jnp.einsum('dbghp,gk->dbkpgh', cc, eye)
        return w.reshape(2, S5_BLK, gpb * S5_STATE, gpb * S5_GROUP_CH)

    wb = jnp.concatenate([pack_b(bb_re), pack_b(bb_im)], axis=-1).astype(BF16)
    wc = jnp.concatenate([pack_c(c_re), -pack_c(c_im)], axis=-2).astype(BF16)
    ns = S5_GROUPS * S5_STATE
    a_re = jnp.broadcast_to(ab_re.reshape(2, 1, ns), (2, SUBLANES, ns))
    a_im = jnp.broadcast_to(ab_im.reshape(2, 1, ns), (2, SUBLANES, ns))
    return wb, wc, a_re, a_im


def _gelu_tanh(x):
    return 0.5 * x * (1.0 + jnp.tanh(math.sqrt(2.0 / math.pi) * (x + 0.044715 * (x * x * x))))


def _glu_kernel(y_ref, u_ref, d_ref, w_ref, b_ref, *rest):
    o_ref = rest[-1]
    y = y_ref[0] + y_ref[1] + d_ref[...] * u_ref[...]
    y = _gelu_tanh(y)
    z = jnp.dot(y.astype(BF16), w_ref[...], preferred_element_type=F32) + b_ref[...]
    o_ref[...] = (y * _sigmoid(z)).astype(BF16)


def _glu(y_dirs, u, d_row, w_glu, b_glu, is_ctx, tt, prev):
    t_len = y_dirs.shape[1]
    nb = y_dirs.shape[2] // S5_WIDTH
    ntt = t_len // tt
    off = _part(is_ctx)[1] // tt
    prev = [prev] if prev is not None else []
    return pl.pallas_call(
        _glu_kernel,
        grid=(nb, ntt),
        in_specs=[
            pl.BlockSpec((2, tt, S5_WIDTH), lambda b, t: (0, t, b)),
            pl.BlockSpec((tt, S5_WIDTH), lambda b, t: (off + b * ntt + t, 0)),
            pl.BlockSpec((1, S5_WIDTH), lambda b, t: (0, 0)),
            pl.BlockSpec((S5_WIDTH, S5_WIDTH), lambda b, t: (0, 0)),
            pl.BlockSpec((1, S5_WIDTH), lambda b, t: (0, 0)),
        ] + [ANY_SPEC] * len(prev),
        out_specs=pl.BlockSpec((tt, S5_WIDTH), lambda b, t: (off + b * ntt + t, 0)),
        out_shape=jax.ShapeDtypeStruct((M, S5_WIDTH), BF16),
        input_output_aliases={5: 0} if prev else {},
        compiler_params=_cparams(("arbitrary", "arbitrary")),
        name="s5_glu",
    )(y_dirs, u, d_row, w_glu, b_glu, *prev)


def _fnet_kernel(x_ref, cs_ref, dft_ref, *rest, t_len):
    o_ref, z_ref = rest[-2:]

    @pl.when(pl.program_id(1) == 0)
    def _():
        x = x_ref[...].astype(BF16)
        for g in range(FNET_GROUPS):
            z = jnp.dot(x[:, g * LANES:(g + 1) * LANES], cs_ref[...], preferred_element_type=F32)
            z_ref[0:t_len, g * LANES:(g + 1) * LANES] = z[:, :LANES].astype(BF16)
            z_ref[t_len:2 * t_len, g * LANES:(g + 1) * LANES] = z[:, LANES:].astype(BF16)

    scale = 1.0 / math.sqrt(t_len * FNET_GROUP_CH)
    o_ref[...] = (jnp.dot(dft_ref[...], z_ref[...], preferred_element_type=F32) * scale).astype(BF16)


def _dft_tables(n):
    j = lax.broadcasted_iota(jnp.int32, (n, n), 0)
    k = lax.broadcasted_iota(jnp.int32, (n, n), 1)
    ang = ((j * k) % n).astype(F32) * (2.0 * math.pi / n)
    return jnp.cos(ang), jnp.sin(ang)


def _fnet_tables(t_len):
    c_t, s_t = _dft_tables(t_len)
    return jnp.concatenate([c_t, -s_t], axis=1).astype(BF16)


def _fnet(u, dft, cs, is_ctx, tr, prev):
    t_len = dft.shape[0]
    rows, row0 = _part(is_ctx)
    nb = rows // t_len
    ntr = t_len // tr
    off = row0 // t_len
    offr = row0 // tr
    col = S5_WIDTH // FNET_WIDTH
    prev = [prev] if prev is not None else []
    return pl.pallas_call(
        functools.partial(_fnet_kernel, t_len=t_len),
        grid=(nb, ntr),
        in_specs=[
            pl.BlockSpec((t_len, FNET_WIDTH), lambda b, r: (off + b, col)),
            pl.BlockSpec((FNET_GROUP_CH, 2 * FNET_GROUP_CH), lambda b, r: (0, 0)),
            pl.BlockSpec((tr, 2 * t_len), lambda b, r: (r, 0)),
        ] + [ANY_SPEC] * len(prev),
        out_specs=pl.BlockSpec((tr, FNET_WIDTH), lambda b, r: (offr + b * ntr + r, 0)),
        out_shape=jax.ShapeDtypeStruct((M, FNET_WIDTH), BF16),
        scratch_shapes=[pltpu.VMEM((2 * t_len, FNET_WIDTH), BF16)],
        input_output_aliases={3: 0} if prev else {},
        compiler_params=_cparams(("arbitrary", "arbitrary")),
        name="fnet",
    )(u, cs, dft, *prev)


def _rope(x, cos, sin):
    half = ROPE_AXIS_DIM // 2
    lane = lax.broadcasted_iota(jnp.int32, x.shape, 1)
    up = pltpu.roll(x, LANES - half, 1)
    dn = pltpu.roll(x, half, 1)
    rot = jnp.where((lane % ROPE_AXIS_DIM) < half, -up, dn)
    return x * cos + rot * sin


ATTN_KEY_CHUNK = 512
ATTN_ROW_BLOCK = 256


def _attn_kernel(lam_ref, q_ref, k_ref, v_ref, *rest, tq, t_len, hb, has_ctx, lam_init):
    g_ref = rest[-6]
    o_ref, kk_ref, vv_ref, qs_ref, s_ref = rest[-5:]
    if has_ctx:
        ck_ref, cv_ref, cos_ref, sin_ref = rest[:4]
    qi = pl.program_id(2)
    n_keys = kk_ref.shape[1]
    kc, rb = ATTN_KEY_CHUNK, ATTN_ROW_BLOCK
    chunks = [(c0, min(c0 + kc, n_keys)) for c0 in range(0, n_keys, kc)]

    @pl.when(qi == 0)
    def _():
        for h in range(hb):
            hs = slice(h * LANES, (h + 1) * LANES)
            k = k_ref[:, hs]
            if has_ctx:
                k = _rope(k, cos_ref[...], sin_ref[...])
                kk_ref[h, t_len:, :] = ck_ref[:, hs].astype(BF16)
                vv_ref[h, t_len:, 0:VALUE_DIM] = cv_ref[:, hs].astype(BF16)
            kk_ref[h, 0:t_len, :] = k.astype(BF16)
            vv_ref[h, 0:t_len, 0:VALUE_DIM] = v_ref[:, hs].astype(BF16)
            vv_ref[h, :, VALUE_DIM:] = jnp.ones((n_keys, VALUE_DIM), BF16)

    lane = lax.broadcasted_iota(jnp.int32, (tq, LANES), 1)
    for h in range(hb):
        hs = slice(h * LANES, (h + 1) * LANES)
        q = q_ref[:, hs]
        if has_ctx:
            r0 = pl.multiple_of(qi * tq, tq)
            q = _rope(q, cos_ref[pl.ds(r0, tq), :], sin_ref[pl.ds(r0, tq), :])
        q = q * (HEAD_DIM ** -0.5)
        qs_ref[0, :, hs] = jnp.where(lane < HEAD_DIM, q, 0.0).astype(BF16)
        qs_ref[1, :, hs] = jnp.where(lane >= HEAD_DIM, q, 0.0).astype(BF16)

    lam = lam_ref[0]
    gain = g_ref[...]
    dn = (((1,), (1,)), ((), ()))

    unit = 0
    for r in range(tq // rb):
        rows = slice(r * rb, (r + 1) * rb)
        for h in range(hb):
            hs = slice(h * LANES, (h + 1) * LANES)
            o = None
            for n in range(2):
                slot = unit % 2
                unit += 1
                qn = qs_ref[n, rows, hs]
                mrun = None
                for c0, c1 in chunks:
                    s_c = lax.dot_general(qn, kk_ref[h, c0:c1, :], dn, preferred_element_type=F32)
                    s_ref[slot, :, c0:c1] = s_c
                    for l0 in range(0, c1 - c0, LANES):
                        mc = s_c[:, l0:l0 + LANES]
                        mrun = mc if mrun is None else jnp.maximum(mrun, mc)
                m_b = jnp.broadcast_to(jnp.max(mrun, axis=-1, keepdims=True), (rb, LANES))
                oa = None
                for c0, c1 in chunks:
                    p = jnp.concatenate(
                        [jnp.exp(s_ref[slot, :, l0:l0 + LANES] - m_b) for l0 in range(c0, c1, LANES)],
                        axis=1).astype(BF16)
                    part = jnp.dot(p, vv_ref[h, c0:c1, :], preferred_element_type=F32)
                    oa = part if oa is None else oa + part
                on = oa[:, :VALUE_DIM] * (1.0 / oa[:, VALUE_DIM:VALUE_DIM + 1])
                o = on if n == 0 else o - lam * on
            o = o * lax.rsqrt(jnp.mean(o * o, axis=-1, keepdims=True) + EPS)
            o_ref[rows, hs] = ((o * gain) * (1.0 - lam_init)).astype(BF16)


def _attention(u, lam, subln_g, lam_init, is_ctx, tq, hb, prev, ctx=None):
    rows, row0 = _part(is_ctx)
    t_len = SEQ if is_ctx else DEC_SEQ
    nb = rows // t_len
    nq = t_len // tq
    off = row0 // t_len
    offq = row0 // tq
    width = hb * LANES
    qcol = (S5_WIDTH + FNET_WIDTH) // width
    kcol = qcol + QK_WIDTH // width
    vcol = kcol + QK_WIDTH // width
    has_ctx = ctx is not None
    n_keys = t_len + (PAST_LEN if has_ctx else 0)
    assert tq % ATTN_ROW_BLOCK == 0 and n_keys % (2 * LANES) == 0
    in_specs = [
        pl.BlockSpec(memory_space=pltpu.SMEM),
        pl.BlockSpec((tq, width), lambda b, h, i: (offq + b * nq + i, qcol + h)),
        pl.BlockSpec((t_len, width), lambda b, h, i: (off + b, kcol + h)),
        pl.BlockSpec((t_len, width), lambda b, h, i: (off + b, vcol + h)),
    ]
    args = [lam, u, u, u]
    if has_ctx:
        ck, cv, layer, cos, sin = ctx
        in_specs += [
            pl.BlockSpec((None, None, PAST_LEN, width), lambda b, h, i: (b, layer, 0, h)),
            pl.BlockSpec((None, None, PAST_LEN, width), lambda b, h, i: (b, layer, 0, h)),
            pl.BlockSpec((t_len, LANES), lambda b, h, i: (0, 0)),
            pl.BlockSpec((t_len, LANES), lambda b, h, i: (0, 0)),
        ]
        args += [ck, cv, cos, sin]
    in_specs.append(pl.BlockSpec((1, LANES), lambda b, h, i: (0, 0)))
    args.append(subln_g)
    aliases = {}
    if prev is not None:
        aliases = {len(args): 0}
        in_specs.append(ANY_SPEC)
        args.append(prev)
    kern = functools.partial(_attn_kernel, tq=tq, t_len=t_len, hb=hb, has_ctx=has_ctx, lam_init=lam_init)
    if prev is not None:
        kern = functools.partial(_skip_ref, kern, len(args) - 1)
    return pl.pallas_call(
        kern,
        grid=(nb, N_HEADS // hb, nq),
        in_specs=in_specs,
        out_specs=pl.BlockSpec((tq, width), lambda b, h, i: (offq + b * nq + i, h)),
        out_shape=jax.ShapeDtypeStruct((M, V_WIDTH), BF16),
        scratch_shapes=[
            pltpu.VMEM((hb, n_keys, LANES), BF16),
            pltpu.VMEM((hb, n_keys, 2 * VALUE_DIM), BF16),
            pltpu.VMEM((2, tq, width), BF16),
            pltpu.VMEM((2, ATTN_ROW_BLOCK, n_keys), F32),
        ],
        input_output_aliases=aliases,
        compiler_params=_cparams(("arbitrary", "arbitrary", "arbitrary")),
        name="diff_attn",
    )(*args)


def _skip_ref(kern, idx, *refs):
    return kern(*refs[:idx], *refs[idx + 1:])


def _rope_tables(t_len):
    rows = t_len // GRID_W
    pos_row = jnp.broadcast_to(jnp.arange(rows, dtype=F32)[:, None], (rows, GRID_W)).reshape(-1)
    pos_col = jnp.broadcast_to(jnp.arange(GRID_W, dtype=F32)[None, :], (rows, GRID_W)).reshape(-1)
    inv = ROPE_BASE ** (-jnp.arange(0, ROPE_AXIS_DIM, 2, dtype=F32) / ROPE_AXIS_DIM)
    ang_r = pos_row[:, None] * inv
    ang_c = pos_col[:, None] * inv
    cos = jnp.concatenate([jnp.cos(ang_r), jnp.cos(ang_r), jnp.cos(ang_c), jnp.cos(ang_c)], axis=-1)
    sin = jnp.concatenate([jnp.sin(ang_r), jnp.sin(ang_r), jnp.sin(ang_c), jnp.sin(ang_c)], axis=-1)
    return jnp.concatenate([cos, cos], axis=-1), jnp.concatenate([sin, sin], axis=-1)


def _merge_kernel(h_ref, ya_ref, yb_ref, yc_ref, wga_ref, wgb_ref, wgc_ref, bga_ref, bgb_ref, bgc_ref,
                  wa_ref, wb_ref, wc_ref, o_ref):
    h = h_ref[...]
    acc = None
    for wg, bg, y, w in ((wga_ref, bga_ref, ya_ref, wa_ref), (wgb_ref, bgb_ref, yb_ref, wb_ref),
                         (wgc_ref, bgc_ref, yc_ref, wc_ref)):
        gate = _sigmoid(jnp.dot(h, wg[...], preferred_element_type=F32) + bg[...])
        term = gate * jnp.dot(y[...], w[...], preferred_element_type=F32)
        acc = term if acc is None else acc + term
    o_ref[...] = acc.astype(BF16)


def _merge(h, ya, yb, yc, w_gate, b_gate, w_a, w_b, w_c):
    tm, tn = 1024, 512
    nj = D // tn
    row = lambda i, j: (i, 0)
    wspecs = [pl.BlockSpec((D, tn), lambda i, j, k=k: (0, k * nj + j)) for k in range(3)]
    bspecs = [pl.BlockSpec((1, tn), lambda i, j, k=k: (0, k * nj + j)) for k in range(3)]
    return pl.pallas_call(
        _merge_kernel,
        grid=(M // tm, nj),
        in_specs=[
            pl.BlockSpec((tm, D), row),
            pl.BlockSpec((tm, S5_WIDTH), row),
            pl.BlockSpec((tm, FNET_WIDTH), row),
            pl.BlockSpec((tm, V_WIDTH), row),
            *wspecs, *bspecs,
            pl.BlockSpec((S5_WIDTH, tn), lambda i, j: (0, j)),
            pl.BlockSpec((FNET_WIDTH, tn), lambda i, j: (0, j)),
            pl.BlockSpec((V_WIDTH, tn), lambda i, j: (0, j)),
        ],
        out_specs=pl.BlockSpec((tm, tn), lambda i, j: (i, j)),
        out_shape=jax.ShapeDtypeStruct((M, D), BF16),
        compiler_params=_cparams(("arbitrary", "arbitrary")),
        name="merge",
    )(h, ya, yb, yc, w_gate, w_gate, w_gate, b_gate, b_gate, b_gate, w_a, w_b, w_c)


def _out_proj_kernel(m_ref, w_ref, x_ref, g1_ref, n2_ref, sc_ref, sh_ref, *rest):
    x1_ref, h2_ref = rest[-2:]
    x1 = x_ref[...] + g1_ref[...] * jnp.dot(m_ref[...], w_ref[...], preferred_element_type=F32)
    x1_ref[...] = x1
    r = x1 * lax.rsqrt(jnp.mean(x1 * x1, axis=-1, keepdims=True) + EPS)
    h2_ref[...] = ((r * n2_ref[...]) * (1.0 + sc_ref[...]) + sh_ref[...]).astype(BF16)


def _out_proj(merged, w_out, x, x_row0, norm2_g, mod4, is_ctx, prev):
    tm = 512
    rows, row0 = _part(is_ctx)
    xoff, ooff = x_row0 // tm, row0 // tm
    mrow = _mod_row_fn(is_ctx, tm)
    orow = lambda i: (ooff + i, 0)
    mspec = lambda k: pl.BlockSpec((None, None, 1, D), lambda i, k=k: (mrow(i), k, 0, 0))
    prev = list(prev) if prev is not None else []
    return pl.pallas_call(
        _out_proj_kernel,
        grid=(rows // tm,),
        in_specs=[
            pl.BlockSpec((tm, D), orow),
            pl.BlockSpec((D, D), lambda i: (0, 0)),
            pl.BlockSpec((tm, D), lambda i: (xoff + i, 0)),
            mspec(2),
            pl.BlockSpec((1, D), lambda i: (0, 0)),
            mspec(4),
            mspec(3),
        ] + [ANY_SPEC] * len(prev),
        out_specs=[pl.BlockSpec((tm, D), orow), pl.BlockSpec((tm, D), orow)],
        out_shape=[jax.ShapeDtypeStruct((M, D), F32), jax.ShapeDtypeStruct((M, D), BF16)],
        input_output_aliases={7: 0, 8: 1} if prev else {},
        compiler_params=_cparams(("arbitrary",)),
        name="out_proj",
    )(merged, w_out, x, mod4, norm2_g.reshape(1, D), mod4, mod4, *prev)


def _segment_sum(x, seg):
    parts = []
    for s in range(ROUTE_TILE // seg):
        tot = jnp.sum(x[:, s * seg:(s + 1) * seg], axis=1, keepdims=True)
        parts.append(jnp.broadcast_to(tot, (x.shape[0], seg)))
    return parts[0] if len(parts) == 1 else jnp.concatenate(parts, axis=1)


def _segment_cumsum(x, seg):
    pos = lax.broadcasted_iota(jnp.int32, x.shape, 1) % seg
    sh = 1
    while sh < seg:
        x = x + jnp.where(pos >= sh, pltpu.roll(x, sh, 1), 0.0)
        sh *= 2
    return x


def _router_kernel(h_ref, wr_ref, slot_ref, gcol_ref, *, seg):
    cap = CAPACITY_FACTOR * seg // N_EXPERTS
    logits = lax.dot_general(wr_ref[...], h_ref[...], (((1,), (1,)), ((), ())),
                             preferred_element_type=F32)
    p = jnp.exp(logits - jnp.max(logits, axis=0, keepdims=True))
    aff = p / jnp.sum(p, axis=0, keepdims=True)
    bits = pltpu.bitcast(aff, jnp.int32)

    def bis(i, thr):
        cand = thr | jnp.left_shift(jnp.int32(1), 29 - i)
        cnt = _segment_sum(jnp.where(bits >= cand, 1.0, 0.0), seg)
        return jnp.where(cnt >= cap, cand, thr)

    thr = lax.fori_loop(0, 30, bis, jnp.zeros(bits.shape, jnp.int32))
    gt = jnp.where(bits > thr, 1.0, 0.0)
    eq = jnp.where(bits == thr, 1.0, 0.0)
    need = cap - _segment_sum(gt, seg)
    eq_rank = _segment_cumsum(eq, seg)
    sel = gt + eq * jnp.where(eq_rank <= need, 1.0, 0.0)
    rank = _segment_cumsum(sel, seg)
    seg_id = lax.broadcasted_iota(jnp.int32, bits.shape, 1) // seg
    slot = seg_id * cap + rank.astype(jnp.int32) - 1
    slot = jnp.where(sel > 0.0, slot, -1)
    slot_ref[...] = slot

    iota_c = lax.broadcasted_iota(jnp.int32, (ROUTE_SLOTS, ROUTE_TILE), 0)
    for e in range(N_EXPERTS):
        hit = iota_c == slot[e:e + 1, :]
        gcol_ref[e] = jnp.sum(jnp.where(hit, aff[e:e + 1, :], 0.0), axis=1, keepdims=True)


def _router(h2, w_router_t):
    def call(seg, tile0, ntiles):
        return pl.pallas_call(
            functools.partial(_router_kernel, seg=seg),
            grid=(ntiles,),
            in_specs=[
                pl.BlockSpec((ROUTE_TILE, D), lambda i: (tile0 + i, 0)),
                pl.BlockSpec((N_EXPERTS, D), lambda i: (0, 0)),
            ],
            out_specs=[
                pl.BlockSpec((None, N_EXPERTS, ROUTE_TILE), lambda i: (i, 0, 0)),
                pl.BlockSpec((N_EXPERTS, None, ROUTE_SLOTS, 1), lambda i: (0, i, 0, 0)),
            ],
            out_shape=[
                jax.ShapeDtypeStruct((ntiles, N_EXPERTS, ROUTE_TILE), jnp.int32),
                jax.ShapeDtypeStruct((N_EXPERTS, ntiles, ROUTE_SLOTS, 1), F32),
            ],
            compiler_params=_cparams(("arbitrary",)),
            name="router",
        )(h2, w_router_t)

    nct = M_CTX // ROUTE_TILE
    slot_c, g_c = call(SEQ, 0, nct)
    slot_l, g_l = call(DEC_SEQ, nct, N_ROUTE_TILES - nct)
    return jnp.concatenate([slot_c, slot_l], axis=0), jnp.concatenate([g_c, g_l], axis=1)


CTX_CAP = CAPACITY_FACTOR * SEQ // N_EXPERTS
CTX_SETS_PER_TILE = ROUTE_TILE // SEQ
N_CTX_TILES = M_CTX // ROUTE_TILE


def _dispatch_ctx_kernel(h_ref, slot_ref, o_ref):
    base = (pl.program_id(0) % CTX_SETS_PER_TILE) * CTX_CAP
    slot = slot_ref[...]
    iota_c = lax.broadcasted_iota(jnp.int32, (CTX_CAP, SEQ), 0) + base
    onehot = jnp.concatenate(
        [jnp.where(iota_c == slot[e:e + 1, :], 1.0, 0.0).astype(BF16) for e in range(N_EXPERTS)], axis=0)
    res = jnp.dot(onehot, h_ref[...], preferred_element_type=F32).astype(BF16)
    for e in range(N_EXPERTS):
        o_ref[e] = res[e * CTX_CAP:(e + 1) * CTX_CAP, :]


def _dispatch_lat_kernel(h_ref, slot_ref, prev_ref, o_ref):
    e = pl.program_id(1)
    srow = slot_ref[pl.ds(e, 1), :]
    iota_c = lax.broadcasted_iota(jnp.int32, (ROUTE_SLOTS, ROUTE_TILE), 0)
    onehot = jnp.where(iota_c == srow, 1.0, 0.0).astype(BF16)
    o_ref[...] = jnp.dot(onehot, h_ref[...], preferred_element_type=F32).astype(BF16)


def _dispatch(h2, slot_et):
    xg = pl.pallas_call(
        _dispatch_ctx_kernel,
        grid=(BATCH,),
        in_specs=[
            pl.BlockSpec((SEQ, D), lambda s: (s, 0)),
            pl.BlockSpec((None, N_EXPERTS, SEQ), lambda s: (s // CTX_SETS_PER_TILE, 0, s % CTX_SETS_PER_TILE)),
        ],
        out_specs=pl.BlockSpec((N_EXPERTS, None, CTX_CAP, D), lambda s: (0, s, 0, 0)),
        out_shape=jax.ShapeDtypeStruct((N_EXPERTS, SLOTS_PER_EXPERT // CTX_CAP, CTX_CAP, D), BF16),
        compiler_params=_cparams(("arbitrary",)),
        name="moe_dispatch_ctx",
    )(h2, slot_et)
    return pl.pallas_call(
        _dispatch_lat_kernel,
        grid=(N_ROUTE_TILES - N_CTX_TILES, N_EXPERTS),
        in_specs=[
            pl.BlockSpec((ROUTE_TILE, D), lambda i, e: (N_CTX_TILES + i, 0)),
            pl.BlockSpec((None, N_EXPERTS, ROUTE_TILE), lambda i, e: (N_CTX_TILES + i, 0, 0)),
            ANY_SPEC,
        ],
        out_specs=pl.BlockSpec((None, ROUTE_SLOTS, D), lambda i, e: (e, N_CTX_TILES + i, 0)),
        out_shape=jax.ShapeDtypeStruct((N_EXPERTS, SLOTS_PER_EXPERT, D), BF16),
        input_output_aliases={2: 0},
        compiler_params=_cparams(("arbitrary", "arbitrary")),
        name="moe_dispatch",
    )(h2, slot_et, xg.reshape(N_EXPERTS, SLOTS_PER_EXPERT, D))


def _ffn_kernel(x_ref, w1_ref, w3_ref, w2_ref, g_ref, o_ref, acc_ref, *, tn):
    f = pl.program_id(2)
    nf = pl.num_programs(2)

    @pl.when(f == 0)
    def _():
        acc_ref[...] = jnp.zeros(acc_ref.shape, F32)

    x = x_ref[...]
    a = jnp.dot(x, w1_ref[...].astype(BF16), preferred_element_type=F32)
    b = jnp.dot(x, w3_ref[...].astype(BF16), preferred_element_type=F32)
    hmid = ((a * _sigmoid(a)) * b).astype(BF16)
    w2 = w2_ref[...].astype(BF16)
    for n in range(D // tn):
        acc_ref[:, n * tn:(n + 1) * tn] += jnp.dot(hmid, w2[:, n * tn:(n + 1) * tn],
                                                  preferred_element_type=F32)

    @pl.when(f == nf - 1)
    def _():
        o_ref[...] = (acc_ref[...] * g_ref[...]).astype(BF16)


def _ffn(xg, w1, w3, w2, gcol, layer):
    tr, tf, tn = 1024, 256, 512
    nr = SLOTS_PER_EXPERT // tr
    return pl.pallas_call(
        functools.partial(_ffn_kernel, tn=tn),
        grid=(N_EXPERTS, nr, EXPERT_FF // tf),
        in_specs=[
            pl.BlockSpec((None, tr, D), lambda e, r, f: (e, r, 0)),
            pl.BlockSpec((None, None, D, tf), lambda e, r, f: (layer, e, 0, f)),
            pl.BlockSpec((None, None, D, tf), lambda e, r, f: (layer, e, 0, f)),
            pl.BlockSpec((None, None, tf, D), lambda e, r, f: (layer, e, f, 0)),
            pl.BlockSpec((None, tr, 1), lambda e, r, f: (e, r, 0)),
        ],
        out_specs=pl.BlockSpec((None, tr, D), lambda e, r, f: (e, r, 0)),
        out_shape=jax.ShapeDtypeStruct((N_EXPERTS, SLOTS_PER_EXPERT, D), BF16),
        scratch_shapes=[pltpu.VMEM((tr, D), F32)],
        compiler_params=_cparams(("arbitrary", "arbitrary", "arbitrary")),
        name="moe_ffn",
    )(xg, w1, w3, w2, gcol)


def _combine_ctx_kernel(slot_ref, y_ref, x_ref, g2_ref, o_ref):
    base = (pl.program_id(0) % CTX_SETS_PER_TILE) * CTX_CAP
    slot = slot_ref[...]
    iota_c = lax.broadcasted_iota(jnp.int32, (SEQ, CTX_CAP), 1) + base
    onehot = jnp.concatenate(
        [jnp.where(iota_c == slot[:, e:e + 1], 1.0, 0.0).astype(BF16) for e in range(N_EXPERTS)], axis=1)
    y = y_ref[...].reshape(N_EXPERTS * CTX_CAP, y_ref.shape[-1])
    moe = jnp.dot(onehot, y, preferred_element_type=F32)
    o_ref[...] = x_ref[...] + g2_ref[...] * moe


def _combine_lat_kernel(slot_ref, y_ref, x_ref, g2_ref, prev_ref, o_ref, *, tr):
    r = pl.program_id(2)
    slot = slot_ref[pl.ds(pl.multiple_of(r * tr, tr), tr), :]
    iota_c = lax.broadcasted_iota(jnp.int32, (tr, ROUTE_SLOTS), 1)
    onehot = jnp.concatenate(
        [jnp.where(iota_c == slot[:, e:e + 1], 1.0, 0.0).astype(BF16) for e in range(N_EXPERTS)], axis=1)
    y = y_ref[...].reshape(N_EXPERTS * ROUTE_SLOTS, y_ref.shape[-1])
    moe = jnp.dot(onehot, y, preferred_element_type=F32)
    o_ref[...] = x_ref[...] + g2_ref[...] * moe


def _combine(slot_te, yg, x1, mod4):
    tn = 1024
    x2 = pl.pallas_call(
        _combine_ctx_kernel,
        grid=(BATCH, D // tn),
        in_specs=[
            pl.BlockSpec((None, SEQ, N_EXPERTS), lambda s, j: (s // CTX_SETS_PER_TILE, s % CTX_SETS_PER_TILE, 0)),
            pl.BlockSpec((N_EXPERTS, None, CTX_CAP, tn), lambda s, j: (0, s, 0, j)),
            pl.BlockSpec((SEQ, tn), lambda s, j: (s, j)),
            pl.BlockSpec((None, None, 1, tn), lambda s, j: (0, 5, 0, j)),
        ],
        out_specs=pl.BlockSpec((SEQ, tn), lambda s, j: (s, j)),
        out_shape=jax.ShapeDtypeStruct((M, D), F32),
        compiler_params=_cparams(("arbitrary", "arbitrary")),
        name="moe_combine_ctx",
    )(slot_te, yg.reshape(N_EXPERTS, SLOTS_PER_EXPERT // CTX_CAP, CTX_CAP, D), x1, mod4)
    tr = 512
    nr = ROUTE_TILE // tr
    nct = N_CTX_TILES
    return pl.pallas_call(
        functools.partial(_combine_lat_kernel, tr=tr),
        grid=(N_ROUTE_TILES - nct, D // tn, nr),
        in_specs=[
            pl.BlockSpec((None, ROUTE_TILE, N_EXPERTS), lambda i, j, r: (nct + i, 0, 0)),
            pl.BlockSpec((N_EXPERTS, None, ROUTE_SLOTS, tn), lambda i, j, r: (0, nct + i, 0, j)),
            pl.BlockSpec((tr, tn), lambda i, j, r: ((nct + i) * nr + r, j)),
            pl.BlockSpec((None, None, 1, tn), lambda i, j, r: (1 + i, 5, 0, j)),
            ANY_SPEC,
        ],
        out_specs=pl.BlockSpec((tr, tn), lambda i, j, r: ((nct + i) * nr + r, j)),
        out_shape=jax.ShapeDtypeStruct((M, D), F32),
        input_output_aliases={4: 0},
        compiler_params=_cparams(("arbitrary", "arbitrary", "arbitrary")),
        name="moe_combine",
    )(slot_te, yg.reshape(N_EXPERTS, N_ROUTE_TILES, ROUTE_SLOTS, D), x1, mod4, x2)


def _final_norm_kernel(x_ref, g_ref, o_ref):
    x = x_ref[...]
    o_ref[...] = (x * lax.rsqrt(jnp.mean(x * x, axis=-1, keepdims=True) + EPS)) * g_ref[...]


def _final_norm(x, g, is_ctx):
    tm = 512
    rows, row0 = _part(is_ctx)
    off = row0 // tm
    return pl.pallas_call(
        _final_norm_kernel,
        grid=(rows // tm,),
        in_specs=[pl.BlockSpec((tm, D), lambda i: (off + i, 0)), pl.BlockSpec((1, D), lambda i: (0, 0))],
        out_specs=pl.BlockSpec((tm, D), lambda i: (i, 0)),
        out_shape=jax.ShapeDtypeStruct((rows, D), F32),
        compiler_params=_cparams(("arbitrary",)),
        name="final_norm",
    )(x, g.reshape(1, D))


def _layer(x_parts, mod, lp, layer, lam_init, tables, cache_k4, cache_v4, st_re, st_im, new_kv):
    (xc, xc_row0), (xl, xl_row0) = x_parts
    rope_cos, rope_sin, dft_ctx, dft_lat, cs, perms = tables
    mod4 = mod.reshape(N_MOD_ROWS, 6, 1, D)
    w_in = lp['w_in'].astype(BF16)
    h, u, ua_ctx, new_k, new_v = _norm_proj(xc, xc_row0, lp['norm1_g'], mod4, w_in, layer, True, new_kv)
    h, u, ua_lat = _norm_proj(xl, xl_row0, lp['norm1_g'], mod4, w_in, layer, False, (h, u))

    wb, wc, a_re, a_im = _s5_params(lp)
    ns = S5_GROUPS * S5_STATE
    ngc = BATCH // SUBLANES
    zeros = jnp.zeros((ngc, 2, SUBLANES, ns), F32)
    y_ctx, fin_re, fin_im = _s5(ua_ctx, perms, wb, wc, a_re, a_im, zeros, zeros)
    h0_re = st_re.reshape(DEC_BATCH, 2, ns).transpose(1, 0, 2)[None]
    h0_im = st_im.reshape(DEC_BATCH, 2, ns).transpose(1, 0, 2)[None]
    y_lat, _, _ = _s5(ua_lat, perms, wb, wc, a_re, a_im, h0_re, h0_im)
    d_row = lp['s5_d'].reshape(1, S5_WIDTH)
    w_glu = lp['w_glu'].astype(BF16)
    b_glu = lp['b_glu'].reshape(1, S5_WIDTH)
    ya = _glu(y_ctx, u, d_row, w_glu, b_glu, True, SEQ, None)
    ya = _glu(y_lat, u, d_row, w_glu, b_glu, False, 512, ya)

    yb = _fnet(u, dft_ctx, cs, True, SEQ, None)
    yb = _fnet(u, dft_lat, cs, False, 512, yb)

    lam = (jnp.exp(jnp.sum(lp['lam_q1'].astype(F32) * lp['lam_k1'].astype(F32)))
           - jnp.exp(jnp.sum(lp['lam_q2'].astype(F32) * lp['lam_k2'].astype(F32))) + lam_init).reshape(1)
    subln = lp['subln_g'].reshape(1, VALUE_DIM)
    yc = _attention(u, lam, subln, lam_init, True, SEQ, N_HEADS, None)
    yc = _attention(u, lam, subln, lam_init, False, 1024, 1, yc,
                    ctx=(cache_k4, cache_v4, layer, rope_cos, rope_sin))

    merged = _merge(h, ya, yb, yc, lp['w_gate'].astype(BF16), lp['b_gate'].reshape(1, 3 * D),
                    lp['w_branch_a'].astype(BF16), lp['w_branch_b'].astype(BF16),
                    lp['w_branch_c'].astype(BF16))
    w_out = lp['w_out'].astype(BF16)
    x1, h2 = _out_proj(merged, w_out, xc, xc_row0, lp['norm2_g'], mod4, True, None)
    x1, h2 = _out_proj(merged, w_out, xl, xl_row0, lp['norm2_g'], mod4, False, (x1, h2))

    slot_et, gcol = _router(h2, lp['w_router'].T.astype(BF16))
    xg = _dispatch(h2, slot_et)
    yg = _ffn(xg, lp['moe_w1_all'], lp['moe_w3_all'], lp['moe_w2_all'],
              gcol.reshape(N_EXPERTS, SLOTS_PER_EXPERT, 1), layer)
    x2 = _combine(slot_et.transpose(0, 2, 1), yg, x1, mod4)

    fin_shape = (ngc, 2, SUBLANES, S5_GROUPS, S5_STATE)
    s_re = fin_re.reshape(fin_shape).transpose(0, 2, 1, 3, 4).reshape(BATCH, 2, S5_GROUPS, S5_STATE)
    s_im = fin_im.reshape(fin_shape).transpose(0, 2, 1, 3, 4).reshape(BATCH, 2, S5_GROUPS, S5_STATE)
    return x2, (new_k, new_v), s_re, s_im


def kernel(x_prompt, x_sample, cache_k, cache_v, state_s5_re, state_s5_im, c, c_ctx, norm1_g, norm2_g, final_norm_g, w_ada, b_ada, w_in, s5_lam_re, s5_lam_im, s5_log_dt, s5_b_re, s5_b_im, s5_c_re, s5_c_im, s5_d, w_glu, b_glu, lam_q1, lam_k1, lam_q2, lam_k2, subln_g, w_branch_a, w_branch_b, w_branch_c, w_gate, b_gate, w_out, w_router, moe_w1, moe_w3, moe_w2):
    params = dict(norm1_g=norm1_g, norm2_g=norm2_g, w_in=w_in,
                  s5_lam_re=s5_lam_re, s5_lam_im=s5_lam_im, s5_log_dt=s5_log_dt,
                  s5_b_re=s5_b_re, s5_b_im=s5_b_im, s5_c_re=s5_c_re, s5_c_im=s5_c_im, s5_d=s5_d,
                  w_glu=w_glu, b_glu=b_glu, lam_q1=lam_q1, lam_k1=lam_k1, lam_q2=lam_q2, lam_k2=lam_k2,
                  subln_g=subln_g, w_branch_a=w_branch_a, w_branch_b=w_branch_b, w_branch_c=w_branch_c,
                  w_gate=w_gate, b_gate=b_gate, w_out=w_out, w_router=w_router)
    cc = jnp.zeros((N_MOD_ROWS, D), F32).at[0].set(c_ctx).at[1:1 + DEC_BATCH].set(c)
    mods = _ada(cc, w_ada, b_ada)
    rope_cos, rope_sin = _rope_tables(DEC_SEQ)
    c_c, s_c = _dft_tables(FNET_GROUP_CH)
    cs = jnp.concatenate([c_c, s_c], axis=1).astype(BF16)
    tables = (rope_cos, rope_sin, _fnet_tables(SEQ), _fnet_tables(DEC_SEQ), cs, _s5_permutations())
    cache_k4 = cache_k.reshape(DEC_BATCH, DEPTH, PAST_LEN, QK_WIDTH)
    cache_v4 = cache_v.reshape(DEC_BATCH, DEPTH, PAST_LEN, V_WIDTH)
    x_parts = ((x_prompt.reshape(M_CTX, D), 0), (x_sample.reshape(M_LAT, D), 0))
    new_kv = None
    s_res, s_ims = [], []
    for l in range(DEPTH):
        lp = {name: arr[l] for name, arr in params.items()}
        lp['moe_w1_all'], lp['moe_w3_all'], lp['moe_w2_all'] = moe_w1, moe_w3, moe_w2
        lam_init = 0.8 - 0.6 * math.exp(-0.3 * l)
        x, new_kv, s_re, s_im = _layer(x_parts, mods[l], lp, l, lam_init, tables, cache_k4, cache_v4,
                                       state_s5_re[:, l], state_s5_im[:, l], new_kv)
        x_parts = ((x, 0), (x, M_CTX))
        s_res.append(s_re)
        s_ims.append(s_im)
    y_prompt = _final_norm(x, final_norm_g, True).reshape(BATCH, SEQ, D)
    y_sample = _final_norm(x, final_norm_g, False).reshape(DEC_BATCH, DEC_SEQ, D)
    kv_shape = (BATCH, DEPTH, SEQ, N_HEADS, VALUE_DIM)
    return (y_prompt, y_sample, new_kv[0].reshape(kv_shape), new_kv[1].reshape(kv_shape),
            jnp.stack(s_res, axis=1), jnp.stack(s_ims, axis=1))
```

```python
import functools
import math

import jax
import jax.numpy as jnp
from jax import lax
from jax.experimental import pallas as pl
from jax.experimental.pallas import tpu as pltpu

F32 = jnp.float32
BF16 = jnp.bfloat16

D = 2048
BATCH, SEQ = 32, 256
DEC_BATCH, DEC_SEQ = 8, 2048
DEPTH = 2
PAST_LEN = 256
GRID_W = 64
EPS = 1e-6
S5_GROUP_CH, S5_GROUPS, S5_STATE = 16, 32, 64
S5_WIDTH = S5_GROUPS * S5_GROUP_CH
FNET_GROUPS, FNET_GROUP_CH = 4, 128
FNET_WIDTH = FNET_GROUPS * FNET_GROUP_CH
N_HEADS, HEAD_DIM = 8, 64
VALUE_DIM = 2 * HEAD_DIM
QK_WIDTH = N_HEADS * 2 * HEAD_DIM
V_WIDTH = N_HEADS * VALUE_DIM
IN_WIDTH = S5_WIDTH + FNET_WIDTH + 2 * QK_WIDTH + V_WIDTH
ROPE_BASE = 10000.0
ROPE_AXIS_DIM = HEAD_DIM // 2
N_EXPERTS = 16
EXPERT_FF = 2048
CAPACITY_FACTOR = 2

M_CTX = BATCH * SEQ
M_LAT = DEC_BATCH * DEC_SEQ
M = M_CTX + M_LAT
N_MOD_ROWS = 16

LANES = 128
SUBLANES = 8
VMEM_LIMIT = 56 * 1024 * 1024

ROUTE_TILE = 2048
ROUTE_SLOTS = CAPACITY_FACTOR * ROUTE_TILE // N_EXPERTS
N_ROUTE_TILES = M // ROUTE_TILE
SLOTS_PER_EXPERT = N_ROUTE_TILES * ROUTE_SLOTS

S5_BLK = 4
S5_BLK_STATES = 512

ANY_SPEC = pl.BlockSpec(memory_space=pl.ANY)


def _cparams(sem):
    return pltpu.CompilerParams(dimension_semantics=sem, vmem_limit_bytes=VMEM_LIMIT)


def _sigmoid(x):
    return 1.0 / (1.0 + jnp.exp(-x))


def _part(is_ctx):
    return (M_CTX, 0) if is_ctx else (M_LAT, M_CTX)


def _mod_row_fn(is_ctx, tm):
    if is_ctx:
        return lambda i: 0
    return lambda i: 1 + i // (DEC_SEQ // tm)


def _ada_kernel(c_ref, w_ref, b_ref, o_ref):
    c = c_ref[...]
    s = (c * _sigmoid(c)).astype(BF16)
    o_ref[...] = jnp.dot(s, w_ref[...].astype(BF16), preferred_element_type=F32) + b_ref[...]


def _ada(cc, w_ada, b_ada):
    tn = 1024
    n = 6 * D
    return pl.pallas_call(
        _ada_kernel,
        grid=(DEPTH, n // tn),
        in_specs=[
            pl.BlockSpec((N_MOD_ROWS, D), lambda l, j: (0, 0)),
            pl.BlockSpec((None, D, tn), lambda l, j: (l, 0, j)),
            pl.BlockSpec((None, 1, tn), lambda l, j: (l, 0, j)),
        ],
        out_specs=pl.BlockSpec((None, N_MOD_ROWS, tn), lambda l, j: (l, 0, j)),
        out_shape=jax.ShapeDtypeStruct((DEPTH, N_MOD_ROWS, n), F32),
        compiler_params=_cparams(("arbitrary", "arbitrary")),
        name="ada",
    )(cc, w_ada, b_ada.reshape(DEPTH, 1, n))


NP_TM_CTX, NP_TM_LAT, NP_TN = 512, 1024, 1024


NP_NJ = IN_WIDTH // NP_TN


def _norm_proj_kernel(x_ref, g_ref, sc_ref, sh_ref, w_ref, *rest, is_ctx, n_alias, tm):
    rest = rest[n_alias:]
    if is_ctx:
        h_ref, u_ref, ua_ref, k_ref, v_ref, ha_ref, hb_ref = rest
    else:
        h_ref, u_ref, ua_ref, ha_ref, hb_ref = rest
    i = pl.program_id(0)
    j = pl.program_id(1)
    q = tm // NP_NJ
    rows = pl.ds(pl.multiple_of(j * q, q), q)

    def norm_quarter(dst_ref):
        x = x_ref[rows, :]
        r = x * lax.rsqrt(jnp.mean(x * x, axis=-1, keepdims=True) + EPS)
        dst_ref[rows, :] = ((r * g_ref[...]) * (1.0 + sc_ref[...]) + sh_ref[...]).astype(BF16)

    def project(src_ref):
        h_ref[rows, :] = src_ref[rows, :]
        u_ref[...] = jnp.dot(src_ref[...], w_ref[...], preferred_element_type=F32)

    @pl.when(i == 0)
    def _():
        norm_quarter(ha_ref)

    @pl.when((i > 0) & (i % 2 == 0))
    def _():
        norm_quarter(ha_ref)
        project(hb_ref)

    @pl.when(i % 2 == 1)
    def _():
        norm_quarter(hb_ref)
        project(ha_ref)

    @pl.when((i > 0) & (j == 0))
    def _():
        if is_ctx:
            for b in range(tm // SEQ):
                ua_ref[:, b * S5_WIDTH:(b + 1) * S5_WIDTH] = u_ref[b * SEQ:(b + 1) * SEQ, :S5_WIDTH]
        else:
            ua_ref[...] = u_ref[:, :S5_WIDTH]

    if is_ctx:
        @pl.when((i > 0) & (j == 2))
        def _():
            for b in range(tm // SEQ):
                k_ref[b] = u_ref[b * SEQ:(b + 1) * SEQ, :]

        @pl.when((i > 0) & (j == 3))
        def _():
            for b in range(tm // SEQ):
                v_ref[b] = u_ref[b * SEQ:(b + 1) * SEQ, :]


def _norm_proj(x, x_row0, g, mod4, w_in, layer, is_ctx, prev):
    tm, tn = (NP_TM_CTX if is_ctx else NP_TM_LAT), NP_TN
    assert QK_WIDTH == tn and V_WIDTH == tn and S5_WIDTH <= tn
    rows, row0 = _part(is_ctx)
    n_i = rows // tm
    xoff, ooff = x_row0 // tm, row0 // tm
    mrow = _mod_row_fn(is_ctx, tm)
    cur = lambda i: jnp.minimum(i, n_i - 1)
    prv = lambda i: jnp.maximum(i - 1, 0)
    prev = list(prev) if prev is not None else []
    in_specs = [
        pl.BlockSpec((tm, D), lambda i, j: (xoff + cur(i), 0)),
        pl.BlockSpec((1, D), lambda i, j: (0, 0)),
        pl.BlockSpec((None, None, 1, D), lambda i, j: (mrow(cur(i)), 1, 0, 0)),
        pl.BlockSpec((None, None, 1, D), lambda i, j: (mrow(cur(i)), 0, 0, 0)),
        pl.BlockSpec((None, D, tn), lambda i, j: (layer, 0, j)),
    ] + [ANY_SPEC] * len(prev)
    out_specs = [
        pl.BlockSpec((tm, D), lambda i, j: (ooff + prv(i), 0)),
        pl.BlockSpec((tm, tn), lambda i, j: (ooff + prv(i), j)),
    ]
    out_shape = [jax.ShapeDtypeStruct((M, D), BF16), jax.ShapeDtypeStruct((M, IN_WIDTH), F32)]
    if is_ctx:
        nbt = tm // SEQ
        out_specs += [
            pl.BlockSpec((SEQ, nbt * S5_WIDTH), lambda i, j: (0, prv(i))),
            pl.BlockSpec((nbt, None, SEQ, QK_WIDTH), lambda i, j: (prv(i), layer, 0, 0)),
            pl.BlockSpec((nbt, None, SEQ, V_WIDTH), lambda i, j: (prv(i), layer, 0, 0)),
        ]
        out_shape += [
            jax.ShapeDtypeStruct((SEQ, BATCH * S5_WIDTH), F32),
            jax.ShapeDtypeStruct((BATCH, DEPTH, SEQ, QK_WIDTH), F32),
            jax.ShapeDtypeStruct((BATCH, DEPTH, SEQ, V_WIDTH), F32),
        ]
        aliases = {5: 3, 6: 4} if prev else {}
    else:
        tpb = DEC_SEQ // tm
        out_specs.append(pl.BlockSpec((tm, S5_WIDTH), lambda i, j: (prv(i) % tpb, prv(i) // tpb)))
        out_shape.append(jax.ShapeDtypeStruct((DEC_SEQ, DEC_BATCH * S5_WIDTH), F32))
        aliases = {5: 0, 6: 1}
    return pl.pallas_call(
        functools.partial(_norm_proj_kernel, is_ctx=is_ctx, n_alias=len(prev), tm=tm),
        grid=(n_i + 1, NP_NJ),
        in_specs=in_specs,
        out_specs=out_specs,
        out_shape=out_shape,
        scratch_shapes=[pltpu.VMEM((tm, D), BF16), pltpu.VMEM((tm, D), BF16)],
        input_output_aliases=aliases,
        compiler_params=_cparams(("arbitrary", "arbitrary")),
        name="norm_proj",
    )(x, g.reshape(1, D), mod4, mod4, w_in, *prev)


S5_CHUNK = 64


def _s5_kernel(u_ref, pin_ref, pout_ref, wb_ref, wc_ref, are_ref, aim_ref, h0re_ref, h0im_ref,
               y_ref, finre_ref, finim_ref, xs0_ref, xs1_ref, xs2_ref, xs3_ref, stre_ref, stim_ref):
    c = pl.program_id(2)
    nc = pl.num_programs(2)
    bs = S5_BLK_STATES
    chunk = S5_CHUNK
    xs_refs = (xs0_ref, xs1_ref, xs2_ref, xs3_ref)

    @pl.when(c == 0)
    def _():
        stre_ref[...] = h0re_ref[...]
        stim_ref[...] = h0im_ref[...]

    ustack = jnp.concatenate(
        [u_ref[:, j * S5_WIDTH:(j + 1) * S5_WIDTH] for j in range(SUBLANES)], axis=0).astype(BF16)
    up = jnp.dot(pin_ref[...], ustack, preferred_element_type=F32).astype(BF16)

    for blk in range(S5_BLK):
        xs_refs[blk][...] = jnp.dot(up[:, blk * LANES:(blk + 1) * LANES], wb_ref[blk],
                                    preferred_element_type=F32)

    for blk in range(S5_BLK):
        xs_ref = xs_refs[blk]
        ar = are_ref[:, blk * bs:(blk + 1) * bs]
        ai = aim_ref[:, blk * bs:(blk + 1) * bs]
        sr = stre_ref[:, blk * bs:(blk + 1) * bs]
        si = stim_ref[:, blk * bs:(blk + 1) * bs]
        for i in range(chunk):
            r8 = slice(i * SUBLANES, (i + 1) * SUBLANES)
            nr = ar * sr - ai * si + xs_ref[r8, 0:bs]
            ni = ar * si + ai * sr + xs_ref[r8, bs:2 * bs]
            xs_ref[r8, 0:bs] = nr
            xs_ref[r8, bs:2 * bs] = ni
            sr, si = nr, ni
        stre_ref[:, blk * bs:(blk + 1) * bs] = sr
        stim_ref[:, blk * bs:(blk + 1) * bs] = si

    y = jnp.concatenate(
        [jnp.dot(xs_refs[blk][...].astype(BF16), wc_ref[blk], preferred_element_type=F32)
         for blk in range(S5_BLK)], axis=1)
    y_hi = y.astype(BF16)
    y_lo = (y - y_hi.astype(F32)).astype(BF16)
    ys = (jnp.dot(pout_ref[...], y_hi, preferred_element_type=F32)
          + jnp.dot(pout_ref[...], y_lo, preferred_element_type=F32))
    for j in range(SUBLANES):
        y_ref[:, j * S5_WIDTH:(j + 1) * S5_WIDTH] = ys[j * chunk:(j + 1) * chunk, :]

    @pl.when(c == nc - 1)
    def _():
        finre_ref[...] = stre_ref[...]
        finim_ref[...] = stim_ref[...]


def _s5_permutations():
    chunk = S5_CHUNK
    rows = chunk * SUBLANES
    r_out = lax.broadcasted_iota(jnp.int32, (rows, rows), 0)
    r_in = lax.broadcasted_iota(jnp.int32, (rows, rows), 1)
    step, seq = r_out // SUBLANES, r_out % SUBLANES
    fwd = r_in == seq * chunk + step
    bwd = r_in == seq * chunk + (chunk - 1 - step)
    p = jnp.stack([fwd, bwd]).astype(BF16)
    return p, p.transpose(0, 2, 1)


def _s5(u2d, perms, wb, wc, a_re, a_im, h0_re, h0_im):
    t_len = u2d.shape[0]
    ng = u2d.shape[1] // (SUBLANES * S5_WIDTH)
    chunk = S5_CHUNK
    nc = t_len // chunk
    rows = chunk * SUBLANES
    ns = S5_BLK * S5_BLK_STATES
    pin, pout = perms

    def ceff(d, c):
        return jnp.where(d == 0, c, nc - 1 - c)

    return pl.pallas_call(
        _s5_kernel,
        grid=(ng, 2, nc),
        in_specs=[
            pl.BlockSpec((chunk, SUBLANES * S5_WIDTH), lambda g, d, c: (ceff(d, c), g)),
            pl.BlockSpec((None, rows, rows), lambda g, d, c: (d, 0, 0)),
            pl.BlockSpec((None, rows, rows), lambda g, d, c: (d, 0, 0)),
            pl.BlockSpec((None, S5_BLK, LANES, 2 * S5_BLK_STATES), lambda g, d, c: (d, 0, 0, 0)),
            pl.BlockSpec((None, S5_BLK, 2 * S5_BLK_STATES, LANES), lambda g, d, c: (d, 0, 0, 0)),
            pl.BlockSpec((None, SUBLANES, ns), lambda g, d, c: (d, 0, 0)),
            pl.BlockSpec((None, SUBLANES, ns), lambda g, d, c: (d, 0, 0)),
            pl.BlockSpec((None, None, SUBLANES, ns), lambda g, d, c: (g, d, 0, 0)),
            pl.BlockSpec((None, None, SUBLANES, ns), lambda g, d, c: (g, d, 0, 0)),
        ],
        out_specs=[
            pl.BlockSpec((None, chunk, SUBLANES * S5_WIDTH), lambda g, d, c: (d, ceff(d, c), g)),
            pl.BlockSpec((None, None, SUBLANES, ns), lambda g, d, c: (g, d, 0, 0)),
            pl.BlockSpec((None, None, SUBLANES, ns), lambda g, d, c: (g, d, 0, 0)),
        ],
        out_shape=[
            jax.ShapeDtypeStruct((2,) + u2d.shape, F32),
            jax.ShapeDtypeStruct((ng, 2, SUBLANES, ns), F32),
            jax.ShapeDtypeStruct((ng, 2, SUBLANES, ns), F32),
        ],
        scratch_shapes=[pltpu.VMEM((rows, 2 * S5_BLK_STATES), F32)] * S5_BLK + [
            pltpu.VMEM((SUBLANES, ns), F32),
            pltpu.VMEM((SUBLANES, ns), F32),
        ],
        compiler_params=_cparams(("arbitrary", "arbitrary", "arbitrary")),
        name="s5_scan",
    )(u2d, pin, pout, wb, wc, a_re, a_im, h0_re, h0_im)


def _s5_params(lp):
    lam_re = lp['s5_lam_re'].astype(F32)
    lam_im = lp['s5_lam_im'].astype(F32)
    dt = jnp.exp(lp['s5_log_dt'].astype(F32))[..., None]
    b_re = lp['s5_b_re'].astype(F32)
    b_im = lp['s5_b_im'].astype(F32)
    c_re = lp['s5_c_re'].astype(F32)
    c_im = lp['s5_c_im'].astype(F32)
    mag = jnp.exp(lam_re * dt)
    ab_re = mag * jnp.cos(lam_im * dt)
    ab_im = mag * jnp.sin(lam_im * dt)
    den = lam_re * lam_re + lam_im * lam_im
    f_re = ((ab_re - 1.0) * lam_re + ab_im * lam_im) / den
    f_im = (ab_im * lam_re - (ab_re - 1.0) * lam_im) / den
    bb_re = f_re[..., None] * b_re - f_im[..., None] * b_im
    bb_im = f_re[..., None] * b_im + f_im[..., None] * b_re
    gpb = S5_GROUPS // S5_BLK
    eye = jnp.eye(gpb, dtype=F32)

    def pack_b(bb):
        bb = bb.reshape(2, S5_BLK, gpb, S5_STATE, S5_GROUP_CH)
        w = jnp.einsum('dbgph,gk->dbghkp', bb, eye)
        return w.reshape(2, S5_BLK, gpb * S5_GROUP_CH, gpb * S5_STATE)

    def pack_c(cc):
        cc = cc.reshape(2, S5_BLK, gpb, S5_GROUP_CH, S5_STATE)
        w = jnp.einsum('dbghp,gk->dbkpgh', cc, eye)
        return w.reshape(2, S5_BLK, gpb * S5_STATE, gpb * S5_GROUP_CH)

    wb = jnp.concatenate([pack_b(bb_re), pack_b(bb_im)], axis=-1).astype(BF16)
    wc = jnp.concatenate([pack_c(c_re), -pack_c(c_im)], axis=-2).astype(BF16)
    ns = S5_GROUPS * S5_STATE
    a_re = jnp.broadcast_to(ab_re.reshape(2, 1, ns), (2, SUBLANES, ns))
    a_im = jnp.broadcast_to(ab_im.reshape(2, 1, ns), (2, SUBLANES, ns))
    return wb, wc, a_re, a_im


def _gelu_tanh(x):
    return 0.5 * x * (1.0 + jnp.tanh(math.sqrt(2.0 / math.pi) * (x + 0.044715 * (x * x * x))))


def _glu_kernel(y_ref, u_ref, d_ref, w_ref, b_ref, *rest):
    o_ref = rest[-1]
    y = y_ref[0] + y_ref[1] + d_ref[...] * u_ref[...]
    y = _gelu_tanh(y)
    z = jnp.dot(y.astype(BF16), w_ref[...], preferred_element_type=F32) + b_ref[...]
    o_ref[...] = (y * _sigmoid(z)).astype(BF16)


def _glu(y_dirs, u, d_row, w_glu, b_glu, is_ctx, tt, prev):
    t_len = y_dirs.shape[1]
    nb = y_dirs.shape[2] // S5_WIDTH
    ntt = t_len // tt
    off = _part(is_ctx)[1] // tt
    prev = [prev] if prev is not None else []
    return pl.pallas_call(
        _glu_kernel,
        grid=(nb, ntt),
        in_specs=[
            pl.BlockSpec((2, tt, S5_WIDTH), lambda b, t: (0, t, b)),
            pl.BlockSpec((tt, S5_WIDTH), lambda b, t: (off + b * ntt + t, 0)),
            pl.BlockSpec((1, S5_WIDTH), lambda b, t: (0, 0)),
            pl.BlockSpec((S5_WIDTH, S5_WIDTH), lambda b, t: (0, 0)),
            pl.BlockSpec((1, S5_WIDTH), lambda b, t: (0, 0)),
        ] + [ANY_SPEC] * len(prev),
        out_specs=pl.BlockSpec((tt, S5_WIDTH), lambda b, t: (off + b * ntt + t, 0)),
        out_shape=jax.ShapeDtypeStruct((M, S5_WIDTH), BF16),
        input_output_aliases={5: 0} if prev else {},
        compiler_params=_cparams(("arbitrary", "arbitrary")),
        name="s5_glu",
    )(y_dirs, u, d_row, w_glu, b_glu, *prev)


def _fnet_kernel(x_ref, cs_ref, dft_ref, *rest, t_len):
    o_ref, z_ref = rest[-2:]

    @pl.when(pl.program_id(1) == 0)
    def _():
        x = x_ref[...].astype(BF16)
        for g in range(FNET_GROUPS):
            z = jnp.dot(x[:, g * LANES:(g + 1) * LANES], cs_ref[...], preferred_element_type=F32)
            z_ref[0:t_len, g * LANES:(g + 1) * LANES] = z[:, :LANES].astype(BF16)
            z_ref[t_len:2 * t_len, g * LANES:(g + 1) * LANES] = z[:, LANES:].astype(BF16)

    scale = 1.0 / math.sqrt(t_len * FNET_GROUP_CH)
    o_ref[...] = (jnp.dot(dft_ref[...], z_ref[...], preferred_element_type=F32) * scale).astype(BF16)


def _dft_tables(n):
    j = lax.broadcasted_iota(jnp.int32, (n, n), 0)
    k = lax.broadcasted_iota(jnp.int32, (n, n), 1)
    ang = ((j * k) % n).astype(F32) * (2.0 * math.pi / n)
    return jnp.cos(ang), jnp.sin(ang)


def _fnet_tables(t_len):
    c_t, s_t = _dft_tables(t_len)
    return jnp.concatenate([c_t, -s_t], axis=1).astype(BF16)


def _fnet(u, dft, cs, is_ctx, tr, prev):
    t_len = dft.shape[0]
    rows, row0 = _part(is_ctx)
    nb = rows // t_len
    ntr = t_len // tr
    off = row0 // t_len
    offr = row0 // tr
    col = S5_WIDTH // FNET_WIDTH
    prev = [prev] if prev is not None else []
    return pl.pallas_call(
        functools.partial(_fnet_kernel, t_len=t_len),
        grid=(nb, ntr),
        in_specs=[
            pl.BlockSpec((t_len, FNET_WIDTH), lambda b, r: (off + b, col)),
            pl.BlockSpec((FNET_GROUP_CH, 2 * FNET_GROUP_CH), lambda b, r: (0, 0)),
            pl.BlockSpec((tr, 2 * t_len), lambda b, r: (r, 0)),
        ] + [ANY_SPEC] * len(prev),
        out_specs=pl.BlockSpec((tr, FNET_WIDTH), lambda b, r: (offr + b * ntr + r, 0)),
        out_shape=jax.ShapeDtypeStruct((M, FNET_WIDTH), BF16),
        scratch_shapes=[pltpu.VMEM((2 * t_len, FNET_WIDTH), BF16)],
        input_output_aliases={3: 0} if prev else {},
        compiler_params=_cparams(("arbitrary", "arbitrary")),
        name="fnet",
    )(u, cs, dft, *prev)


def _rope(x, cos, sin):
    half = ROPE_AXIS_DIM // 2
    lane = lax.broadcasted_iota(jnp.int32, x.shape, 1)
    up = pltpu.roll(x, LANES - half, 1)
    dn = pltpu.roll(x, half, 1)
    rot = jnp.where((lane % ROPE_AXIS_DIM) < half, -up, dn)
    return x * cos + rot * sin


ATTN_KEY_CHUNK = 512
ATTN_ROW_BLOCK = 256


def _attn_kernel(lam_ref, q_ref, k_ref, v_ref, *rest, tq, t_len, hb, has_ctx, lam_init):
    g_ref = rest[-6]
    o_ref, kk_ref, vv_ref, qs_ref, s_ref = rest[-5:]
    if has_ctx:
        ck_ref, cv_ref, cos_ref, sin_ref = rest[:4]
    qi = pl.program_id(2)
    n_keys = kk_ref.shape[1]
    kc, rb = ATTN_KEY_CHUNK, ATTN_ROW_BLOCK
    chunks = [(c0, min(c0 + kc, n_keys)) for c0 in range(0, n_keys, kc)]

    @pl.when(qi == 0)
    def _():
        for h in range(hb):
            hs = slice(h * LANES, (h + 1) * LANES)
            k = k_ref[:, hs]
            if has_ctx:
                k = _rope(k, cos_ref[...], sin_ref[...])
                kk_ref[h, t_len:, :] = ck_ref[:, hs].astype(BF16)
                vv_ref[h, t_len:, 0:VALUE_DIM] = cv_ref[:, hs].astype(BF16)
            kk_ref[h, 0:t_len, :] = k.astype(BF16)
            vv_ref[h, 0:t_len, 0:VALUE_DIM] = v_ref[:, hs].astype(BF16)
            vv_ref[h, :, VALUE_DIM:] = jnp.ones((n_keys, VALUE_DIM), BF16)

    lane = lax.broadcasted_iota(jnp.int32, (tq, LANES), 1)
    for h in range(hb):
        hs = slice(h * LANES, (h + 1) * LANES)
        q = q_ref[:, hs]
        if has_ctx:
            r0 = pl.multiple_of(qi * tq, tq)
            q = _rope(q, cos_ref[pl.ds(r0, tq), :], sin_ref[pl.ds(r0, tq), :])
        q = q * (HEAD_DIM ** -0.5)
        qs_ref[0, :, hs] = jnp.where(lane < HEAD_DIM, q, 0.0).astype(BF16)
        qs_ref[1, :, hs] = jnp.where(lane >= HEAD_DIM, q, 0.0).astype(BF16)

    lam = lam_ref[0]
    gain = g_ref[...]
    dn = (((1,), (1,)), ((), ()))

    unit = 0
    for r in range(tq // rb):
        rows = slice(r * rb, (r + 1) * rb)
        for h in range(hb):
            hs = slice(h * LANES, (h + 1) * LANES)
            o = None
            for n in range(2):
                slot = unit % 2
                unit += 1
                qn = qs_ref[n, rows, hs]
                mrun = None
                for c0, c1 in chunks:
                    s_c = lax.dot_general(qn, kk_ref[h, c0:c1, :], dn, preferred_element_type=F32)
                    s_ref[slot, :, c0:c1] = s_c
                    for l0 in range(0, c1 - c0, LANES):
                        mc = s_c[:, l0:l0 + LANES]
                        mrun = mc if mrun is None else jnp.maximum(mrun, mc)
                m_b = jnp.broadcast_to(jnp.max(mrun, axis=-1, keepdims=True), (rb, LANES))
                oa = None
                for c0, c1 in chunks:
                    p = jnp.concatenate(
                        [jnp.exp(s_ref[slot, :, l0:l0 + LANES] - m_b) for l0 in range(c0, c1, LANES)],
                        axis=1).astype(BF16)
                    part = jnp.dot(p, vv_ref[h, c0:c1, :], preferred_element_type=F32)
                    oa = part if oa is None else oa + part
                on = oa[:, :VALUE_DIM] * (1.0 / oa[:, VALUE_DIM:VALUE_DIM + 1])
                o = on if n == 0 else o - lam * on
            o = o * lax.rsqrt(jnp.mean(o * o, axis=-1, keepdims=True) + EPS)
            o_ref[rows, hs] = ((o * gain) * (1.0 - lam_init)).astype(BF16)


def _attention(u, lam, subln_g, lam_init, is_ctx, tq, hb, prev, ctx=None):
    rows, row0 = _part(is_ctx)
    t_len = SEQ if is_ctx else DEC_SEQ
    nb = rows // t_len
    nq = t_len // tq
    off = row0 // t_len
    offq = row0 // tq
    width = hb * LANES
    qcol = (S5_WIDTH + FNET_WIDTH) // width
    kcol = qcol + QK_WIDTH // width
    vcol = kcol + QK_WIDTH // width
    has_ctx = ctx is not None
    n_keys = t_len + (PAST_LEN if has_ctx else 0)
    assert tq % ATTN_ROW_BLOCK == 0 and n_keys % (2 * LANES) == 0
    in_specs = [
        pl.BlockSpec(memory_space=pltpu.SMEM),
        pl.BlockSpec((tq, width), lambda b, h, i: (offq + b * nq + i, qcol + h)),
        pl.BlockSpec((t_len, width), lambda b, h, i: (off + b, kcol + h)),
        pl.BlockSpec((t_len, width), lambda b, h, i: (off + b, vcol + h)),
    ]
    args = [lam, u, u, u]
    if has_ctx:
        ck, cv, layer, cos, sin = ctx
        in_specs += [
            pl.BlockSpec((None, None, PAST_LEN, width), lambda b, h, i: (b, layer, 0, h)),
            pl.BlockSpec((None, None, PAST_LEN, width), lambda b, h, i: (b, layer, 0, h)),
            pl.BlockSpec((t_len, LANES), lambda b, h, i: (0, 0)),
            pl.BlockSpec((t_len, LANES), lambda b, h, i: (0, 0)),
        ]
        args += [ck, cv, cos, sin]
    in_specs.append(pl.BlockSpec((1, LANES), lambda b, h, i: (0, 0)))
    args.append(subln_g)
    aliases = {}
    if prev is not None:
        aliases = {len(args): 0}
        in_specs.append(ANY_SPEC)
        args.append(prev)
    kern = functools.partial(_attn_kernel, tq=tq, t_len=t_len, hb=hb, has_ctx=has_ctx, lam_init=lam_init)
    if prev is not None:
        kern = functools.partial(_skip_ref, kern, len(args) - 1)
    return pl.pallas_call(
        kern,
        grid=(nb, N_HEADS // hb, nq),
        in_specs=in_specs,
        out_specs=pl.BlockSpec((tq, width), lambda b, h, i: (offq + b * nq + i, h)),
        out_shape=jax.ShapeDtypeStruct((M, V_WIDTH), BF16),
        scratch_shapes=[
            pltpu.VMEM((hb, n_keys, LANES), BF16),
            pltpu.VMEM((hb, n_keys, 2 * VALUE_DIM), BF16),
            pltpu.VMEM((2, tq, width), BF16),
            pltpu.VMEM((2, ATTN_ROW_BLOCK, n_keys), F32),
        ],
        input_output_aliases=aliases,
        compiler_params=_cparams(("arbitrary", "arbitrary", "arbitrary")),
        name="diff_attn",
    )(*args)


def _skip_ref(kern, idx, *refs):
    return kern(*refs[:idx], *refs[idx + 1:])


def _rope_tables(t_len):
    rows = t_len // GRID_W
    pos_row = jnp.broadcast_to(jnp.arange(rows, dtype=F32)[:, None], (rows, GRID_W)).reshape(-1)
    pos_col = jnp.broadcast_to(jnp.arange(GRID_W, dtype=F32)[None, :], (rows, GRID_W)).reshape(-1)
    inv = ROPE_BASE ** (-jnp.arange(0, ROPE_AXIS_DIM, 2, dtype=F32) / ROPE_AXIS_DIM)
    ang_r = pos_row[:, None] * inv
    ang_c = pos_col[:, None] * inv
    cos = jnp.concatenate([jnp.cos(ang_r), jnp.cos(ang_r), jnp.cos(ang_c), jnp.cos(ang_c)], axis=-1)
    sin = jnp.concatenate([jnp.sin(ang_r), jnp.sin(ang_r), jnp.sin(ang_c), jnp.sin(ang_c)], axis=-1)
    return jnp.concatenate([cos, cos], axis=-1), jnp.concatenate([sin, sin], axis=-1)


def _merge_kernel(h_ref, ya_ref, yb_ref, yc_ref, wga_ref, wgb_ref, wgc_ref, bga_ref, bgb_ref, bgc_ref,
                  wa_ref, wb_ref, wc_ref, o_ref):
    h = h_ref[...]
    acc = None
    for wg, bg, y, w in ((wga_ref, bga_ref, ya_ref, wa_ref), (wgb_ref, bgb_ref, yb_ref, wb_ref),
                         (wgc_ref, bgc_ref, yc_ref, wc_ref)):
        gate = _sigmoid(jnp.dot(h, wg[...], preferred_element_type=F32) + bg[...])
        term = gate * jnp.dot(y[...], w[...], preferred_element_type=F32)
        acc = term if acc is None else acc + term
    o_ref[...] = acc.astype(BF16)


def _merge(h, ya, yb, yc, w_gate, b_gate, w_a, w_b, w_c, layer):
    tm, tn = 1024, 512
    nj = D // tn
    row = lambda i, j: (i, 0)
    wspecs = [pl.BlockSpec((None, D, tn), lambda i, j, k=k: (layer, 0, k * nj + j)) for k in range(3)]
    bspecs = [pl.BlockSpec((None, 1, tn), lambda i, j, k=k: (layer, 0, k * nj + j)) for k in range(3)]
    return pl.pallas_call(
        _merge_kernel,
        grid=(M // tm, nj),
        in_specs=[
            pl.BlockSpec((tm, D), row),
            pl.BlockSpec((tm, S5_WIDTH), row),
            pl.BlockSpec((tm, FNET_WIDTH), row),
            pl.BlockSpec((tm, V_WIDTH), row),
            *wspecs, *bspecs,
            pl.BlockSpec((None, S5_WIDTH, tn), lambda i, j: (layer, 0, j)),
            pl.BlockSpec((None, FNET_WIDTH, tn), lambda i, j: (layer, 0, j)),
            pl.BlockSpec((None, V_WIDTH, tn), lambda i, j: (layer, 0, j)),
        ],
        out_specs=pl.BlockSpec((tm, tn), lambda i, j: (i, j)),
        out_shape=jax.ShapeDtypeStruct((M, D), BF16),
        compiler_params=_cparams(("arbitrary", "arbitrary")),
        name="merge",
    )(h, ya, yb, yc, w_gate, w_gate, w_gate, b_gate, b_gate, b_gate, w_a, w_b, w_c)


def _out_proj_kernel(m_ref, w_ref, x_ref, g1_ref, n2_ref, sc_ref, sh_ref, *rest):
    x1_ref, h2_ref = rest[-2:]
    x1 = x_ref[...] + g1_ref[...] * jnp.dot(m_ref[...], w_ref[...], preferred_element_type=F32)
    x1_ref[...] = x1
    r = x1 * lax.rsqrt(jnp.mean(x1 * x1, axis=-1, keepdims=True) + EPS)
    h2_ref[...] = ((r * n2_ref[...]) * (1.0 + sc_ref[...]) + sh_ref[...]).astype(BF16)


def _out_proj(merged, w_out, layer, x, x_row0, norm2_g, mod4, is_ctx, prev):
    tm = 512
    rows, row0 = _part(is_ctx)
    xoff, ooff = x_row0 // tm, row0 // tm
    mrow = _mod_row_fn(is_ctx, tm)
    orow = lambda i: (ooff + i, 0)
    mspec = lambda k: pl.BlockSpec((None, None, 1, D), lambda i, k=k: (mrow(i), k, 0, 0))
    prev = list(prev) if prev is not None else []
    return pl.pallas_call(
        _out_proj_kernel,
        grid=(rows // tm,),
        in_specs=[
            pl.BlockSpec((tm, D), orow),
            pl.BlockSpec((None, D, D), lambda i: (layer, 0, 0)),
            pl.BlockSpec((tm, D), lambda i: (xoff + i, 0)),
            mspec(2),
            pl.BlockSpec((1, D), lambda i: (0, 0)),
            mspec(4),
            mspec(3),
        ] + [ANY_SPEC] * len(prev),
        out_specs=[pl.BlockSpec((tm, D), orow), pl.BlockSpec((tm, D), orow)],
        out_shape=[jax.ShapeDtypeStruct((M, D), F32), jax.ShapeDtypeStruct((M, D), BF16)],
        input_output_aliases={7: 0, 8: 1} if prev else {},
        compiler_params=_cparams(("arbitrary",)),
        name="out_proj",
    )(merged, w_out, x, mod4, norm2_g.reshape(1, D), mod4, mod4, *prev)


def _segment_sum(x, seg):
    parts = []
    for s in range(ROUTE_TILE // seg):
        tot = jnp.sum(x[:, s * seg:(s + 1) * seg], axis=1, keepdims=True)
        parts.append(jnp.broadcast_to(tot, (x.shape[0], seg)))
    return parts[0] if len(parts) == 1 else jnp.concatenate(parts, axis=1)


def _segment_cumsum(x, seg):
    pos = lax.broadcasted_iota(jnp.int32, x.shape, 1) % seg
    sh = 1
    while sh < seg:
        x = x + jnp.where(pos >= sh, pltpu.roll(x, sh, 1), 0.0)
        sh *= 2
    return x


def _router_kernel(h_ref, wr_ref, slot_ref, gcol_ref, *, seg):
    cap = CAPACITY_FACTOR * seg // N_EXPERTS
    logits = lax.dot_general(wr_ref[...], h_ref[...], (((1,), (1,)), ((), ())),
                             preferred_element_type=F32)
    p = jnp.exp(logits - jnp.max(logits, axis=0, keepdims=True))
    aff = p / jnp.sum(p, axis=0, keepdims=True)
    bits = pltpu.bitcast(aff, jnp.int32)

    def bis(i, thr):
        cand = thr | jnp.left_shift(jnp.int32(1), 29 - i)
        cnt = _segment_sum(jnp.where(bits >= cand, 1.0, 0.0), seg)
        return jnp.where(cnt >= cap, cand, thr)

    thr = lax.fori_loop(0, 30, bis, jnp.zeros(bits.shape, jnp.int32))
    gt = jnp.where(bits > thr, 1.0, 0.0)
    eq = jnp.where(bits == thr, 1.0, 0.0)
    need = cap - _segment_sum(gt, seg)
    eq_rank = _segment_cumsum(eq, seg)
    sel = gt + eq * jnp.where(eq_rank <= need, 1.0, 0.0)
    rank = _segment_cumsum(sel, seg)
    seg_id = lax.broadcasted_iota(jnp.int32, bits.shape, 1) // seg
    slot = seg_id * cap + rank.astype(jnp.int32) - 1
    slot = jnp.where(sel > 0.0, slot, -1)
    slot_ref[...] = slot

    iota_c = lax.broadcasted_iota(jnp.int32, (ROUTE_SLOTS, ROUTE_TILE), 0)
    for e in range(N_EXPERTS):
        hit = iota_c == slot[e:e + 1, :]
        gcol_ref[e] = jnp.sum(jnp.where(hit, aff[e:e + 1, :], 0.0), axis=1, keepdims=True)


def _router(h2, w_router_t):
    def call(seg, tile0, ntiles):
        return pl.pallas_call(
            functools.partial(_router_kernel, seg=seg),
            grid=(ntiles,),
            in_specs=[
                pl.BlockSpec((ROUTE_TILE, D), lambda i: (tile0 + i, 0)),
                pl.BlockSpec((N_EXPERTS, D), lambda i: (0, 0)),
            ],
            out_specs=[
                pl.BlockSpec((None, N_EXPERTS, ROUTE_TILE), lambda i: (i, 0, 0)),
                pl.BlockSpec((N_EXPERTS, None, ROUTE_SLOTS, 1), lambda i: (0, i, 0, 0)),
            ],
            out_shape=[
                jax.ShapeDtypeStruct((ntiles, N_EXPERTS, ROUTE_TILE), jnp.int32),
                jax.ShapeDtypeStruct((N_EXPERTS, ntiles, ROUTE_SLOTS, 1), F32),
            ],
            compiler_params=_cparams(("arbitrary",)),
            name="router",
        )(h2, w_router_t)

    nct = M_CTX // ROUTE_TILE
    slot_c, g_c = call(SEQ, 0, nct)
    slot_l, g_l = call(DEC_SEQ, nct, N_ROUTE_TILES - nct)
    return jnp.concatenate([slot_c, slot_l], axis=0), jnp.concatenate([g_c, g_l], axis=1)


CTX_CAP = CAPACITY_FACTOR * SEQ // N_EXPERTS
CTX_SETS_PER_TILE = ROUTE_TILE // SEQ
N_CTX_TILES = M_CTX // ROUTE_TILE


def _dispatch_ctx_kernel(h_ref, slot_ref, o_ref):
    base = (pl.program_id(0) % CTX_SETS_PER_TILE) * CTX_CAP
    slot = slot_ref[...]
    iota_c = lax.broadcasted_iota(jnp.int32, (CTX_CAP, SEQ), 0) + base
    onehot = jnp.concatenate(
        [jnp.where(iota_c == slot[e:e + 1, :], 1.0, 0.0).astype(BF16) for e in range(N_EXPERTS)], axis=0)
    res = jnp.dot(onehot, h_ref[...], preferred_element_type=F32).astype(BF16)
    for e in range(N_EXPERTS):
        o_ref[e] = res[e * CTX_CAP:(e + 1) * CTX_CAP, :]


def _dispatch_lat_kernel(h_ref, slot_ref, prev_ref, o_ref):
    e = pl.program_id(1)
    srow = slot_ref[pl.ds(e, 1), :]
    iota_c = lax.broadcasted_iota(jnp.int32, (ROUTE_SLOTS, ROUTE_TILE), 0)
    onehot = jnp.where(iota_c == srow, 1.0, 0.0).astype(BF16)
    o_ref[...] = jnp.dot(onehot, h_ref[...], preferred_element_type=F32).astype(BF16)


def _dispatch(h2, slot_et):
    xg = pl.pallas_call(
        _dispatch_ctx_kernel,
        grid=(BATCH,),
        in_specs=[
            pl.BlockSpec((SEQ, D), lambda s: (s, 0)),
            pl.BlockSpec((None, N_EXPERTS, SEQ), lambda s: (s // CTX_SETS_PER_TILE, 0, s % CTX_SETS_PER_TILE)),
        ],
        out_specs=pl.BlockSpec((N_EXPERTS, None, CTX_CAP, D), lambda s: (0, s, 0, 0)),
        out_shape=jax.ShapeDtypeStruct((N_EXPERTS, SLOTS_PER_EXPERT // CTX_CAP, CTX_CAP, D), BF16),
        compiler_params=_cparams(("arbitrary",)),
        name="moe_dispatch_ctx",
    )(h2, slot_et)
    return pl.pallas_call(
        _dispatch_lat_kernel,
        grid=(N_ROUTE_TILES - N_CTX_TILES, N_EXPERTS),
        in_specs=[
            pl.BlockSpec((ROUTE_TILE, D), lambda i, e: (N_CTX_TILES + i, 0)),
            pl.BlockSpec((None, N_EXPERTS, ROUTE_TILE), lambda i, e: (N_CTX_TILES + i, 0, 0)),
            ANY_SPEC,
        ],
        out_specs=pl.BlockSpec((None, ROUTE_SLOTS, D), lambda i, e: (e, N_CTX_TILES + i, 0)),
        out_shape=jax.ShapeDtypeStruct((N_EXPERTS, SLOTS_PER_EXPERT, D), BF16),
        input_output_aliases={2: 0},
        compiler_params=_cparams(("arbitrary", "arbitrary")),
        name="moe_dispatch",
    )(h2, slot_et, xg.reshape(N_EXPERTS, SLOTS_PER_EXPERT, D))


def _ffn_kernel(x_ref, w1_ref, w3_ref, w2_ref, g_ref, o_ref, acc_ref, *, tn):
    f = pl.program_id(2)
    nf = pl.num_programs(2)

    @pl.when(f == 0)
    def _():
        acc_ref[...] = jnp.zeros(acc_ref.shape, F32)

    x = x_ref[...]
    a = jnp.dot(x, w1_ref[...].astype(BF16), preferred_element_type=F32)
    b = jnp.dot(x, w3_ref[...].astype(BF16), preferred_element_type=F32)
    hmid = ((a * _sigmoid(a)) * b).astype(BF16)
    w2 = w2_ref[...].astype(BF16)
    for n in range(D // tn):
        acc_ref[:, n * tn:(n + 1) * tn] += jnp.dot(hmid, w2[:, n * tn:(n + 1) * tn],
                                                  preferred_element_type=F32)

    @pl.when(f == nf - 1)
    def _():
        o_ref[...] = (acc_ref[...] * g_ref[...]).astype(BF16)


def _ffn(xg, w1, w3, w2, gcol, layer):
    tr, tf, tn = 1024, 256, 512
    nr = SLOTS_PER_EXPERT // tr
    return pl.pallas_call(
        functools.partial(_ffn_kernel, tn=tn),
        grid=(N_EXPERTS, nr, EXPERT_FF // tf),
        in_specs=[
            pl.BlockSpec((None, tr, D), lambda e, r, f: (e, r, 0)),
            pl.BlockSpec((None, None, D, tf), lambda e, r, f: (layer, e, 0, f)),
            pl.BlockSpec((None, None, D, tf), lambda e, r, f: (layer, e, 0, f)),
            pl.BlockSpec((None, None, tf, D), lambda e, r, f: (layer, e, f, 0)),
            pl.BlockSpec((None, tr, 1), lambda e, r, f: (e, r, 0)),
        ],
        out_specs=pl.BlockSpec((None, tr, D), lambda e, r, f: (e, r, 0)),
        out_shape=jax.ShapeDtypeStruct((N_EXPERTS, SLOTS_PER_EXPERT, D), BF16),
        scratch_shapes=[pltpu.VMEM((tr, D), F32)],
        compiler_params=_cparams(("arbitrary", "arbitrary", "arbitrary")),
        name="moe_ffn",
    )(xg, w1, w3, w2, gcol)


def _combine_ctx_kernel(slot_ref, y_ref, x_ref, g2_ref, o_ref):
    base = (pl.program_id(0) % CTX_SETS_PER_TILE) * CTX_CAP
    slot = slot_ref[...]
    iota_c = lax.broadcasted_iota(jnp.int32, (SEQ, CTX_CAP), 1) + base
    onehot = jnp.concatenate(
        [jnp.where(iota_c == slot[:, e:e + 1], 1.0, 0.0).astype(BF16) for e in range(N_EXPERTS)], axis=1)
    y = y_ref[...].reshape(N_EXPERTS * CTX_CAP, y_ref.shape[-1])
    moe = jnp.dot(onehot, y, preferred_element_type=F32)
    o_ref[...] = x_ref[...] + g2_ref[...] * moe


def _combine_lat_kernel(slot_ref, y_ref, x_ref, g2_ref, prev_ref, o_ref, *, tr):
    r = pl.program_id(2)
    slot = slot_ref[pl.ds(pl.multiple_of(r * tr, tr), tr), :]
    iota_c = lax.broadcasted_iota(jnp.int32, (tr, ROUTE_SLOTS), 1)
    onehot = jnp.concatenate(
        [jnp.where(iota_c == slot[:, e:e + 1], 1.0, 0.0).astype(BF16) for e in range(N_EXPERTS)], axis=1)
    y = y_ref[...].reshape(N_EXPERTS * ROUTE_SLOTS, y_ref.shape[-1])
    moe = jnp.dot(onehot, y, preferred_element_type=F32)
    o_ref[...] = x_ref[...] + g2_ref[...] * moe


def _combine(slot_te, yg, x1, mod4):
    tn = 1024
    x2 = pl.pallas_call(
        _combine_ctx_kernel,
        grid=(BATCH, D // tn),
        in_specs=[
            pl.BlockSpec((None, SEQ, N_EXPERTS), lambda s, j: (s // CTX_SETS_PER_TILE, s % CTX_SETS_PER_TILE, 0)),
            pl.BlockSpec((N_EXPERTS, None, CTX_CAP, tn), lambda s, j: (0, s, 0, j)),
            pl.BlockSpec((SEQ, tn), lambda s, j: (s, j)),
            pl.BlockSpec((None, None, 1, tn), lambda s, j: (0, 5, 0, j)),
        ],
        out_specs=pl.BlockSpec((SEQ, tn), lambda s, j: (s, j)),
        out_shape=jax.ShapeDtypeStruct((M, D), F32),
        compiler_params=_cparams(("arbitrary", "arbitrary")),
        name="moe_combine_ctx",
    )(slot_te, yg.reshape(N_EXPERTS, SLOTS_PER_EXPERT // CTX_CAP, CTX_CAP, D), x1, mod4)
    tr = 512
    nr = ROUTE_TILE // tr
    nct = N_CTX_TILES
    return pl.pallas_call(
        functools.partial(_combine_lat_kernel, tr=tr),
        grid=(N_ROUTE_TILES - nct, D // tn, nr),
        in_specs=[
            pl.BlockSpec((None, ROUTE_TILE, N_EXPERTS), lambda i, j, r: (nct + i, 0, 0)),
            pl.BlockSpec((N_EXPERTS, None, ROUTE_SLOTS, tn), lambda i, j, r: (0, nct + i, 0, j)),
            pl.BlockSpec((tr, tn), lambda i, j, r: ((nct + i) * nr + r, j)),
            pl.BlockSpec((None, None, 1, tn), lambda i, j, r: (1 + i, 5, 0, j)),
            ANY_SPEC,
        ],
        out_specs=pl.BlockSpec((tr, tn), lambda i, j, r: ((nct + i) * nr + r, j)),
        out_shape=jax.ShapeDtypeStruct((M, D), F32),
        input_output_aliases={4: 0},
        compiler_params=_cparams(("arbitrary", "arbitrary", "arbitrary")),
        name="moe_combine",
    )(slot_te, yg.reshape(N_EXPERTS, N_ROUTE_TILES, ROUTE_SLOTS, D), x1, mod4, x2)


def _final_norm_kernel(x_ref, g_ref, o_ref):
    x = x_ref[...]
    o_ref[...] = (x * lax.rsqrt(jnp.mean(x * x, axis=-1, keepdims=True) + EPS)) * g_ref[...]


def _final_norm(x, g, is_ctx):
    tm = 512
    rows, row0 = _part(is_ctx)
    off = row0 // tm
    return pl.pallas_call(
        _final_norm_kernel,
        grid=(rows // tm,),
        in_specs=[pl.BlockSpec((tm, D), lambda i: (off + i, 0)), pl.BlockSpec((1, D), lambda i: (0, 0))],
        out_specs=pl.BlockSpec((tm, D), lambda i: (i, 0)),
        out_shape=jax.ShapeDtypeStruct((rows, D), F32),
        compiler_params=_cparams(("arbitrary",)),
        name="final_norm",
    )(x, g.reshape(1, D))


def _layer(x_parts, mod, lp, layer, lam_init, tables, cache_k4, cache_v4, st_re, st_im, new_kv):
    (xc, xc_row0), (xl, xl_row0) = x_parts
    rope_cos, rope_sin, dft_ctx, dft_lat, cs, perms = tables
    mod4 = mod.reshape(N_MOD_ROWS, 6, 1, D)
    w_in = lp['stacked']['w_in']
    h, u, ua_ctx, new_k, new_v = _norm_proj(xc, xc_row0, lp['norm1_g'], mod4, w_in, layer, True, new_kv)
    h, u, ua_lat = _norm_proj(xl, xl_row0, lp['norm1_g'], mod4, w_in, layer, False, (h, u))

    wb, wc, a_re, a_im = _s5_params(lp)
    ns = S5_GROUPS * S5_STATE
    ngc = BATCH // SUBLANES
    zeros = jnp.zeros((ngc, 2, SUBLANES, ns), F32)
    y_ctx, fin_re, fin_im = _s5(ua_ctx, perms, wb, wc, a_re, a_im, zeros, zeros)
    h0_re = st_re.reshape(DEC_BATCH, 2, ns).transpose(1, 0, 2)[None]
    h0_im = st_im.reshape(DEC_BATCH, 2, ns).transpose(1, 0, 2)[None]
    y_lat, _, _ = _s5(ua_lat, perms, wb, wc, a_re, a_im, h0_re, h0_im)
    d_row = lp['s5_d'].reshape(1, S5_WIDTH)
    w_glu = lp['w_glu'].astype(BF16)
    b_glu = lp['b_glu'].reshape(1, S5_WIDTH)
    ya = _glu(y_ctx, u, d_row, w_glu, b_glu, True, SEQ, None)
    ya = _glu(y_lat, u, d_row, w_glu, b_glu, False, 512, ya)

    yb = _fnet(u, dft_ctx, cs, True, SEQ, None)
    yb = _fnet(u, dft_lat, cs, False, 512, yb)

    lam = (jnp.exp(jnp.sum(lp['lam_q1'].astype(F32) * lp['lam_k1'].astype(F32)))
           - jnp.exp(jnp.sum(lp['lam_q2'].astype(F32) * lp['lam_k2'].astype(F32))) + lam_init).reshape(1)
    subln = lp['subln_g'].reshape(1, VALUE_DIM)
    yc = _attention(u, lam, subln, lam_init, True, SEQ, N_HEADS, None)
    yc = _attention(u, lam, subln, lam_init, False, DEC_SEQ, 1, yc,
                    ctx=(cache_k4, cache_v4, layer, rope_cos, rope_sin))

    st = lp['stacked']
    merged = _merge(h, ya, yb, yc, st['w_gate'], st['b_gate'], st['w_branch_a'], st['w_branch_b'],
                    st['w_branch_c'], layer)
    x1, h2 = _out_proj(merged, st['w_out'], layer, xc, xc_row0, lp['norm2_g'], mod4, True, None)
    x1, h2 = _out_proj(merged, st['w_out'], layer, xl, xl_row0, lp['norm2_g'], mod4, False, (x1, h2))

    slot_et, gcol = _router(h2, lp['w_router'].T.astype(BF16))
    xg = _dispatch(h2, slot_et)
    yg = _ffn(xg, lp['moe_w1_all'], lp['moe_w3_all'], lp['moe_w2_all'],
              gcol.reshape(N_EXPERTS, SLOTS_PER_EXPERT, 1), layer)
    x2 = _combine(slot_et.transpose(0, 2, 1), yg, x1, mod4)

    fin_shape = (ngc, 2, SUBLANES, S5_GROUPS, S5_STATE)
    s_re = fin_re.reshape(fin_shape).transpose(0, 2, 1, 3, 4).reshape(BATCH, 2, S5_GROUPS, S5_STATE)
    s_im = fin_im.reshape(fin_shape).transpose(0, 2, 1, 3, 4).reshape(BATCH, 2, S5_GROUPS, S5_STATE)
    return x2, (new_k, new_v), s_re, s_im


def kernel(x_prompt, x_sample, cache_k, cache_v, state_s5_re, state_s5_im, c, c_ctx, norm1_g, norm2_g, final_norm_g, w_ada, b_ada, w_in, s5_lam_re, s5_lam_im, s5_log_dt, s5_b_re, s5_b_im, s5_c_re, s5_c_im, s5_d, w_glu, b_glu, lam_q1, lam_k1, lam_q2, lam_k2, subln_g, w_branch_a, w_branch_b, w_branch_c, w_gate, b_gate, w_out, w_router, moe_w1, moe_w3, moe_w2):
    params = dict(norm1_g=norm1_g, norm2_g=norm2_g,
                  s5_lam_re=s5_lam_re, s5_lam_im=s5_lam_im, s5_log_dt=s5_log_dt,
                  s5_b_re=s5_b_re, s5_b_im=s5_b_im, s5_c_re=s5_c_re, s5_c_im=s5_c_im, s5_d=s5_d,
                  w_glu=w_glu, b_glu=b_glu, lam_q1=lam_q1, lam_k1=lam_k1, lam_q2=lam_q2, lam_k2=lam_k2,
                  subln_g=subln_g, w_router=w_router)
    stacked = dict(w_in=w_in.astype(BF16), w_gate=w_gate.astype(BF16), b_gate=b_gate.reshape(DEPTH, 1, 3 * D),
                   w_branch_a=w_branch_a.astype(BF16), w_branch_b=w_branch_b.astype(BF16),
                   w_branch_c=w_branch_c.astype(BF16), w_out=w_out.astype(BF16))
    cc = jnp.zeros((N_MOD_ROWS, D), F32).at[0].set(c_ctx).at[1:1 + DEC_BATCH].set(c)
    mods = _ada(cc, w_ada, b_ada)
    rope_cos, rope_sin = _rope_tables(DEC_SEQ)
    c_c, s_c = _dft_tables(FNET_GROUP_CH)
    cs = jnp.concatenate([c_c, s_c], axis=1).astype(BF16)
    tables = (rope_cos, rope_sin, _fnet_tables(SEQ), _fnet_tables(DEC_SEQ), cs, _s5_permutations())
    cache_k4 = cache_k.reshape(DEC_BATCH, DEPTH, PAST_LEN, QK_WIDTH)
    cache_v4 = cache_v.reshape(DEC_BATCH, DEPTH, PAST_LEN, V_WIDTH)
    x_parts = ((x_prompt.reshape(M_CTX, D), 0), (x_sample.reshape(M_LAT, D), 0))
    new_kv = None
    s_res, s_ims = [], []
    for l in range(DEPTH):
        lp = {name: arr[l] for name, arr in params.items()}
        lp['moe_w1_all'], lp['moe_w3_all'], lp['moe_w2_all'] = moe_w1, moe_w3, moe_w2
        lp['stacked'] = stacked
        lam_init = 0.8 - 0.6 * math.exp(-0.3 * l)
        x, new_kv, s_re, s_im = _layer(x_parts, mods[l], lp, l, lam_init, tables, cache_k4, cache_v4,
                                       state_s5_re[:, l], state_s5_im[:, l], new_kv)
        x_parts = ((x, 0), (x, M_CTX))
        s_res.append(s_re)
        s_ims.append(s_im)
    y_prompt = _final_norm(x, final_norm_g, True).reshape(BATCH, SEQ, D)
    y_sample = _final_norm(x, final_norm_g, False).reshape(DEC_BATCH, DEC_SEQ, D)
    kv_shape = (BATCH, DEPTH, SEQ, N_HEADS, VALUE_DIM)
    return (y_prompt, y_sample, new_kv[0].reshape(kv_shape), new_kv[1].reshape(kv_shape),
            jnp.stack(s_res, axis=1), jnp.stack(s_ims, axis=1))
```

```python
import functools
import math

import jax
import jax.numpy as jnp
import numpy as np
from jax import lax
from jax.experimental import pallas as pl
from jax.experimental.pallas import tpu as pltpu

F32 = jnp.float32
BF16 = jnp.bfloat16

D = 2048
BATCH, SEQ = 32, 256
DEC_BATCH, DEC_SEQ = 8, 2048
DEPTH = 2
PAST_LEN = 256
GRID_W = 64
EPS = 1e-6
S5_GROUP_CH, S5_GROUPS, S5_STATE = 16, 32, 64
S5_WIDTH = S5_GROUPS * S5_GROUP_CH
FNET_GROUPS, FNET_GROUP_CH = 4, 128
FNET_WIDTH = FNET_GROUPS * FNET_GROUP_CH
N_HEADS, HEAD_DIM = 8, 64
VALUE_DIM = 2 * HEAD_DIM
QK_WIDTH = N_HEADS * 2 * HEAD_DIM
V_WIDTH = N_HEADS * VALUE_DIM
IN_WIDTH = S5_WIDTH + FNET_WIDTH + 2 * QK_WIDTH + V_WIDTH
ROPE_BASE = 10000.0
ROPE_AXIS_DIM = HEAD_DIM // 2
N_EXPERTS = 16
EXPERT_FF = 2048
CAPACITY_FACTOR = 2

M_CTX = BATCH * SEQ
M_LAT = DEC_BATCH * DEC_SEQ
M = M_CTX + M_LAT
N_MOD_ROWS = 16

LANES = 128
SUBLANES = 8
VMEM_LIMIT = 56 * 1024 * 1024

ROUTE_TILE = 2048
ROUTE_SLOTS = CAPACITY_FACTOR * ROUTE_TILE // N_EXPERTS
N_ROUTE_TILES = M // ROUTE_TILE
SLOTS_PER_EXPERT = N_ROUTE_TILES * ROUTE_SLOTS

S5_BLK = 4
S5_BLK_STATES = 512

ANY_SPEC = pl.BlockSpec(memory_space=pl.ANY)


def _cparams(sem):
    return pltpu.CompilerParams(dimension_semantics=sem, vmem_limit_bytes=VMEM_LIMIT)


def _sigmoid(x):
    return 1.0 / (1.0 + jnp.exp(-x))


def _part(is_ctx):
    return (M_CTX, 0) if is_ctx else (M_LAT, M_CTX)


def _mod_row_fn(is_ctx, tm):
    if is_ctx:
        return lambda i: 0
    return lambda i: 1 + i // (DEC_SEQ // tm)


def _ada_kernel(c_ref, w_ref, b_ref, o_ref):
    c = c_ref[...]
    s = (c * _sigmoid(c)).astype(BF16)
    o_ref[...] = jnp.dot(s, w_ref[...].astype(BF16), preferred_element_type=F32) + b_ref[...]


def _ada(cc, w_ada, b_ada):
    tn = 1024
    n = 6 * D
    return pl.pallas_call(
        _ada_kernel,
        grid=(DEPTH, n // tn),
        in_specs=[
            pl.BlockSpec((N_MOD_ROWS, D), lambda l, j: (0, 0)),
            pl.BlockSpec((None, D, tn), lambda l, j: (l, 0, j)),
            pl.BlockSpec((None, 1, tn), lambda l, j: (l, 0, j)),
        ],
        out_specs=pl.BlockSpec((None, N_MOD_ROWS, tn), lambda l, j: (l, 0, j)),
        out_shape=jax.ShapeDtypeStruct((DEPTH, N_MOD_ROWS, n), F32),
        compiler_params=_cparams(("arbitrary", "arbitrary")),
        name="ada",
    )(cc, w_ada, b_ada.reshape(DEPTH, 1, n))


NP_TM, NP_TN = 1024, 1024


NP_NJ = IN_WIDTH // NP_TN


def _norm_proj_kernel(x_ref, g_ref, sc_ref, sh_ref, w_ref, *rest, is_ctx, n_alias, tm):
    h_ref, u_ref, ua_ref, ha_ref, hb_ref = rest[n_alias:]
    i = pl.program_id(0)
    j = pl.program_id(1)
    q = tm // NP_NJ
    rows = pl.ds(pl.multiple_of(j * q, q), q)

    def norm_quarter(dst_ref):
        x = x_ref[rows, :]
        r = x * lax.rsqrt(jnp.mean(x * x, axis=-1, keepdims=True) + EPS)
        dst_ref[rows, :] = ((r * g_ref[...]) * (1.0 + sc_ref[...]) + sh_ref[...]).astype(BF16)

    def project(src_ref):
        h_ref[rows, :] = src_ref[rows, :]
        u_ref[...] = jnp.dot(src_ref[...], w_ref[...], preferred_element_type=F32)

    @pl.when(i == 0)
    def _():
        norm_quarter(ha_ref)

    @pl.when((i > 0) & (i % 2 == 0))
    def _():
        norm_quarter(ha_ref)
        project(hb_ref)

    @pl.when(i % 2 == 1)
    def _():
        norm_quarter(hb_ref)
        project(ha_ref)

    @pl.when((i > 0) & (j == 0))
    def _():
        if is_ctx:
            for b in range(tm // SEQ):
                ua_ref[:, b * S5_WIDTH:(b + 1) * S5_WIDTH] = u_ref[b * SEQ:(b + 1) * SEQ, :S5_WIDTH]
        else:
            ua_ref[...] = u_ref[:, :S5_WIDTH]


def _norm_proj(x, x_row0, g, mod4, w_in, layer, is_ctx, prev):
    tm, tn = NP_TM, NP_TN
    assert S5_WIDTH <= tn
    rows, row0 = _part(is_ctx)
    n_i = rows // tm
    xoff, ooff = x_row0 // tm, row0 // tm
    mrow = _mod_row_fn(is_ctx, tm)
    cur = lambda i: jnp.minimum(i, n_i - 1)
    prv = lambda i: jnp.maximum(i - 1, 0)
    prev = list(prev) if prev is not None else []
    in_specs = [
        pl.BlockSpec((tm, D), lambda i, j: (xoff + cur(i), 0)),
        pl.BlockSpec((1, D), lambda i, j: (0, 0)),
        pl.BlockSpec((None, None, 1, D), lambda i, j: (mrow(cur(i)), 1, 0, 0)),
        pl.BlockSpec((None, None, 1, D), lambda i, j: (mrow(cur(i)), 0, 0, 0)),
        pl.BlockSpec((None, D, tn), lambda i, j: (layer, 0, j)),
    ] + [ANY_SPEC] * len(prev)
    out_specs = [
        pl.BlockSpec((tm, D), lambda i, j: (ooff + prv(i), 0)),
        pl.BlockSpec((tm, tn), lambda i, j: (ooff + prv(i), j)),
    ]
    out_shape = [jax.ShapeDtypeStruct((M, D), BF16), jax.ShapeDtypeStruct((M, IN_WIDTH), F32)]
    if is_ctx:
        nbt = tm // SEQ
        out_specs.append(pl.BlockSpec((SEQ, nbt * S5_WIDTH), lambda i, j: (0, prv(i))))
        out_shape.append(jax.ShapeDtypeStruct((SEQ, BATCH * S5_WIDTH), F32))
        aliases = {}
    else:
        tpb = DEC_SEQ // tm
        out_specs.append(pl.BlockSpec((tm, S5_WIDTH), lambda i, j: (prv(i) % tpb, prv(i) // tpb)))
        out_shape.append(jax.ShapeDtypeStruct((DEC_SEQ, DEC_BATCH * S5_WIDTH), F32))
        aliases = {5: 0, 6: 1}
    return pl.pallas_call(
        functools.partial(_norm_proj_kernel, is_ctx=is_ctx, n_alias=len(prev), tm=tm),
        grid=(n_i + 1, NP_NJ),
        in_specs=in_specs,
        out_specs=out_specs,
        out_shape=out_shape,
        scratch_shapes=[pltpu.VMEM((tm, D), BF16), pltpu.VMEM((tm, D), BF16)],
        input_output_aliases=aliases,
        compiler_params=_cparams(("arbitrary", "arbitrary")),
        name="norm_proj",
    )(x, g.reshape(1, D), mod4, mod4, w_in, *prev)


S5_CHUNK = 64


def _s5_kernel(u_ref, pin_ref, pout_ref, wb_ref, wc_ref, are_ref, aim_ref, h0re_ref, h0im_ref,
               y_ref, finre_ref, finim_ref, xs0_ref, xs1_ref, xs2_ref, xs3_ref, stre_ref, stim_ref):
    c = pl.program_id(2)
    nc = pl.num_programs(2)
    bs = S5_BLK_STATES
    chunk = S5_CHUNK
    xs_refs = (xs0_ref, xs1_ref, xs2_ref, xs3_ref)

    @pl.when(c == 0)
    def _():
        stre_ref[...] = h0re_ref[...]
        stim_ref[...] = h0im_ref[...]

    ustack = jnp.concatenate(
        [u_ref[:, j * S5_WIDTH:(j + 1) * S5_WIDTH] for j in range(SUBLANES)], axis=0).astype(BF16)
    up = jnp.dot(pin_ref[...], ustack, preferred_element_type=F32).astype(BF16)

    for blk in range(S5_BLK):
        xs_refs[blk][...] = jnp.dot(up[:, blk * LANES:(blk + 1) * LANES], wb_ref[blk],
                                    preferred_element_type=F32)

    for blk in range(S5_BLK):
        xs_ref = xs_refs[blk]
        ar = are_ref[:, blk * bs:(blk + 1) * bs]
        ai = aim_ref[:, blk * bs:(blk + 1) * bs]
        sr = stre_ref[:, blk * bs:(blk + 1) * bs]
        si = stim_ref[:, blk * bs:(blk + 1) * bs]
        for i in range(chunk):
            r8 = slice(i * SUBLANES, (i + 1) * SUBLANES)
            nr = ar * sr - ai * si + xs_ref[r8, 0:bs]
            ni = ar * si + ai * sr + xs_ref[r8, bs:2 * bs]
            xs_ref[r8, 0:bs] = nr
            xs_ref[r8, bs:2 * bs] = ni
            sr, si = nr, ni
        stre_ref[:, blk * bs:(blk + 1) * bs] = sr
        stim_ref[:, blk * bs:(blk + 1) * bs] = si

    y = jnp.concatenate(
        [jnp.dot(xs_refs[blk][...].astype(BF16), wc_ref[blk], preferred_element_type=F32)
         for blk in range(S5_BLK)], axis=1)
    y_hi = y.astype(BF16)
    y_lo = (y - y_hi.astype(F32)).astype(BF16)
    ys = (jnp.dot(pout_ref[...], y_hi, preferred_element_type=F32)
          + jnp.dot(pout_ref[...], y_lo, preferred_element_type=F32))
    for j in range(SUBLANES):
        y_ref[:, j * S5_WIDTH:(j + 1) * S5_WIDTH] = ys[j * chunk:(j + 1) * chunk, :]

    @pl.when(c == nc - 1)
    def _():
        finre_ref[...] = stre_ref[...]
        finim_ref[...] = stim_ref[...]


def _s5_permutations():
    chunk = S5_CHUNK
    rows = chunk * SUBLANES
    r_out = lax.broadcasted_iota(jnp.int32, (rows, rows), 0)
    r_in = lax.broadcasted_iota(jnp.int32, (rows, rows), 1)
    step, seq = r_out // SUBLANES, r_out % SUBLANES
    fwd = r_in == seq * chunk + step
    bwd = r_in == seq * chunk + (chunk - 1 - step)
    p = jnp.stack([fwd, bwd]).astype(BF16)
    return p, p.transpose(0, 2, 1)


def _s5(u2d, perms, wb, wc, a_re, a_im, h0_re, h0_im):
    t_len = u2d.shape[0]
    ng = u2d.shape[1] // (SUBLANES * S5_WIDTH)
    chunk = S5_CHUNK
    nc = t_len // chunk
    rows = chunk * SUBLANES
    ns = S5_BLK * S5_BLK_STATES
    pin, pout = perms

    def ceff(d, c):
        return jnp.where(d == 0, c, nc - 1 - c)

    return pl.pallas_call(
        _s5_kernel,
        grid=(ng, 2, nc),
        in_specs=[
            pl.BlockSpec((chunk, SUBLANES * S5_WIDTH), lambda g, d, c: (ceff(d, c), g)),
            pl.BlockSpec((None, rows, rows), lambda g, d, c: (d, 0, 0)),
            pl.BlockSpec((None, rows, rows), lambda g, d, c: (d, 0, 0)),
            pl.BlockSpec((None, S5_BLK, LANES, 2 * S5_BLK_STATES), lambda g, d, c: (d, 0, 0, 0)),
            pl.BlockSpec((None, S5_BLK, 2 * S5_BLK_STATES, LANES), lambda g, d, c: (d, 0, 0, 0)),
            pl.BlockSpec((None, SUBLANES, ns), lambda g, d, c: (d, 0, 0)),
            pl.BlockSpec((None, SUBLANES, ns), lambda g, d, c: (d, 0, 0)),
            pl.BlockSpec((None, None, SUBLANES, ns), lambda g, d, c: (g, d, 0, 0)),
            pl.BlockSpec((None, None, SUBLANES, ns), lambda g, d, c: (g, d, 0, 0)),
        ],
        out_specs=[
            pl.BlockSpec((None, chunk, SUBLANES * S5_WIDTH), lambda g, d, c: (d, ceff(d, c), g)),
            pl.BlockSpec((None, None, SUBLANES, ns), lambda g, d, c: (g, d, 0, 0)),
            pl.BlockSpec((None, None, SUBLANES, ns), lambda g, d, c: (g, d, 0, 0)),
        ],
        out_shape=[
            jax.ShapeDtypeStruct((2,) + u2d.shape, F32),
            jax.ShapeDtypeStruct((ng, 2, SUBLANES, ns), F32),
            jax.ShapeDtypeStruct((ng, 2, SUBLANES, ns), F32),
        ],
        scratch_shapes=[pltpu.VMEM((rows, 2 * S5_BLK_STATES), F32)] * S5_BLK + [
            pltpu.VMEM((SUBLANES, ns), F32),
            pltpu.VMEM((SUBLANES, ns), F32),
        ],
        compiler_params=_cparams(("arbitrary", "arbitrary", "arbitrary")),
        name="s5_scan",
    )(u2d, pin, pout, wb, wc, a_re, a_im, h0_re, h0_im)


def _s5_params(lp):
    lam_re = lp['s5_lam_re'].astype(F32)
    lam_im = lp['s5_lam_im'].astype(F32)
    dt = jnp.exp(lp['s5_log_dt'].astype(F32))[..., None]
    b_re = lp['s5_b_re'].astype(F32)
    b_im = lp['s5_b_im'].astype(F32)
    c_re = lp['s5_c_re'].astype(F32)
    c_im = lp['s5_c_im'].astype(F32)
    mag = jnp.exp(lam_re * dt)
    ab_re = mag * jnp.cos(lam_im * dt)
    ab_im = mag * jnp.sin(lam_im * dt)
    den = lam_re * lam_re + lam_im * lam_im
    f_re = ((ab_re - 1.0) * lam_re + ab_im * lam_im) / den
    f_im = (ab_im * lam_re - (ab_re - 1.0) * lam_im) / den
    bb_re = f_re[..., None] * b_re - f_im[..., None] * b_im
    bb_im = f_re[..., None] * b_im + f_im[..., None] * b_re
    gpb = S5_GROUPS // S5_BLK
    eye = jnp.eye(gpb, dtype=F32)

    def pack_b(bb):
        bb = bb.reshape(2, S5_BLK, gpb, S5_STATE, S5_GROUP_CH)
        w = jnp.einsum('dbgph,gk->dbghkp', bb, eye)
        return w.reshape(2, S5_BLK, gpb * S5_GROUP_CH, gpb * S5_STATE)

    def pack_c(cc):
        cc = cc.reshape(2, S5_BLK, gpb, S5_GROUP_CH, S5_STATE)
        w = jnp.einsum('dbghp,gk->dbkpgh', cc, eye)
        return w.reshape(2, S5_BLK, gpb * S5_STATE, gpb * S5_GROUP_CH)

    wb = jnp.concatenate([pack_b(bb_re), pack_b(bb_im)], axis=-1).astype(BF16)
    wc = jnp.concatenate([pack_c(c_re), -pack_c(c_im)], axis=-2).astype(BF16)
    ns = S5_GROUPS * S5_STATE
    a_re = jnp.broadcast_to(ab_re.reshape(2, 1, ns), (2, SUBLANES, ns))
    a_im = jnp.broadcast_to(ab_im.reshape(2, 1, ns), (2, SUBLANES, ns))
    return wb, wc, a_re, a_im


def _gelu_tanh(x):
    return 0.5 * x * (1.0 + jnp.tanh(math.sqrt(2.0 / math.pi) * (x + 0.044715 * (x * x * x))))


def _glu_kernel(y_ref, u_ref, d_ref, w_ref, b_ref, *rest):
    o_ref = rest[-1]
    y = y_ref[0] + y_ref[1] + d_ref[...] * u_ref[...]
    y = _gelu_tanh(y)
    z = jnp.dot(y.astype(BF16), w_ref[...], preferred_element_type=F32) + b_ref[...]
    o_ref[...] = (y * _sigmoid(z)).astype(BF16)


def _glu(y_dirs, u, d_row, w_glu, b_glu, is_ctx, tt, prev):
    t_len = y_dirs.shape[1]
    nb = y_dirs.shape[2] // S5_WIDTH
    ntt = t_len // tt
    off = _part(is_ctx)[1] // tt
    prev = [prev] if prev is not None else []
    return pl.pallas_call(
        _glu_kernel,
        grid=(nb, ntt),
        in_specs=[
            pl.BlockSpec((2, tt, S5_WIDTH), lambda b, t: (0, t, b)),
            pl.BlockSpec((tt, S5_WIDTH), lambda b, t: (off + b * ntt + t, 0)),
            pl.BlockSpec((1, S5_WIDTH), lambda b, t: (0, 0)),
            pl.BlockSpec((S5_WIDTH, S5_WIDTH), lambda b, t: (0, 0)),
            pl.BlockSpec((1, S5_WIDTH), lambda b, t: (0, 0)),
        ] + [ANY_SPEC] * len(prev),
        out_specs=pl.BlockSpec((tt, S5_WIDTH), lambda b, t: (off + b * ntt + t, 0)),
        out_shape=jax.ShapeDtypeStruct((M, S5_WIDTH), BF16),
        input_output_aliases={5: 0} if prev else {},
        compiler_params=_cparams(("arbitrary", "arbitrary")),
        name="s5_glu",
    )(y_dirs, u, d_row, w_glu, b_glu, *prev)


def _fnet_kernel(x_ref, cs_ref, dft_ref, *rest, t_len):
    o_ref, z_ref = rest[-2:]

    @pl.when(pl.program_id(1) == 0)
    def _():
        x = x_ref[...].astype(BF16)
        for g in range(FNET_GROUPS):
            z = jnp.dot(x[:, g * LANES:(g + 1) * LANES], cs_ref[...], preferred_element_type=F32)
            z_ref[0:t_len, g * LANES:(g + 1) * LANES] = z[:, :LANES].astype(BF16)
            z_ref[t_len:2 * t_len, g * LANES:(g + 1) * LANES] = z[:, LANES:].astype(BF16)

    scale = 1.0 / math.sqrt(t_len * FNET_GROUP_CH)
    o_ref[...] = (jnp.dot(dft_ref[...], z_ref[...], preferred_element_type=F32) * scale).astype(BF16)


def _dft_tables(n):
    j = lax.broadcasted_iota(jnp.int32, (n, n), 0)
    k = lax.broadcasted_iota(jnp.int32, (n, n), 1)
    ang = ((j * k) % n).astype(F32) * (2.0 * math.pi / n)
    return jnp.cos(ang), jnp.sin(ang)


def _fnet_tables(t_len):
    j = np.arange(t_len, dtype=np.int64)
    ang = ((j[:, None] * j[None, :]) % t_len).astype(np.float64) * (2.0 * math.pi / t_len)
    return jnp.asarray(np.concatenate([np.cos(ang), -np.sin(ang)], axis=1), dtype=BF16)


def _fnet(u, dft, cs, is_ctx, tr, prev):
    t_len = dft.shape[0]
    rows, row0 = _part(is_ctx)
    nb = rows // t_len
    ntr = t_len // tr
    off = row0 // t_len
    offr = row0 // tr
    col = S5_WIDTH // FNET_WIDTH
    prev = [prev] if prev is not None else []
    return pl.pallas_call(
        functools.partial(_fnet_kernel, t_len=t_len),
        grid=(nb, ntr),
        in_specs=[
            pl.BlockSpec((t_len, FNET_WIDTH), lambda b, r: (off + b, col)),
            pl.BlockSpec((FNET_GROUP_CH, 2 * FNET_GROUP_CH), lambda b, r: (0, 0)),
            pl.BlockSpec((tr, 2 * t_len), lambda b, r: (r, 0)),
        ] + [ANY_SPEC] * len(prev),
        out_specs=pl.BlockSpec((tr, FNET_WIDTH), lambda b, r: (offr + b * ntr + r, 0)),
        out_shape=jax.ShapeDtypeStruct((M, FNET_WIDTH), BF16),
        scratch_shapes=[pltpu.VMEM((2 * t_len, FNET_WIDTH), BF16)],
        input_output_aliases={3: 0} if prev else {},
        compiler_params=_cparams(("arbitrary", "arbitrary")),
        name="fnet",
    )(u, cs, dft, *prev)


def _rope(x, cos, sin):
    half = ROPE_AXIS_DIM // 2
    lane = lax.broadcasted_iota(jnp.int32, x.shape, 1)
    up = pltpu.roll(x, LANES - half, 1)
    dn = pltpu.roll(x, half, 1)
    rot = jnp.where((lane % ROPE_AXIS_DIM) < half, -up, dn)
    return x * cos + rot * sin


ATTN_KEY_CHUNK = 512
ATTN_ROW_BLOCK = 256


def _attn_kernel(lam_ref, q_ref, k_ref, v_ref, *rest, tq, t_len, hb, has_ctx, lam_init, n_alias, emit_kv):
    pos = 0
    if has_ctx:
        ck_ref, cv_ref, cos_ref, sin_ref = rest[:4]
        pos = 4
    g_ref = rest[pos]
    pos += 1 + n_alias
    o_ref = rest[pos]
    pos += 1
    if emit_kv:
        ko_ref, vo_ref = rest[pos:pos + 2]
        pos += 2
    kk_ref, vv_ref, qs_ref, s_ref = rest[pos:]
    qi = pl.program_id(2)
    if emit_kv:
        @pl.when(qi == 0)
        def _():
            ko_ref[...] = k_ref[...]
            vo_ref[...] = v_ref[...]
    n_keys = kk_ref.shape[1]
    kc, rb = ATTN_KEY_CHUNK, ATTN_ROW_BLOCK
    chunks = [(c0, min(c0 + kc, n_keys)) for c0 in range(0, n_keys, kc)]

    @pl.when(qi == 0)
    def _():
        for h in range(hb):
            hs = slice(h * LANES, (h + 1) * LANES)
            k = k_ref[:, hs]
            if has_ctx:
                k = _rope(k, cos_ref[...], sin_ref[...])
                kk_ref[h, t_len:, :] = ck_ref[:, hs].astype(BF16)
                vv_ref[h, t_len:, 0:VALUE_DIM] = cv_ref[:, hs].astype(BF16)
            kk_ref[h, 0:t_len, :] = k.astype(BF16)
            vv_ref[h, 0:t_len, 0:VALUE_DIM] = v_ref[:, hs].astype(BF16)
            vv_ref[h, :, VALUE_DIM:] = jnp.ones((n_keys, VALUE_DIM), BF16)

    lane = lax.broadcasted_iota(jnp.int32, (tq, LANES), 1)
    for h in range(hb):
        hs = slice(h * LANES, (h + 1) * LANES)
        q = q_ref[:, hs]
        if has_ctx:
            r0 = pl.multiple_of(qi * tq, tq)
            q = _rope(q, cos_ref[pl.ds(r0, tq), :], sin_ref[pl.ds(r0, tq), :])
        q = q * (HEAD_DIM ** -0.5)
        qs_ref[0, :, hs] = jnp.where(lane < HEAD_DIM, q, 0.0).astype(BF16)
        qs_ref[1, :, hs] = jnp.where(lane >= HEAD_DIM, q, 0.0).astype(BF16)

    lam = lam_ref[0]
    gain = g_ref[...]
    dn = (((1,), (1,)), ((), ()))

    unit = 0
    for r in range(tq // rb):
        rows = slice(r * rb, (r + 1) * rb)
        for h in range(hb):
            hs = slice(h * LANES, (h + 1) * LANES)
            o = None
            for n in range(2):
                slot = unit % 2
                unit += 1
                qn = qs_ref[n, rows, hs]
                mrun = None
                for c0, c1 in chunks:
                    s_c = lax.dot_general(qn, kk_ref[h, c0:c1, :], dn, preferred_element_type=F32)
                    s_ref[slot, :, c0:c1] = s_c
                    for l0 in range(0, c1 - c0, LANES):
                        mc = s_c[:, l0:l0 + LANES]
                        mrun = mc if mrun is None else jnp.maximum(mrun, mc)
                m_b = jnp.broadcast_to(jnp.max(mrun, axis=-1, keepdims=True), (rb, LANES))
                oa = None
                for c0, c1 in chunks:
                    p = jnp.concatenate(
                        [jnp.exp(s_ref[slot, :, l0:l0 + LANES] - m_b) for l0 in range(c0, c1, LANES)],
                        axis=1).astype(BF16)
                    part = jnp.dot(p, vv_ref[h, c0:c1, :], preferred_element_type=F32)
                    oa = part if oa is None else oa + part
                on = oa[:, :VALUE_DIM] * (1.0 / oa[:, VALUE_DIM:VALUE_DIM + 1])
                o = on if n == 0 else o - lam * on
            o = o * lax.rsqrt(jnp.mean(o * o, axis=-1, keepdims=True) + EPS)
            o_ref[rows, hs] = ((o * gain) * (1.0 - lam_init)).astype(BF16)


def _attention(u, lam, subln_g, lam_init, is_ctx, tq, hb, prev, ctx=None, kv_out=None):
    rows, row0 = _part(is_ctx)
    t_len = SEQ if is_ctx else DEC_SEQ
    nb = rows // t_len
    nq = t_len // tq
    off = row0 // t_len
    offq = row0 // tq
    width = hb * LANES
    qcol = (S5_WIDTH + FNET_WIDTH) // width
    kcol = qcol + QK_WIDTH // width
    vcol = kcol + QK_WIDTH // width
    has_ctx = ctx is not None
    n_keys = t_len + (PAST_LEN if has_ctx else 0)
    assert tq % ATTN_ROW_BLOCK == 0 and n_keys % (2 * LANES) == 0
    in_specs = [
        pl.BlockSpec(memory_space=pltpu.SMEM),
        pl.BlockSpec((tq, width), lambda b, h, i: (offq + b * nq + i, qcol + h)),
        pl.BlockSpec((t_len, width), lambda b, h, i: (off + b, kcol + h)),
        pl.BlockSpec((t_len, width), lambda b, h, i: (off + b, vcol + h)),
    ]
    args = [lam, u, u, u]
    if has_ctx:
        ck, cv, layer, cos, sin = ctx
        in_specs += [
            pl.BlockSpec((None, None, PAST_LEN, width), lambda b, h, i: (b, layer, 0, h)),
            pl.BlockSpec((None, None, PAST_LEN, width), lambda b, h, i: (b, layer, 0, h)),
            pl.BlockSpec((t_len, LANES), lambda b, h, i: (0, 0)),
            pl.BlockSpec((t_len, LANES), lambda b, h, i: (0, 0)),
        ]
        args += [ck, cv, cos, sin]
    in_specs.append(pl.BlockSpec((1, LANES), lambda b, h, i: (0, 0)))
    args.append(subln_g)
    aliases = {}
    n_fixed = len(args)
    if prev is not None:
        aliases[len(args)] = 0
        in_specs.append(ANY_SPEC)
        args.append(prev)
    out_specs = [pl.BlockSpec((tq, width), lambda b, h, i: (offq + b * nq + i, h))]
    out_shape = [jax.ShapeDtypeStruct((M, V_WIDTH), BF16)]
    if kv_out is not None:
        kv_layer, prev_kv = kv_out
        assert hb == N_HEADS and tq == t_len and nb == BATCH
        out_specs += [pl.BlockSpec((None, None, SEQ, QK_WIDTH), lambda b, h, i: (b, kv_layer, 0, 0)),
                      pl.BlockSpec((None, None, SEQ, V_WIDTH), lambda b, h, i: (b, kv_layer, 0, 0))]
        out_shape += [jax.ShapeDtypeStruct((BATCH, DEPTH, SEQ, QK_WIDTH), F32),
                      jax.ShapeDtypeStruct((BATCH, DEPTH, SEQ, V_WIDTH), F32)]
        if prev_kv is not None:
            for o_idx, arr in enumerate(prev_kv):
                aliases[len(args)] = 1 + o_idx
                in_specs.append(ANY_SPEC)
                args.append(arr)
    kern = functools.partial(_attn_kernel, tq=tq, t_len=t_len, hb=hb, has_ctx=has_ctx, lam_init=lam_init,
                             n_alias=len(args) - n_fixed, emit_kv=kv_out is not None)
    return pl.pallas_call(
        kern,
        grid=(nb, N_HEADS // hb, nq),
        in_specs=in_specs,
        out_specs=out_specs,
        out_shape=out_shape,
        scratch_shapes=[
            pltpu.VMEM((hb, n_keys, LANES), BF16),
            pltpu.VMEM((hb, n_keys, 2 * VALUE_DIM), BF16),
            pltpu.VMEM((2, tq, width), BF16),
            pltpu.VMEM((2, ATTN_ROW_BLOCK, n_keys), F32),
        ],
        input_output_aliases=aliases,
        compiler_params=_cparams(("arbitrary", "arbitrary", "arbitrary")),
        name="diff_attn",
    )(*args)


def _rope_tables(t_len):
    rows = t_len // GRID_W
    pos_row = jnp.broadcast_to(jnp.arange(rows, dtype=F32)[:, None], (rows, GRID_W)).reshape(-1)
    pos_col = jnp.broadcast_to(jnp.arange(GRID_W, dtype=F32)[None, :], (rows, GRID_W)).reshape(-1)
    inv = ROPE_BASE ** (-jnp.arange(0, ROPE_AXIS_DIM, 2, dtype=F32) / ROPE_AXIS_DIM)
    ang_r = pos_row[:, None] * inv
    ang_c = pos_col[:, None] * inv
    cos = jnp.concatenate([jnp.cos(ang_r), jnp.cos(ang_r), jnp.cos(ang_c), jnp.cos(ang_c)], axis=-1)
    sin = jnp.concatenate([jnp.sin(ang_r), jnp.sin(ang_r), jnp.sin(ang_c), jnp.sin(ang_c)], axis=-1)
    return jnp.concatenate([cos, cos], axis=-1), jnp.concatenate([sin, sin], axis=-1)


def _merge_kernel(h_ref, ya_ref, yb_ref, yc_ref, wga_ref, wgb_ref, wgc_ref, bga_ref, bgb_ref, bgc_ref,
                  wa_ref, wb_ref, wc_ref, o_ref):
    h = h_ref[...]
    acc = None
    for wg, bg, y, w in ((wga_ref, bga_ref, ya_ref, wa_ref), (wgb_ref, bgb_ref, yb_ref, wb_ref),
                         (wgc_ref, bgc_ref, yc_ref, wc_ref)):
        gate = _sigmoid(jnp.dot(h, wg[...], preferred_element_type=F32) + bg[...])
        term = gate * jnp.dot(y[...], w[...], preferred_element_type=F32)
        acc = term if acc is None else acc + term
    o_ref[...] = acc.astype(BF16)


def _merge(h, ya, yb, yc, w_gate, b_gate, w_a, w_b, w_c, layer):
    tm, tn = 1024, 512
    nj = D // tn
    row = lambda i, j: (i, 0)
    wspecs = [pl.BlockSpec((None, D, tn), lambda i, j, k=k: (layer, 0, k * nj + j)) for k in range(3)]
    bspecs = [pl.BlockSpec((None, 1, tn), lambda i, j, k=k: (layer, 0, k * nj + j)) for k in range(3)]
    return pl.pallas_call(
        _merge_kernel,
        grid=(M // tm, nj),
        in_specs=[
            pl.BlockSpec((tm, D), row),
            pl.BlockSpec((tm, S5_WIDTH), row),
            pl.BlockSpec((tm, FNET_WIDTH), row),
            pl.BlockSpec((tm, V_WIDTH), row),
            *wspecs, *bspecs,
            pl.BlockSpec((None, S5_WIDTH, tn), lambda i, j: (layer, 0, j)),
            pl.BlockSpec((None, FNET_WIDTH, tn), lambda i, j: (layer, 0, j)),
            pl.BlockSpec((None, V_WIDTH, tn), lambda i, j: (layer, 0, j)),
        ],
        out_specs=pl.BlockSpec((tm, tn), lambda i, j: (i, j)),
        out_shape=jax.ShapeDtypeStruct((M, D), BF16),
        compiler_params=_cparams(("arbitrary", "arbitrary")),
        name="merge",
    )(h, ya, yb, yc, w_gate, w_gate, w_gate, b_gate, b_gate, b_gate, w_a, w_b, w_c)


def _out_proj_kernel(m_ref, w_ref, x_ref, g1_ref, n2_ref, sc_ref, sh_ref, *rest):
    x1_ref, h2_ref = rest[-2:]
    x1 = x_ref[...] + g1_ref[...] * jnp.dot(m_ref[...], w_ref[...], preferred_element_type=F32)
    x1_ref[...] = x1
    r = x1 * lax.rsqrt(jnp.mean(x1 * x1, axis=-1, keepdims=True) + EPS)
    h2_ref[...] = ((r * n2_ref[...]) * (1.0 + sc_ref[...]) + sh_ref[...]).astype(BF16)


def _out_proj(merged, w_out, layer, x, x_row0, norm2_g, mod4, is_ctx, prev):
    tm = 512
    rows, row0 = _part(is_ctx)
    xoff, ooff = x_row0 // tm, row0 // tm
    mrow = _mod_row_fn(is_ctx, tm)
    orow = lambda i: (ooff + i, 0)
    mspec = lambda k: pl.BlockSpec((None, None, 1, D), lambda i, k=k: (mrow(i), k, 0, 0))
    prev = list(prev) if prev is not None else []
    return pl.pallas_call(
        _out_proj_kernel,
        grid=(rows // tm,),
        in_specs=[
            pl.BlockSpec((tm, D), orow),
            pl.BlockSpec((None, D, D), lambda i: (layer, 0, 0)),
            pl.BlockSpec((tm, D), lambda i: (xoff + i, 0)),
            mspec(2),
            pl.BlockSpec((1, D), lambda i: (0, 0)),
            mspec(4),
            mspec(3),
        ] + [ANY_SPEC] * len(prev),
        out_specs=[pl.BlockSpec((tm, D), orow), pl.BlockSpec((tm, D), orow)],
        out_shape=[jax.ShapeDtypeStruct((M, D), F32), jax.ShapeDtypeStruct((M, D), BF16)],
        input_output_aliases={7: 0, 8: 1} if prev else {},
        compiler_params=_cparams(("arbitrary",)),
        name="out_proj",
    )(merged, w_out, x, mod4, norm2_g.reshape(1, D), mod4, mod4, *prev)


def _segment_sum(x, seg):
    parts = []
    for s in range(ROUTE_TILE // seg):
        tot = jnp.sum(x[:, s * seg:(s + 1) * seg], axis=1, keepdims=True)
        parts.append(jnp.broadcast_to(tot, (x.shape[0], seg)))
    return parts[0] if len(parts) == 1 else jnp.concatenate(parts, axis=1)


def _segment_cumsum(x, seg):
    pos = lax.broadcasted_iota(jnp.int32, x.shape, 1) % seg
    sh = 1
    while sh < seg:
        x = x + jnp.where(pos >= sh, pltpu.roll(x, sh, 1), 0.0)
        sh *= 2
    return x


def _router_kernel(h_ref, wr_ref, slot_ref, gcol_ref, *, seg):
    cap = CAPACITY_FACTOR * seg // N_EXPERTS
    logits = lax.dot_general(wr_ref[...], h_ref[...], (((1,), (1,)), ((), ())),
                             preferred_element_type=F32)
    p = jnp.exp(logits - jnp.max(logits, axis=0, keepdims=True))
    aff = p / jnp.sum(p, axis=0, keepdims=True)
    bits = pltpu.bitcast(aff, jnp.int32)

    def bis(i, thr):
        cand = thr | jnp.left_shift(jnp.int32(1), 29 - i)
        cnt = _segment_sum(jnp.where(bits >= cand, 1.0, 0.0), seg)
        return jnp.where(cnt >= cap, cand, thr)

    thr = lax.fori_loop(0, 30, bis, jnp.zeros(bits.shape, jnp.int32))
    gt = jnp.where(bits > thr, 1.0, 0.0)
    eq = jnp.where(bits == thr, 1.0, 0.0)
    need = cap - _segment_sum(gt, seg)
    eq_rank = _segment_cumsum(eq, seg)
    sel = gt + eq * jnp.where(eq_rank <= need, 1.0, 0.0)
    rank = _segment_cumsum(sel, seg)
    seg_id = lax.broadcasted_iota(jnp.int32, bits.shape, 1) // seg
    slot = seg_id * cap + rank.astype(jnp.int32) - 1
    slot = jnp.where(sel > 0.0, slot, -1)
    slot_ref[...] = slot

    iota_c = lax.broadcasted_iota(jnp.int32, (ROUTE_SLOTS, ROUTE_TILE), 0)
    for e in range(N_EXPERTS):
        hit = iota_c == slot[e:e + 1, :]
        gcol_ref[e] = jnp.sum(jnp.where(hit, aff[e:e + 1, :], 0.0), axis=1, keepdims=True)


def _router(h2, w_router_t):
    def call(seg, tile0, ntiles):
        return pl.pallas_call(
            functools.partial(_router_kernel, seg=seg),
            grid=(ntiles,),
            in_specs=[
                pl.BlockSpec((ROUTE_TILE, D), lambda i: (tile0 + i, 0)),
                pl.BlockSpec((N_EXPERTS, D), lambda i: (0, 0)),
            ],
            out_specs=[
                pl.BlockSpec((None, N_EXPERTS, ROUTE_TILE), lambda i: (i, 0, 0)),
                pl.BlockSpec((N_EXPERTS, None, ROUTE_SLOTS, 1), lambda i: (0, i, 0, 0)),
            ],
            out_shape=[
                jax.ShapeDtypeStruct((ntiles, N_EXPERTS, ROUTE_TILE), jnp.int32),
                jax.ShapeDtypeStruct((N_EXPERTS, ntiles, ROUTE_SLOTS, 1), F32),
            ],
            compiler_params=_cparams(("arbitrary",)),
            name="router",
        )(h2, w_router_t)

    nct = M_CTX // ROUTE_TILE
    slot_c, g_c = call(SEQ, 0, nct)
    slot_l, g_l = call(DEC_SEQ, nct, N_ROUTE_TILES - nct)
    return jnp.concatenate([slot_c, slot_l], axis=0), jnp.concatenate([g_c, g_l], axis=1)


CTX_CAP = CAPACITY_FACTOR * SEQ // N_EXPERTS
CTX_SETS_PER_TILE = ROUTE_TILE // SEQ
N_CTX_TILES = M_CTX // ROUTE_TILE


def _dispatch_ctx_kernel(h_ref, slot_ref, o_ref):
    base = (pl.program_id(0) % CTX_SETS_PER_TILE) * CTX_CAP
    slot = slot_ref[...]
    iota_c = lax.broadcasted_iota(jnp.int32, (CTX_CAP, SEQ), 0) + base
    onehot = jnp.concatenate(
        [jnp.where(iota_c == slot[e:e + 1, :], 1.0, 0.0).astype(BF16) for e in range(N_EXPERTS)], axis=0)
    res = jnp.dot(onehot, h_ref[...], preferred_element_type=F32).astype(BF16)
    for e in range(N_EXPERTS):
        o_ref[e] = res[e * CTX_CAP:(e + 1) * CTX_CAP, :]


def _dispatch_lat_kernel(h_ref, slot_ref, prev_ref, o_ref):
    e = pl.program_id(1)
    srow = slot_ref[pl.ds(e, 1), :]
    iota_c = lax.broadcasted_iota(jnp.int32, (ROUTE_SLOTS, ROUTE_TILE), 0)
    onehot = jnp.where(iota_c == srow, 1.0, 0.0).astype(BF16)
    o_ref[...] = jnp.dot(onehot, h_ref[...], preferred_element_type=F32).astype(BF16)


def _dispatch(h2, slot_et):
    xg = pl.pallas_call(
        _dispatch_ctx_kernel,
        grid=(BATCH,),
        in_specs=[
            pl.BlockSpec((SEQ, D), lambda s: (s, 0)),
            pl.BlockSpec((None, N_EXPERTS, SEQ), lambda s: (s // CTX_SETS_PER_TILE, 0, s % CTX_SETS_PER_TILE)),
        ],
        out_specs=pl.BlockSpec((N_EXPERTS, None, CTX_CAP, D), lambda s: (0, s, 0, 0)),
        out_shape=jax.ShapeDtypeStruct((N_EXPERTS, SLOTS_PER_EXPERT // CTX_CAP, CTX_CAP, D), BF16),
        compiler_params=_cparams(("arbitrary",)),
        name="moe_dispatch_ctx",
    )(h2, slot_et)
    return pl.pallas_call(
        _dispatch_lat_kernel,
        grid=(N_ROUTE_TILES - N_CTX_TILES, N_EXPERTS),
        in_specs=[
            pl.BlockSpec((ROUTE_TILE, D), lambda i, e: (N_CTX_TILES + i, 0)),
            pl.BlockSpec((None, N_EXPERTS, ROUTE_TILE), lambda i, e: (N_CTX_TILES + i, 0, 0)),
            ANY_SPEC,
        ],
        out_specs=pl.BlockSpec((None, ROUTE_SLOTS, D), lambda i, e: (e, N_CTX_TILES + i, 0)),
        out_shape=jax.ShapeDtypeStruct((N_EXPERTS, SLOTS_PER_EXPERT, D), BF16),
        input_output_aliases={2: 0},
        compiler_params=_cparams(("arbitrary", "arbitrary")),
        name="moe_dispatch",
    )(h2, slot_et, xg.reshape(N_EXPERTS, SLOTS_PER_EXPERT, D))


def _ffn_kernel(x_ref, w1_ref, w3_ref, w2_ref, g_ref, o_ref, acc_ref, *, tn):
    f = pl.program_id(2)
    nf = pl.num_programs(2)

    @pl.when(f == 0)
    def _():
        acc_ref[...] = jnp.zeros(acc_ref.shape, F32)

    x = x_ref[...]
    a = jnp.dot(x, w1_ref[...].astype(BF16), preferred_element_type=F32)
    b = jnp.dot(x, w3_ref[...].astype(BF16), preferred_element_type=F32)
    hmid = ((a * _sigmoid(a)) * b).astype(BF16)
    w2 = w2_ref[...].astype(BF16)
    for n in range(D // tn):
        acc_ref[:, n * tn:(n + 1) * tn] += jnp.dot(hmid, w2[:, n * tn:(n + 1) * tn],
                                                  preferred_element_type=F32)

    @pl.when(f == nf - 1)
    def _():
        o_ref[...] = (acc_ref[...] * g_ref[...]).astype(BF16)


def _ffn(xg, w1, w3, w2, gcol, layer):
    tr, tf, tn = 1024, 256, 512
    nr = SLOTS_PER_EXPERT // tr
    return pl.pallas_call(
        functools.partial(_ffn_kernel, tn=tn),
        grid=(N_EXPERTS, nr, EXPERT_FF // tf),
        in_specs=[
            pl.BlockSpec((None, tr, D), lambda e, r, f: (e, r, 0)),
            pl.BlockSpec((None, None, D, tf), lambda e, r, f: (layer, e, 0, f)),
            pl.BlockSpec((None, None, D, tf), lambda e, r, f: (layer, e, 0, f)),
            pl.BlockSpec((None, None, tf, D), lambda e, r, f: (layer, e, f, 0)),
            pl.BlockSpec((None, tr, 1), lambda e, r, f: (e, r, 0)),
        ],
        out_specs=pl.BlockSpec((None, tr, D), lambda e, r, f: (e, r, 0)),
        out_shape=jax.ShapeDtypeStruct((N_EXPERTS, SLOTS_PER_EXPERT, D), BF16),
        scratch_shapes=[pltpu.VMEM((tr, D), F32)],
        compiler_params=_cparams(("arbitrary", "arbitrary", "arbitrary")),
        name="moe_ffn",
    )(xg, w1, w3, w2, gcol)


def _combine_ctx_kernel(slot_ref, y_ref, x_ref, g2_ref, o_ref):
    base = (pl.program_id(0) % CTX_SETS_PER_TILE) * CTX_CAP
    slot = slot_ref[...]
    iota_c = lax.broadcasted_iota(jnp.int32, (SEQ, CTX_CAP), 1) + base
    onehot = jnp.concatenate(
        [jnp.where(iota_c == slot[:, e:e + 1], 1.0, 0.0).astype(BF16) for e in range(N_EXPERTS)], axis=1)
    y = y_ref[...].reshape(N_EXPERTS * CTX_CAP, y_ref.shape[-1])
    moe = jnp.dot(onehot, y, preferred_element_type=F32)
    o_ref[...] = x_ref[...] + g2_ref[...] * moe


def _combine_lat_kernel(slot_ref, y_ref, x_ref, g2_ref, prev_ref, o_ref, *, tr):
    r = pl.program_id(2)
    slot = slot_ref[pl.ds(pl.multiple_of(r * tr, tr), tr), :]
    iota_c = lax.broadcasted_iota(jnp.int32, (tr, ROUTE_SLOTS), 1)
    onehot = jnp.concatenate(
        [jnp.where(iota_c == slot[:, e:e + 1], 1.0, 0.0).astype(BF16) for e in range(N_EXPERTS)], axis=1)
    y = y_ref[...].reshape(N_EXPERTS * ROUTE_SLOTS, y_ref.shape[-1])
    moe = jnp.dot(onehot, y, preferred_element_type=F32)
    o_ref[...] = x_ref[...] + g2_ref[...] * moe


def _combine(slot_te, yg, x1, mod4):
    tn = D
    x2 = pl.pallas_call(
        _combine_ctx_kernel,
        grid=(BATCH, D // tn),
        in_specs=[
            pl.BlockSpec((None, SEQ, N_EXPERTS), lambda s, j: (s // CTX_SETS_PER_TILE, s % CTX_SETS_PER_TILE, 0)),
            pl.BlockSpec((N_EXPERTS, None, CTX_CAP, tn), lambda s, j: (0, s, 0, j)),
            pl.BlockSpec((SEQ, tn), lambda s, j: (s, j)),
            pl.BlockSpec((None, None, 1, tn), lambda s, j: (0, 5, 0, j)),
        ],
        out_specs=pl.BlockSpec((SEQ, tn), lambda s, j: (s, j)),
        out_shape=jax.ShapeDtypeStruct((M, D), F32),
        compiler_params=_cparams(("arbitrary", "arbitrary")),
        name="moe_combine_ctx",
    )(slot_te, yg.reshape(N_EXPERTS, SLOTS_PER_EXPERT // CTX_CAP, CTX_CAP, D), x1, mod4)
    tn = 1024
    tr = 512
    nr = ROUTE_TILE // tr
    nct = N_CTX_TILES
    return pl.pallas_call(
        functools.partial(_combine_lat_kernel, tr=tr),
        grid=(N_ROUTE_TILES - nct, D // tn, nr),
        in_specs=[
            pl.BlockSpec((None, ROUTE_TILE, N_EXPERTS), lambda i, j, r: (nct + i, 0, 0)),
            pl.BlockSpec((N_EXPERTS, None, ROUTE_SLOTS, tn), lambda i, j, r: (0, nct + i, 0, j)),
            pl.BlockSpec((tr, tn), lambda i, j, r: ((nct + i) * nr + r, j)),
            pl.BlockSpec((None, None, 1, tn), lambda i, j, r: (1 + i, 5, 0, j)),
            ANY_SPEC,
        ],
        out_specs=pl.BlockSpec((tr, tn), lambda i, j, r: ((nct + i) * nr + r, j)),
        out_shape=jax.ShapeDtypeStruct((M, D), F32),
        input_output_aliases={4: 0},
        compiler_params=_cparams(("arbitrary", "arbitrary", "arbitrary")),
        name="moe_combine",
    )(slot_te, yg.reshape(N_EXPERTS, N_ROUTE_TILES, ROUTE_SLOTS, D), x1, mod4, x2)


def _final_norm_kernel(x_ref, g_ref, o_ref):
    x = x_ref[...]
    o_ref[...] = (x * lax.rsqrt(jnp.mean(x * x, axis=-1, keepdims=True) + EPS)) * g_ref[...]


def _final_norm(x, g, is_ctx):
    tm = 512
    rows, row0 = _part(is_ctx)
    off = row0 // tm
    return pl.pallas_call(
        _final_norm_kernel,
        grid=(rows // tm,),
        in_specs=[pl.BlockSpec((tm, D), lambda i: (off + i, 0)), pl.BlockSpec((1, D), lambda i: (0, 0))],
        out_specs=pl.BlockSpec((tm, D), lambda i: (i, 0)),
        out_shape=jax.ShapeDtypeStruct((rows, D), F32),
        compiler_params=_cparams(("arbitrary",)),
        name="final_norm",
    )(x, g.reshape(1, D))


def _layer(x_parts, mod, lp, layer, lam_init, tables, cache_k4, cache_v4, st_re, st_im, new_kv):
    (xc, xc_row0), (xl, xl_row0) = x_parts
    rope_cos, rope_sin, dft_ctx, dft_lat, cs, perms = tables
    mod4 = mod.reshape(N_MOD_ROWS, 6, 1, D)
    w_in = lp['stacked']['w_in']
    h, u, ua_ctx = _norm_proj(xc, xc_row0, lp['norm1_g'], mod4, w_in, layer, True, None)
    h, u, ua_lat = _norm_proj(xl, xl_row0, lp['norm1_g'], mod4, w_in, layer, False, (h, u))

    wb, wc, a_re, a_im = _s5_params(lp)
    ns = S5_GROUPS * S5_STATE
    ngc = BATCH // SUBLANES
    zeros = jnp.zeros((ngc, 2, SUBLANES, ns), F32)
    y_ctx, fin_re, fin_im = _s5(ua_ctx, perms, wb, wc, a_re, a_im, zeros, zeros)
    h0_re = st_re.reshape(DEC_BATCH, 2, ns).transpose(1, 0, 2)[None]
    h0_im = st_im.reshape(DEC_BATCH, 2, ns).transpose(1, 0, 2)[None]
    y_lat, _, _ = _s5(ua_lat, perms, wb, wc, a_re, a_im, h0_re, h0_im)
    d_row = lp['s5_d'].reshape(1, S5_WIDTH)
    w_glu = lp['w_glu'].astype(BF16)
    b_glu = lp['b_glu'].reshape(1, S5_WIDTH)
    ya = _glu(y_ctx, u, d_row, w_glu, b_glu, True, SEQ, None)
    ya = _glu(y_lat, u, d_row, w_glu, b_glu, False, 512, ya)

    yb = _fnet(u, dft_ctx, cs, True, SEQ, None)
    yb = _fnet(u, dft_lat, cs, False, 512, yb)

    lam = (jnp.exp(jnp.sum(lp['lam_q1'].astype(F32) * lp['lam_k1'].astype(F32)))
           - jnp.exp(jnp.sum(lp['lam_q2'].astype(F32) * lp['lam_k2'].astype(F32))) + lam_init).reshape(1)
    subln = lp['subln_g'].reshape(1, VALUE_DIM)
    yc, new_k, new_v = _attention(u, lam, subln, lam_init, True, SEQ, N_HEADS, None, kv_out=(layer, new_kv))
    (yc,) = _attention(u, lam, subln, lam_init, False, DEC_SEQ, 1, yc,
                    ctx=(cache_k4, cache_v4, layer, rope_cos, rope_sin))

    st = lp['stacked']
    merged = _merge(h, ya, yb, yc, st['w_gate'], st['b_gate'], st['w_branch_a'], st['w_branch_b'],
                    st['w_branch_c'], layer)
    x1, h2 = _out_proj(merged, st['w_out'], layer, xc, xc_row0, lp['norm2_g'], mod4, True, None)
    x1, h2 = _out_proj(merged, st['w_out'], layer, xl, xl_row0, lp['norm2_g'], mod4, False, (x1, h2))

    slot_et, gcol = _router(h2, lp['w_router'].T.astype(BF16))
    xg = _dispatch(h2, slot_et)
    yg = _ffn(xg, lp['moe_w1_all'], lp['moe_w3_all'], lp['moe_w2_all'],
              gcol.reshape(N_EXPERTS, SLOTS_PER_EXPERT, 1), layer)
    x2 = _combine(slot_et.transpose(0, 2, 1), yg, x1, mod4)

    fin_shape = (ngc, 2, SUBLANES, S5_GROUPS, S5_STATE)
    s_re = fin_re.reshape(fin_shape).transpose(0, 2, 1, 3, 4).reshape(BATCH, 2, S5_GROUPS, S5_STATE)
    s_im = fin_im.reshape(fin_shape).transpose(0, 2, 1, 3, 4).reshape(BATCH, 2, S5_GROUPS, S5_STATE)
    return x2, (new_k, new_v), s_re, s_im


def kernel(x_prompt, x_sample, cache_k, cache_v, state_s5_re, state_s5_im, c, c_ctx, norm1_g, norm2_g, final_norm_g, w_ada, b_ada, w_in, s5_lam_re, s5_lam_im, s5_log_dt, s5_b_re, s5_b_im, s5_c_re, s5_c_im, s5_d, w_glu, b_glu, lam_q1, lam_k1, lam_q2, lam_k2, subln_g, w_branch_a, w_branch_b, w_branch_c, w_gate, b_gate, w_out, w_router, moe_w1, moe_w3, moe_w2):
    params = dict(norm1_g=norm1_g, norm2_g=norm2_g,
                  s5_lam_re=s5_lam_re, s5_lam_im=s5_lam_im, s5_log_dt=s5_log_dt,
                  s5_b_re=s5_b_re, s5_b_im=s5_b_im, s5_c_re=s5_c_re, s5_c_im=s5_c_im, s5_d=s5_d,
                  w_glu=w_glu, b_glu=b_glu, lam_q1=lam_q1, lam_k1=lam_k1, lam_q2=lam_q2, lam_k2=lam_k2,
                  subln_g=subln_g, w_router=w_router)
    stacked = dict(w_in=w_in.astype(BF16), w_gate=w_gate.astype(BF16), b_gate=b_gate.reshape(DEPTH, 1, 3 * D),
                   w_branch_a=w_branch_a.astype(BF16), w_branch_b=w_branch_b.astype(BF16),
                   w_branch_c=w_branch_c.astype(BF16), w_out=w_out.astype(BF16))
    cc = jnp.zeros((N_MOD_ROWS, D), F32).at[0].set(c_ctx).at[1:1 + DEC_BATCH].set(c)
    mods = _ada(cc, w_ada, b_ada)
    rope_cos, rope_sin = _rope_tables(DEC_SEQ)
    c_c, s_c = _dft_tables(FNET_GROUP_CH)
    cs = jnp.concatenate([c_c, s_c], axis=1).astype(BF16)
    tables = (rope_cos, rope_sin, _fnet_tables(SEQ), _fnet_tables(DEC_SEQ), cs, _s5_permutations())
    cache_k4 = cache_k.reshape(DEC_BATCH, DEPTH, PAST_LEN, QK_WIDTH)
    cache_v4 = cache_v.reshape(DEC_BATCH, DEPTH, PAST_LEN, V_WIDTH)
    x_parts = ((x_prompt.reshape(M_CTX, D), 0), (x_sample.reshape(M_LAT, D), 0))
    new_kv = None
    s_res, s_ims = [], []
    for l in range(DEPTH):
        lp = {name: arr[l] for name, arr in params.items()}
        lp['moe_w1_all'], lp['moe_w3_all'], lp['moe_w2_all'] = moe_w1, moe_w3, moe_w2
        lp['stacked'] = stacked
        lam_init = 0.8 - 0.6 * math.exp(-0.3 * l)
        x, new_kv, s_re, s_im = _layer(x_parts, mods[l], lp, l, lam_init, tables, cache_k4, cache_v4,
                                       state_s5_re[:, l], state_s5_im[:, l], new_kv)
        x_parts = ((x, 0), (x, M_CTX))
        s_res.append(s_re)
        s_ims.append(s_im)
    y_prompt = _final_norm(x, final_norm_g, True).reshape(BATCH, SEQ, D)
    y_sample = _final_norm(x, final_norm_g, False).reshape(DEC_BATCH, DEC_SEQ, D)
    kv_shape = (BATCH, DEPTH, SEQ, N_HEADS, VALUE_DIM)
    return (y_prompt, y_sample, new_kv[0].reshape(kv_shape), new_kv[1].reshape(kv_shape),
            jnp.stack(s_res, axis=1), jnp.stack(s_ims, axis=1))
```

```python
import functools
import math

import jax
import jax.numpy as jnp
import numpy as np
from jax import lax
from jax.experimental import pallas as pl
from jax.experimental.pallas import tpu as pltpu

F32 = jnp.float32
BF16 = jnp.bfloat16

D = 2048
BATCH, SEQ = 32, 256
DEC_BATCH, DEC_SEQ = 8, 2048
DEPTH = 2
PAST_LEN = 256
GRID_W = 64
EPS = 1e-6
S5_GROUP_CH, S5_GROUPS, S5_STATE = 16, 32, 64
S5_WIDTH = S5_GROUPS * S5_GROUP_CH
FNET_GROUPS, FNET_GROUP_CH = 4, 128
FNET_WIDTH = FNET_GROUPS * FNET_GROUP_CH
N_HEADS, HEAD_DIM = 8, 64
VALUE_DIM = 2 * HEAD_DIM
QK_WIDTH = N_HEADS * 2 * HEAD_DIM
V_WIDTH = N_HEADS * VALUE_DIM
IN_WIDTH = S5_WIDTH + FNET_WIDTH + 2 * QK_WIDTH + V_WIDTH
ROPE_BASE = 10000.0
ROPE_AXIS_DIM = HEAD_DIM // 2
N_EXPERTS = 16
EXPERT_FF = 2048
CAPACITY_FACTOR = 2

M_CTX = BATCH * SEQ
M_LAT = DEC_BATCH * DEC_SEQ
M = M_CTX + M_LAT
N_MOD_ROWS = 16

LANES = 128
SUBLANES = 8
VMEM_LIMIT = 56 * 1024 * 1024

ROUTE_TILE = 2048
ROUTE_SLOTS = CAPACITY_FACTOR * ROUTE_TILE // N_EXPERTS
N_ROUTE_TILES = M // ROUTE_TILE
SLOTS_PER_EXPERT = N_ROUTE_TILES * ROUTE_SLOTS

S5_BLK = 4
S5_BLK_STATES = 512

ANY_SPEC = pl.BlockSpec(memory_space=pl.ANY)


def _cparams(sem):
    return pltpu.CompilerParams(dimension_semantics=sem, vmem_limit_bytes=VMEM_LIMIT)


def _sigmoid(x):
    return 1.0 / (1.0 + jnp.exp(-x))


def _part(is_ctx):
    return (M_CTX, 0) if is_ctx else (M_LAT, M_CTX)


def _mod_row_fn(is_ctx, tm):
    if is_ctx:
        return lambda i: 0
    return lambda i: 1 + i // (DEC_SEQ // tm)


def _ada_kernel(c_ref, w_ref, b_ref, o_ref):
    c = c_ref[...]
    s = (c * _sigmoid(c)).astype(BF16)
    o_ref[...] = jnp.dot(s, w_ref[...].astype(BF16), preferred_element_type=F32) + b_ref[...]


def _ada(cc, w_ada, b_ada):
    tn = 1024
    n = 6 * D
    return pl.pallas_call(
        _ada_kernel,
        grid=(DEPTH, n // tn),
        in_specs=[
            pl.BlockSpec((N_MOD_ROWS, D), lambda l, j: (0, 0)),
            pl.BlockSpec((None, D, tn), lambda l, j: (l, 0, j)),
            pl.BlockSpec((None, 1, tn), lambda l, j: (l, 0, j)),
        ],
        out_specs=pl.BlockSpec((None, N_MOD_ROWS, tn), lambda l, j: (l, 0, j)),
        out_shape=jax.ShapeDtypeStruct((DEPTH, N_MOD_ROWS, n), F32),
        compiler_params=_cparams(("arbitrary", "arbitrary")),
        name="ada",
    )(cc, w_ada, b_ada.reshape(DEPTH, 1, n))


NP_TM, NP_TN = 1024, 1024


NP_NJ = IN_WIDTH // NP_TN


def _norm_proj_kernel(x_ref, g_ref, sc_ref, sh_ref, w_ref, *rest, is_ctx, n_alias, tm):
    h_ref, u_ref, ua_ref, ha_ref, hb_ref = rest[n_alias:]
    i = pl.program_id(0)
    j = pl.program_id(1)
    q = tm // NP_NJ
    rows = pl.ds(pl.multiple_of(j * q, q), q)

    def norm_quarter(dst_ref):
        x = x_ref[rows, :]
        r = x * lax.rsqrt(jnp.mean(x * x, axis=-1, keepdims=True) + EPS)
        dst_ref[rows, :] = ((r * g_ref[...]) * (1.0 + sc_ref[...]) + sh_ref[...]).astype(BF16)

    def project(src_ref):
        h_ref[rows, :] = src_ref[rows, :]
        u_ref[...] = jnp.dot(src_ref[...], w_ref[...], preferred_element_type=F32)

    @pl.when(i == 0)
    def _():
        norm_quarter(ha_ref)

    @pl.when((i > 0) & (i % 2 == 0))
    def _():
        norm_quarter(ha_ref)
        project(hb_ref)

    @pl.when(i % 2 == 1)
    def _():
        norm_quarter(hb_ref)
        project(ha_ref)

    @pl.when((i > 0) & (j == 0))
    def _():
        if is_ctx:
            for b in range(tm // SEQ):
                ua_ref[:, b * S5_WIDTH:(b + 1) * S5_WIDTH] = u_ref[b * SEQ:(b + 1) * SEQ, :S5_WIDTH]
        else:
            ua_ref[...] = u_ref[:, :S5_WIDTH]


def _norm_proj(x, x_row0, g, mod4, w_in, layer, is_ctx, prev):
    tm, tn = NP_TM, NP_TN
    assert S5_WIDTH <= tn
    rows, row0 = _part(is_ctx)
    n_i = rows // tm
    xoff, ooff = x_row0 // tm, row0 // tm
    mrow = _mod_row_fn(is_ctx, tm)
    cur = lambda i: jnp.minimum(i, n_i - 1)
    prv = lambda i: jnp.maximum(i - 1, 0)
    prev = list(prev) if prev is not None else []
    in_specs = [
        pl.BlockSpec((tm, D), lambda i, j: (xoff + cur(i), 0)),
        pl.BlockSpec((1, D), lambda i, j: (0, 0)),
        pl.BlockSpec((None, None, 1, D), lambda i, j: (mrow(cur(i)), 1, 0, 0)),
        pl.BlockSpec((None, None, 1, D), lambda i, j: (mrow(cur(i)), 0, 0, 0)),
        pl.BlockSpec((None, D, tn), lambda i, j: (layer, 0, j)),
    ] + [ANY_SPEC] * len(prev)
    out_specs = [
        pl.BlockSpec((tm, D), lambda i, j: (ooff + prv(i), 0)),
        pl.BlockSpec((tm, tn), lambda i, j: (ooff + prv(i), j)),
    ]
    out_shape = [jax.ShapeDtypeStruct((M, D), BF16), jax.ShapeDtypeStruct((M, IN_WIDTH), F32)]
    if is_ctx:
        nbt = tm // SEQ
        out_specs.append(pl.BlockSpec((SEQ, nbt * S5_WIDTH), lambda i, j: (0, prv(i))))
        out_shape.append(jax.ShapeDtypeStruct((SEQ, BATCH * S5_WIDTH), F32))
        aliases = {}
    else:
        tpb = DEC_SEQ // tm
        out_specs.append(pl.BlockSpec((tm, S5_WIDTH), lambda i, j: (prv(i) % tpb, prv(i) // tpb)))
        out_shape.append(jax.ShapeDtypeStruct((DEC_SEQ, DEC_BATCH * S5_WIDTH), F32))
        aliases = {5: 0, 6: 1}
    return pl.pallas_call(
        functools.partial(_norm_proj_kernel, is_ctx=is_ctx, n_alias=len(prev), tm=tm),
        grid=(n_i + 1, NP_NJ),
        in_specs=in_specs,
        out_specs=out_specs,
        out_shape=out_shape,
        scratch_shapes=[pltpu.VMEM((tm, D), BF16), pltpu.VMEM((tm, D), BF16)],
        input_output_aliases=aliases,
        compiler_params=_cparams(("arbitrary", "arbitrary")),
        name="norm_proj",
    )(x, g.reshape(1, D), mod4, mod4, w_in, *prev)


S5_CHUNK = 64


def _s5_kernel(u_ref, pin_ref, pout_ref, wb_ref, wc_ref, are_ref, aim_ref, h0re_ref, h0im_ref,
               y_ref, finre_ref, finim_ref, xs0_ref, xs1_ref, xs2_ref, xs3_ref, stre_ref, stim_ref):
    c = pl.program_id(2)
    nc = pl.num_programs(2)
    bs = S5_BLK_STATES
    chunk = S5_CHUNK
    xs_refs = (xs0_ref, xs1_ref, xs2_ref, xs3_ref)

    @pl.when(c == 0)
    def _():
        stre_ref[...] = h0re_ref[...]
        stim_ref[...] = h0im_ref[...]

    ustack = jnp.concatenate(
        [u_ref[:, j * S5_WIDTH:(j + 1) * S5_WIDTH] for j in range(SUBLANES)], axis=0).astype(BF16)
    up = jnp.dot(pin_ref[...], ustack, preferred_element_type=F32).astype(BF16)

    for blk in range(S5_BLK):
        xs_refs[blk][...] = jnp.dot(up[:, blk * LANES:(blk + 1) * LANES], wb_ref[blk],
                                    preferred_element_type=F32)

    for blk in range(S5_BLK):
        xs_ref = xs_refs[blk]
        ar = are_ref[:, blk * bs:(blk + 1) * bs]
        ai = aim_ref[:, blk * bs:(blk + 1) * bs]
        sr = stre_ref[:, blk * bs:(blk + 1) * bs]
        si = stim_ref[:, blk * bs:(blk + 1) * bs]
        for i in range(chunk):
            r8 = slice(i * SUBLANES, (i + 1) * SUBLANES)
            nr = ar * sr - ai * si + xs_ref[r8, 0:bs]
            ni = ar * si + ai * sr + xs_ref[r8, bs:2 * bs]
            xs_ref[r8, 0:bs] = nr
            xs_ref[r8, bs:2 * bs] = ni
            sr, si = nr, ni
        stre_ref[:, blk * bs:(blk + 1) * bs] = sr
        stim_ref[:, blk * bs:(blk + 1) * bs] = si

    y = jnp.concatenate(
        [jnp.dot(xs_refs[blk][...].astype(BF16), wc_ref[blk], preferred_element_type=F32)
         for blk in range(S5_BLK)], axis=1)
    y_hi = y.astype(BF16)
    y_lo = (y - y_hi.astype(F32)).astype(BF16)
    ys = (jnp.dot(pout_ref[...], y_hi, preferred_element_type=F32)
          + jnp.dot(pout_ref[...], y_lo, preferred_element_type=F32))
    for j in range(SUBLANES):
        y_ref[:, j * S5_WIDTH:(j + 1) * S5_WIDTH] = ys[j * chunk:(j + 1) * chunk, :]

    @pl.when(c == nc - 1)
    def _():
        finre_ref[...] = stre_ref[...]
        finim_ref[...] = stim_ref[...]


def _s5_permutations():
    chunk = S5_CHUNK
    rows = chunk * SUBLANES
    r_out = lax.broadcasted_iota(jnp.int32, (rows, rows), 0)
    r_in = lax.broadcasted_iota(jnp.int32, (rows, rows), 1)
    step, seq = r_out // SUBLANES, r_out % SUBLANES
    fwd = r_in == seq * chunk + step
    bwd = r_in == seq * chunk + (chunk - 1 - step)
    p = jnp.stack([fwd, bwd]).astype(BF16)
    return p, p.transpose(0, 2, 1)


def _s5(u2d, perms, wb, wc, a_re, a_im, h0_re, h0_im):
    t_len = u2d.shape[0]
    ng = u2d.shape[1] // (SUBLANES * S5_WIDTH)
    chunk = S5_CHUNK
    nc = t_len // chunk
    rows = chunk * SUBLANES
    ns = S5_BLK * S5_BLK_STATES
    pin, pout = perms

    def ceff(d, c):
        return jnp.where(d == 0, c, nc - 1 - c)

    return pl.pallas_call(
        _s5_kernel,
        grid=(ng, 2, nc),
        in_specs=[
            pl.BlockSpec((chunk, SUBLANES * S5_WIDTH), lambda g, d, c: (ceff(d, c), g)),
            pl.BlockSpec((None, rows, rows), lambda g, d, c: (d, 0, 0)),
            pl.BlockSpec((None, rows, rows), lambda g, d, c: (d, 0, 0)),
            pl.BlockSpec((None, S5_BLK, LANES, 2 * S5_BLK_STATES), lambda g, d, c: (d, 0, 0, 0)),
            pl.BlockSpec((None, S5_BLK, 2 * S5_BLK_STATES, LANES), lambda g, d, c: (d, 0, 0, 0)),
            pl.BlockSpec((None, SUBLANES, ns), lambda g, d, c: (d, 0, 0)),
            pl.BlockSpec((None, SUBLANES, ns), lambda g, d, c: (d, 0, 0)),
            pl.BlockSpec((None, None, SUBLANES, ns), lambda g, d, c: (g, d, 0, 0)),
            pl.BlockSpec((None, None, SUBLANES, ns), lambda g, d, c: (g, d, 0, 0)),
        ],
        out_specs=[
            pl.BlockSpec((None, chunk, SUBLANES * S5_WIDTH), lambda g, d, c: (d, ceff(d, c), g)),
            pl.BlockSpec((None, None, SUBLANES, ns), lambda g, d, c: (g, d, 0, 0)),
            pl.BlockSpec((None, None, SUBLANES, ns), lambda g, d, c: (g, d, 0, 0)),
        ],
        out_shape=[
            jax.ShapeDtypeStruct((2,) + u2d.shape, F32),
            jax.ShapeDtypeStruct((ng, 2, SUBLANES, ns), F32),
            jax.ShapeDtypeStruct((ng, 2, SUBLANES, ns), F32),
        ],
        scratch_shapes=[pltpu.VMEM((rows, 2 * S5_BLK_STATES), F32)] * S5_BLK + [
            pltpu.VMEM((SUBLANES, ns), F32),
            pltpu.VMEM((SUBLANES, ns), F32),
        ],
        compiler_params=_cparams(("arbitrary", "arbitrary", "arbitrary")),
        name="s5_scan",
    )(u2d, pin, pout, wb, wc, a_re, a_im, h0_re, h0_im)


def _s5_params(lp):
    lam_re = lp['s5_lam_re'].astype(F32)
    lam_im = lp['s5_lam_im'].astype(F32)
    dt = jnp.exp(lp['s5_log_dt'].astype(F32))[..., None]
    b_re = lp['s5_b_re'].astype(F32)
    b_im = lp['s5_b_im'].astype(F32)
    c_re = lp['s5_c_re'].astype(F32)
    c_im = lp['s5_c_im'].astype(F32)
    mag = jnp.exp(lam_re * dt)
    ab_re = mag * jnp.cos(lam_im * dt)
    ab_im = mag * jnp.sin(lam_im * dt)
    den = lam_re * lam_re + lam_im * lam_im
    f_re = ((ab_re - 1.0) * lam_re + ab_im * lam_im) / den
    f_im = (ab_im * lam_re - (ab_re - 1.0) * lam_im) / den
    bb_re = f_re[..., None] * b_re - f_im[..., None] * b_im
    bb_im = f_re[..., None] * b_im + f_im[..., None] * b_re
    gpb = S5_GROUPS // S5_BLK
    eye = jnp.eye(gpb, dtype=F32)

    def pack_b(bb):
        bb = bb.reshape(2, S5_BLK, gpb, S5_STATE, S5_GROUP_CH)
        w = jnp.einsum('dbgph,gk->dbghkp', bb, eye)
        return w.reshape(2, S5_BLK, gpb * S5_GROUP_CH, gpb * S5_STATE)

    def pack_c(cc):
        cc = cc.reshape(2, S5_BLK, gpb, S5_GROUP_CH, S5_STATE)
        w = jnp.einsum('dbghp,gk->dbkpgh', cc, eye)
        return w.reshape(2, S5_BLK, gpb * S5_STATE, gpb * S5_GROUP_CH)

    wb = jnp.concatenate([pack_b(bb_re), pack_b(bb_im)], axis=-1).astype(BF16)
    wc = jnp.concatenate([pack_c(c_re), -pack_c(c_im)], axis=-2).astype(BF16)
    ns = S5_GROUPS * S5_STATE
    a_re = jnp.broadcast_to(ab_re.reshape(2, 1, ns), (2, SUBLANES, ns))
    a_im = jnp.broadcast_to(ab_im.reshape(2, 1, ns), (2, SUBLANES, ns))
    return wb, wc, a_re, a_im


def _gelu_tanh(x):
    return 0.5 * x * (1.0 + jnp.tanh(math.sqrt(2.0 / math.pi) * (x + 0.044715 * (x * x * x))))


def _glu_kernel(y_ref, u_ref, d_ref, w_ref, b_ref, *rest):
    o_ref = rest[-1]
    y = y_ref[0] + y_ref[1] + d_ref[...] * u_ref[...]
    y = _gelu_tanh(y)
    z = jnp.dot(y.astype(BF16), w_ref[...], preferred_element_type=F32) + b_ref[...]
    o_ref[...] = (y * _sigmoid(z)).astype(BF16)


def _glu(y_dirs, u, d_row, w_glu, b_glu, is_ctx, tt, prev):
    t_len = y_dirs.shape[1]
    nb = y_dirs.shape[2] // S5_WIDTH
    ntt = t_len // tt
    off = _part(is_ctx)[1] // tt
    prev = [prev] if prev is not None else []
    return pl.pallas_call(
        _glu_kernel,
        grid=(nb, ntt),
        in_specs=[
            pl.BlockSpec((2, tt, S5_WIDTH), lambda b, t: (0, t, b)),
            pl.BlockSpec((tt, S5_WIDTH), lambda b, t: (off + b * ntt + t, 0)),
            pl.BlockSpec((1, S5_WIDTH), lambda b, t: (0, 0)),
            pl.BlockSpec((S5_WIDTH, S5_WIDTH), lambda b, t: (0, 0)),
            pl.BlockSpec((1, S5_WIDTH), lambda b, t: (0, 0)),
        ] + [ANY_SPEC] * len(prev),
        out_specs=pl.BlockSpec((tt, S5_WIDTH), lambda b, t: (off + b * ntt + t, 0)),
        out_shape=jax.ShapeDtypeStruct((M, S5_WIDTH), BF16),
        input_output_aliases={5: 0} if prev else {},
        compiler_params=_cparams(("arbitrary", "arbitrary")),
        name="s5_glu",
    )(y_dirs, u, d_row, w_glu, b_glu, *prev)


def _fnet_kernel(x_ref, cs_ref, dft_ref, *rest, t_len):
    o_ref, z_ref = rest[-2:]

    @pl.when(pl.program_id(1) == 0)
    def _():
        x = x_ref[...].astype(BF16)
        for g in range(FNET_GROUPS):
            z = jnp.dot(x[:, g * LANES:(g + 1) * LANES], cs_ref[...], preferred_element_type=F32)
            z_ref[0:t_len, g * LANES:(g + 1) * LANES] = z[:, :LANES].astype(BF16)
            z_ref[t_len:2 * t_len, g * LANES:(g + 1) * LANES] = z[:, LANES:].astype(BF16)

    scale = 1.0 / math.sqrt(t_len * FNET_GROUP_CH)
    o_ref[...] = (jnp.dot(dft_ref[...], z_ref[...], preferred_element_type=F32) * scale).astype(BF16)


def _dft_tables(n):
    j = lax.broadcasted_iota(jnp.int32, (n, n), 0)
    k = lax.broadcasted_iota(jnp.int32, (n, n), 1)
    ang = ((j * k) % n).astype(F32) * (2.0 * math.pi / n)
    return jnp.cos(ang), jnp.sin(ang)


def _fnet_tables(t_len):
    j = np.arange(t_len, dtype=np.int64)
    ang = ((j[:, None] * j[None, :]) % t_len).astype(np.float64) * (2.0 * math.pi / t_len)
    return jnp.asarray(np.concatenate([np.cos(ang), -np.sin(ang)], axis=1), dtype=BF16)


def _fnet(u, dft, cs, is_ctx, tr, prev):
    t_len = dft.shape[0]
    rows, row0 = _part(is_ctx)
    nb = rows // t_len
    ntr = t_len // tr
    off = row0 // t_len
    offr = row0 // tr
    col = S5_WIDTH // FNET_WIDTH
    prev = [prev] if prev is not None else []
    return pl.pallas_call(
        functools.partial(_fnet_kernel, t_len=t_len),
        grid=(nb, ntr),
        in_specs=[
            pl.BlockSpec((t_len, FNET_WIDTH), lambda b, r: (off + b, col)),
            pl.BlockSpec((FNET_GROUP_CH, 2 * FNET_GROUP_CH), lambda b, r: (0, 0)),
            pl.BlockSpec((tr, 2 * t_len), lambda b, r: (r, 0)),
        ] + [ANY_SPEC] * len(prev),
        out_specs=pl.BlockSpec((tr, FNET_WIDTH), lambda b, r: (offr + b * ntr + r, 0)),
        out_shape=jax.ShapeDtypeStruct((M, FNET_WIDTH), BF16),
        scratch_shapes=[pltpu.VMEM((2 * t_len, FNET_WIDTH), BF16)],
        input_output_aliases={3: 0} if prev else {},
        compiler_params=_cparams(("arbitrary", "arbitrary")),
        name="fnet",
    )(u, cs, dft, *prev)


def _rope(x, cos, sin):
    half = ROPE_AXIS_DIM // 2
    lane = lax.broadcasted_iota(jnp.int32, x.shape, 1)
    up = pltpu.roll(x, LANES - half, 1)
    dn = pltpu.roll(x, half, 1)
    rot = jnp.where((lane % ROPE_AXIS_DIM) < half, -up, dn)
    return x * cos + rot * sin


ATTN_KEY_CHUNK = 512
ATTN_ROW_BLOCK = 256


def _attn_kernel(lam_ref, q_ref, k_ref, v_ref, *rest, tq, t_len, hb, has_ctx, lam_init, n_alias, emit_kv):
    pos = 0
    if has_ctx:
        ck_ref, cv_ref, cos_ref, sin_ref = rest[:4]
        pos = 4
    g_ref = rest[pos]
    pos += 1 + n_alias
    o_ref = rest[pos]
    pos += 1
    if emit_kv:
        ko_ref, vo_ref = rest[pos:pos + 2]
        pos += 2
    kk_ref, vv_ref, qs_ref, s_ref = rest[pos:]
    qi = pl.program_id(2)
    if emit_kv:
        @pl.when(qi == 0)
        def _():
            ko_ref[...] = k_ref[...]
            vo_ref[...] = v_ref[...]
    n_keys = kk_ref.shape[1]
    kc, rb = ATTN_KEY_CHUNK, ATTN_ROW_BLOCK
    chunks = [(c0, min(c0 + kc, n_keys)) for c0 in range(0, n_keys, kc)]

    @pl.when(qi == 0)
    def _():
        for h in range(hb):
            hs = slice(h * LANES, (h + 1) * LANES)
            k = k_ref[:, hs]
            if has_ctx:
                k = _rope(k, cos_ref[...], sin_ref[...])
                kk_ref[h, t_len:, :] = ck_ref[:, hs].astype(BF16)
                vv_ref[h, t_len:, 0:VALUE_DIM] = cv_ref[:, hs].astype(BF16)
            kk_ref[h, 0:t_len, :] = k.astype(BF16)
            vv_ref[h, 0:t_len, 0:VALUE_DIM] = v_ref[:, hs].astype(BF16)
            vv_ref[h, :, VALUE_DIM:] = jnp.ones((n_keys, VALUE_DIM), BF16)

    lane = lax.broadcasted_iota(jnp.int32, (tq, LANES), 1)
    for h in range(hb):
        hs = slice(h * LANES, (h + 1) * LANES)
        q = q_ref[:, hs]
        if has_ctx:
            r0 = pl.multiple_of(qi * tq, tq)
            q = _rope(q, cos_ref[pl.ds(r0, tq), :], sin_ref[pl.ds(r0, tq), :])
        q = q * (HEAD_DIM ** -0.5)
        qs_ref[0, :, hs] = jnp.where(lane < HEAD_DIM, q, 0.0).astype(BF16)
        qs_ref[1, :, hs] = jnp.where(lane >= HEAD_DIM, q, 0.0).astype(BF16)

    lam = lam_ref[0]
    gain = g_ref[...]
    dn = (((1,), (1,)), ((), ()))

    unit = 0
    for r in range(tq // rb):
        rows = slice(r * rb, (r + 1) * rb)
        for h in range(hb):
            hs = slice(h * LANES, (h + 1) * LANES)
            o = None
            for n in range(2):
                slot = unit % 2
                unit += 1
                qn = qs_ref[n, rows, hs]
                mrun = None
                for c0, c1 in chunks:
                    s_c = lax.dot_general(qn, kk_ref[h, c0:c1, :], dn, preferred_element_type=F32)
                    s_ref[slot, :, c0:c1] = s_c
                    for l0 in range(0, c1 - c0, LANES):
                        mc = s_c[:, l0:l0 + LANES]
                        mrun = mc if mrun is None else jnp.maximum(mrun, mc)
                m_b = jnp.broadcast_to(jnp.max(mrun, axis=-1, keepdims=True), (rb, LANES))
                oa = None
                for c0, c1 in chunks:
                    p = jnp.concatenate(
                        [jnp.exp(s_ref[slot, :, l0:l0 + LANES] - m_b) for l0 in range(c0, c1, LANES)],
                        axis=1).astype(BF16)
                    part = jnp.dot(p, vv_ref[h, c0:c1, :], preferred_element_type=F32)
                    oa = part if oa is None else oa + part
                on = oa[:, :VALUE_DIM] * (1.0 / oa[:, VALUE_DIM:VALUE_DIM + 1])
                o = on if n == 0 else o - lam * on
            o = o * lax.rsqrt(jnp.mean(o * o, axis=-1, keepdims=True) + EPS)
            o_ref[rows, hs] = ((o * gain) * (1.0 - lam_init)).astype(BF16)


def _attention(u, lam, subln_g, lam_init, is_ctx, tq, hb, prev, ctx=None, kv_out=None):
    rows, row0 = _part(is_ctx)
    t_len = SEQ if is_ctx else DEC_SEQ
    nb = rows // t_len
    nq = t_len // tq
    off = row0 // t_len
    offq = row0 // tq
    width = hb * LANES
    qcol = (S5_WIDTH + FNET_WIDTH) // width
    kcol = qcol + QK_WIDTH // width
    vcol = kcol + QK_WIDTH // width
    has_ctx = ctx is not None
    n_keys = t_len + (PAST_LEN if has_ctx else 0)
    assert tq % ATTN_ROW_BLOCK == 0 and n_keys % (2 * LANES) == 0
    in_specs = [
        pl.BlockSpec(memory_space=pltpu.SMEM),
        pl.BlockSpec((tq, width), lambda b, h, i: (offq + b * nq + i, qcol + h)),
        pl.BlockSpec((t_len, width), lambda b, h, i: (off + b, kcol + h)),
        pl.BlockSpec((t_len, width), lambda b, h, i: (off + b, vcol + h)),
    ]
    args = [lam, u, u, u]
    if has_ctx:
        ck, cv, layer, cos, sin = ctx
        in_specs += [
            pl.BlockSpec((None, None, PAST_LEN, width), lambda b, h, i: (b, layer, 0, h)),
            pl.BlockSpec((None, None, PAST_LEN, width), lambda b, h, i: (b, layer, 0, h)),
            pl.BlockSpec((t_len, LANES), lambda b, h, i: (0, 0)),
            pl.BlockSpec((t_len, LANES), lambda b, h, i: (0, 0)),
        ]
        args += [ck, cv, cos, sin]
    in_specs.append(pl.BlockSpec((1, LANES), lambda b, h, i: (0, 0)))
    args.append(subln_g)
    aliases = {}
    n_fixed = len(args)
    if prev is not None:
        aliases[len(args)] = 0
        in_specs.append(ANY_SPEC)
        args.append(prev)
    out_specs = [pl.BlockSpec((tq, width), lambda b, h, i: (offq + b * nq + i, h))]
    out_shape = [jax.ShapeDtypeStruct((M, V_WIDTH), BF16)]
    if kv_out is not None:
        kv_layer, prev_kv = kv_out
        assert hb == N_HEADS and tq == t_len and nb == BATCH
        out_specs += [pl.BlockSpec((None, None, SEQ, QK_WIDTH), lambda b, h, i: (b, kv_layer, 0, 0)),
                      pl.BlockSpec((None, None, SEQ, V_WIDTH), lambda b, h, i: (b, kv_layer, 0, 0))]
        out_shape += [jax.ShapeDtypeStruct((BATCH, DEPTH, SEQ, QK_WIDTH), F32),
                      jax.ShapeDtypeStruct((BATCH, DEPTH, SEQ, V_WIDTH), F32)]
        if prev_kv is not None:
            for o_idx, arr in enumerate(prev_kv):
                aliases[len(args)] = 1 + o_idx
                in_specs.append(ANY_SPEC)
                args.append(arr)
    kern = functools.partial(_attn_kernel, tq=tq, t_len=t_len, hb=hb, has_ctx=has_ctx, lam_init=lam_init,
                             n_alias=len(args) - n_fixed, emit_kv=kv_out is not None)
    return pl.pallas_call(
        kern,
        grid=(nb, N_HEADS // hb, nq),
        in_specs=in_specs,
        out_specs=out_specs,
        out_shape=out_shape,
        scratch_shapes=[
            pltpu.VMEM((hb, n_keys, LANES), BF16),
            pltpu.VMEM((hb, n_keys, 2 * VALUE_DIM), BF16),
            pltpu.VMEM((2, tq, width), BF16),
            pltpu.VMEM((2, ATTN_ROW_BLOCK, n_keys), F32),
        ],
        input_output_aliases=aliases,
        compiler_params=_cparams(("arbitrary", "arbitrary", "arbitrary")),
        name="diff_attn",
    )(*args)


def _rope_tables(t_len):
    rows = t_len // GRID_W
    pos_row = jnp.broadcast_to(jnp.arange(rows, dtype=F32)[:, None], (rows, GRID_W)).reshape(-1)
    pos_col = jnp.broadcast_to(jnp.arange(GRID_W, dtype=F32)[None, :], (rows, GRID_W)).reshape(-1)
    inv = ROPE_BASE ** (-jnp.arange(0, ROPE_AXIS_DIM, 2, dtype=F32) / ROPE_AXIS_DIM)
    ang_r = pos_row[:, None] * inv
    ang_c = pos_col[:, None] * inv
    cos = jnp.concatenate([jnp.cos(ang_r), jnp.cos(ang_r), jnp.cos(ang_c), jnp.cos(ang_c)], axis=-1)
    sin = jnp.concatenate([jnp.sin(ang_r), jnp.sin(ang_r), jnp.sin(ang_c), jnp.sin(ang_c)], axis=-1)
    return jnp.concatenate([cos, cos], axis=-1), jnp.concatenate([sin, sin], axis=-1)


def _merge_kernel(h_ref, ya_ref, yb_ref, yc_ref, wga_ref, wgb_ref, wgc_ref, bga_ref, bgb_ref, bgc_ref,
                  wa_ref, wb_ref, wc_ref, o_ref):
    h = h_ref[...]
    acc = None
    for wg, bg, y, w in ((wga_ref, bga_ref, ya_ref, wa_ref), (wgb_ref, bgb_ref, yb_ref, wb_ref),
                         (wgc_ref, bgc_ref, yc_ref, wc_ref)):
        gate = _sigmoid(jnp.dot(h, wg[...], preferred_element_type=F32) + bg[...])
        term = gate * jnp.dot(y[...], w[...], preferred_element_type=F32)
        acc = term if acc is None else acc + term
    o_ref[...] = acc.astype(BF16)


def _merge(h, ya, yb, yc, w_gate, b_gate, w_a, w_b, w_c, layer):
    tm, tn = 1024, 512
    nj = D // tn
    row = lambda i, j: (i, 0)
    wspecs = [pl.BlockSpec((None, D, tn), lambda i, j, k=k: (layer, 0, k * nj + j)) for k in range(3)]
    bspecs = [pl.BlockSpec((None, 1, tn), lambda i, j, k=k: (layer, 0, k * nj + j)) for k in range(3)]
    return pl.pallas_call(
        _merge_kernel,
        grid=(M // tm, nj),
        in_specs=[
            pl.BlockSpec((tm, D), row),
            pl.BlockSpec((tm, S5_WIDTH), row),
            pl.BlockSpec((tm, FNET_WIDTH), row),
            pl.BlockSpec((tm, V_WIDTH), row),
            *wspecs, *bspecs,
            pl.BlockSpec((None, S5_WIDTH, tn), lambda i, j: (layer, 0, j)),
            pl.BlockSpec((None, FNET_WIDTH, tn), lambda i, j: (layer, 0, j)),
            pl.BlockSpec((None, V_WIDTH, tn), lambda i, j: (layer, 0, j)),
        ],
        out_specs=pl.BlockSpec((tm, tn), lambda i, j: (i, j)),
        out_shape=jax.ShapeDtypeStruct((M, D), BF16),
        compiler_params=_cparams(("arbitrary", "arbitrary")),
        name="merge",
    )(h, ya, yb, yc, w_gate, w_gate, w_gate, b_gate, b_gate, b_gate, w_a, w_b, w_c)


def _out_proj_kernel(m_ref, w_ref, x_ref, g1_ref, n2_ref, sc_ref, sh_ref, *rest):
    x1_ref, h2_ref = rest[-2:]
    x1 = x_ref[...] + g1_ref[...] * jnp.dot(m_ref[...], w_ref[...], preferred_element_type=F32)
    x1_ref[...] = x1
    r = x1 * lax.rsqrt(jnp.mean(x1 * x1, axis=-1, keepdims=True) + EPS)
    h2_ref[...] = ((r * n2_ref[...]) * (1.0 + sc_ref[...]) + sh_ref[...]).astype(BF16)


def _out_proj(merged, w_out, layer, x, x_row0, norm2_g, mod4, is_ctx, prev):
    tm = 512
    rows, row0 = _part(is_ctx)
    xoff, ooff = x_row0 // tm, row0 // tm
    mrow = _mod_row_fn(is_ctx, tm)
    orow = lambda i: (ooff + i, 0)
    mspec = lambda k: pl.BlockSpec((None, None, 1, D), lambda i, k=k: (mrow(i), k, 0, 0))
    prev = list(prev) if prev is not None else []
    return pl.pallas_call(
        _out_proj_kernel,
        grid=(rows // tm,),
        in_specs=[
            pl.BlockSpec((tm, D), orow),
            pl.BlockSpec((None, D, D), lambda i: (layer, 0, 0)),
            pl.BlockSpec((tm, D), lambda i: (xoff + i, 0)),
            mspec(2),
            pl.BlockSpec((1, D), lambda i: (0, 0)),
            mspec(4),
            mspec(3),
        ] + [ANY_SPEC] * len(prev),
        out_specs=[pl.BlockSpec((tm, D), orow), pl.BlockSpec((tm, D), orow)],
        out_shape=[jax.ShapeDtypeStruct((M, D), F32), jax.ShapeDtypeStruct((M, D), BF16)],
        input_output_aliases={7: 0, 8: 1} if prev else {},
        compiler_params=_cparams(("arbitrary",)),
        name="out_proj",
    )(merged, w_out, x, mod4, norm2_g.reshape(1, D), mod4, mod4, *prev)


def _segment_sum(x, seg):
    parts = []
    for s in range(ROUTE_TILE // seg):
        tot = jnp.sum(x[:, s * seg:(s + 1) * seg], axis=1, keepdims=True)
        parts.append(jnp.broadcast_to(tot, (x.shape[0], seg)))
    return parts[0] if len(parts) == 1 else jnp.concatenate(parts, axis=1)


def _segment_cumsum(x, seg):
    pos = lax.broadcasted_iota(jnp.int32, x.shape, 1) % seg
    sh = 1
    while sh < seg:
        x = x + jnp.where(pos >= sh, pltpu.roll(x, sh, 1), 0.0)
        sh *= 2
    return x


def _router_kernel(h_ref, wr_ref, slot_ref, gcol_ref, *, seg):
    cap = CAPACITY_FACTOR * seg // N_EXPERTS
    logits = lax.dot_general(wr_ref[...], h_ref[...], (((1,), (1,)), ((), ())),
                             preferred_element_type=F32)
    p = jnp.exp(logits - jnp.max(logits, axis=0, keepdims=True))
    aff = p / jnp.sum(p, axis=0, keepdims=True)
    bits = pltpu.bitcast(aff, jnp.int32)

    def bis(i, thr):
        cand = thr | jnp.left_shift(jnp.int32(1), 29 - i)
        cnt = _segment_sum(jnp.where(bits >= cand, 1.0, 0.0), seg)
        return jnp.where(cnt >= cap, cand, thr)

    thr = lax.fori_loop(0, 30, bis, jnp.zeros(bits.shape, jnp.int32))
    gt = jnp.where(bits > thr, 1.0, 0.0)
    eq = jnp.where(bits == thr, 1.0, 0.0)
    need = cap - _segment_sum(gt, seg)
    eq_rank = _segment_cumsum(eq, seg)
    sel = gt + eq * jnp.where(eq_rank <= need, 1.0, 0.0)
    rank = _segment_cumsum(sel, seg)
    seg_id = lax.broadcasted_iota(jnp.int32, bits.shape, 1) // seg
    slot = seg_id * cap + rank.astype(jnp.int32) - 1
    slot = jnp.where(sel > 0.0, slot, -1)
    slot_ref[...] = slot

    iota_c = lax.broadcasted_iota(jnp.int32, (ROUTE_SLOTS, ROUTE_TILE), 0)
    for e in range(N_EXPERTS):
        hit = iota_c == slot[e:e + 1, :]
        gcol_ref[e] = jnp.sum(jnp.where(hit, aff[e:e + 1, :], 0.0), axis=1, keepdims=True)


def _router(h2, w_router_t):
    def call(seg, tile0, ntiles):
        return pl.pallas_call(
            functools.partial(_router_kernel, seg=seg),
            grid=(ntiles,),
            in_specs=[
                pl.BlockSpec((ROUTE_TILE, D), lambda i: (tile0 + i, 0)),
                pl.BlockSpec((N_EXPERTS, D), lambda i: (0, 0)),
            ],
            out_specs=[
                pl.BlockSpec((None, N_EXPERTS, ROUTE_TILE), lambda i: (i, 0, 0)),
                pl.BlockSpec((N_EXPERTS, None, ROUTE_SLOTS, 1), lambda i: (0, i, 0, 0)),
            ],
            out_shape=[
                jax.ShapeDtypeStruct((ntiles, N_EXPERTS, ROUTE_TILE), jnp.int32),
                jax.ShapeDtypeStruct((N_EXPERTS, ntiles, ROUTE_SLOTS, 1), F32),
            ],
            compiler_params=_cparams(("arbitrary",)),
            name="router",
        )(h2, w_router_t)

    nct = M_CTX // ROUTE_TILE
    slot_c, g_c = call(SEQ, 0, nct)
    slot_l, g_l = call(DEC_SEQ, nct, N_ROUTE_TILES - nct)
    return jnp.concatenate([slot_c, slot_l], axis=0), jnp.concatenate([g_c, g_l], axis=1)


CTX_CAP = CAPACITY_FACTOR * SEQ // N_EXPERTS
CTX_SETS_PER_TILE = ROUTE_TILE // SEQ
N_CTX_TILES = M_CTX // ROUTE_TILE


def _dispatch_ctx_kernel(h_ref, slot_ref, o_ref):
    base = (pl.program_id(0) % CTX_SETS_PER_TILE) * CTX_CAP
    slot = slot_ref[...]
    iota_c = lax.broadcasted_iota(jnp.int32, (CTX_CAP, SEQ), 0) + base
    onehot = jnp.concatenate(
        [jnp.where(iota_c == slot[e:e + 1, :], 1.0, 0.0).astype(BF16) for e in range(N_EXPERTS)], axis=0)
    res = jnp.dot(onehot, h_ref[...], preferred_element_type=F32).astype(BF16)
    for e in range(N_EXPERTS):
        o_ref[e] = res[e * CTX_CAP:(e + 1) * CTX_CAP, :]


def _dispatch_lat_kernel(h_ref, slot_ref, prev_ref, o_ref):
    e = pl.program_id(1)
    srow = slot_ref[pl.ds(e, 1), :]
    iota_c = lax.broadcasted_iota(jnp.int32, (ROUTE_SLOTS, ROUTE_TILE), 0)
    onehot = jnp.where(iota_c == srow, 1.0, 0.0).astype(BF16)
    o_ref[...] = jnp.dot(onehot, h_ref[...], preferred_element_type=F32).astype(BF16)


def _dispatch(h2, slot_et):
    xg = pl.pallas_call(
        _dispatch_ctx_kernel,
        grid=(BATCH,),
        in_specs=[
            pl.BlockSpec((SEQ, D), lambda s: (s, 0)),
            pl.BlockSpec((None, N_EXPERTS, SEQ), lambda s: (s // CTX_SETS_PER_TILE, 0, s % CTX_SETS_PER_TILE)),
        ],
        out_specs=pl.BlockSpec((N_EXPERTS, None, CTX_CAP, D), lambda s: (0, s, 0, 0)),
        out_shape=jax.ShapeDtypeStruct((N_EXPERTS, SLOTS_PER_EXPERT // CTX_CAP, CTX_CAP, D), BF16),
        compiler_params=_cparams(("arbitrary",)),
        name="moe_dispatch_ctx",
    )(h2, slot_et)
    return pl.pallas_call(
        _dispatch_lat_kernel,
        grid=(N_ROUTE_TILES - N_CTX_TILES, N_EXPERTS),
        in_specs=[
            pl.BlockSpec((ROUTE_TILE, D), lambda i, e: (N_CTX_TILES + i, 0)),
            pl.BlockSpec((None, N_EXPERTS, ROUTE_TILE), lambda i, e: (N_CTX_TILES + i, 0, 0)),
            ANY_SPEC,
        ],
        out_specs=pl.BlockSpec((None, ROUTE_SLOTS, D), lambda i, e: (e, N_CTX_TILES + i, 0)),
        out_shape=jax.ShapeDtypeStruct((N_EXPERTS, SLOTS_PER_EXPERT, D), BF16),
        input_output_aliases={2: 0},
        compiler_params=_cparams(("arbitrary", "arbitrary")),
        name="moe_dispatch",
    )(h2, slot_et, xg.reshape(N_EXPERTS, SLOTS_PER_EXPERT, D))


def _ffn_kernel(x_ref, w1_ref, w3_ref, w2_ref, g_ref, o_ref, acc_ref, *, tn):
    f = pl.program_id(2)
    nf = pl.num_programs(2)

    @pl.when(f == 0)
    def _():
        acc_ref[...] = jnp.zeros(acc_ref.shape, F32)

    a = b = None
    for k in range(D // tn):
        ks = slice(k * tn, (k + 1) * tn)
        xk = x_ref[:, ks]
        pa = jnp.dot(xk, w1_ref[ks, :].astype(BF16), preferred_element_type=F32)
        pb = jnp.dot(xk, w3_ref[ks, :].astype(BF16), preferred_element_type=F32)
        a = pa if a is None else a + pa
        b = pb if b is None else b + pb
    hmid = ((a * _sigmoid(a)) * b).astype(BF16)
    for n in range(D // tn):
        cols = slice(n * tn, (n + 1) * tn)
        acc_ref[:, cols] += jnp.dot(hmid, w2_ref[:, cols].astype(BF16), preferred_element_type=F32)

    @pl.when(f == nf - 1)
    def _():
        o_ref[...] = (acc_ref[...] * g_ref[...]).astype(BF16)


def _ffn(xg, w1, w3, w2, gcol, layer):
    tr, tf, tn = 1024, 512, 512
    nr = SLOTS_PER_EXPERT // tr
    return pl.pallas_call(
        functools.partial(_ffn_kernel, tn=tn),
        grid=(N_EXPERTS, nr, EXPERT_FF // tf),
        in_specs=[
            pl.BlockSpec((None, tr, D), lambda e, r, f: (e, r, 0)),
            pl.BlockSpec((None, None, D, tf), lambda e, r, f: (layer, e, 0, f)),
            pl.BlockSpec((None, None, D, tf), lambda e, r, f: (layer, e, 0, f)),
            pl.BlockSpec((None, None, tf, D), lambda e, r, f: (layer, e, f, 0)),
            pl.BlockSpec((None, tr, 1), lambda e, r, f: (e, r, 0)),
        ],
        out_specs=pl.BlockSpec((None, tr, D), lambda e, r, f: (e, r, 0)),
        out_shape=jax.ShapeDtypeStruct((N_EXPERTS, SLOTS_PER_EXPERT, D), BF16),
        scratch_shapes=[pltpu.VMEM((tr, D), F32)],
        compiler_params=_cparams(("arbitrary", "arbitrary", "arbitrary")),
        name="moe_ffn",
    )(xg, w1, w3, w2, gcol)


def _combine_ctx_kernel(slot_ref, y_ref, x_ref, g2_ref, o_ref):
    base = (pl.program_id(0) % CTX_SETS_PER_TILE) * CTX_CAP
    slot = slot_ref[...]
    iota_c = lax.broadcasted_iota(jnp.int32, (SEQ, CTX_CAP), 1) + base
    onehot = jnp.concatenate(
        [jnp.where(iota_c == slot[:, e:e + 1], 1.0, 0.0).astype(BF16) for e in range(N_EXPERTS)], axis=1)
    y = y_ref[...].reshape(N_EXPERTS * CTX_CAP, y_ref.shape[-1])
    moe = jnp.dot(onehot, y, preferred_element_type=F32)
    o_ref[...] = x_ref[...] + g2_ref[...] * moe


def _combine_lat_kernel(slot_ref, y_ref, x_ref, g2_ref, prev_ref, o_ref, *, tr):
    r = pl.program_id(2)
    slot = slot_ref[pl.ds(pl.multiple_of(r * tr, tr), tr), :]
    iota_c = lax.broadcasted_iota(jnp.int32, (tr, ROUTE_SLOTS), 1)
    onehot = jnp.concatenate(
        [jnp.where(iota_c == slot[:, e:e + 1], 1.0, 0.0).astype(BF16) for e in range(N_EXPERTS)], axis=1)
    y = y_ref[...].reshape(N_EXPERTS * ROUTE_SLOTS, y_ref.shape[-1])
    moe = jnp.dot(onehot, y, preferred_element_type=F32)
    o_ref[...] = x_ref[...] + g2_ref[...] * moe


def _combine(slot_te, yg, x1, mod4):
    tn = D
    x2 = pl.pallas_call(
        _combine_ctx_kernel,
        grid=(BATCH, D // tn),
        in_specs=[
            pl.BlockSpec((None, SEQ, N_EXPERTS), lambda s, j: (s // CTX_SETS_PER_TILE, s % CTX_SETS_PER_TILE, 0)),
            pl.BlockSpec((N_EXPERTS, None, CTX_CAP, tn), lambda s, j: (0, s, 0, j)),
            pl.BlockSpec((SEQ, tn), lambda s, j: (s, j)),
            pl.BlockSpec((None, None, 1, tn), lambda s, j: (0, 5, 0, j)),
        ],
        out_specs=pl.BlockSpec((SEQ, tn), lambda s, j: (s, j)),
        out_shape=jax.ShapeDtypeStruct((M, D), F32),
        compiler_params=_cparams(("arbitrary", "arbitrary")),
        name="moe_combine_ctx",
    )(slot_te, yg.reshape(N_EXPERTS, SLOTS_PER_EXPERT // CTX_CAP, CTX_CAP, D), x1, mod4)
    tn = 1024
    tr = 512
    nr = ROUTE_TILE // tr
    nct = N_CTX_TILES
    return pl.pallas_call(
        functools.partial(_combine_lat_kernel, tr=tr),
        grid=(N_ROUTE_TILES - nct, D // tn, nr),
        in_specs=[
            pl.BlockSpec((None, ROUTE_TILE, N_EXPERTS), lambda i, j, r: (nct + i, 0, 0)),
            pl.BlockSpec((N_EXPERTS, None, ROUTE_SLOTS, tn), lambda i, j, r: (0, nct + i, 0, j)),
            pl.BlockSpec((tr, tn), lambda i, j, r: ((nct + i) * nr + r, j)),
            pl.BlockSpec((None, None, 1, tn), lambda i, j, r: (1 + i, 5, 0, j)),
            ANY_SPEC,
        ],
        out_specs=pl.BlockSpec((tr, tn), lambda i, j, r: ((nct + i) * nr + r, j)),
        out_shape=jax.ShapeDtypeStruct((M, D), F32),
        input_output_aliases={4: 0},
        compiler_params=_cparams(("arbitrary", "arbitrary", "arbitrary")),
        name="moe_combine",
    )(slot_te, yg.reshape(N_EXPERTS, N_ROUTE_TILES, ROUTE_SLOTS, D), x1, mod4, x2)


def _final_norm_kernel(x_ref, g_ref, o_ref):
    x = x_ref[...]
    o_ref[...] = (x * lax.rsqrt(jnp.mean(x * x, axis=-1, keepdims=True) + EPS)) * g_ref[...]


def _final_norm(x, g, is_ctx):
    tm = 512
    rows, row0 = _part(is_ctx)
    off = row0 // tm
    return pl.pallas_call(
        _final_norm_kernel,
        grid=(rows // tm,),
        in_specs=[pl.BlockSpec((tm, D), lambda i: (off + i, 0)), pl.BlockSpec((1, D), lambda i: (0, 0))],
        out_specs=pl.BlockSpec((tm, D), lambda i: (i, 0)),
        out_shape=jax.ShapeDtypeStruct((rows, D), F32),
        compiler_params=_cparams(("arbitrary",)),
        name="final_norm",
    )(x, g.reshape(1, D))


def _layer(x_parts, mod, lp, layer, lam_init, tables, cache_k4, cache_v4, st_re, st_im, new_kv):
    (xc, xc_row0), (xl, xl_row0) = x_parts
    rope_cos, rope_sin, dft_ctx, dft_lat, cs, perms = tables
    mod4 = mod.reshape(N_MOD_ROWS, 6, 1, D)
    w_in = lp['stacked']['w_in']
    h, u, ua_ctx = _norm_proj(xc, xc_row0, lp['norm1_g'], mod4, w_in, layer, True, None)
    h, u, ua_lat = _norm_proj(xl, xl_row0, lp['norm1_g'], mod4, w_in, layer, False, (h, u))

    wb, wc, a_re, a_im = _s5_params(lp)
    ns = S5_GROUPS * S5_STATE
    ngc = BATCH // SUBLANES
    zeros = jnp.zeros((ngc, 2, SUBLANES, ns), F32)
    y_ctx, fin_re, fin_im = _s5(ua_ctx, perms, wb, wc, a_re, a_im, zeros, zeros)
    h0_re = st_re.reshape(DEC_BATCH, 2, ns).transpose(1, 0, 2)[None]
    h0_im = st_im.reshape(DEC_BATCH, 2, ns).transpose(1, 0, 2)[None]
    y_lat, _, _ = _s5(ua_lat, perms, wb, wc, a_re, a_im, h0_re, h0_im)
    d_row = lp['s5_d'].reshape(1, S5_WIDTH)
    w_glu = lp['w_glu'].astype(BF16)
    b_glu = lp['b_glu'].reshape(1, S5_WIDTH)
    ya = _glu(y_ctx, u, d_row, w_glu, b_glu, True, SEQ, None)
    ya = _glu(y_lat, u, d_row, w_glu, b_glu, False, 512, ya)

    yb = _fnet(u, dft_ctx, cs, True, SEQ, None)
    yb = _fnet(u, dft_lat, cs, False, 512, yb)

    lam = (jnp.exp(jnp.sum(lp['lam_q1'].astype(F32) * lp['lam_k1'].astype(F32)))
           - jnp.exp(jnp.sum(lp['lam_q2'].astype(F32) * lp['lam_k2'].astype(F32))) + lam_init).reshape(1)
    subln = lp['subln_g'].reshape(1, VALUE_DIM)
    yc, new_k, new_v = _attention(u, lam, subln, lam_init, True, SEQ, N_HEADS, None, kv_out=(layer, new_kv))
    (yc,) = _attention(u, lam, subln, lam_init, False, DEC_SEQ, 1, yc,
                    ctx=(cache_k4, cache_v4, layer, rope_cos, rope_sin))

    st = lp['stacked']
    merged = _merge(h, ya, yb, yc, st['w_gate'], st['b_gate'], st['w_branch_a'], st['w_branch_b'],
                    st['w_branch_c'], layer)
    x1, h2 = _out_proj(merged, st['w_out'], layer, xc, xc_row0, lp['norm2_g'], mod4, True, None)
    x1, h2 = _out_proj(merged, st['w_out'], layer, xl, xl_row0, lp['norm2_g'], mod4, False, (x1, h2))

    slot_et, gcol = _router(h2, lp['w_router'].T.astype(BF16))
    xg = _dispatch(h2, slot_et)
    yg = _ffn(xg, lp['moe_w1_all'], lp['moe_w3_all'], lp['moe_w2_all'],
              gcol.reshape(N_EXPERTS, SLOTS_PER_EXPERT, 1), layer)
    x2 = _combine(slot_et.transpose(0, 2, 1), yg, x1, mod4)

    fin_shape = (ngc, 2, SUBLANES, S5_GROUPS, S5_STATE)
    s_re = fin_re.reshape(fin_shape).transpose(0, 2, 1, 3, 4).reshape(BATCH, 2, S5_GROUPS, S5_STATE)
    s_im = fin_im.reshape(fin_shape).transpose(0, 2, 1, 3, 4).reshape(BATCH, 2, S5_GROUPS, S5_STATE)
    return x2, (new_k, new_v), s_re, s_im


def kernel(x_prompt, x_sample, cache_k, cache_v, state_s5_re, state_s5_im, c, c_ctx, norm1_g, norm2_g, final_norm_g, w_ada, b_ada, w_in, s5_lam_re, s5_lam_im, s5_log_dt, s5_b_re, s5_b_im, s5_c_re, s5_c_im, s5_d, w_glu, b_glu, lam_q1, lam_k1, lam_q2, lam_k2, subln_g, w_branch_a, w_branch_b, w_branch_c, w_gate, b_gate, w_out, w_router, moe_w1, moe_w3, moe_w2):
    params = dict(norm1_g=norm1_g, norm2_g=norm2_g,
                  s5_lam_re=s5_lam_re, s5_lam_im=s5_lam_im, s5_log_dt=s5_log_dt,
                  s5_b_re=s5_b_re, s5_b_im=s5_b_im, s5_c_re=s5_c_re, s5_c_im=s5_c_im, s5_d=s5_d,
                  w_glu=w_glu, b_glu=b_glu, lam_q1=lam_q1, lam_k1=lam_k1, lam_q2=lam_q2, lam_k2=lam_k2,
                  subln_g=subln_g, w_router=w_router)
    stacked = dict(w_in=w_in.astype(BF16), w_gate=w_gate.astype(BF16), b_gate=b_gate.reshape(DEPTH, 1, 3 * D),
                   w_branch_a=w_branch_a.astype(BF16), w_branch_b=w_branch_b.astype(BF16),
                   w_branch_c=w_branch_c.astype(BF16), w_out=w_out.astype(BF16))
    cc = jnp.zeros((N_MOD_ROWS, D), F32).at[0].set(c_ctx).at[1:1 + DEC_BATCH].set(c)
    mods = _ada(cc, w_ada, b_ada)
    rope_cos, rope_sin = _rope_tables(DEC_SEQ)
    c_c, s_c = _dft_tables(FNET_GROUP_CH)
    cs = jnp.concatenate([c_c, s_c], axis=1).astype(BF16)
    tables = (rope_cos, rope_sin, _fnet_tables(SEQ), _fnet_tables(DEC_SEQ), cs, _s5_permutations())
    cache_k4 = cache_k.reshape(DEC_BATCH, DEPTH, PAST_LEN, QK_WIDTH)
    cache_v4 = cache_v.reshape(DEC_BATCH, DEPTH, PAST_LEN, V_WIDTH)
    x_parts = ((x_prompt.reshape(M_CTX, D), 0), (x_sample.reshape(M_LAT, D), 0))
    new_kv = None
    s_res, s_ims = [], []
    for l in range(DEPTH):
        lp = {name: arr[l] for name, arr in params.items()}
        lp['moe_w1_all'], lp['moe_w3_all'], lp['moe_w2_all'] = moe_w1, moe_w3, moe_w2
        lp['stacked'] = stacked
        lam_init = 0.8 - 0.6 * math.exp(-0.3 * l)
        x, new_kv, s_re, s_im = _layer(x_parts, mods[l], lp, l, lam_init, tables, cache_k4, cache_v4,
                                       state_s5_re[:, l], state_s5_im[:, l], new_kv)
        x_parts = ((x, 0), (x, M_CTX))
        s_res.append(s_re)
        s_ims.append(s_im)
    y_prompt = _final_norm(x, final_norm_g, True).reshape(BATCH, SEQ, D)
    y_sample = _final_norm(x, final_norm_g, False).reshape(DEC_BATCH, DEC_SEQ, D)
    kv_shape = (BATCH, DEPTH, SEQ, N_HEADS, VALUE_DIM)
    return (y_prompt, y_sample, new_kv[0].reshape(kv_shape), new_kv[1].reshape(kv_shape),
            jnp.stack(s_res, axis=1), jnp.stack(s_ims, axis=1))
```

```python
import functools
import math

import jax
import jax.numpy as jnp
import numpy as np
from jax import lax
from jax.experimental import pallas as pl
from jax.experimental.pallas import tpu as pltpu

F32 = jnp.float32
BF16 = jnp.bfloat16

D = 2048
BATCH, SEQ = 32, 256
DEC_BATCH, DEC_SEQ = 8, 2048
DEPTH = 2
PAST_LEN = 256
GRID_W = 64
EPS = 1e-6
S5_GROUP_CH, S5_GROUPS, S5_STATE = 16, 32, 64
S5_WIDTH = S5_GROUPS * S5_GROUP_CH
FNET_GROUPS, FNET_GROUP_CH = 4, 128
FNET_WIDTH = FNET_GROUPS * FNET_GROUP_CH
N_HEADS, HEAD_DIM = 8, 64
VALUE_DIM = 2 * HEAD_DIM
QK_WIDTH = N_HEADS * 2 * HEAD_DIM
V_WIDTH = N_HEADS * VALUE_DIM
IN_WIDTH = S5_WIDTH + FNET_WIDTH + 2 * QK_WIDTH + V_WIDTH
ROPE_BASE = 10000.0
ROPE_AXIS_DIM = HEAD_DIM // 2
N_EXPERTS = 16
EXPERT_FF = 2048
CAPACITY_FACTOR = 2

M_CTX = BATCH * SEQ
M_LAT = DEC_BATCH * DEC_SEQ
M = M_CTX + M_LAT
N_MOD_ROWS = 16

LANES = 128
SUBLANES = 8
VMEM_LIMIT = 56 * 1024 * 1024

ROUTE_TILE = 2048
ROUTE_SLOTS = CAPACITY_FACTOR * ROUTE_TILE // N_EXPERTS
N_ROUTE_TILES = M // ROUTE_TILE
SLOTS_PER_EXPERT = N_ROUTE_TILES * ROUTE_SLOTS

S5_BLK = 4
S5_BLK_STATES = 512

ANY_SPEC = pl.BlockSpec(memory_space=pl.ANY)


def _cparams(sem):
    return pltpu.CompilerParams(dimension_semantics=sem, vmem_limit_bytes=VMEM_LIMIT)


def _sigmoid(x):
    return 1.0 / (1.0 + jnp.exp(-x))


def _part(is_ctx):
    return (M_CTX, 0) if is_ctx else (M_LAT, M_CTX)


def _mod_row_fn(is_ctx, tm):
    if is_ctx:
        return lambda i: 0
    return lambda i: 1 + i // (DEC_SEQ // tm)


def _ada_kernel(c_ref, w_ref, b_ref, o_ref):
    c = c_ref[...]
    s = (c * _sigmoid(c)).astype(BF16)
    o_ref[...] = jnp.dot(s, w_ref[...].astype(BF16), preferred_element_type=F32) + b_ref[...]


def _ada(cc, w_ada, b_ada):
    tn = 1024
    n = 6 * D
    return pl.pallas_call(
        _ada_kernel,
        grid=(DEPTH, n // tn),
        in_specs=[
            pl.BlockSpec((N_MOD_ROWS, D), lambda l, j: (0, 0)),
            pl.BlockSpec((None, D, tn), lambda l, j: (l, 0, j)),
            pl.BlockSpec((None, 1, tn), lambda l, j: (l, 0, j)),
        ],
        out_specs=pl.BlockSpec((None, N_MOD_ROWS, tn), lambda l, j: (l, 0, j)),
        out_shape=jax.ShapeDtypeStruct((DEPTH, N_MOD_ROWS, n), F32),
        compiler_params=_cparams(("arbitrary", "arbitrary")),
        name="ada",
    )(cc, w_ada, b_ada.reshape(DEPTH, 1, n))


NP_TM, NP_TN = 1024, 1024


NP_NJ = IN_WIDTH // NP_TN


def _norm_proj_kernel(x_ref, g_ref, sc_ref, sh_ref, w_ref, *rest, is_ctx, n_alias, tm):
    h_ref, u_ref, ua_ref, ha_ref, hb_ref = rest[n_alias:]
    i = pl.program_id(0)
    j = pl.program_id(1)
    q = tm // NP_NJ
    rows = pl.ds(pl.multiple_of(j * q, q), q)

    def norm_quarter(dst_ref):
        x = x_ref[rows, :]
        r = x * lax.rsqrt(jnp.mean(x * x, axis=-1, keepdims=True) + EPS)
        dst_ref[rows, :] = ((r * g_ref[...]) * (1.0 + sc_ref[...]) + sh_ref[...]).astype(BF16)

    def project(src_ref):
        h_ref[rows, :] = src_ref[rows, :]
        u_ref[...] = jnp.dot(src_ref[...], w_ref[...], preferred_element_type=F32)

    @pl.when(i == 0)
    def _():
        norm_quarter(ha_ref)

    @pl.when((i > 0) & (i % 2 == 0))
    def _():
        norm_quarter(ha_ref)
        project(hb_ref)

    @pl.when(i % 2 == 1)
    def _():
        norm_quarter(hb_ref)
        project(ha_ref)

    @pl.when((i > 0) & (j == 0))
    def _():
        if is_ctx:
            for b in range(tm // SEQ):
                ua_ref[:, b * S5_WIDTH:(b + 1) * S5_WIDTH] = u_ref[b * SEQ:(b + 1) * SEQ, :S5_WIDTH]
        else:
            ua_ref[...] = u_ref[:, :S5_WIDTH]


def _norm_proj(x, x_row0, g, mod4, w_in, layer, is_ctx, prev):
    tm, tn = NP_TM, NP_TN
    assert S5_WIDTH <= tn
    rows, row0 = _part(is_ctx)
    n_i = rows // tm
    xoff, ooff = x_row0 // tm, row0 // tm
    mrow = _mod_row_fn(is_ctx, tm)
    cur = lambda i: jnp.minimum(i, n_i - 1)
    prv = lambda i: jnp.maximum(i - 1, 0)
    prev = list(prev) if prev is not None else []
    in_specs = [
        pl.BlockSpec((tm, D), lambda i, j: (xoff + cur(i), 0)),
        pl.BlockSpec((1, D), lambda i, j: (0, 0)),
        pl.BlockSpec((None, None, 1, D), lambda i, j: (mrow(cur(i)), 1, 0, 0)),
        pl.BlockSpec((None, None, 1, D), lambda i, j: (mrow(cur(i)), 0, 0, 0)),
        pl.BlockSpec((None, D, tn), lambda i, j: (layer, 0, j)),
    ] + [ANY_SPEC] * len(prev)
    out_specs = [
        pl.BlockSpec((tm, D), lambda i, j: (ooff + prv(i), 0)),
        pl.BlockSpec((tm, tn), lambda i, j: (ooff + prv(i), j)),
    ]
    out_shape = [jax.ShapeDtypeStruct((M, D), BF16), jax.ShapeDtypeStruct((M, IN_WIDTH), F32)]
    if is_ctx:
        nbt = tm // SEQ
        out_specs.append(pl.BlockSpec((SEQ, nbt * S5_WIDTH), lambda i, j: (0, prv(i))))
        out_shape.append(jax.ShapeDtypeStruct((SEQ, BATCH * S5_WIDTH), F32))
        aliases = {}
    else:
        tpb = DEC_SEQ // tm
        out_specs.append(pl.BlockSpec((tm, S5_WIDTH), lambda i, j: (prv(i) % tpb, prv(i) // tpb)))
        out_shape.append(jax.ShapeDtypeStruct((DEC_SEQ, DEC_BATCH * S5_WIDTH), F32))
        aliases = {5: 0, 6: 1}
    return pl.pallas_call(
        functools.partial(_norm_proj_kernel, is_ctx=is_ctx, n_alias=len(prev), tm=tm),
        grid=(n_i + 1, NP_NJ),
        in_specs=in_specs,
        out_specs=out_specs,
        out_shape=out_shape,
        scratch_shapes=[pltpu.VMEM((tm, D), BF16), pltpu.VMEM((tm, D), BF16)],
        input_output_aliases=aliases,
        compiler_params=_cparams(("arbitrary", "arbitrary")),
        name="norm_proj",
    )(x, g.reshape(1, D), mod4, mod4, w_in, *prev)


S5_CHUNK = 64


def _s5_kernel(u_ref, pin_ref, pout_ref, wb_ref, wc_ref, are_ref, aim_ref, h0re_ref, h0im_ref,
               y_ref, finre_ref, finim_ref, xs0_ref, xs1_ref, xs2_ref, xs3_ref, stre_ref, stim_ref):
    c = pl.program_id(2)
    nc = pl.num_programs(2)
    bs = S5_BLK_STATES
    chunk = S5_CHUNK
    xs_refs = (xs0_ref, xs1_ref, xs2_ref, xs3_ref)

    @pl.when(c == 0)
    def _():
        stre_ref[...] = h0re_ref[...]
        stim_ref[...] = h0im_ref[...]

    ustack = jnp.concatenate(
        [u_ref[:, j * S5_WIDTH:(j + 1) * S5_WIDTH] for j in range(SUBLANES)], axis=0).astype(BF16)
    up = jnp.dot(pin_ref[...], ustack, preferred_element_type=F32).astype(BF16)

    for blk in range(S5_BLK):
        xs_refs[blk][...] = jnp.dot(up[:, blk * LANES:(blk + 1) * LANES], wb_ref[blk],
                                    preferred_element_type=F32)

    for blk in range(S5_BLK):
        xs_ref = xs_refs[blk]
        ar = are_ref[:, blk * bs:(blk + 1) * bs]
        ai = aim_ref[:, blk * bs:(blk + 1) * bs]
        sr = stre_ref[:, blk * bs:(blk + 1) * bs]
        si = stim_ref[:, blk * bs:(blk + 1) * bs]
        for i in range(chunk):
            r8 = slice(i * SUBLANES, (i + 1) * SUBLANES)
            nr = ar * sr - ai * si + xs_ref[r8, 0:bs]
            ni = ar * si + ai * sr + xs_ref[r8, bs:2 * bs]
            xs_ref[r8, 0:bs] = nr
            xs_ref[r8, bs:2 * bs] = ni
            sr, si = nr, ni
        stre_ref[:, blk * bs:(blk + 1) * bs] = sr
        stim_ref[:, blk * bs:(blk + 1) * bs] = si

    y = jnp.concatenate(
        [jnp.dot(xs_refs[blk][...].astype(BF16), wc_ref[blk], preferred_element_type=F32)
         for blk in range(S5_BLK)], axis=1)
    y_hi = y.astype(BF16)
    y_lo = (y - y_hi.astype(F32)).astype(BF16)
    ys = (jnp.dot(pout_ref[...], y_hi, preferred_element_type=F32)
          + jnp.dot(pout_ref[...], y_lo, preferred_element_type=F32))
    for j in range(SUBLANES):
        y_ref[:, j * S5_WIDTH:(j + 1) * S5_WIDTH] = ys[j * chunk:(j + 1) * chunk, :]

    @pl.when(c == nc - 1)
    def _():
        finre_ref[...] = stre_ref[...]
        finim_ref[...] = stim_ref[...]


def _s5_permutations():
    chunk = S5_CHUNK
    rows = chunk * SUBLANES
    r_out = lax.broadcasted_iota(jnp.int32, (rows, rows), 0)
    r_in = lax.broadcasted_iota(jnp.int32, (rows, rows), 1)
    step, seq = r_out // SUBLANES, r_out % SUBLANES
    fwd = r_in == seq * chunk + step
    bwd = r_in == seq * chunk + (chunk - 1 - step)
    p = jnp.stack([fwd, bwd]).astype(BF16)
    return p, p.transpose(0, 2, 1)


def _s5(u2d, perms, wb, wc, a_re, a_im, h0_re, h0_im):
    t_len = u2d.shape[0]
    ng = u2d.shape[1] // (SUBLANES * S5_WIDTH)
    chunk = S5_CHUNK
    nc = t_len // chunk
    rows = chunk * SUBLANES
    ns = S5_BLK * S5_BLK_STATES
    pin, pout = perms

    def ceff(d, c):
        return jnp.where(d == 0, c, nc - 1 - c)

    return pl.pallas_call(
        _s5_kernel,
        grid=(ng, 2, nc),
        in_specs=[
            pl.BlockSpec((chunk, SUBLANES * S5_WIDTH), lambda g, d, c: (ceff(d, c), g)),
            pl.BlockSpec((None, rows, rows), lambda g, d, c: (d, 0, 0)),
            pl.BlockSpec((None, rows, rows), lambda g, d, c: (d, 0, 0)),
            pl.BlockSpec((None, S5_BLK, LANES, 2 * S5_BLK_STATES), lambda g, d, c: (d, 0, 0, 0)),
            pl.BlockSpec((None, S5_BLK, 2 * S5_BLK_STATES, LANES), lambda g, d, c: (d, 0, 0, 0)),
            pl.BlockSpec((None, SUBLANES, ns), lambda g, d, c: (d, 0, 0)),
            pl.BlockSpec((None, SUBLANES, ns), lambda g, d, c: (d, 0, 0)),
            pl.BlockSpec((None, None, SUBLANES, ns), lambda g, d, c: (g, d, 0, 0)),
            pl.BlockSpec((None, None, SUBLANES, ns), lambda g, d, c: (g, d, 0, 0)),
        ],
        out_specs=[
            pl.BlockSpec((None, chunk, SUBLANES * S5_WIDTH), lambda g, d, c: (d, ceff(d, c), g)),
            pl.BlockSpec((None, None, SUBLANES, ns), lambda g, d, c: (g, d, 0, 0)),
            pl.BlockSpec((None, None, SUBLANES, ns), lambda g, d, c: (g, d, 0, 0)),
        ],
        out_shape=[
            jax.ShapeDtypeStruct((2,) + u2d.shape, F32),
            jax.ShapeDtypeStruct((ng, 2, SUBLANES, ns), F32),
            jax.ShapeDtypeStruct((ng, 2, SUBLANES, ns), F32),
        ],
        scratch_shapes=[pltpu.VMEM((rows, 2 * S5_BLK_STATES), F32)] * S5_BLK + [
            pltpu.VMEM((SUBLANES, ns), F32),
            pltpu.VMEM((SUBLANES, ns), F32),
        ],
        compiler_params=_cparams(("arbitrary", "arbitrary", "arbitrary")),
        name="s5_scan",
    )(u2d, pin, pout, wb, wc, a_re, a_im, h0_re, h0_im)


def _s5_params(lp):
    lam_re = lp['s5_lam_re'].astype(F32)
    lam_im = lp['s5_lam_im'].astype(F32)
    dt = jnp.exp(lp['s5_log_dt'].astype(F32))[..., None]
    b_re = lp['s5_b_re'].astype(F32)
    b_im = lp['s5_b_im'].astype(F32)
    c_re = lp['s5_c_re'].astype(F32)
    c_im = lp['s5_c_im'].astype(F32)
    mag = jnp.exp(lam_re * dt)
    ab_re = mag * jnp.cos(lam_im * dt)
    ab_im = mag * jnp.sin(lam_im * dt)
    den = lam_re * lam_re + lam_im * lam_im
    f_re = ((ab_re - 1.0) * lam_re + ab_im * lam_im) / den
    f_im = (ab_im * lam_re - (ab_re - 1.0) * lam_im) / den
    bb_re = f_re[..., None] * b_re - f_im[..., None] * b_im
    bb_im = f_re[..., None] * b_im + f_im[..., None] * b_re
    gpb = S5_GROUPS // S5_BLK
    eye = jnp.eye(gpb, dtype=F32)

    def pack_b(bb):
        bb = bb.reshape(2, S5_BLK, gpb, S5_STATE, S5_GROUP_CH)
        w = jnp.einsum('dbgph,gk->dbghkp', bb, eye)
        return w.reshape(2, S5_BLK, gpb * S5_GROUP_CH, gpb * S5_STATE)

    def pack_c(cc):
        cc = cc.reshape(2, S5_BLK, gpb, S5_GROUP_CH, S5_STATE)
        w = jnp.einsum('dbghp,gk->dbkpgh', cc, eye)
        return w.reshape(2, S5_BLK, gpb * S5_STATE, gpb * S5_GROUP_CH)

    wb = jnp.concatenate([pack_b(bb_re), pack_b(bb_im)], axis=-1).astype(BF16)
    wc = jnp.concatenate([pack_c(c_re), -pack_c(c_im)], axis=-2).astype(BF16)
    ns = S5_GROUPS * S5_STATE
    a_re = jnp.broadcast_to(ab_re.reshape(2, 1, ns), (2, SUBLANES, ns))
    a_im = jnp.broadcast_to(ab_im.reshape(2, 1, ns), (2, SUBLANES, ns))
    return wb, wc, a_re, a_im


def _gelu_tanh(x):
    return 0.5 * x * (1.0 + jnp.tanh(math.sqrt(2.0 / math.pi) * (x + 0.044715 * (x * x * x))))


def _glu_kernel(y_ref, u_ref, d_ref, w_ref, b_ref, *rest):
    o_ref = rest[-1]
    y = y_ref[0] + y_ref[1] + d_ref[...] * u_ref[...]
    y = _gelu_tanh(y)
    z = jnp.dot(y.astype(BF16), w_ref[...], preferred_element_type=F32) + b_ref[...]
    o_ref[...] = (y * _sigmoid(z)).astype(BF16)


def _glu(y_dirs, u, d_row, w_glu, b_glu, is_ctx, tt, prev):
    t_len = y_dirs.shape[1]
    nb = y_dirs.shape[2] // S5_WIDTH
    ntt = t_len // tt
    off = _part(is_ctx)[1] // tt
    prev = [prev] if prev is not None else []
    return pl.pallas_call(
        _glu_kernel,
        grid=(nb, ntt),
        in_specs=[
            pl.BlockSpec((2, tt, S5_WIDTH), lambda b, t: (0, t, b)),
            pl.BlockSpec((tt, S5_WIDTH), lambda b, t: (off + b * ntt + t, 0)),
            pl.BlockSpec((1, S5_WIDTH), lambda b, t: (0, 0)),
            pl.BlockSpec((S5_WIDTH, S5_WIDTH), lambda b, t: (0, 0)),
            pl.BlockSpec((1, S5_WIDTH), lambda b, t: (0, 0)),
        ] + [ANY_SPEC] * len(prev),
        out_specs=pl.BlockSpec((tt, S5_WIDTH), lambda b, t: (off + b * ntt + t, 0)),
        out_shape=jax.ShapeDtypeStruct((M, S5_WIDTH), BF16),
        input_output_aliases={5: 0} if prev else {},
        compiler_params=_cparams(("arbitrary", "arbitrary")),
        name="s5_glu",
    )(y_dirs, u, d_row, w_glu, b_glu, *prev)


def _fnet_kernel(x_ref, cs_ref, dft_ref, *rest, t_len):
    o_ref, z_ref = rest[-2:]

    @pl.when(pl.program_id(1) == 0)
    def _():
        x = x_ref[...].astype(BF16)
        for g in range(FNET_GROUPS):
            z = jnp.dot(x[:, g * LANES:(g + 1) * LANES], cs_ref[...], preferred_element_type=F32)
            z_ref[0:t_len, g * LANES:(g + 1) * LANES] = z[:, :LANES].astype(BF16)
            z_ref[t_len:2 * t_len, g * LANES:(g + 1) * LANES] = z[:, LANES:].astype(BF16)

    scale = 1.0 / math.sqrt(t_len * FNET_GROUP_CH)
    o_ref[...] = (jnp.dot(dft_ref[...], z_ref[...], preferred_element_type=F32) * scale).astype(BF16)


def _dft_tables(n):
    j = lax.broadcasted_iota(jnp.int32, (n, n), 0)
    k = lax.broadcasted_iota(jnp.int32, (n, n), 1)
    ang = ((j * k) % n).astype(F32) * (2.0 * math.pi / n)
    return jnp.cos(ang), jnp.sin(ang)


def _fnet_tables(t_len):
    j = np.arange(t_len, dtype=np.int64)
    ang = ((j[:, None] * j[None, :]) % t_len).astype(np.float64) * (2.0 * math.pi / t_len)
    return jnp.asarray(np.concatenate([np.cos(ang), -np.sin(ang)], axis=1), dtype=BF16)


def _fnet(u, dft, cs, is_ctx, tr, prev):
    t_len = dft.shape[0]
    rows, row0 = _part(is_ctx)
    nb = rows // t_len
    ntr = t_len // tr
    off = row0 // t_len
    offr = row0 // tr
    col = S5_WIDTH // FNET_WIDTH
    prev = [prev] if prev is not None else []
    return pl.pallas_call(
        functools.partial(_fnet_kernel, t_len=t_len),
        grid=(nb, ntr),
        in_specs=[
            pl.BlockSpec((t_len, FNET_WIDTH), lambda b, r: (off + b, col)),
            pl.BlockSpec((FNET_GROUP_CH, 2 * FNET_GROUP_CH), lambda b, r: (0, 0)),
            pl.BlockSpec((tr, 2 * t_len), lambda b, r: (r, 0)),
        ] + [ANY_SPEC] * len(prev),
        out_specs=pl.BlockSpec((tr, FNET_WIDTH), lambda b, r: (offr + b * ntr + r, 0)),
        out_shape=jax.ShapeDtypeStruct((M, FNET_WIDTH), BF16),
        scratch_shapes=[pltpu.VMEM((2 * t_len, FNET_WIDTH), BF16)],
        input_output_aliases={3: 0} if prev else {},
        compiler_params=_cparams(("arbitrary", "arbitrary")),
        name="fnet",
    )(u, cs, dft, *prev)


def _rope(x, cos, sin):
    half = ROPE_AXIS_DIM // 2
    lane = lax.broadcasted_iota(jnp.int32, x.shape, 1)
    up = pltpu.roll(x, LANES - half, 1)
    dn = pltpu.roll(x, half, 1)
    rot = jnp.where((lane % ROPE_AXIS_DIM) < half, -up, dn)
    return x * cos + rot * sin


ATTN_KEY_CHUNK = 512
ATTN_ROW_BLOCK = 256


def _attn_kernel(lam_ref, q_ref, k_ref, v_ref, *rest, tq, t_len, hb, has_ctx, lam_init, n_alias, emit_kv):
    pos = 0
    if has_ctx:
        ck_ref, cv_ref, cos_ref, sin_ref = rest[:4]
        pos = 4
    g_ref = rest[pos]
    pos += 1 + n_alias
    o_ref = rest[pos]
    pos += 1
    if emit_kv:
        ko_ref, vo_ref = rest[pos:pos + 2]
        pos += 2
    kk_ref, vv_ref, qs_ref, s_ref = rest[pos:]
    qi = pl.program_id(2)
    if emit_kv:
        @pl.when(qi == 0)
        def _():
            ko_ref[...] = k_ref[...]
            vo_ref[...] = v_ref[...]
    n_keys = kk_ref.shape[1]
    kc, rb = ATTN_KEY_CHUNK, ATTN_ROW_BLOCK
    chunks = [(c0, min(c0 + kc, n_keys)) for c0 in range(0, n_keys, kc)]

    @pl.when(qi == 0)
    def _():
        for h in range(hb):
            hs = slice(h * LANES, (h + 1) * LANES)
            k = k_ref[:, hs]
            if has_ctx:
                k = _rope(k, cos_ref[...], sin_ref[...])
                kk_ref[h, t_len:, :] = ck_ref[:, hs].astype(BF16)
                vv_ref[h, t_len:, 0:VALUE_DIM] = cv_ref[:, hs].astype(BF16)
            kk_ref[h, 0:t_len, :] = k.astype(BF16)
            vv_ref[h, 0:t_len, 0:VALUE_DIM] = v_ref[:, hs].astype(BF16)
            vv_ref[h, :, VALUE_DIM:] = jnp.ones((n_keys, VALUE_DIM), BF16)

    lane = lax.broadcasted_iota(jnp.int32, (tq, LANES), 1)
    for h in range(hb):
        hs = slice(h * LANES, (h + 1) * LANES)
        q = q_ref[:, hs]
        if has_ctx:
            r0 = pl.multiple_of(qi * tq, tq)
            q = _rope(q, cos_ref[pl.ds(r0, tq), :], sin_ref[pl.ds(r0, tq), :])
        q = q * (HEAD_DIM ** -0.5)
        qs_ref[0, :, hs] = jnp.where(lane < HEAD_DIM, q, 0.0).astype(BF16)
        qs_ref[1, :, hs] = jnp.where(lane >= HEAD_DIM, q, 0.0).astype(BF16)

    lam = lam_ref[0]
    gain = g_ref[...]
    dn = (((1,), (1,)), ((), ()))

    unit = 0
    for r in range(tq // rb):
        rows = slice(r * rb, (r + 1) * rb)
        for h in range(hb):
            hs = slice(h * LANES, (h + 1) * LANES)
            o = None
            for n in range(2):
                slot = unit % 2
                unit += 1
                qn = qs_ref[n, rows, hs]
                mrun = None
                for c0, c1 in chunks:
                    s_c = lax.dot_general(qn, kk_ref[h, c0:c1, :], dn, preferred_element_type=F32)
                    s_ref[slot, :, c0:c1] = s_c
                    for l0 in range(0, c1 - c0, LANES):
                        mc = s_c[:, l0:l0 + LANES]
                        mrun = mc if mrun is None else jnp.maximum(mrun, mc)
                m_b = jnp.broadcast_to(jnp.max(mrun, axis=-1, keepdims=True), (rb, LANES))
                oa = None
                for c0, c1 in chunks:
                    p = jnp.concatenate(
                        [jnp.exp(s_ref[slot, :, l0:l0 + LANES] - m_b) for l0 in range(c0, c1, LANES)],
                        axis=1).astype(BF16)
                    part = jnp.dot(p, vv_ref[h, c0:c1, :], preferred_element_type=F32)
                    oa = part if oa is None else oa + part
                on = oa[:, :VALUE_DIM] * (1.0 / oa[:, VALUE_DIM:VALUE_DIM + 1])
                o = on if n == 0 else o - lam * on
            o = o * lax.rsqrt(jnp.mean(o * o, axis=-1, keepdims=True) + EPS)
            o_ref[rows, hs] = ((o * gain) * (1.0 - lam_init)).astype(BF16)


def _attention(u, lam, subln_g, lam_init, is_ctx, tq, hb, prev, ctx=None, kv_out=None):
    rows, row0 = _part(is_ctx)
    t_len = SEQ if is_ctx else DEC_SEQ
    nb = rows // t_len
    nq = t_len // tq
    off = row0 // t_len
    offq = row0 // tq
    width = hb * LANES
    qcol = (S5_WIDTH + FNET_WIDTH) // width
    kcol = qcol + QK_WIDTH // width
    vcol = kcol + QK_WIDTH // width
    has_ctx = ctx is not None
    n_keys = t_len + (PAST_LEN if has_ctx else 0)
    assert tq % ATTN_ROW_BLOCK == 0 and n_keys % (2 * LANES) == 0
    in_specs = [
        pl.BlockSpec(memory_space=pltpu.SMEM),
        pl.BlockSpec((tq, width), lambda b, h, i: (offq + b * nq + i, qcol + h)),
        pl.BlockSpec((t_len, width), lambda b, h, i: (off + b, kcol + h)),
        pl.BlockSpec((t_len, width), lambda b, h, i: (off + b, vcol + h)),
    ]
    args = [lam, u, u, u]
    if has_ctx:
        ck, cv, layer, cos, sin = ctx
        in_specs += [
            pl.BlockSpec((None, None, PAST_LEN, width), lambda b, h, i: (b, layer, 0, h)),
            pl.BlockSpec((None, None, PAST_LEN, width), lambda b, h, i: (b, layer, 0, h)),
            pl.BlockSpec((t_len, LANES), lambda b, h, i: (0, 0)),
            pl.BlockSpec((t_len, LANES), lambda b, h, i: (0, 0)),
        ]
        args += [ck, cv, cos, sin]
    in_specs.append(pl.BlockSpec((1, LANES), lambda b, h, i: (0, 0)))
    args.append(subln_g)
    aliases = {}
    n_fixed = len(args)
    if prev is not None:
        aliases[len(args)] = 0
        in_specs.append(ANY_SPEC)
        args.append(prev)
    out_specs = [pl.BlockSpec((tq, width), lambda b, h, i: (offq + b * nq + i, h))]
    out_shape = [jax.ShapeDtypeStruct((M, V_WIDTH), BF16)]
    if kv_out is not None:
        kv_layer, prev_kv = kv_out
        assert hb == N_HEADS and tq == t_len and nb == BATCH
        out_specs += [pl.BlockSpec((None, None, SEQ, QK_WIDTH), lambda b, h, i: (b, kv_layer, 0, 0)),
                      pl.BlockSpec((None, None, SEQ, V_WIDTH), lambda b, h, i: (b, kv_layer, 0, 0))]
        out_shape += [jax.ShapeDtypeStruct((BATCH, DEPTH, SEQ, QK_WIDTH), F32),
                      jax.ShapeDtypeStruct((BATCH, DEPTH, SEQ, V_WIDTH), F32)]
        if prev_kv is not None:
            for o_idx, arr in enumerate(prev_kv):
                aliases[len(args)] = 1 + o_idx
                in_specs.append(ANY_SPEC)
                args.append(arr)
    kern = functools.partial(_attn_kernel, tq=tq, t_len=t_len, hb=hb, has_ctx=has_ctx, lam_init=lam_init,
                             n_alias=len(args) - n_fixed, emit_kv=kv_out is not None)
    return pl.pallas_call(
        kern,
        grid=(nb, N_HEADS // hb, nq),
        in_specs=in_specs,
        out_specs=out_specs,
        out_shape=out_shape,
        scratch_shapes=[
            pltpu.VMEM((hb, n_keys, LANES), BF16),
            pltpu.VMEM((hb, n_keys, 2 * VALUE_DIM), BF16),
            pltpu.VMEM((2, tq, width), BF16),
            pltpu.VMEM((2, ATTN_ROW_BLOCK, n_keys), F32),
        ],
        input_output_aliases=aliases,
        compiler_params=_cparams(("arbitrary", "arbitrary", "arbitrary")),
        name="diff_attn",
    )(*args)


def _rope_tables(t_len):
    rows = t_len // GRID_W
    pos_row = jnp.broadcast_to(jnp.arange(rows, dtype=F32)[:, None], (rows, GRID_W)).reshape(-1)
    pos_col = jnp.broadcast_to(jnp.arange(GRID_W, dtype=F32)[None, :], (rows, GRID_W)).reshape(-1)
    inv = ROPE_BASE ** (-jnp.arange(0, ROPE_AXIS_DIM, 2, dtype=F32) / ROPE_AXIS_DIM)
    ang_r = pos_row[:, None] * inv
    ang_c = pos_col[:, None] * inv
    cos = jnp.concatenate([jnp.cos(ang_r), jnp.cos(ang_r), jnp.cos(ang_c), jnp.cos(ang_c)], axis=-1)
    sin = jnp.concatenate([jnp.sin(ang_r), jnp.sin(ang_r), jnp.sin(ang_c), jnp.sin(ang_c)], axis=-1)
    return jnp.concatenate([cos, cos], axis=-1), jnp.concatenate([sin, sin], axis=-1)


def _merge_kernel(h_ref, ya_ref, yb_ref, yc_ref, wga_ref, wgb_ref, wgc_ref, bga_ref, bgb_ref, bgc_ref,
                  wa_ref, wb_ref, wc_ref, o_ref):
    h = h_ref[...]
    acc = None
    for wg, bg, y, w in ((wga_ref, bga_ref, ya_ref, wa_ref), (wgb_ref, bgb_ref, yb_ref, wb_ref),
                         (wgc_ref, bgc_ref, yc_ref, wc_ref)):
        gate = _sigmoid(jnp.dot(h, wg[...], preferred_element_type=F32) + bg[...])
        term = gate * jnp.dot(y[...], w[...], preferred_element_type=F32)
        acc = term if acc is None else acc + term
    o_ref[...] = acc.astype(BF16)


def _merge(h, ya, yb, yc, w_gate, b_gate, w_a, w_b, w_c, layer):
    tm, tn = 1024, 512
    nj = D // tn
    row = lambda i, j: (i, 0)
    wspecs = [pl.BlockSpec((None, D, tn), lambda i, j, k=k: (layer, 0, k * nj + j)) for k in range(3)]
    bspecs = [pl.BlockSpec((None, 1, tn), lambda i, j, k=k: (layer, 0, k * nj + j)) for k in range(3)]
    return pl.pallas_call(
        _merge_kernel,
        grid=(M // tm, nj),
        in_specs=[
            pl.BlockSpec((tm, D), row),
            pl.BlockSpec((tm, S5_WIDTH), row),
            pl.BlockSpec((tm, FNET_WIDTH), row),
            pl.BlockSpec((tm, V_WIDTH), row),
            *wspecs, *bspecs,
            pl.BlockSpec((None, S5_WIDTH, tn), lambda i, j: (layer, 0, j)),
            pl.BlockSpec((None, FNET_WIDTH, tn), lambda i, j: (layer, 0, j)),
            pl.BlockSpec((None, V_WIDTH, tn), lambda i, j: (layer, 0, j)),
        ],
        out_specs=pl.BlockSpec((tm, tn), lambda i, j: (i, j)),
        out_shape=jax.ShapeDtypeStruct((M, D), BF16),
        compiler_params=_cparams(("arbitrary", "arbitrary")),
        name="merge",
    )(h, ya, yb, yc, w_gate, w_gate, w_gate, b_gate, b_gate, b_gate, w_a, w_b, w_c)


def _out_proj_kernel(m_ref, w_ref, x_ref, g1_ref, n2_ref, sc_ref, sh_ref, *rest):
    x1_ref, h2_ref = rest[-2:]
    x1 = x_ref[...] + g1_ref[...] * jnp.dot(m_ref[...], w_ref[...], preferred_element_type=F32)
    x1_ref[...] = x1
    r = x1 * lax.rsqrt(jnp.mean(x1 * x1, axis=-1, keepdims=True) + EPS)
    h2_ref[...] = ((r * n2_ref[...]) * (1.0 + sc_ref[...]) + sh_ref[...]).astype(BF16)


def _out_proj(merged, w_out, layer, x, x_row0, norm2_g, mod4, is_ctx, prev):
    tm = 512
    rows, row0 = _part(is_ctx)
    xoff, ooff = x_row0 // tm, row0 // tm
    mrow = _mod_row_fn(is_ctx, tm)
    orow = lambda i: (ooff + i, 0)
    mspec = lambda k: pl.BlockSpec((None, None, 1, D), lambda i, k=k: (mrow(i), k, 0, 0))
    prev = list(prev) if prev is not None else []
    return pl.pallas_call(
        _out_proj_kernel,
        grid=(rows // tm,),
        in_specs=[
            pl.BlockSpec((tm, D), orow),
            pl.BlockSpec((None, D, D), lambda i: (layer, 0, 0)),
            pl.BlockSpec((tm, D), lambda i: (xoff + i, 0)),
            mspec(2),
            pl.BlockSpec((1, D), lambda i: (0, 0)),
            mspec(4),
            mspec(3),
        ] + [ANY_SPEC] * len(prev),
        out_specs=[pl.BlockSpec((tm, D), orow), pl.BlockSpec((tm, D), orow)],
        out_shape=[jax.ShapeDtypeStruct((M, D), F32), jax.ShapeDtypeStruct((M, D), BF16)],
        input_output_aliases={7: 0, 8: 1} if prev else {},
        compiler_params=_cparams(("arbitrary",)),
        name="out_proj",
    )(merged, w_out, x, mod4, norm2_g.reshape(1, D), mod4, mod4, *prev)


def _segment_sum(x, seg):
    parts = []
    for s in range(ROUTE_TILE // seg):
        tot = jnp.sum(x[:, s * seg:(s + 1) * seg], axis=1, keepdims=True)
        parts.append(jnp.broadcast_to(tot, (x.shape[0], seg)))
    return parts[0] if len(parts) == 1 else jnp.concatenate(parts, axis=1)


def _segment_cumsum(x, seg):
    pos = lax.broadcasted_iota(jnp.int32, x.shape, 1) % seg
    sh = 1
    while sh < seg:
        x = x + jnp.where(pos >= sh, pltpu.roll(x, sh, 1), 0.0)
        sh *= 2
    return x


def _router_kernel(h_ref, wr_ref, slot_ref, gcol_ref, *, seg):
    cap = CAPACITY_FACTOR * seg // N_EXPERTS
    logits = lax.dot_general(wr_ref[...], h_ref[...], (((1,), (1,)), ((), ())),
                             preferred_element_type=F32)
    p = jnp.exp(logits - jnp.max(logits, axis=0, keepdims=True))
    aff = p / jnp.sum(p, axis=0, keepdims=True)
    bits = pltpu.bitcast(aff, jnp.int32)

    def bis(i, thr):
        cand = thr | jnp.left_shift(jnp.int32(1), 29 - i)
        cnt = _segment_sum(jnp.where(bits >= cand, 1.0, 0.0), seg)
        return jnp.where(cnt >= cap, cand, thr)

    thr = lax.fori_loop(0, 30, bis, jnp.zeros(bits.shape, jnp.int32))
    gt = jnp.where(bits > thr, 1.0, 0.0)
    eq = jnp.where(bits == thr, 1.0, 0.0)
    need = cap - _segment_sum(gt, seg)
    eq_rank = _segment_cumsum(eq, seg)
    sel = gt + eq * jnp.where(eq_rank <= need, 1.0, 0.0)
    rank = _segment_cumsum(sel, seg)
    seg_id = lax.broadcasted_iota(jnp.int32, bits.shape, 1) // seg
    slot = seg_id * cap + rank.astype(jnp.int32) - 1
    slot = jnp.where(sel > 0.0, slot, -1)
    slot_ref[...] = slot

    iota_c = lax.broadcasted_iota(jnp.int32, (ROUTE_SLOTS, ROUTE_TILE), 0)
    for e in range(N_EXPERTS):
        hit = iota_c == slot[e:e + 1, :]
        gcol_ref[e] = jnp.sum(jnp.where(hit, aff[e:e + 1, :], 0.0), axis=1, keepdims=True)


def _router(h2, w_router_t):
    def call(seg, tile0, ntiles):
        return pl.pallas_call(
            functools.partial(_router_kernel, seg=seg),
            grid=(ntiles,),
            in_specs=[
                pl.BlockSpec((ROUTE_TILE, D), lambda i: (tile0 + i, 0)),
                pl.BlockSpec((N_EXPERTS, D), lambda i: (0, 0)),
            ],
            out_specs=[
                pl.BlockSpec((None, N_EXPERTS, ROUTE_TILE), lambda i: (i, 0, 0)),
                pl.BlockSpec((N_EXPERTS, None, ROUTE_SLOTS, 1), lambda i: (0, i, 0, 0)),
            ],
            out_shape=[
                jax.ShapeDtypeStruct((ntiles, N_EXPERTS, ROUTE_TILE), jnp.int32),
                jax.ShapeDtypeStruct((N_EXPERTS, ntiles, ROUTE_SLOTS, 1), F32),
            ],
            compiler_params=_cparams(("arbitrary",)),
            name="router",
        )(h2, w_router_t)

    nct = M_CTX // ROUTE_TILE
    slot_c, g_c = call(SEQ, 0, nct)
    slot_l, g_l = call(DEC_SEQ, nct, N_ROUTE_TILES - nct)
    return jnp.concatenate([slot_c, slot_l], axis=0), jnp.concatenate([g_c, g_l], axis=1)


CTX_CAP = CAPACITY_FACTOR * SEQ // N_EXPERTS
CTX_SETS_PER_TILE = ROUTE_TILE // SEQ
N_CTX_TILES = M_CTX // ROUTE_TILE


def _dispatch_ctx_kernel(h_ref, slot_ref, o_ref):
    base = (pl.program_id(0) % CTX_SETS_PER_TILE) * CTX_CAP
    slot = slot_ref[...]
    iota_c = lax.broadcasted_iota(jnp.int32, (CTX_CAP, SEQ), 0) + base
    onehot = jnp.concatenate(
        [jnp.where(iota_c == slot[e:e + 1, :], 1.0, 0.0).astype(BF16) for e in range(N_EXPERTS)], axis=0)
    res = jnp.dot(onehot, h_ref[...], preferred_element_type=F32).astype(BF16)
    for e in range(N_EXPERTS):
        o_ref[e] = res[e * CTX_CAP:(e + 1) * CTX_CAP, :]


DISPATCH_EXPERTS = 2


def _dispatch_lat_kernel(h_ref, slot_ref, prev_ref, o_ref):
    e0 = pl.program_id(1) * DISPATCH_EXPERTS
    iota_c = lax.broadcasted_iota(jnp.int32, (ROUTE_SLOTS, ROUTE_TILE), 0)
    onehot = jnp.concatenate(
        [jnp.where(iota_c == slot_ref[pl.ds(e0 + k, 1), :], 1.0, 0.0).astype(BF16)
         for k in range(DISPATCH_EXPERTS)], axis=0)
    res = jnp.dot(onehot, h_ref[...], preferred_element_type=F32).astype(BF16)
    for k in range(DISPATCH_EXPERTS):
        o_ref[k] = res[k * ROUTE_SLOTS:(k + 1) * ROUTE_SLOTS, :]


def _dispatch(h2, slot_et):
    xg = pl.pallas_call(
        _dispatch_ctx_kernel,
        grid=(BATCH,),
        in_specs=[
            pl.BlockSpec((SEQ, D), lambda s: (s, 0)),
            pl.BlockSpec((None, N_EXPERTS, SEQ), lambda s: (s // CTX_SETS_PER_TILE, 0, s % CTX_SETS_PER_TILE)),
        ],
        out_specs=pl.BlockSpec((N_EXPERTS, None, CTX_CAP, D), lambda s: (0, s, 0, 0)),
        out_shape=jax.ShapeDtypeStruct((N_EXPERTS, SLOTS_PER_EXPERT // CTX_CAP, CTX_CAP, D), BF16),
        compiler_params=_cparams(("arbitrary",)),
        name="moe_dispatch_ctx",
    )(h2, slot_et)
    return pl.pallas_call(
        _dispatch_lat_kernel,
        grid=(N_ROUTE_TILES - N_CTX_TILES, N_EXPERTS // DISPATCH_EXPERTS),
        in_specs=[
            pl.BlockSpec((ROUTE_TILE, D), lambda i, e: (N_CTX_TILES + i, 0)),
            pl.BlockSpec((None, N_EXPERTS, ROUTE_TILE), lambda i, e: (N_CTX_TILES + i, 0, 0)),
            ANY_SPEC,
        ],
        out_specs=pl.BlockSpec((DISPATCH_EXPERTS, ROUTE_SLOTS, D), lambda i, e: (e, N_CTX_TILES + i, 0)),
        out_shape=jax.ShapeDtypeStruct((N_EXPERTS, SLOTS_PER_EXPERT, D), BF16),
        input_output_aliases={2: 0},
        compiler_params=_cparams(("arbitrary", "arbitrary")),
        name="moe_dispatch",
    )(h2, slot_et, xg.reshape(N_EXPERTS, SLOTS_PER_EXPERT, D))


def _ffn_kernel(x_ref, w1_ref, w3_ref, w2_ref, g_ref, o_ref, acc_ref, *, tn):
    f = pl.program_id(2)
    nf = pl.num_programs(2)

    @pl.when(f == 0)
    def _():
        acc_ref[...] = jnp.zeros(acc_ref.shape, F32)

    a = b = None
    for k in range(D // tn):
        ks = slice(k * tn, (k + 1) * tn)
        xk = x_ref[:, ks]
        pa = jnp.dot(xk, w1_ref[ks, :].astype(BF16), preferred_element_type=F32)
        pb = jnp.dot(xk, w3_ref[ks, :].astype(BF16), preferred_element_type=F32)
        a = pa if a is None else a + pa
        b = pb if b is None else b + pb
    hmid = ((a * _sigmoid(a)) * b).astype(BF16)
    for n in range(D // tn):
        cols = slice(n * tn, (n + 1) * tn)
        acc_ref[:, cols] += jnp.dot(hmid, w2_ref[:, cols].astype(BF16), preferred_element_type=F32)

    @pl.when(f == nf - 1)
    def _():
        o_ref[...] = (acc_ref[...] * g_ref[...]).astype(BF16)


def _ffn(xg, w1, w3, w2, gcol, layer):
    tr, tf, tn = 1024, 512, 512
    nr = SLOTS_PER_EXPERT // tr
    return pl.pallas_call(
        functools.partial(_ffn_kernel, tn=tn),
        grid=(N_EXPERTS, nr, EXPERT_FF // tf),
        in_specs=[
            pl.BlockSpec((None, tr, D), lambda e, r, f: (e, r, 0)),
            pl.BlockSpec((None, None, D, tf), lambda e, r, f: (layer, e, 0, f)),
            pl.BlockSpec((None, None, D, tf), lambda e, r, f: (layer, e, 0, f)),
            pl.BlockSpec((None, None, tf, D), lambda e, r, f: (layer, e, f, 0)),
            pl.BlockSpec((None, tr, 1), lambda e, r, f: (e, r, 0)),
        ],
        out_specs=pl.BlockSpec((None, tr, D), lambda e, r, f: (e, r, 0)),
        out_shape=jax.ShapeDtypeStruct((N_EXPERTS, SLOTS_PER_EXPERT, D), BF16),
        scratch_shapes=[pltpu.VMEM((tr, D), F32)],
        compiler_params=_cparams(("arbitrary", "arbitrary", "arbitrary")),
        name="moe_ffn",
    )(xg, w1, w3, w2, gcol)


def _combine_rows(slot, base, n_slots, y_ref, x_ref, g2_ref, fg_ref):
    rows = slot.shape[0]
    iota_c = lax.broadcasted_iota(jnp.int32, (rows, n_slots), 1) + base
    onehot = jnp.concatenate(
        [jnp.where(iota_c == slot[:, e:e + 1], 1.0, 0.0).astype(BF16) for e in range(N_EXPERTS)], axis=1)
    y = y_ref[...].reshape(N_EXPERTS * n_slots, y_ref.shape[-1])
    x = x_ref[...] + g2_ref[...] * jnp.dot(onehot, y, preferred_element_type=F32)
    if fg_ref is not None:
        x = (x * lax.rsqrt(jnp.mean(x * x, axis=-1, keepdims=True) + EPS)) * fg_ref[...]
    return x


def _combine_ctx_kernel(slot_ref, y_ref, x_ref, g2_ref, *rest, final):
    o_ref = rest[-1]
    base = (pl.program_id(0) % CTX_SETS_PER_TILE) * CTX_CAP
    o_ref[...] = _combine_rows(slot_ref[...], base, CTX_CAP, y_ref, x_ref, g2_ref, rest[0] if final else None)


def _combine_lat_kernel(slot_ref, y_ref, x_ref, g2_ref, *rest, tr, final):
    o_ref = rest[-1]
    r = pl.program_id(2)
    slot = slot_ref[pl.ds(pl.multiple_of(r * tr, tr), tr), :]
    o_ref[...] = _combine_rows(slot, 0, ROUTE_SLOTS, y_ref, x_ref, g2_ref, rest[0] if final else None)


def _combine(slot_te, yg, x1, mod4, final_g=None):
    final = final_g is not None
    extra = [final_g.reshape(1, D)] if final else []
    nct = N_CTX_TILES
    y_ctx = pl.pallas_call(
        functools.partial(_combine_ctx_kernel, final=final),
        grid=(BATCH,),
        in_specs=[
            pl.BlockSpec((None, SEQ, N_EXPERTS), lambda s: (s // CTX_SETS_PER_TILE, s % CTX_SETS_PER_TILE, 0)),
            pl.BlockSpec((N_EXPERTS, None, CTX_CAP, D), lambda s: (0, s, 0, 0)),
            pl.BlockSpec((SEQ, D), lambda s: (s, 0)),
            pl.BlockSpec((None, None, 1, D), lambda s: (0, 5, 0, 0)),
        ] + [pl.BlockSpec((1, D), lambda s: (0, 0))] * len(extra),
        out_specs=pl.BlockSpec((SEQ, D), lambda s: (s, 0)),
        out_shape=jax.ShapeDtypeStruct((M_CTX if final else M, D), F32),
        compiler_params=_cparams(("arbitrary",)),
        name="moe_combine_ctx",
    )(slot_te, yg.reshape(N_EXPERTS, SLOTS_PER_EXPERT // CTX_CAP, CTX_CAP, D), x1, mod4, *extra)
    tn, tr = (D, 256) if final else (1024, 512)
    nr = ROUTE_TILE // tr
    orow0 = 0 if final else nct * nr
    in_specs = [
        pl.BlockSpec((None, ROUTE_TILE, N_EXPERTS), lambda i, j, r: (nct + i, 0, 0)),
        pl.BlockSpec((N_EXPERTS, None, ROUTE_SLOTS, tn), lambda i, j, r: (0, nct + i, 0, j)),
        pl.BlockSpec((tr, tn), lambda i, j, r: ((nct + i) * nr + r, j)),
        pl.BlockSpec((None, None, 1, tn), lambda i, j, r: (1 + i, 5, 0, j)),
    ]
    if final:
        in_specs.append(pl.BlockSpec((1, D), lambda i, j, r: (0, 0)))
        args, aliases = extra, {}
    else:
        in_specs.append(ANY_SPEC)
        args, aliases = [y_ctx], {4: 0}
    y_lat = pl.pallas_call(
        functools.partial(_combine_lat_kernel, tr=tr, final=final),
        grid=(N_ROUTE_TILES - nct, D // tn, nr),
        in_specs=in_specs,
        out_specs=pl.BlockSpec((tr, tn), lambda i, j, r: (orow0 + i * nr + r, j)),
        out_shape=jax.ShapeDtypeStruct((M_LAT if final else M, D), F32),
        input_output_aliases=aliases,
        compiler_params=_cparams(("arbitrary", "arbitrary", "arbitrary")),
        name="moe_combine",
    )(slot_te, yg.reshape(N_EXPERTS, N_ROUTE_TILES, ROUTE_SLOTS, D), x1, mod4, *args)
    return (y_ctx, y_lat) if final else y_lat


def _layer(x_parts, mod, lp, layer, lam_init, tables, cache_k4, cache_v4, st_re, st_im, new_kv, final_g):
    (xc, xc_row0), (xl, xl_row0) = x_parts
    rope_cos, rope_sin, dft_ctx, dft_lat, cs, perms = tables
    mod4 = mod.reshape(N_MOD_ROWS, 6, 1, D)
    w_in = lp['stacked']['w_in']
    h, u, ua_ctx = _norm_proj(xc, xc_row0, lp['norm1_g'], mod4, w_in, layer, True, None)
    h, u, ua_lat = _norm_proj(xl, xl_row0, lp['norm1_g'], mod4, w_in, layer, False, (h, u))

    wb, wc, a_re, a_im = _s5_params(lp)
    ns = S5_GROUPS * S5_STATE
    ngc = BATCH // SUBLANES
    zeros = jnp.zeros((ngc, 2, SUBLANES, ns), F32)
    y_ctx, fin_re, fin_im = _s5(ua_ctx, perms, wb, wc, a_re, a_im, zeros, zeros)
    h0_re = st_re.reshape(DEC_BATCH, 2, ns).transpose(1, 0, 2)[None]
    h0_im = st_im.reshape(DEC_BATCH, 2, ns).transpose(1, 0, 2)[None]
    y_lat, _, _ = _s5(ua_lat, perms, wb, wc, a_re, a_im, h0_re, h0_im)
    d_row = lp['s5_d'].reshape(1, S5_WIDTH)
    w_glu = lp['w_glu'].astype(BF16)
    b_glu = lp['b_glu'].reshape(1, S5_WIDTH)
    ya = _glu(y_ctx, u, d_row, w_glu, b_glu, True, SEQ, None)
    ya = _glu(y_lat, u, d_row, w_glu, b_glu, False, 512, ya)

    yb = _fnet(u, dft_ctx, cs, True, SEQ, None)
    yb = _fnet(u, dft_lat, cs, False, 512, yb)

    lam = (jnp.exp(jnp.sum(lp['lam_q1'].astype(F32) * lp['lam_k1'].astype(F32)))
           - jnp.exp(jnp.sum(lp['lam_q2'].astype(F32) * lp['lam_k2'].astype(F32))) + lam_init).reshape(1)
    subln = lp['subln_g'].reshape(1, VALUE_DIM)
    yc, new_k, new_v = _attention(u, lam, subln, lam_init, True, SEQ, N_HEADS, None, kv_out=(layer, new_kv))
    (yc,) = _attention(u, lam, subln, lam_init, False, DEC_SEQ, 1, yc,
                    ctx=(cache_k4, cache_v4, layer, rope_cos, rope_sin))

    st = lp['stacked']
    merged = _merge(h, ya, yb, yc, st['w_gate'], st['b_gate'], st['w_branch_a'], st['w_branch_b'],
                    st['w_branch_c'], layer)
    x1, h2 = _out_proj(merged, st['w_out'], layer, xc, xc_row0, lp['norm2_g'], mod4, True, None)
    x1, h2 = _out_proj(merged, st['w_out'], layer, xl, xl_row0, lp['norm2_g'], mod4, False, (x1, h2))

    slot_et, gcol = _router(h2, lp['w_router'].T.astype(BF16))
    xg = _dispatch(h2, slot_et)
    yg = _ffn(xg, lp['moe_w1_all'], lp['moe_w3_all'], lp['moe_w2_all'],
              gcol.reshape(N_EXPERTS, SLOTS_PER_EXPERT, 1), layer)
    x2 = _combine(slot_et.transpose(0, 2, 1), yg, x1, mod4, final_g)

    fin_shape = (ngc, 2, SUBLANES, S5_GROUPS, S5_STATE)
    s_re = fin_re.reshape(fin_shape).transpose(0, 2, 1, 3, 4).reshape(BATCH, 2, S5_GROUPS, S5_STATE)
    s_im = fin_im.reshape(fin_shape).transpose(0, 2, 1, 3, 4).reshape(BATCH, 2, S5_GROUPS, S5_STATE)
    return x2, (new_k, new_v), s_re, s_im


def kernel(x_prompt, x_sample, cache_k, cache_v, state_s5_re, state_s5_im, c, c_ctx, norm1_g, norm2_g, final_norm_g, w_ada, b_ada, w_in, s5_lam_re, s5_lam_im, s5_log_dt, s5_b_re, s5_b_im, s5_c_re, s5_c_im, s5_d, w_glu, b_glu, lam_q1, lam_k1, lam_q2, lam_k2, subln_g, w_branch_a, w_branch_b, w_branch_c, w_gate, b_gate, w_out, w_router, moe_w1, moe_w3, moe_w2):
    params = dict(norm1_g=norm1_g, norm2_g=norm2_g,
                  s5_lam_re=s5_lam_re, s5_lam_im=s5_lam_im, s5_log_dt=s5_log_dt,
                  s5_b_re=s5_b_re, s5_b_im=s5_b_im, s5_c_re=s5_c_re, s5_c_im=s5_c_im, s5_d=s5_d,
                  w_glu=w_glu, b_glu=b_glu, lam_q1=lam_q1, lam_k1=lam_k1, lam_q2=lam_q2, lam_k2=lam_k2,
                  subln_g=subln_g, w_router=w_router)
    stacked = dict(w_in=w_in.astype(BF16), w_gate=w_gate.astype(BF16), b_gate=b_gate.reshape(DEPTH, 1, 3 * D),
                   w_branch_a=w_branch_a.astype(BF16), w_branch_b=w_branch_b.astype(BF16),
                   w_branch_c=w_branch_c.astype(BF16), w_out=w_out.astype(BF16))
    cc = jnp.zeros((N_MOD_ROWS, D), F32).at[0].set(c_ctx).at[1:1 + DEC_BATCH].set(c)
    mods = _ada(cc, w_ada, b_ada)
    rope_cos, rope_sin = _rope_tables(DEC_SEQ)
    c_c, s_c = _dft_tables(FNET_GROUP_CH)
    cs = jnp.concatenate([c_c, s_c], axis=1).astype(BF16)
    tables = (rope_cos, rope_sin, _fnet_tables(SEQ), _fnet_tables(DEC_SEQ), cs, _s5_permutations())
    cache_k4 = cache_k.reshape(DEC_BATCH, DEPTH, PAST_LEN, QK_WIDTH)
    cache_v4 = cache_v.reshape(DEC_BATCH, DEPTH, PAST_LEN, V_WIDTH)
    x_parts = ((x_prompt.reshape(M_CTX, D), 0), (x_sample.reshape(M_LAT, D), 0))
    new_kv = None
    s_res, s_ims = [], []
    for l in range(DEPTH):
        lp = {name: arr[l] for name, arr in params.items()}
        lp['moe_w1_all'], lp['moe_w3_all'], lp['moe_w2_all'] = moe_w1, moe_w3, moe_w2
        lp['stacked'] = stacked
        lam_init = 0.8 - 0.6 * math.exp(-0.3 * l)
        x, new_kv, s_re, s_im = _layer(x_parts, mods[l], lp, l, lam_init, tables, cache_k4, cache_v4,
                                       state_s5_re[:, l], state_s5_im[:, l], new_kv,
                                       final_norm_g if l == DEPTH - 1 else None)
        x_parts = ((x, 0), (x, M_CTX))
        s_res.append(s_re)
        s_ims.append(s_im)
    y_prompt = x[0].reshape(BATCH, SEQ, D)
    y_sample = x[1].reshape(DEC_BATCH, DEC_SEQ, D)
    kv_shape = (BATCH, DEPTH, SEQ, N_HEADS, VALUE_DIM)
    return (y_prompt, y_sample, new_kv[0].reshape(kv_shape), new_kv[1].reshape(kv_shape),
            jnp.stack(s_res, axis=1), jnp.stack(s_ims, axis=1))
```

```python
import functools
import math

import jax
import jax.numpy as jnp
import numpy as np
from jax import lax
from jax.experimental import pallas as pl
from jax.experimental.pallas import tpu as pltpu

F32 = jnp.float32
BF16 = jnp.bfloat16

D = 2048
BATCH, SEQ = 32, 256
DEC_BATCH, DEC_SEQ = 8, 2048
DEPTH = 2
PAST_LEN = 256
GRID_W = 64
EPS = 1e-6
S5_GROUP_CH, S5_GROUPS, S5_STATE = 16, 32, 64
S5_WIDTH = S5_GROUPS * S5_GROUP_CH
FNET_GROUPS, FNET_GROUP_CH = 4, 128
FNET_WIDTH = FNET_GROUPS * FNET_GROUP_CH
N_HEADS, HEAD_DIM = 8, 64
VALUE_DIM = 2 * HEAD_DIM
QK_WIDTH = N_HEADS * 2 * HEAD_DIM
V_WIDTH = N_HEADS * VALUE_DIM
IN_WIDTH = S5_WIDTH + FNET_WIDTH + 2 * QK_WIDTH + V_WIDTH
ROPE_BASE = 10000.0
ROPE_AXIS_DIM = HEAD_DIM // 2
N_EXPERTS = 16
EXPERT_FF = 2048
CAPACITY_FACTOR = 2

M_CTX = BATCH * SEQ
M_LAT = DEC_BATCH * DEC_SEQ
M = M_CTX + M_LAT
N_MOD_ROWS = 16

LANES = 128
SUBLANES = 8
BF16_SUBLANES = 16
VMEM_LIMIT = 56 * 1024 * 1024

ROUTE_TILE = 2048
ROUTE_SLOTS = CAPACITY_FACTOR * ROUTE_TILE // N_EXPERTS
N_ROUTE_TILES = M // ROUTE_TILE
SLOTS_PER_EXPERT = N_ROUTE_TILES * ROUTE_SLOTS

S5_BLK = 4
S5_BLK_STATES = 512

ANY_SPEC = pl.BlockSpec(memory_space=pl.ANY)


def _cparams(sem):
    return pltpu.CompilerParams(dimension_semantics=sem, vmem_limit_bytes=VMEM_LIMIT)


def _sigmoid(x):
    return 1.0 / (1.0 + jnp.exp(-x))


def _part(is_ctx):
    return (M_CTX, 0) if is_ctx else (M_LAT, M_CTX)


def _mod_row_fn(is_ctx, tm):
    if is_ctx:
        return lambda i: 0
    return lambda i: 1 + i // (DEC_SEQ // tm)


def _ada_kernel(c_ref, w_ref, b_ref, o_ref):
    c = c_ref[...]
    s = (c * _sigmoid(c)).astype(BF16)
    o_ref[...] = jnp.dot(s, w_ref[...].astype(BF16), preferred_element_type=F32) + b_ref[...]


def _ada(cc, w_ada, b_ada):
    tn = 1024
    n = 6 * D
    return pl.pallas_call(
        _ada_kernel,
        grid=(DEPTH, n // tn),
        in_specs=[
            pl.BlockSpec((N_MOD_ROWS, D), lambda l, j: (0, 0)),
            pl.BlockSpec((None, D, tn), lambda l, j: (l, 0, j)),
            pl.BlockSpec((None, 1, tn), lambda l, j: (l, 0, j)),
        ],
        out_specs=pl.BlockSpec((None, N_MOD_ROWS, tn), lambda l, j: (l, 0, j)),
        out_shape=jax.ShapeDtypeStruct((DEPTH, N_MOD_ROWS, n), F32),
        compiler_params=_cparams(("arbitrary", "arbitrary")),
        name="ada",
    )(cc, w_ada, b_ada.reshape(DEPTH, 1, n))


NP_TM, NP_TN = 1024, 1024


NP_NJ = IN_WIDTH // NP_TN


def _norm_proj_kernel(x_ref, g_ref, sc_ref, sh_ref, w_ref, *rest, is_ctx, n_alias, tm):
    h_ref, u_ref, ua_ref, ha_ref, hb_ref = rest[n_alias:]
    i = pl.program_id(0)
    j = pl.program_id(1)
    q = tm // NP_NJ
    rows = pl.ds(pl.multiple_of(j * q, q), q)

    def norm_quarter(dst_ref):
        x = x_ref[rows, :]
        r = x * lax.rsqrt(jnp.mean(x * x, axis=-1, keepdims=True) + EPS)
        dst_ref[rows, :] = ((r * g_ref[...]) * (1.0 + sc_ref[...]) + sh_ref[...]).astype(BF16)

    def project(src_ref):
        h_ref[rows, :] = src_ref[rows, :]
        u_ref[...] = jnp.dot(src_ref[...], w_ref[...], preferred_element_type=F32)

    @pl.when(i == 0)
    def _():
        norm_quarter(ha_ref)

    @pl.when((i > 0) & (i % 2 == 0))
    def _():
        norm_quarter(ha_ref)
        project(hb_ref)

    @pl.when(i % 2 == 1)
    def _():
        norm_quarter(hb_ref)
        project(ha_ref)

    @pl.when((i > 0) & (j == 0))
    def _():
        if is_ctx:
            for b in range(tm // SEQ):
                ua_ref[:, b * S5_WIDTH:(b + 1) * S5_WIDTH] = u_ref[b * SEQ:(b + 1) * SEQ, :S5_WIDTH]
        else:
            ua_ref[...] = u_ref[:, :S5_WIDTH]


def _norm_proj(x, x_row0, g, mod4, w_in, layer, is_ctx, prev):
    tm, tn = NP_TM, NP_TN
    assert S5_WIDTH <= tn
    rows, row0 = _part(is_ctx)
    n_i = rows // tm
    xoff, ooff = x_row0 // tm, row0 // tm
    mrow = _mod_row_fn(is_ctx, tm)
    cur = lambda i: jnp.minimum(i, n_i - 1)
    prv = lambda i: jnp.maximum(i - 1, 0)
    prev = list(prev) if prev is not None else []
    in_specs = [
        pl.BlockSpec((tm, D), lambda i, j: (xoff + cur(i), 0)),
        pl.BlockSpec((1, D), lambda i, j: (0, 0)),
        pl.BlockSpec((None, None, 1, D), lambda i, j: (mrow(cur(i)), 1, 0, 0)),
        pl.BlockSpec((None, None, 1, D), lambda i, j: (mrow(cur(i)), 0, 0, 0)),
        pl.BlockSpec((None, D, tn), lambda i, j: (layer, 0, j)),
    ] + [ANY_SPEC] * len(prev)
    out_specs = [
        pl.BlockSpec((tm, D), lambda i, j: (ooff + prv(i), 0)),
        pl.BlockSpec((tm, tn), lambda i, j: (ooff + prv(i), j)),
    ]
    out_shape = [jax.ShapeDtypeStruct((M, D), BF16), jax.ShapeDtypeStruct((M, IN_WIDTH), F32)]
    if is_ctx:
        nbt = tm // SEQ
        out_specs.append(pl.BlockSpec((SEQ, nbt * S5_WIDTH), lambda i, j: (0, prv(i))))
        out_shape.append(jax.ShapeDtypeStruct((SEQ, BATCH * S5_WIDTH), F32))
        aliases = {}
    else:
        tpb = DEC_SEQ // tm
        out_specs.append(pl.BlockSpec((tm, S5_WIDTH), lambda i, j: (prv(i) % tpb, prv(i) // tpb)))
        out_shape.append(jax.ShapeDtypeStruct((DEC_SEQ, DEC_BATCH * S5_WIDTH), F32))
        aliases = {5: 0, 6: 1}
    return pl.pallas_call(
        functools.partial(_norm_proj_kernel, is_ctx=is_ctx, n_alias=len(prev), tm=tm),
        grid=(n_i + 1, NP_NJ),
        in_specs=in_specs,
        out_specs=out_specs,
        out_shape=out_shape,
        scratch_shapes=[pltpu.VMEM((tm, D), BF16), pltpu.VMEM((tm, D), BF16)],
        input_output_aliases=aliases,
        compiler_params=_cparams(("arbitrary", "arbitrary")),
        name="norm_proj",
    )(x, g.reshape(1, D), mod4, mod4, w_in, *prev)


S5_CHUNK = 64


def _s5_kernel(u_ref, pin_ref, pout_ref, wb_ref, wc_ref, are_ref, aim_ref, h0re_ref, h0im_ref,
               y_ref, finre_ref, finim_ref, xs0_ref, xs1_ref, xs2_ref, xs3_ref, stre_ref, stim_ref):
    c = pl.program_id(2)
    nc = pl.num_programs(2)
    bs = S5_BLK_STATES
    chunk = S5_CHUNK
    xs_refs = (xs0_ref, xs1_ref, xs2_ref, xs3_ref)

    @pl.when(c == 0)
    def _():
        stre_ref[...] = h0re_ref[...]
        stim_ref[...] = h0im_ref[...]

    ustack = jnp.concatenate(
        [u_ref[:, j * S5_WIDTH:(j + 1) * S5_WIDTH] for j in range(SUBLANES)], axis=0).astype(BF16)
    up = jnp.dot(pin_ref[...], ustack, preferred_element_type=F32).astype(BF16)

    for blk in range(S5_BLK):
        xs_refs[blk][...] = jnp.dot(up[:, blk * LANES:(blk + 1) * LANES], wb_ref[blk],
                                    preferred_element_type=F32)

    for blk in range(S5_BLK):
        xs_ref = xs_refs[blk]
        ar = are_ref[:, blk * bs:(blk + 1) * bs]
        ai = aim_ref[:, blk * bs:(blk + 1) * bs]
        sr = stre_ref[:, blk * bs:(blk + 1) * bs]
        si = stim_ref[:, blk * bs:(blk + 1) * bs]
        for i in range(chunk):
            r8 = slice(i * SUBLANES, (i + 1) * SUBLANES)
            nr = ar * sr - ai * si + xs_ref[r8, 0:bs]
            ni = ar * si + ai * sr + xs_ref[r8, bs:2 * bs]
            xs_ref[r8, 0:bs] = nr
            xs_ref[r8, bs:2 * bs] = ni
            sr, si = nr, ni
        stre_ref[:, blk * bs:(blk + 1) * bs] = sr
        stim_ref[:, blk * bs:(blk + 1) * bs] = si

    y = jnp.concatenate(
        [jnp.dot(xs_refs[blk][...].astype(BF16), wc_ref[blk], preferred_element_type=F32)
         for blk in range(S5_BLK)], axis=1)
    y_hi = y.astype(BF16)
    y_lo = (y - y_hi.astype(F32)).astype(BF16)
    ys = (jnp.dot(pout_ref[...], y_hi, preferred_element_type=F32)
          + jnp.dot(pout_ref[...], y_lo, preferred_element_type=F32))
    for j in range(SUBLANES):
        y_ref[:, j * S5_WIDTH:(j + 1) * S5_WIDTH] = ys[j * chunk:(j + 1) * chunk, :]

    @pl.when(c == nc - 1)
    def _():
        finre_ref[...] = stre_ref[...]
        finim_ref[...] = stim_ref[...]


def _s5_permutations():
    chunk = S5_CHUNK
    rows = chunk * SUBLANES
    r_out = lax.broadcasted_iota(jnp.int32, (rows, rows), 0)
    r_in = lax.broadcasted_iota(jnp.int32, (rows, rows), 1)
    step, seq = r_out // SUBLANES, r_out % SUBLANES
    fwd = r_in == seq * chunk + step
    bwd = r_in == seq * chunk + (chunk - 1 - step)
    p = jnp.stack([fwd, bwd]).astype(BF16)
    return p, p.transpose(0, 2, 1)


def _s5(u2d, perms, wb, wc, a_re, a_im, h0_re, h0_im):
    t_len = u2d.shape[0]
    ng = u2d.shape[1] // (SUBLANES * S5_WIDTH)
    chunk = S5_CHUNK
    nc = t_len // chunk
    rows = chunk * SUBLANES
    ns = S5_BLK * S5_BLK_STATES
    pin, pout = perms

    def ceff(d, c):
        return jnp.where(d == 0, c, nc - 1 - c)

    return pl.pallas_call(
        _s5_kernel,
        grid=(ng, 2, nc),
        in_specs=[
            pl.BlockSpec((chunk, SUBLANES * S5_WIDTH), lambda g, d, c: (ceff(d, c), g)),
            pl.BlockSpec((None, rows, rows), lambda g, d, c: (d, 0, 0)),
            pl.BlockSpec((None, rows, rows), lambda g, d, c: (d, 0, 0)),
            pl.BlockSpec((None, S5_BLK, LANES, 2 * S5_BLK_STATES), lambda g, d, c: (d, 0, 0, 0)),
            pl.BlockSpec((None, S5_BLK, 2 * S5_BLK_STATES, LANES), lambda g, d, c: (d, 0, 0, 0)),
            pl.BlockSpec((None, SUBLANES, ns), lambda g, d, c: (d, 0, 0)),
            pl.BlockSpec((None, SUBLANES, ns), lambda g, d, c: (d, 0, 0)),
            pl.BlockSpec((None, None, SUBLANES, ns), lambda g, d, c: (g, d, 0, 0)),
            pl.BlockSpec((None, None, SUBLANES, ns), lambda g, d, c: (g, d, 0, 0)),
        ],
        out_specs=[
            pl.BlockSpec((None, chunk, SUBLANES * S5_WIDTH), lambda g, d, c: (d, ceff(d, c), g)),
            pl.BlockSpec((None, None, SUBLANES, ns), lambda g, d, c: (g, d, 0, 0)),
            pl.BlockSpec((None, None, SUBLANES, ns), lambda g, d, c: (g, d, 0, 0)),
        ],
        out_shape=[
            jax.ShapeDtypeStruct((2,) + u2d.shape, F32),
            jax.ShapeDtypeStruct((ng, 2, SUBLANES, ns), F32),
            jax.ShapeDtypeStruct((ng, 2, SUBLANES, ns), F32),
        ],
        scratch_shapes=[pltpu.VMEM((rows, 2 * S5_BLK_STATES), F32)] * S5_BLK + [
            pltpu.VMEM((SUBLANES, ns), F32),
            pltpu.VMEM((SUBLANES, ns), F32),
        ],
        compiler_params=_cparams(("arbitrary", "arbitrary", "arbitrary")),
        name="s5_scan",
    )(u2d, pin, pout, wb, wc, a_re, a_im, h0_re, h0_im)


def _s5_params(lp):
    lam_re = lp['s5_lam_re'].astype(F32)
    lam_im = lp['s5_lam_im'].astype(F32)
    dt = jnp.exp(lp['s5_log_dt'].astype(F32))[..., None]
    b_re = lp['s5_b_re'].astype(F32)
    b_im = lp['s5_b_im'].astype(F32)
    c_re = lp['s5_c_re'].astype(F32)
    c_im = lp['s5_c_im'].astype(F32)
    mag = jnp.exp(lam_re * dt)
    ab_re = mag * jnp.cos(lam_im * dt)
    ab_im = mag * jnp.sin(lam_im * dt)
    den = lam_re * lam_re + lam_im * lam_im
    f_re = ((ab_re - 1.0) * lam_re + ab_im * lam_im) / den
    f_im = (ab_im * lam_re - (ab_re - 1.0) * lam_im) / den
    bb_re = f_re[..., None] * b_re - f_im[..., None] * b_im
    bb_im = f_re[..., None] * b_im + f_im[..., None] * b_re
    gpb = S5_GROUPS // S5_BLK
    eye = jnp.eye(gpb, dtype=F32)

    def pack_b(bb):
        bb = bb.reshape(2, S5_BLK, gpb, S5_STATE, S5_GROUP_CH)
        w = jnp.einsum('dbgph,gk->dbghkp', bb, eye)
        return w.reshape(2, S5_BLK, gpb * S5_GROUP_CH, gpb * S5_STATE)

    def pack_c(cc):
        cc = cc.reshape(2, S5_BLK, gpb, S5_GROUP_CH, S5_STATE)
        w = jnp.einsum('dbghp,gk->dbkpgh', cc, eye)
        return w.reshape(2, S5_BLK, gpb * S5_STATE, gpb * S5_GROUP_CH)

    wb = jnp.concatenate([pack_b(bb_re), pack_b(bb_im)], axis=-1).astype(BF16)
    wc = jnp.concatenate([pack_c(c_re), -pack_c(c_im)], axis=-2).astype(BF16)
    ns = S5_GROUPS * S5_STATE
    a_re = jnp.broadcast_to(ab_re.reshape(2, 1, ns), (2, SUBLANES, ns))
    a_im = jnp.broadcast_to(ab_im.reshape(2, 1, ns), (2, SUBLANES, ns))
    return wb, wc, a_re, a_im


def _gelu_tanh(x):
    return 0.5 * x * (1.0 + jnp.tanh(math.sqrt(2.0 / math.pi) * (x + 0.044715 * (x * x * x))))


def _glu_kernel(y_ref, u_ref, d_ref, w_ref, b_ref, *rest):
    o_ref = rest[-1]
    y = y_ref[0] + y_ref[1] + d_ref[...] * u_ref[...]
    y = _gelu_tanh(y)
    z = jnp.dot(y.astype(BF16), w_ref[...], preferred_element_type=F32) + b_ref[...]
    o_ref[...] = (y * _sigmoid(z)).astype(BF16)


def _glu(y_dirs, u, d_row, w_glu, b_glu, is_ctx, tt, prev):
    t_len = y_dirs.shape[1]
    nb = y_dirs.shape[2] // S5_WIDTH
    ntt = t_len // tt
    off = _part(is_ctx)[1] // tt
    prev = [prev] if prev is not None else []
    return pl.pallas_call(
        _glu_kernel,
        grid=(nb, ntt),
        in_specs=[
            pl.BlockSpec((2, tt, S5_WIDTH), lambda b, t: (0, t, b)),
            pl.BlockSpec((tt, S5_WIDTH), lambda b, t: (off + b * ntt + t, 0)),
            pl.BlockSpec((1, S5_WIDTH), lambda b, t: (0, 0)),
            pl.BlockSpec((S5_WIDTH, S5_WIDTH), lambda b, t: (0, 0)),
            pl.BlockSpec((1, S5_WIDTH), lambda b, t: (0, 0)),
        ] + [ANY_SPEC] * len(prev),
        out_specs=pl.BlockSpec((tt, S5_WIDTH), lambda b, t: (off + b * ntt + t, 0)),
        out_shape=jax.ShapeDtypeStruct((M, S5_WIDTH), BF16),
        input_output_aliases={5: 0} if prev else {},
        compiler_params=_cparams(("arbitrary", "arbitrary")),
        name="s5_glu",
    )(y_dirs, u, d_row, w_glu, b_glu, *prev)


def _fnet_kernel(x_ref, cs_ref, dft_ref, *rest, t_len):
    o_ref, z_ref = rest[-2:]

    @pl.when(pl.program_id(1) == 0)
    def _():
        x = x_ref[...].astype(BF16)
        for g in range(FNET_GROUPS):
            z = jnp.dot(x[:, g * LANES:(g + 1) * LANES], cs_ref[...], preferred_element_type=F32)
            z_ref[0:t_len, g * LANES:(g + 1) * LANES] = z[:, :LANES].astype(BF16)
            z_ref[t_len:2 * t_len, g * LANES:(g + 1) * LANES] = z[:, LANES:].astype(BF16)

    scale = 1.0 / math.sqrt(t_len * FNET_GROUP_CH)
    o_ref[...] = (jnp.dot(dft_ref[...], z_ref[...], preferred_element_type=F32) * scale).astype(BF16)


def _dft_tables(n):
    j = lax.broadcasted_iota(jnp.int32, (n, n), 0)
    k = lax.broadcasted_iota(jnp.int32, (n, n), 1)
    ang = ((j * k) % n).astype(F32) * (2.0 * math.pi / n)
    return jnp.cos(ang), jnp.sin(ang)


def _fnet_tables(t_len):
    j = np.arange(t_len, dtype=np.int64)
    ang = ((j[:, None] * j[None, :]) % t_len).astype(np.float64) * (2.0 * math.pi / t_len)
    return jnp.asarray(np.concatenate([np.cos(ang), -np.sin(ang)], axis=1), dtype=BF16)


def _fnet(u, dft, cs, is_ctx, tr, prev):
    t_len = dft.shape[0]
    rows, row0 = _part(is_ctx)
    nb = rows // t_len
    ntr = t_len // tr
    off = row0 // t_len
    offr = row0 // tr
    col = S5_WIDTH // FNET_WIDTH
    prev = [prev] if prev is not None else []
    return pl.pallas_call(
        functools.partial(_fnet_kernel, t_len=t_len),
        grid=(nb, ntr),
        in_specs=[
            pl.BlockSpec((t_len, FNET_WIDTH), lambda b, r: (off + b, col)),
            pl.BlockSpec((FNET_GROUP_CH, 2 * FNET_GROUP_CH), lambda b, r: (0, 0)),
            pl.BlockSpec((tr, 2 * t_len), lambda b, r: (r, 0)),
        ] + [ANY_SPEC] * len(prev),
        out_specs=pl.BlockSpec((tr, FNET_WIDTH), lambda b, r: (offr + b * ntr + r, 0)),
        out_shape=jax.ShapeDtypeStruct((M, FNET_WIDTH), BF16),
        scratch_shapes=[pltpu.VMEM((2 * t_len, FNET_WIDTH), BF16)],
        input_output_aliases={3: 0} if prev else {},
        compiler_params=_cparams(("arbitrary", "arbitrary")),
        name="fnet",
    )(u, cs, dft, *prev)


def _rope(x, cos, sin):
    half = ROPE_AXIS_DIM // 2
    lane = lax.broadcasted_iota(jnp.int32, x.shape, 1)
    up = pltpu.roll(x, LANES - half, 1)
    dn = pltpu.roll(x, half, 1)
    rot = jnp.where((lane % ROPE_AXIS_DIM) < half, -up, dn)
    return x * cos + rot * sin


ATTN_KEY_CHUNK = 512
ATTN_ROW_BLOCK = 256


def _attn_kernel(lam_ref, q_ref, k_ref, v_ref, *rest, tq, t_len, hb, has_ctx, lam_init, n_alias, emit_kv):
    pos = 0
    if has_ctx:
        ck_ref, cv_ref, cos_ref, sin_ref = rest[:4]
        pos = 4
    g_ref = rest[pos]
    pos += 1 + n_alias
    o_ref = rest[pos]
    pos += 1
    if emit_kv:
        ko_ref, vo_ref = rest[pos:pos + 2]
        pos += 2
    kk_ref, vv_ref, qs_ref, s_ref = rest[pos:]
    qi = pl.program_id(2)
    if emit_kv:
        @pl.when(qi == 0)
        def _():
            ko_ref[...] = k_ref[...]
            vo_ref[...] = v_ref[...]
    n_keys = kk_ref.shape[1]
    kc, rb = ATTN_KEY_CHUNK, ATTN_ROW_BLOCK
    chunks = [(c0, min(c0 + kc, n_keys)) for c0 in range(0, n_keys, kc)]

    @pl.when(qi == 0)
    def _():
        for h in range(hb):
            hs = slice(h * LANES, (h + 1) * LANES)
            k = k_ref[:, hs]
            if has_ctx:
                k = _rope(k, cos_ref[...], sin_ref[...])
                kk_ref[h, t_len:, :] = ck_ref[:, hs].astype(BF16)
                vv_ref[h, t_len:, 0:VALUE_DIM] = cv_ref[:, hs].astype(BF16)
            kk_ref[h, 0:t_len, :] = k.astype(BF16)
            vv_ref[h, 0:t_len, 0:VALUE_DIM] = v_ref[:, hs].astype(BF16)
            vv_ref[h, :, VALUE_DIM:] = jnp.ones((n_keys, VALUE_DIM), BF16)

    lane = lax.broadcasted_iota(jnp.int32, (tq, LANES), 1)
    for h in range(hb):
        hs = slice(h * LANES, (h + 1) * LANES)
        q = q_ref[:, hs]
        if has_ctx:
            r0 = pl.multiple_of(qi * tq, tq)
            q = _rope(q, cos_ref[pl.ds(r0, tq), :], sin_ref[pl.ds(r0, tq), :])
        q = q * (HEAD_DIM ** -0.5)
        qs_ref[0, :, hs] = jnp.where(lane < HEAD_DIM, q, 0.0).astype(BF16)
        qs_ref[1, :, hs] = jnp.where(lane >= HEAD_DIM, q, 0.0).astype(BF16)

    lam = lam_ref[0]
    gain = g_ref[...]
    dn = (((1,), (1,)), ((), ()))

    unit = 0
    for r in range(tq // rb):
        rows = slice(r * rb, (r + 1) * rb)
        for h in range(hb):
            hs = slice(h * LANES, (h + 1) * LANES)
            o = None
            for n in range(2):
                slot = unit % 2
                unit += 1
                qn = qs_ref[n, rows, hs]
                mrun = None
                for c0, c1 in chunks:
                    s_c = lax.dot_general(qn, kk_ref[h, c0:c1, :], dn, preferred_element_type=F32)
                    s_ref[slot, :, c0:c1] = s_c
                    for l0 in range(0, c1 - c0, LANES):
                        mc = s_c[:, l0:l0 + LANES]
                        mrun = mc if mrun is None else jnp.maximum(mrun, mc)
                m_b = jnp.broadcast_to(jnp.max(mrun, axis=-1, keepdims=True), (rb, LANES))
                oa = None
                for c0, c1 in chunks:
                    p = jnp.concatenate(
                        [jnp.exp(s_ref[slot, :, l0:l0 + LANES] - m_b) for l0 in range(c0, c1, LANES)],
                        axis=1).astype(BF16)
                    part = jnp.dot(p, vv_ref[h, c0:c1, :], preferred_element_type=F32)
                    oa = part if oa is None else oa + part
                on = oa[:, :VALUE_DIM] * (1.0 / oa[:, VALUE_DIM:VALUE_DIM + 1])
                o = on if n == 0 else o - lam * on
            o = o * lax.rsqrt(jnp.mean(o * o, axis=-1, keepdims=True) + EPS)
            o_ref[rows, hs] = ((o * gain) * (1.0 - lam_init)).astype(BF16)


def _attention(u, lam, subln_g, lam_init, is_ctx, tq, hb, prev, ctx=None, kv_out=None):
    rows, row0 = _part(is_ctx)
    t_len = SEQ if is_ctx else DEC_SEQ
    nb = rows // t_len
    nq = t_len // tq
    off = row0 // t_len
    offq = row0 // tq
    width = hb * LANES
    qcol = (S5_WIDTH + FNET_WIDTH) // width
    kcol = qcol + QK_WIDTH // width
    vcol = kcol + QK_WIDTH // width
    has_ctx = ctx is not None
    n_keys = t_len + (PAST_LEN if has_ctx else 0)
    assert tq % ATTN_ROW_BLOCK == 0 and n_keys % (2 * LANES) == 0
    in_specs = [
        pl.BlockSpec(memory_space=pltpu.SMEM),
        pl.BlockSpec((tq, width), lambda b, h, i: (offq + b * nq + i, qcol + h)),
        pl.BlockSpec((t_len, width), lambda b, h, i: (off + b, kcol + h)),
        pl.BlockSpec((t_len, width), lambda b, h, i: (off + b, vcol + h)),
    ]
    args = [lam, u, u, u]
    if has_ctx:
        ck, cv, layer, cos, sin = ctx
        in_specs += [
            pl.BlockSpec((None, None, PAST_LEN, width), lambda b, h, i: (b, layer, 0, h)),
            pl.BlockSpec((None, None, PAST_LEN, width), lambda b, h, i: (b, layer, 0, h)),
            pl.BlockSpec((t_len, LANES), lambda b, h, i: (0, 0)),
            pl.BlockSpec((t_len, LANES), lambda b, h, i: (0, 0)),
        ]
        args += [ck, cv, cos, sin]
    in_specs.append(pl.BlockSpec((1, LANES), lambda b, h, i: (0, 0)))
    args.append(subln_g)
    aliases = {}
    n_fixed = len(args)
    if prev is not None:
        aliases[len(args)] = 0
        in_specs.append(ANY_SPEC)
        args.append(prev)
    out_specs = [pl.BlockSpec((tq, width), lambda b, h, i: (offq + b * nq + i, h))]
    out_shape = [jax.ShapeDtypeStruct((M, V_WIDTH), BF16)]
    if kv_out is not None:
        kv_layer, prev_kv = kv_out
        assert hb == N_HEADS and tq == t_len and nb == BATCH
        out_specs += [pl.BlockSpec((None, None, SEQ, QK_WIDTH), lambda b, h, i: (b, kv_layer, 0, 0)),
                      pl.BlockSpec((None, None, SEQ, V_WIDTH), lambda b, h, i: (b, kv_layer, 0, 0))]
        out_shape += [jax.ShapeDtypeStruct((BATCH, DEPTH, SEQ, QK_WIDTH), F32),
                      jax.ShapeDtypeStruct((BATCH, DEPTH, SEQ, V_WIDTH), F32)]
        if prev_kv is not None:
            for o_idx, arr in enumerate(prev_kv):
                aliases[len(args)] = 1 + o_idx
                in_specs.append(ANY_SPEC)
                args.append(arr)
    kern = functools.partial(_attn_kernel, tq=tq, t_len=t_len, hb=hb, has_ctx=has_ctx, lam_init=lam_init,
                             n_alias=len(args) - n_fixed, emit_kv=kv_out is not None)
    return pl.pallas_call(
        kern,
        grid=(nb, N_HEADS // hb, nq),
        in_specs=in_specs,
        out_specs=out_specs,
        out_shape=out_shape,
        scratch_shapes=[
            pltpu.VMEM((hb, n_keys, LANES), BF16),
            pltpu.VMEM((hb, n_keys, 2 * VALUE_DIM), BF16),
            pltpu.VMEM((2, tq, width), BF16),
            pltpu.VMEM((2, ATTN_ROW_BLOCK, n_keys), F32),
        ],
        input_output_aliases=aliases,
        compiler_params=_cparams(("arbitrary", "arbitrary", "arbitrary")),
        name="diff_attn",
    )(*args)


def _rope_tables(t_len):
    rows = t_len // GRID_W
    pos_row = jnp.broadcast_to(jnp.arange(rows, dtype=F32)[:, None], (rows, GRID_W)).reshape(-1)
    pos_col = jnp.broadcast_to(jnp.arange(GRID_W, dtype=F32)[None, :], (rows, GRID_W)).reshape(-1)
    inv = ROPE_BASE ** (-jnp.arange(0, ROPE_AXIS_DIM, 2, dtype=F32) / ROPE_AXIS_DIM)
    ang_r = pos_row[:, None] * inv
    ang_c = pos_col[:, None] * inv
    cos = jnp.concatenate([jnp.cos(ang_r), jnp.cos(ang_r), jnp.cos(ang_c), jnp.cos(ang_c)], axis=-1)
    sin = jnp.concatenate([jnp.sin(ang_r), jnp.sin(ang_r), jnp.sin(ang_c), jnp.sin(ang_c)], axis=-1)
    return jnp.concatenate([cos, cos], axis=-1), jnp.concatenate([sin, sin], axis=-1)


def _merge_kernel(h_ref, ya_ref, yb_ref, yc_ref, wga_ref, wgb_ref, wgc_ref, bga_ref, bgb_ref, bgc_ref,
                  wa_ref, wb_ref, wc_ref, o_ref):
    h = h_ref[...]
    acc = None
    for wg, bg, y, w in ((wga_ref, bga_ref, ya_ref, wa_ref), (wgb_ref, bgb_ref, yb_ref, wb_ref),
                         (wgc_ref, bgc_ref, yc_ref, wc_ref)):
        gate = _sigmoid(jnp.dot(h, wg[...], preferred_element_type=F32) + bg[...])
        term = gate * jnp.dot(y[...], w[...], preferred_element_type=F32)
        acc = term if acc is None else acc + term
    o_ref[...] = acc.astype(BF16)


def _merge(h, ya, yb, yc, w_gate, b_gate, w_a, w_b, w_c, layer):
    tm, tn = 1024, 512
    nj = D // tn
    row = lambda i, j: (i, 0)
    wspecs = [pl.BlockSpec((None, D, tn), lambda i, j, k=k: (layer, 0, k * nj + j)) for k in range(3)]
    bspecs = [pl.BlockSpec((None, 1, tn), lambda i, j, k=k: (layer, 0, k * nj + j)) for k in range(3)]
    return pl.pallas_call(
        _merge_kernel,
        grid=(M // tm, nj),
        in_specs=[
            pl.BlockSpec((tm, D), row),
            pl.BlockSpec((tm, S5_WIDTH), row),
            pl.BlockSpec((tm, FNET_WIDTH), row),
            pl.BlockSpec((tm, V_WIDTH), row),
            *wspecs, *bspecs,
            pl.BlockSpec((None, S5_WIDTH, tn), lambda i, j: (layer, 0, j)),
            pl.BlockSpec((None, FNET_WIDTH, tn), lambda i, j: (layer, 0, j)),
            pl.BlockSpec((None, V_WIDTH, tn), lambda i, j: (layer, 0, j)),
        ],
        out_specs=pl.BlockSpec((tm, tn), lambda i, j: (i, j)),
        out_shape=jax.ShapeDtypeStruct((M, D), BF16),
        compiler_params=_cparams(("arbitrary", "arbitrary")),
        name="merge",
    )(h, ya, yb, yc, w_gate, w_gate, w_gate, b_gate, b_gate, b_gate, w_a, w_b, w_c)


def _out_proj_kernel(m_ref, w_ref, x_ref, g1_ref, n2_ref, sc_ref, sh_ref, *rest):
    x1_ref, h2_ref = rest[-2:]
    x1 = x_ref[...] + g1_ref[...] * jnp.dot(m_ref[...], w_ref[...], preferred_element_type=F32)
    x1_ref[...] = x1
    r = x1 * lax.rsqrt(jnp.mean(x1 * x1, axis=-1, keepdims=True) + EPS)
    h2_ref[...] = ((r * n2_ref[...]) * (1.0 + sc_ref[...]) + sh_ref[...]).astype(BF16)


def _out_proj(merged, w_out, layer, x, x_row0, norm2_g, mod4, is_ctx, prev):
    tm = 512
    rows, row0 = _part(is_ctx)
    xoff, ooff = x_row0 // tm, row0 // tm
    mrow = _mod_row_fn(is_ctx, tm)
    orow = lambda i: (ooff + i, 0)
    mspec = lambda k: pl.BlockSpec((None, None, 1, D), lambda i, k=k: (mrow(i), k, 0, 0))
    prev = list(prev) if prev is not None else []
    return pl.pallas_call(
        _out_proj_kernel,
        grid=(rows // tm,),
        in_specs=[
            pl.BlockSpec((tm, D), orow),
            pl.BlockSpec((None, D, D), lambda i: (layer, 0, 0)),
            pl.BlockSpec((tm, D), lambda i: (xoff + i, 0)),
            mspec(2),
            pl.BlockSpec((1, D), lambda i: (0, 0)),
            mspec(4),
            mspec(3),
        ] + [ANY_SPEC] * len(prev),
        out_specs=[pl.BlockSpec((tm, D), orow), pl.BlockSpec((tm, D), orow)],
        out_shape=[jax.ShapeDtypeStruct((M, D), F32), jax.ShapeDtypeStruct((M, D), BF16)],
        input_output_aliases={7: 0, 8: 1} if prev else {},
        compiler_params=_cparams(("arbitrary",)),
        name="out_proj",
    )(merged, w_out, x, mod4, norm2_g.reshape(1, D), mod4, mod4, *prev)


def _segment_sum(x, seg):
    parts = []
    for s in range(ROUTE_TILE // seg):
        tot = jnp.sum(x[:, s * seg:(s + 1) * seg], axis=1, keepdims=True)
        parts.append(jnp.broadcast_to(tot, (x.shape[0], seg)))
    return parts[0] if len(parts) == 1 else jnp.concatenate(parts, axis=1)


def _segment_cumsum(x, seg):
    pos = lax.broadcasted_iota(jnp.int32, x.shape, 1) % seg
    sh = 1
    while sh < seg:
        x = x + jnp.where(pos >= sh, pltpu.roll(x, sh, 1), 0.0)
        sh *= 2
    return x


CUT_ROWS = 256


def _router_kernel(h_ref, wr_ref, slot_ref, gcol_ref, cuts_ref, *, seg):
    cap = CAPACITY_FACTOR * seg // N_EXPERTS
    logits = lax.dot_general(wr_ref[...], h_ref[...], (((1,), (1,)), ((), ())),
                             preferred_element_type=F32)
    p = jnp.exp(logits - jnp.max(logits, axis=0, keepdims=True))
    aff = p / jnp.sum(p, axis=0, keepdims=True)
    bits = pltpu.bitcast(aff, jnp.int32)

    def bis(i, thr):
        cand = thr | jnp.left_shift(jnp.int32(1), 29 - i)
        cnt = _segment_sum(jnp.where(bits >= cand, 1.0, 0.0), seg)
        return jnp.where(cnt >= cap, cand, thr)

    thr = lax.fori_loop(0, 30, bis, jnp.zeros(bits.shape, jnp.int32))
    gt = jnp.where(bits > thr, 1.0, 0.0)
    eq = jnp.where(bits == thr, 1.0, 0.0)
    need = cap - _segment_sum(gt, seg)
    eq_rank = _segment_cumsum(eq, seg)
    sel = gt + eq * jnp.where(eq_rank <= need, 1.0, 0.0)
    rank = _segment_cumsum(sel, seg)
    seg_id = lax.broadcasted_iota(jnp.int32, bits.shape, 1) // seg
    slot = seg_id * cap + rank.astype(jnp.int32) - 1
    slot = jnp.where(sel > 0.0, slot, -1)
    slot_ref[...] = slot
    ends = [rank[:, (k + 1) * CUT_ROWS - 1:(k + 1) * CUT_ROWS] for k in range(ROUTE_TILE // CUT_ROWS)]
    pad = jnp.zeros((N_EXPERTS, LANES - len(ends)), F32)
    cuts_ref[...] = jnp.concatenate(ends + [pad], axis=1).astype(jnp.int32)

    iota_c = lax.broadcasted_iota(jnp.int32, (ROUTE_SLOTS, ROUTE_TILE), 0)
    for e in range(N_EXPERTS):
        hit = iota_c == slot[e:e + 1, :]
        gcol_ref[e] = jnp.sum(jnp.where(hit, aff[e:e + 1, :], 0.0), axis=1, keepdims=True)


def _router(h2, w_router_t):
    def call(seg, tile0, ntiles):
        return pl.pallas_call(
            functools.partial(_router_kernel, seg=seg),
            grid=(ntiles,),
            in_specs=[
                pl.BlockSpec((ROUTE_TILE, D), lambda i: (tile0 + i, 0)),
                pl.BlockSpec((N_EXPERTS, D), lambda i: (0, 0)),
            ],
            out_specs=[
                pl.BlockSpec((None, N_EXPERTS, ROUTE_TILE), lambda i: (i, 0, 0)),
                pl.BlockSpec((N_EXPERTS, None, ROUTE_SLOTS, 1), lambda i: (0, i, 0, 0)),
                pl.BlockSpec((None, N_EXPERTS, LANES), lambda i: (i, 0, 0)),
            ],
            out_shape=[
                jax.ShapeDtypeStruct((ntiles, N_EXPERTS, ROUTE_TILE), jnp.int32),
                jax.ShapeDtypeStruct((N_EXPERTS, ntiles, ROUTE_SLOTS, 1), F32),
                jax.ShapeDtypeStruct((ntiles, N_EXPERTS, LANES), jnp.int32),
            ],
            compiler_params=_cparams(("arbitrary",)),
            name="router",
        )(h2, w_router_t)

    nct = M_CTX // ROUTE_TILE
    slot_c, g_c, _ = call(SEQ, 0, nct)
    slot_l, g_l, cuts_l = call(DEC_SEQ, nct, N_ROUTE_TILES - nct)
    return jnp.concatenate([slot_c, slot_l], axis=0), jnp.concatenate([g_c, g_l], axis=1), cuts_l


CTX_CAP = CAPACITY_FACTOR * SEQ // N_EXPERTS
CTX_SETS_PER_TILE = ROUTE_TILE // SEQ
N_CTX_TILES = M_CTX // ROUTE_TILE


def _dispatch_ctx_kernel(h_ref, slot_ref, o_ref):
    base = (pl.program_id(0) % CTX_SETS_PER_TILE) * CTX_CAP
    slot = slot_ref[...]
    iota_c = lax.broadcasted_iota(jnp.int32, (CTX_CAP, SEQ), 0) + base
    onehot = jnp.concatenate(
        [jnp.where(iota_c == slot[e:e + 1, :], 1.0, 0.0).astype(BF16) for e in range(N_EXPERTS)], axis=0)
    res = jnp.dot(onehot, h_ref[...], preferred_element_type=F32).astype(BF16)
    for e in range(N_EXPERTS):
        o_ref[e] = res[e * CTX_CAP:(e + 1) * CTX_CAP, :]


DISPATCH_EXPERTS = 2


def _dispatch_lat_kernel(h_ref, slot_ref, prev_ref, o_ref):
    e0 = pl.program_id(1) * DISPATCH_EXPERTS
    iota_c = lax.broadcasted_iota(jnp.int32, (ROUTE_SLOTS, ROUTE_TILE), 0)
    onehot = jnp.concatenate(
        [jnp.where(iota_c == slot_ref[pl.ds(e0 + k, 1), :], 1.0, 0.0).astype(BF16)
         for k in range(DISPATCH_EXPERTS)], axis=0)
    res = jnp.dot(onehot, h_ref[...], preferred_element_type=F32).astype(BF16)
    for k in range(DISPATCH_EXPERTS):
        o_ref[k] = res[k * ROUTE_SLOTS:(k + 1) * ROUTE_SLOTS, :]


def _dispatch(h2, slot_et):
    xg = pl.pallas_call(
        _dispatch_ctx_kernel,
        grid=(BATCH,),
        in_specs=[
            pl.BlockSpec((SEQ, D), lambda s: (s, 0)),
            pl.BlockSpec((None, N_EXPERTS, SEQ), lambda s: (s // CTX_SETS_PER_TILE, 0, s % CTX_SETS_PER_TILE)),
        ],
        out_specs=pl.BlockSpec((N_EXPERTS, None, CTX_CAP, D), lambda s: (0, s, 0, 0)),
        out_shape=jax.ShapeDtypeStruct((N_EXPERTS, SLOTS_PER_EXPERT // CTX_CAP, CTX_CAP, D), BF16),
        compiler_params=_cparams(("arbitrary",)),
        name="moe_dispatch_ctx",
    )(h2, slot_et)
    return pl.pallas_call(
        _dispatch_lat_kernel,
        grid=(N_ROUTE_TILES - N_CTX_TILES, N_EXPERTS // DISPATCH_EXPERTS),
        in_specs=[
            pl.BlockSpec((ROUTE_TILE, D), lambda i, e: (N_CTX_TILES + i, 0)),
            pl.BlockSpec((None, N_EXPERTS, ROUTE_TILE), lambda i, e: (N_CTX_TILES + i, 0, 0)),
            ANY_SPEC,
        ],
        out_specs=pl.BlockSpec((DISPATCH_EXPERTS, ROUTE_SLOTS, D), lambda i, e: (e, N_CTX_TILES + i, 0)),
        out_shape=jax.ShapeDtypeStruct((N_EXPERTS, SLOTS_PER_EXPERT, D), BF16),
        input_output_aliases={2: 0},
        compiler_params=_cparams(("arbitrary", "arbitrary")),
        name="moe_dispatch",
    )(h2, slot_et, xg.reshape(N_EXPERTS, SLOTS_PER_EXPERT, D))


def _ffn_kernel(x_ref, w1_ref, w3_ref, w2_ref, g_ref, o_ref, acc_ref, *, tn):
    f = pl.program_id(2)
    nf = pl.num_programs(2)

    @pl.when(f == 0)
    def _():
        acc_ref[...] = jnp.zeros(acc_ref.shape, F32)

    a = b = None
    for k in range(D // tn):
        ks = slice(k * tn, (k + 1) * tn)
        xk = x_ref[:, ks]
        pa = jnp.dot(xk, w1_ref[ks, :].astype(BF16), preferred_element_type=F32)
        pb = jnp.dot(xk, w3_ref[ks, :].astype(BF16), preferred_element_type=F32)
        a = pa if a is None else a + pa
        b = pb if b is None else b + pb
    hmid = ((a * _sigmoid(a)) * b).astype(BF16)
    for n in range(D // tn):
        cols = slice(n * tn, (n + 1) * tn)
        acc_ref[:, cols] += jnp.dot(hmid, w2_ref[:, cols].astype(BF16), preferred_element_type=F32)

    @pl.when(f == nf - 1)
    def _():
        o_ref[...] = (acc_ref[...] * g_ref[...]).astype(BF16)


def _ffn(xg, w1, w3, w2, gcol, layer):
    tr, tf, tn = 1024, 512, 512
    nr = SLOTS_PER_EXPERT // tr
    return pl.pallas_call(
        functools.partial(_ffn_kernel, tn=tn),
        grid=(N_EXPERTS, nr, EXPERT_FF // tf),
        in_specs=[
            pl.BlockSpec((None, tr, D), lambda e, r, f: (e, r, 0)),
            pl.BlockSpec((None, None, D, tf), lambda e, r, f: (layer, e, 0, f)),
            pl.BlockSpec((None, None, D, tf), lambda e, r, f: (layer, e, 0, f)),
            pl.BlockSpec((None, None, tf, D), lambda e, r, f: (layer, e, f, 0)),
            pl.BlockSpec((None, tr, 1), lambda e, r, f: (e, r, 0)),
        ],
        out_specs=pl.BlockSpec((None, tr, D), lambda e, r, f: (e, r, 0)),
        out_shape=jax.ShapeDtypeStruct((N_EXPERTS, SLOTS_PER_EXPERT, D), BF16),
        scratch_shapes=[pltpu.VMEM((tr, D), F32)],
        compiler_params=_cparams(("arbitrary", "arbitrary", "arbitrary")),
        name="moe_ffn",
    )(xg, w1, w3, w2, gcol)


def _combine_rows(slot, bases, n_slots, y_parts, x_ref, g2_ref, fg_ref):
    rows = slot.shape[0]
    iota_c = lax.broadcasted_iota(jnp.int32, (rows, n_slots), 1)
    onehot = jnp.concatenate(
        [jnp.where(iota_c + bases[e] == slot[:, e:e + 1], 1.0, 0.0).astype(BF16) for e in range(N_EXPERTS)],
        axis=1)
    y = jnp.concatenate(y_parts, axis=0)
    x = x_ref[...] + g2_ref[...] * jnp.dot(onehot, y, preferred_element_type=F32)
    if fg_ref is not None:
        x = (x * lax.rsqrt(jnp.mean(x * x, axis=-1, keepdims=True) + EPS)) * fg_ref[...]
    return x


def _combine_ctx_kernel(slot_ref, y_ref, x_ref, g2_ref, *rest, final):
    o_ref = rest[-1]
    base = (pl.program_id(0) % CTX_SETS_PER_TILE) * CTX_CAP
    o_ref[...] = _combine_rows(slot_ref[...], [base] * N_EXPERTS, CTX_CAP, [y_ref[e] for e in range(N_EXPERTS)],
                               x_ref, g2_ref, rest[0] if final else None)


def _combine_lat_kernel(cuts_ref, slot_ref, y_ref, x_ref, g2_ref, *rest, tr, final):
    o_ref = rest[-1]
    fg_ref = rest[0] if final else None
    r = pl.program_id(2)
    slot = slot_ref[pl.ds(pl.multiple_of(r * tr, tr), tr), :]
    win = 2 * ROUTE_SLOTS * tr // ROUTE_TILE
    bpt = tr // CUT_ROWS
    starts = []
    fits = None
    for e in range(N_EXPERTS):
        lo = jnp.where(r == 0, 0, cuts_ref[e, jnp.maximum(r * bpt - 1, 0)])
        hi = cuts_ref[e, (r + 1) * bpt - 1]
        start = jnp.minimum((lo // BF16_SUBLANES) * BF16_SUBLANES, ROUTE_SLOTS - win)
        starts.append(pl.multiple_of(start, BF16_SUBLANES))
        ok = hi - start <= win
        fits = ok if fits is None else jnp.logical_and(fits, ok)

    @pl.when(fits)
    def _():
        parts = [y_ref[e, pl.ds(starts[e], win), :] for e in range(N_EXPERTS)]
        o_ref[...] = _combine_rows(slot, starts, win, parts, x_ref, g2_ref, fg_ref)

    @pl.when(jnp.logical_not(fits))
    def _():
        parts = [y_ref[e] for e in range(N_EXPERTS)]
        o_ref[...] = _combine_rows(slot, [0] * N_EXPERTS, ROUTE_SLOTS, parts, x_ref, g2_ref, fg_ref)


def _combine(slot_te, cuts, yg, x1, mod4, final_g=None):
    final = final_g is not None
    extra = [final_g.reshape(1, D)] if final else []
    nct = N_CTX_TILES
    y_ctx = pl.pallas_call(
        functools.partial(_combine_ctx_kernel, final=final),
        grid=(BATCH,),
        in_specs=[
            pl.BlockSpec((None, SEQ, N_EXPERTS), lambda s: (s // CTX_SETS_PER_TILE, s % CTX_SETS_PER_TILE, 0)),
            pl.BlockSpec((N_EXPERTS, None, CTX_CAP, D), lambda s: (0, s, 0, 0)),
            pl.BlockSpec((SEQ, D), lambda s: (s, 0)),
            pl.BlockSpec((None, None, 1, D), lambda s: (0, 5, 0, 0)),
        ] + [pl.BlockSpec((1, D), lambda s: (0, 0))] * len(extra),
        out_specs=pl.BlockSpec((SEQ, D), lambda s: (s, 0)),
        out_shape=jax.ShapeDtypeStruct((M_CTX if final else M, D), F32),
        compiler_params=_cparams(("arbitrary",)),
        name="moe_combine_ctx",
    )(slot_te, yg.reshape(N_EXPERTS, SLOTS_PER_EXPERT // CTX_CAP, CTX_CAP, D), x1, mod4, *extra)
    tn, tr = (D, 256) if final else (1024, 512)
    nr = ROUTE_TILE // tr
    orow0 = 0 if final else nct * nr
    in_specs = [
        pl.BlockSpec((None, N_EXPERTS, LANES), lambda i, j, r: (i, 0, 0), memory_space=pltpu.SMEM),
        pl.BlockSpec((None, ROUTE_TILE, N_EXPERTS), lambda i, j, r: (nct + i, 0, 0)),
        pl.BlockSpec((N_EXPERTS, None, ROUTE_SLOTS, tn), lambda i, j, r: (0, nct + i, 0, j)),
        pl.BlockSpec((tr, tn), lambda i, j, r: ((nct + i) * nr + r, j)),
        pl.BlockSpec((None, None, 1, tn), lambda i, j, r: (1 + i, 5, 0, j)),
    ]
    if final:
        in_specs.append(pl.BlockSpec((1, D), lambda i, j, r: (0, 0)))
        args, aliases = extra, {}
    else:
        in_specs.append(ANY_SPEC)
        args, aliases = [y_ctx], {5: 0}
    y_lat = pl.pallas_call(
        functools.partial(_combine_lat_kernel, tr=tr, final=final),
        grid=(N_ROUTE_TILES - nct, D // tn, nr),
        in_specs=in_specs,
        out_specs=pl.BlockSpec((tr, tn), lambda i, j, r: (orow0 + i * nr + r, j)),
        out_shape=jax.ShapeDtypeStruct((M_LAT if final else M, D), F32),
        input_output_aliases=aliases,
        compiler_params=_cparams(("arbitrary", "arbitrary", "arbitrary")),
        name="moe_combine",
    )(cuts, slot_te, yg.reshape(N_EXPERTS, N_ROUTE_TILES, ROUTE_SLOTS, D), x1, mod4, *args)
    return (y_ctx, y_lat) if final else y_lat


def _layer(x_parts, mod, lp, layer, lam_init, tables, cache_k4, cache_v4, st_re, st_im, new_kv, final_g):
    (xc, xc_row0), (xl, xl_row0) = x_parts
    rope_cos, rope_sin, dft_ctx, dft_lat, cs, perms = tables
    mod4 = mod.reshape(N_MOD_ROWS, 6, 1, D)
    w_in = lp['stacked']['w_in']
    h, u, ua_ctx = _norm_proj(xc, xc_row0, lp['norm1_g'], mod4, w_in, layer, True, None)
    h, u, ua_lat = _norm_proj(xl, xl_row0, lp['norm1_g'], mod4, w_in, layer, False, (h, u))

    wb, wc, a_re, a_im = _s5_params(lp)
    ns = S5_GROUPS * S5_STATE
    ngc = BATCH // SUBLANES
    zeros = jnp.zeros((ngc, 2, SUBLANES, ns), F32)
    y_ctx, fin_re, fin_im = _s5(ua_ctx, perms, wb, wc, a_re, a_im, zeros, zeros)
    h0_re = st_re.reshape(DEC_BATCH, 2, ns).transpose(1, 0, 2)[None]
    h0_im = st_im.reshape(DEC_BATCH, 2, ns).transpose(1, 0, 2)[None]
    y_lat, _, _ = _s5(ua_lat, perms, wb, wc, a_re, a_im, h0_re, h0_im)
    d_row = lp['s5_d'].reshape(1, S5_WIDTH)
    w_glu = lp['w_glu'].astype(BF16)
    b_glu = lp['b_glu'].reshape(1, S5_WIDTH)
    ya = _glu(y_ctx, u, d_row, w_glu, b_glu, True, SEQ, None)
    ya = _glu(y_lat, u, d_row, w_glu, b_glu, False, 512, ya)

    yb = _fnet(u, dft_ctx, cs, True, SEQ, None)
    yb = _fnet(u, dft_lat, cs, False, 512, yb)

    lam = (jnp.exp(jnp.sum(lp['lam_q1'].astype(F32) * lp['lam_k1'].astype(F32)))
           - jnp.exp(jnp.sum(lp['lam_q2'].astype(F32) * lp['lam_k2'].astype(F32))) + lam_init).reshape(1)
    subln = lp['subln_g'].reshape(1, VALUE_DIM)
    yc, new_k, new_v = _attention(u, lam, subln, lam_init, True, SEQ, N_HEADS, None, kv_out=(layer, new_kv))
    (yc,) = _attention(u, lam, subln, lam_init, False, DEC_SEQ, 1, yc,
                    ctx=(cache_k4, cache_v4, layer, rope_cos, rope_sin))

    st = lp['stacked']
    merged = _merge(h, ya, yb, yc, st['w_gate'], st['b_gate'], st['w_branch_a'], st['w_branch_b'],
                    st['w_branch_c'], layer)
    x1, h2 = _out_proj(merged, st['w_out'], layer, xc, xc_row0, lp['norm2_g'], mod4, True, None)
    x1, h2 = _out_proj(merged, st['w_out'], layer, xl, xl_row0, lp['norm2_g'], mod4, False, (x1, h2))

    slot_et, gcol, cuts = _router(h2, lp['w_router'].T.astype(BF16))
    xg = _dispatch(h2, slot_et)
    yg = _ffn(xg, lp['moe_w1_all'], lp['moe_w3_all'], lp['moe_w2_all'],
              gcol.reshape(N_EXPERTS, SLOTS_PER_EXPERT, 1), layer)
    x2 = _combine(slot_et.transpose(0, 2, 1), cuts, yg, x1, mod4, final_g)

    fin_shape = (ngc, 2, SUBLANES, S5_GROUPS, S5_STATE)
    s_re = fin_re.reshape(fin_shape).transpose(0, 2, 1, 3, 4).reshape(BATCH, 2, S5_GROUPS, S5_STATE)
    s_im = fin_im.reshape(fin_shape).transpose(0, 2, 1, 3, 4).reshape(BATCH, 2, S5_GROUPS, S5_STATE)
    return x2, (new_k, new_v), s_re, s_im


def kernel(x_prompt, x_sample, cache_k, cache_v, state_s5_re, state_s5_im, c, c_ctx, norm1_g, norm2_g, final_norm_g, w_ada, b_ada, w_in, s5_lam_re, s5_lam_im, s5_log_dt, s5_b_re, s5_b_im, s5_c_re, s5_c_im, s5_d, w_glu, b_glu, lam_q1, lam_k1, lam_q2, lam_k2, subln_g, w_branch_a, w_branch_b, w_branch_c, w_gate, b_gate, w_out, w_router, moe_w1, moe_w3, moe_w2):
    params = dict(norm1_g=norm1_g, norm2_g=norm2_g,
                  s5_lam_re=s5_lam_re, s5_lam_im=s5_lam_im, s5_log_dt=s5_log_dt,
                  s5_b_re=s5_b_re, s5_b_im=s5_b_im, s5_c_re=s5_c_re, s5_c_im=s5_c_im, s5_d=s5_d,
                  w_glu=w_glu, b_glu=b_glu, lam_q1=lam_q1, lam_k1=lam_k1, lam_q2=lam_q2, lam_k2=lam_k2,
                  subln_g=subln_g, w_router=w_router)
    stacked = dict(w_in=w_in.astype(BF16), w_gate=w_gate.astype(BF16), b_gate=b_gate.reshape(DEPTH, 1, 3 * D),
                   w_branch_a=w_branch_a.astype(BF16), w_branch_b=w_branch_b.astype(BF16),
                   w_branch_c=w_branch_c.astype(BF16), w_out=w_out.astype(BF16))
    cc = jnp.zeros((N_MOD_ROWS, D), F32).at[0].set(c_ctx).at[1:1 + DEC_BATCH].set(c)
    mods = _ada(cc, w_ada, b_ada)
    rope_cos, rope_sin = _rope_tables(DEC_SEQ)
    c_c, s_c = _dft_tables(FNET_GROUP_CH)
    cs = jnp.concatenate([c_c, s_c], axis=1).astype(BF16)
    tables = (rope_cos, rope_sin, _fnet_tables(SEQ), _fnet_tables(DEC_SEQ), cs, _s5_permutations())
    cache_k4 = cache_k.reshape(DEC_BATCH, DEPTH, PAST_LEN, QK_WIDTH)
    cache_v4 = cache_v.reshape(DEC_BATCH, DEPTH, PAST_LEN, V_WIDTH)
    x_parts = ((x_prompt.reshape(M_CTX, D), 0), (x_sample.reshape(M_LAT, D), 0))
    new_kv = None
    s_res, s_ims = [], []
    for l in range(DEPTH):
        lp = {name: arr[l] for name, arr in params.items()}
        lp['moe_w1_all'], lp['moe_w3_all'], lp['moe_w2_all'] = moe_w1, moe_w3, moe_w2
        lp['stacked'] = stacked
        lam_init = 0.8 - 0.6 * math.exp(-0.3 * l)
        x, new_kv, s_re, s_im = _layer(x_parts, mods[l], lp, l, lam_init, tables, cache_k4, cache_v4,
                                       state_s5_re[:, l], state_s5_im[:, l], new_kv,
                                       final_norm_g if l == DEPTH - 1 else None)
        x_parts = ((x, 0), (x, M_CTX))
        s_res.append(s_re)
        s_ims.append(s_im)
    y_prompt = x[0].reshape(BATCH, SEQ, D)
    y_sample = x[1].reshape(DEC_BATCH, DEC_SEQ, D)
    kv_shape = (BATCH, DEPTH, SEQ, N_HEADS, VALUE_DIM)
    return (y_prompt, y_sample, new_kv[0].reshape(kv_shape), new_kv[1].reshape(kv_shape),
            jnp.stack(s_res, axis=1), jnp.stack(s_ims, axis=1))
```

```python
import functools
import math

import jax
import jax.numpy as jnp
import numpy as np
from jax import lax
from jax.experimental import pallas as pl
from jax.experimental.pallas import tpu as pltpu

F32 = jnp.float32
BF16 = jnp.bfloat16

D = 2048
BATCH, SEQ = 32, 256
DEC_BATCH, DEC_SEQ = 8, 2048
DEPTH = 2
PAST_LEN = 256
GRID_W = 64
EPS = 1e-6
S5_GROUP_CH, S5_GROUPS, S5_STATE = 16, 32, 64
S5_WIDTH = S5_GROUPS * S5_GROUP_CH
FNET_GROUPS, FNET_GROUP_CH = 4, 128
FNET_WIDTH = FNET_GROUPS * FNET_GROUP_CH
N_HEADS, HEAD_DIM = 8, 64
VALUE_DIM = 2 * HEAD_DIM
QK_WIDTH = N_HEADS * 2 * HEAD_DIM
V_WIDTH = N_HEADS * VALUE_DIM
IN_WIDTH = S5_WIDTH + FNET_WIDTH + 2 * QK_WIDTH + V_WIDTH
ROPE_BASE = 10000.0
ROPE_AXIS_DIM = HEAD_DIM // 2
N_EXPERTS = 16
EXPERT_FF = 2048
CAPACITY_FACTOR = 2

M_CTX = BATCH * SEQ
M_LAT = DEC_BATCH * DEC_SEQ
M = M_CTX + M_LAT
N_MOD_ROWS = 16

LANES = 128
SUBLANES = 8
BF16_SUBLANES = 16
VMEM_LIMIT = 56 * 1024 * 1024

ROUTE_TILE = 2048
ROUTE_SLOTS = CAPACITY_FACTOR * ROUTE_TILE // N_EXPERTS
N_ROUTE_TILES = M // ROUTE_TILE
SLOTS_PER_EXPERT = N_ROUTE_TILES * ROUTE_SLOTS

S5_BLK = 4
S5_BLK_STATES = 512

ANY_SPEC = pl.BlockSpec(memory_space=pl.ANY)


def _cparams(sem):
    return pltpu.CompilerParams(dimension_semantics=sem, vmem_limit_bytes=VMEM_LIMIT)


def _sigmoid(x):
    return 1.0 / (1.0 + jnp.exp(-x))


def _part(is_ctx):
    return (M_CTX, 0) if is_ctx else (M_LAT, M_CTX)


def _mod_row_fn(is_ctx, tm):
    if is_ctx:
        return lambda i: 0
    return lambda i: 1 + i // (DEC_SEQ // tm)


def _ada_kernel(c_ref, w_ref, b_ref, o_ref):
    c = c_ref[...]
    s = (c * _sigmoid(c)).astype(BF16)
    o_ref[...] = jnp.dot(s, w_ref[...].astype(BF16), preferred_element_type=F32) + b_ref[...]


def _ada(cc, w_ada, b_ada):
    tn = 1024
    n = 6 * D
    return pl.pallas_call(
        _ada_kernel,
        grid=(DEPTH, n // tn),
        in_specs=[
            pl.BlockSpec((N_MOD_ROWS, D), lambda l, j: (0, 0)),
            pl.BlockSpec((None, D, tn), lambda l, j: (l, 0, j)),
            pl.BlockSpec((None, 1, tn), lambda l, j: (l, 0, j)),
        ],
        out_specs=pl.BlockSpec((None, N_MOD_ROWS, tn), lambda l, j: (l, 0, j)),
        out_shape=jax.ShapeDtypeStruct((DEPTH, N_MOD_ROWS, n), F32),
        compiler_params=_cparams(("arbitrary", "arbitrary")),
        name="ada",
    )(cc, w_ada, b_ada.reshape(DEPTH, 1, n))


NP_TM, NP_TN = 1024, 1024


NP_NJ = IN_WIDTH // NP_TN


def _norm_proj_kernel(x_ref, g_ref, sc_ref, sh_ref, w_ref, *rest, is_ctx, n_alias, tm):
    h_ref, u_ref, ua_ref, ha_ref, hb_ref = rest[n_alias:]
    i = pl.program_id(0)
    j = pl.program_id(1)
    q = tm // NP_NJ
    rows = pl.ds(pl.multiple_of(j * q, q), q)

    def norm_quarter(dst_ref):
        x = x_ref[rows, :]
        r = x * lax.rsqrt(jnp.mean(x * x, axis=-1, keepdims=True) + EPS)
        dst_ref[rows, :] = ((r * g_ref[...]) * (1.0 + sc_ref[...]) + sh_ref[...]).astype(BF16)

    def project(src_ref):
        h_ref[rows, :] = src_ref[rows, :]
        u_ref[...] = jnp.dot(src_ref[...], w_ref[...], preferred_element_type=F32)

    @pl.when(i == 0)
    def _():
        norm_quarter(ha_ref)

    @pl.when((i > 0) & (i % 2 == 0))
    def _():
        norm_quarter(ha_ref)
        project(hb_ref)

    @pl.when(i % 2 == 1)
    def _():
        norm_quarter(hb_ref)
        project(ha_ref)

    @pl.when((i > 0) & (j == 0))
    def _():
        if is_ctx:
            for b in range(tm // SEQ):
                ua_ref[:, b * S5_WIDTH:(b + 1) * S5_WIDTH] = u_ref[b * SEQ:(b + 1) * SEQ, :S5_WIDTH]
        else:
            ua_ref[...] = u_ref[:, :S5_WIDTH]


def _norm_proj(x, x_row0, g, mod4, w_in, layer, is_ctx, prev):
    tm, tn = NP_TM, NP_TN
    assert S5_WIDTH <= tn
    rows, row0 = _part(is_ctx)
    n_i = rows // tm
    xoff, ooff = x_row0 // tm, row0 // tm
    mrow = _mod_row_fn(is_ctx, tm)
    cur = lambda i: jnp.minimum(i, n_i - 1)
    prv = lambda i: jnp.maximum(i - 1, 0)
    prev = list(prev) if prev is not None else []
    in_specs = [
        pl.BlockSpec((tm, D), lambda i, j: (xoff + cur(i), 0)),
        pl.BlockSpec((1, D), lambda i, j: (0, 0)),
        pl.BlockSpec((None, None, 1, D), lambda i, j: (mrow(cur(i)), 1, 0, 0)),
        pl.BlockSpec((None, None, 1, D), lambda i, j: (mrow(cur(i)), 0, 0, 0)),
        pl.BlockSpec((None, D, tn), lambda i, j: (layer, 0, j)),
    ] + [ANY_SPEC] * len(prev)
    out_specs = [
        pl.BlockSpec((tm, D), lambda i, j: (ooff + prv(i), 0)),
        pl.BlockSpec((tm, tn), lambda i, j: (ooff + prv(i), j)),
    ]
    out_shape = [jax.ShapeDtypeStruct((M, D), BF16), jax.ShapeDtypeStruct((M, IN_WIDTH), F32)]
    if is_ctx:
        nbt = tm // SEQ
        out_specs.append(pl.BlockSpec((SEQ, nbt * S5_WIDTH), lambda i, j: (0, prv(i))))
        out_shape.append(jax.ShapeDtypeStruct((SEQ, BATCH * S5_WIDTH), F32))
        aliases = {}
    else:
        tpb = DEC_SEQ // tm
        out_specs.append(pl.BlockSpec((tm, S5_WIDTH), lambda i, j: (prv(i) % tpb, prv(i) // tpb)))
        out_shape.append(jax.ShapeDtypeStruct((DEC_SEQ, DEC_BATCH * S5_WIDTH), F32))
        aliases = {5: 0, 6: 1}
    return pl.pallas_call(
        functools.partial(_norm_proj_kernel, is_ctx=is_ctx, n_alias=len(prev), tm=tm),
        grid=(n_i + 1, NP_NJ),
        in_specs=in_specs,
        out_specs=out_specs,
        out_shape=out_shape,
        scratch_shapes=[pltpu.VMEM((tm, D), BF16), pltpu.VMEM((tm, D), BF16)],
        input_output_aliases=aliases,
        compiler_params=_cparams(("arbitrary", "arbitrary")),
        name="norm_proj",
    )(x, g.reshape(1, D), mod4, mod4, w_in, *prev)


S5_CHUNK = 64


def _s5_kernel(u_ref, pin_ref, pout_ref, wb_ref, wc_ref, are_ref, aim_ref, h0re_ref, h0im_ref,
               y_ref, finre_ref, finim_ref, xs0_ref, xs1_ref, xs2_ref, xs3_ref, stre_ref, stim_ref):
    c = pl.program_id(2)
    nc = pl.num_programs(2)
    bs = S5_BLK_STATES
    chunk = S5_CHUNK
    xs_refs = (xs0_ref, xs1_ref, xs2_ref, xs3_ref)

    @pl.when(c == 0)
    def _():
        stre_ref[...] = h0re_ref[...]
        stim_ref[...] = h0im_ref[...]

    ustack = jnp.concatenate(
        [u_ref[:, j * S5_WIDTH:(j + 1) * S5_WIDTH] for j in range(SUBLANES)], axis=0).astype(BF16)
    up = jnp.dot(pin_ref[...], ustack, preferred_element_type=F32).astype(BF16)

    for blk in range(S5_BLK):
        xs_refs[blk][...] = jnp.dot(up[:, blk * LANES:(blk + 1) * LANES], wb_ref[blk],
                                    preferred_element_type=F32)

    for blk in range(S5_BLK):
        xs_ref = xs_refs[blk]
        ar = are_ref[:, blk * bs:(blk + 1) * bs]
        ai = aim_ref[:, blk * bs:(blk + 1) * bs]
        sr = stre_ref[:, blk * bs:(blk + 1) * bs]
        si = stim_ref[:, blk * bs:(blk + 1) * bs]
        for i in range(chunk):
            r8 = slice(i * SUBLANES, (i + 1) * SUBLANES)
            nr = ar * sr - ai * si + xs_ref[r8, 0:bs]
            ni = ar * si + ai * sr + xs_ref[r8, bs:2 * bs]
            xs_ref[r8, 0:bs] = nr
            xs_ref[r8, bs:2 * bs] = ni
            sr, si = nr, ni
        stre_ref[:, blk * bs:(blk + 1) * bs] = sr
        stim_ref[:, blk * bs:(blk + 1) * bs] = si

    y = jnp.concatenate(
        [jnp.dot(xs_refs[blk][...].astype(BF16), wc_ref[blk], preferred_element_type=F32)
         for blk in range(S5_BLK)], axis=1)
    y_hi = y.astype(BF16)
    y_lo = (y - y_hi.astype(F32)).astype(BF16)
    ys = (jnp.dot(pout_ref[...], y_hi, preferred_element_type=F32)
          + jnp.dot(pout_ref[...], y_lo, preferred_element_type=F32))
    for j in range(SUBLANES):
        y_ref[:, j * S5_WIDTH:(j + 1) * S5_WIDTH] = ys[j * chunk:(j + 1) * chunk, :]

    @pl.when(c == nc - 1)
    def _():
        finre_ref[...] = stre_ref[...]
        finim_ref[...] = stim_ref[...]


def _s5_permutations():
    chunk = S5_CHUNK
    rows = chunk * SUBLANES
    r_out = lax.broadcasted_iota(jnp.int32, (rows, rows), 0)
    r_in = lax.broadcasted_iota(jnp.int32, (rows, rows), 1)
    step, seq = r_out // SUBLANES, r_out % SUBLANES
    fwd = r_in == seq * chunk + step
    bwd = r_in == seq * chunk + (chunk - 1 - step)
    p = jnp.stack([fwd, bwd]).astype(BF16)
    return p, p.transpose(0, 2, 1)


def _s5(u2d, perms, wb, wc, a_re, a_im, h0_re, h0_im):
    t_len = u2d.shape[0]
    ng = u2d.shape[1] // (SUBLANES * S5_WIDTH)
    chunk = S5_CHUNK
    nc = t_len // chunk
    rows = chunk * SUBLANES
    ns = S5_BLK * S5_BLK_STATES
    pin, pout = perms

    def ceff(d, c):
        return jnp.where(d == 0, c, nc - 1 - c)

    return pl.pallas_call(
        _s5_kernel,
        grid=(ng, 2, nc),
        in_specs=[
            pl.BlockSpec((chunk, SUBLANES * S5_WIDTH), lambda g, d, c: (ceff(d, c), g)),
            pl.BlockSpec((None, rows, rows), lambda g, d, c: (d, 0, 0)),
            pl.BlockSpec((None, rows, rows), lambda g, d, c: (d, 0, 0)),
            pl.BlockSpec((None, S5_BLK, LANES, 2 * S5_BLK_STATES), lambda g, d, c: (d, 0, 0, 0)),
            pl.BlockSpec((None, S5_BLK, 2 * S5_BLK_STATES, LANES), lambda g, d, c: (d, 0, 0, 0)),
            pl.BlockSpec((None, SUBLANES, ns), lambda g, d, c: (d, 0, 0)),
            pl.BlockSpec((None, SUBLANES, ns), lambda g, d, c: (d, 0, 0)),
            pl.BlockSpec((None, None, SUBLANES, ns), lambda g, d, c: (g, d, 0, 0)),
            pl.BlockSpec((None, None, SUBLANES, ns), lambda g, d, c: (g, d, 0, 0)),
        ],
        out_specs=[
            pl.BlockSpec((None, chunk, SUBLANES * S5_WIDTH), lambda g, d, c: (d, ceff(d, c), g)),
            pl.BlockSpec((None, None, SUBLANES, ns), lambda g, d, c: (g, d, 0, 0)),
            pl.BlockSpec((None, None, SUBLANES, ns), lambda g, d, c: (g, d, 0, 0)),
        ],
        out_shape=[
            jax.ShapeDtypeStruct((2,) + u2d.shape, F32),
            jax.ShapeDtypeStruct((ng, 2, SUBLANES, ns), F32),
            jax.ShapeDtypeStruct((ng, 2, SUBLANES, ns), F32),
        ],
        scratch_shapes=[pltpu.VMEM((rows, 2 * S5_BLK_STATES), F32)] * S5_BLK + [
            pltpu.VMEM((SUBLANES, ns), F32),
            pltpu.VMEM((SUBLANES, ns), F32),
        ],
        compiler_params=_cparams(("arbitrary", "arbitrary", "arbitrary")),
        name="s5_scan",
    )(u2d, pin, pout, wb, wc, a_re, a_im, h0_re, h0_im)


def _s5_params(lp):
    lam_re = lp['s5_lam_re'].astype(F32)
    lam_im = lp['s5_lam_im'].astype(F32)
    dt = jnp.exp(lp['s5_log_dt'].astype(F32))[..., None]
    b_re = lp['s5_b_re'].astype(F32)
    b_im = lp['s5_b_im'].astype(F32)
    c_re = lp['s5_c_re'].astype(F32)
    c_im = lp['s5_c_im'].astype(F32)
    mag = jnp.exp(lam_re * dt)
    ab_re = mag * jnp.cos(lam_im * dt)
    ab_im = mag * jnp.sin(lam_im * dt)
    den = lam_re * lam_re + lam_im * lam_im
    f_re = ((ab_re - 1.0) * lam_re + ab_im * lam_im) / den
    f_im = (ab_im * lam_re - (ab_re - 1.0) * lam_im) / den
    bb_re = f_re[..., None] * b_re - f_im[..., None] * b_im
    bb_im = f_re[..., None] * b_im + f_im[..., None] * b_re
    gpb = S5_GROUPS // S5_BLK
    eye = jnp.eye(gpb, dtype=F32)

    def pack_b(bb):
        bb = bb.reshape(2, S5_BLK, gpb, S5_STATE, S5_GROUP_CH)
        w = jnp.einsum('dbgph,gk->dbghkp', bb, eye)
        return w.reshape(2, S5_BLK, gpb * S5_GROUP_CH, gpb * S5_STATE)

    def pack_c(cc):
        cc = cc.reshape(2, S5_BLK, gpb, S5_GROUP_CH, S5_STATE)
        w = jnp.einsum('dbghp,gk->dbkpgh', cc, eye)
        return w.reshape(2, S5_BLK, gpb * S5_STATE, gpb * S5_GROUP_CH)

    wb = jnp.concatenate([pack_b(bb_re), pack_b(bb_im)], axis=-1).astype(BF16)
    wc = jnp.concatenate([pack_c(c_re), -pack_c(c_im)], axis=-2).astype(BF16)
    ns = S5_GROUPS * S5_STATE
    a_re = jnp.broadcast_to(ab_re.reshape(2, 1, ns), (2, SUBLANES, ns))
    a_im = jnp.broadcast_to(ab_im.reshape(2, 1, ns), (2, SUBLANES, ns))
    return wb, wc, a_re, a_im


def _gelu_tanh(x):
    return 0.5 * x * (1.0 + jnp.tanh(math.sqrt(2.0 / math.pi) * (x + 0.044715 * (x * x * x))))


def _glu_kernel(y_ref, u_ref, d_ref, w_ref, b_ref, *rest):
    o_ref = rest[-1]
    y = y_ref[0] + y_ref[1] + d_ref[...] * u_ref[...]
    y = _gelu_tanh(y)
    z = jnp.dot(y.astype(BF16), w_ref[...], preferred_element_type=F32) + b_ref[...]
    o_ref[...] = (y * _sigmoid(z)).astype(BF16)


def _glu(y_dirs, u, d_row, w_glu, b_glu, is_ctx, tt, prev):
    t_len = y_dirs.shape[1]
    nb = y_dirs.shape[2] // S5_WIDTH
    ntt = t_len // tt
    off = _part(is_ctx)[1] // tt
    prev = [prev] if prev is not None else []
    return pl.pallas_call(
        _glu_kernel,
        grid=(nb, ntt),
        in_specs=[
            pl.BlockSpec((2, tt, S5_WIDTH), lambda b, t: (0, t, b)),
            pl.BlockSpec((tt, S5_WIDTH), lambda b, t: (off + b * ntt + t, 0)),
            pl.BlockSpec((1, S5_WIDTH), lambda b, t: (0, 0)),
            pl.BlockSpec((S5_WIDTH, S5_WIDTH), lambda b, t: (0, 0)),
            pl.BlockSpec((1, S5_WIDTH), lambda b, t: (0, 0)),
        ] + [ANY_SPEC] * len(prev),
        out_specs=pl.BlockSpec((tt, S5_WIDTH), lambda b, t: (off + b * ntt + t, 0)),
        out_shape=jax.ShapeDtypeStruct((M, S5_WIDTH), BF16),
        input_output_aliases={5: 0} if prev else {},
        compiler_params=_cparams(("arbitrary", "arbitrary")),
        name="s5_glu",
    )(y_dirs, u, d_row, w_glu, b_glu, *prev)


def _fnet_kernel(x_ref, cs_ref, dft_ref, *rest, t_len):
    o_ref, z_ref = rest[-2:]

    @pl.when(pl.program_id(1) == 0)
    def _():
        x = x_ref[...].astype(BF16)
        for g in range(FNET_GROUPS):
            z = jnp.dot(x[:, g * LANES:(g + 1) * LANES], cs_ref[...], preferred_element_type=F32)
            z_ref[0:t_len, g * LANES:(g + 1) * LANES] = z[:, :LANES].astype(BF16)
            z_ref[t_len:2 * t_len, g * LANES:(g + 1) * LANES] = z[:, LANES:].astype(BF16)

    scale = 1.0 / math.sqrt(t_len * FNET_GROUP_CH)
    o_ref[...] = (jnp.dot(dft_ref[...], z_ref[...], preferred_element_type=F32) * scale).astype(BF16)


def _dft_tables(n):
    j = lax.broadcasted_iota(jnp.int32, (n, n), 0)
    k = lax.broadcasted_iota(jnp.int32, (n, n), 1)
    ang = ((j * k) % n).astype(F32) * (2.0 * math.pi / n)
    return jnp.cos(ang), jnp.sin(ang)


def _fnet_tables(t_len):
    j = np.arange(t_len, dtype=np.int64)
    ang = ((j[:, None] * j[None, :]) % t_len).astype(np.float64) * (2.0 * math.pi / t_len)
    return jnp.asarray(np.concatenate([np.cos(ang), -np.sin(ang)], axis=1), dtype=BF16)


def _fnet(u, dft, cs, is_ctx, tr, prev):
    t_len = dft.shape[0]
    rows, row0 = _part(is_ctx)
    nb = rows // t_len
    ntr = t_len // tr
    off = row0 // t_len
    offr = row0 // tr
    col = S5_WIDTH // FNET_WIDTH
    prev = [prev] if prev is not None else []
    return pl.pallas_call(
        functools.partial(_fnet_kernel, t_len=t_len),
        grid=(nb, ntr),
        in_specs=[
            pl.BlockSpec((t_len, FNET_WIDTH), lambda b, r: (off + b, col)),
            pl.BlockSpec((FNET_GROUP_CH, 2 * FNET_GROUP_CH), lambda b, r: (0, 0)),
            pl.BlockSpec((tr, 2 * t_len), lambda b, r: (r, 0)),
        ] + [ANY_SPEC] * len(prev),
        out_specs=pl.BlockSpec((tr, FNET_WIDTH), lambda b, r: (offr + b * ntr + r, 0)),
        out_shape=jax.ShapeDtypeStruct((M, FNET_WIDTH), BF16),
        scratch_shapes=[pltpu.VMEM((2 * t_len, FNET_WIDTH), BF16)],
        input_output_aliases={3: 0} if prev else {},
        compiler_params=_cparams(("arbitrary", "arbitrary")),
        name="fnet",
    )(u, cs, dft, *prev)


def _rope(x, cos, sin):
    half = ROPE_AXIS_DIM // 2
    lane = lax.broadcasted_iota(jnp.int32, x.shape, 1)
    up = pltpu.roll(x, LANES - half, 1)
    dn = pltpu.roll(x, half, 1)
    rot = jnp.where((lane % ROPE_AXIS_DIM) < half, -up, dn)
    return x * cos + rot * sin


ATTN_KEY_CHUNK = 512
ATTN_ROW_BLOCK = 256


def _attn_kernel(lam_ref, q_ref, k_ref, v_ref, *rest, tq, t_len, hb, has_ctx, lam_init, n_alias, emit_kv):
    pos = 0
    if has_ctx:
        ck_ref, cv_ref, cos_ref, sin_ref = rest[:4]
        pos = 4
    g_ref = rest[pos]
    pos += 1 + n_alias
    o_ref = rest[pos]
    pos += 1
    if emit_kv:
        ko_ref, vo_ref = rest[pos:pos + 2]
        pos += 2
    kk_ref, vv_ref, qs_ref, s_ref = rest[pos:]
    qi = pl.program_id(2)
    if emit_kv:
        @pl.when(qi == 0)
        def _():
            ko_ref[...] = k_ref[...]
            vo_ref[...] = v_ref[...]
    n_keys = kk_ref.shape[1]
    kc, rb = ATTN_KEY_CHUNK, ATTN_ROW_BLOCK
    chunks = [(c0, min(c0 + kc, n_keys)) for c0 in range(0, n_keys, kc)]

    @pl.when(qi == 0)
    def _():
        for h in range(hb):
            hs = slice(h * LANES, (h + 1) * LANES)
            k = k_ref[:, hs]
            if has_ctx:
                k = _rope(k, cos_ref[...], sin_ref[...])
                kk_ref[h, t_len:, :] = ck_ref[:, hs].astype(BF16)
                vv_ref[h, t_len:, 0:VALUE_DIM] = cv_ref[:, hs].astype(BF16)
            kk_ref[h, 0:t_len, :] = k.astype(BF16)
            vv_ref[h, 0:t_len, 0:VALUE_DIM] = v_ref[:, hs].astype(BF16)
            vv_ref[h, :, VALUE_DIM:] = jnp.ones((n_keys, VALUE_DIM), BF16)

    lane = lax.broadcasted_iota(jnp.int32, (tq, LANES), 1)
    for h in range(hb):
        hs = slice(h * LANES, (h + 1) * LANES)
        q = q_ref[:, hs]
        if has_ctx:
            r0 = pl.multiple_of(qi * tq, tq)
            q = _rope(q, cos_ref[pl.ds(r0, tq), :], sin_ref[pl.ds(r0, tq), :])
        q = q * (HEAD_DIM ** -0.5)
        qs_ref[0, :, hs] = jnp.where(lane < HEAD_DIM, q, 0.0).astype(BF16)
        qs_ref[1, :, hs] = jnp.where(lane >= HEAD_DIM, q, 0.0).astype(BF16)

    lam = lam_ref[0]
    gain = g_ref[...]
    dn = (((1,), (1,)), ((), ()))

    unit = 0
    for r in range(tq // rb):
        rows = slice(r * rb, (r + 1) * rb)
        for h in range(hb):
            hs = slice(h * LANES, (h + 1) * LANES)
            o = None
            for n in range(2):
                slot = unit % 2
                unit += 1
                qn = qs_ref[n, rows, hs]
                mrun = None
                for c0, c1 in chunks:
                    s_c = lax.dot_general(qn, kk_ref[h, c0:c1, :], dn, preferred_element_type=F32)
                    s_ref[slot, :, c0:c1] = s_c
                    for l0 in range(0, c1 - c0, LANES):
                        mc = s_c[:, l0:l0 + LANES]
                        mrun = mc if mrun is None else jnp.maximum(mrun, mc)
                m_b = jnp.broadcast_to(jnp.max(mrun, axis=-1, keepdims=True), (rb, LANES))
                oa = None
                for c0, c1 in chunks:
                    p = jnp.concatenate(
                        [jnp.exp(s_ref[slot, :, l0:l0 + LANES] - m_b) for l0 in range(c0, c1, LANES)],
                        axis=1).astype(BF16)
                    part = jnp.dot(p, vv_ref[h, c0:c1, :], preferred_element_type=F32)
                    oa = part if oa is None else oa + part
                on = oa[:, :VALUE_DIM] * (1.0 / oa[:, VALUE_DIM:VALUE_DIM + 1])
                o = on if n == 0 else o - lam * on
            o = o * lax.rsqrt(jnp.mean(o * o, axis=-1, keepdims=True) + EPS)
            o_ref[rows, hs] = ((o * gain) * (1.0 - lam_init)).astype(BF16)


def _attention(u, lam, subln_g, lam_init, is_ctx, tq, hb, prev, ctx=None, kv_out=None):
    rows, row0 = _part(is_ctx)
    t_len = SEQ if is_ctx else DEC_SEQ
    nb = rows // t_len
    nq = t_len // tq
    off = row0 // t_len
    offq = row0 // tq
    width = hb * LANES
    qcol = (S5_WIDTH + FNET_WIDTH) // width
    kcol = qcol + QK_WIDTH // width
    vcol = kcol + QK_WIDTH // width
    has_ctx = ctx is not None
    n_keys = t_len + (PAST_LEN if has_ctx else 0)
    assert tq % ATTN_ROW_BLOCK == 0 and n_keys % (2 * LANES) == 0
    in_specs = [
        pl.BlockSpec(memory_space=pltpu.SMEM),
        pl.BlockSpec((tq, width), lambda b, h, i: (offq + b * nq + i, qcol + h)),
        pl.BlockSpec((t_len, width), lambda b, h, i: (off + b, kcol + h)),
        pl.BlockSpec((t_len, width), lambda b, h, i: (off + b, vcol + h)),
    ]
    args = [lam, u, u, u]
    if has_ctx:
        ck, cv, layer, cos, sin = ctx
        in_specs += [
            pl.BlockSpec((None, None, PAST_LEN, width), lambda b, h, i: (b, layer, 0, h)),
            pl.BlockSpec((None, None, PAST_LEN, width), lambda b, h, i: (b, layer, 0, h)),
            pl.BlockSpec((t_len, LANES), lambda b, h, i: (0, 0)),
            pl.BlockSpec((t_len, LANES), lambda b, h, i: (0, 0)),
        ]
        args += [ck, cv, cos, sin]
    in_specs.append(pl.BlockSpec((1, LANES), lambda b, h, i: (0, 0)))
    args.append(subln_g)
    aliases = {}
    n_fixed = len(args)
    if prev is not None:
        aliases[len(args)] = 0
        in_specs.append(ANY_SPEC)
        args.append(prev)
    out_specs = [pl.BlockSpec((tq, width), lambda b, h, i: (offq + b * nq + i, h))]
    out_shape = [jax.ShapeDtypeStruct((M, V_WIDTH), BF16)]
    if kv_out is not None:
        kv_layer, prev_kv = kv_out
        assert hb == N_HEADS and tq == t_len and nb == BATCH
        out_specs += [pl.BlockSpec((None, None, SEQ, QK_WIDTH), lambda b, h, i: (b, kv_layer, 0, 0)),
                      pl.BlockSpec((None, None, SEQ, V_WIDTH), lambda b, h, i: (b, kv_layer, 0, 0))]
        out_shape += [jax.ShapeDtypeStruct((BATCH, DEPTH, SEQ, QK_WIDTH), F32),
                      jax.ShapeDtypeStruct((BATCH, DEPTH, SEQ, V_WIDTH), F32)]
        if prev_kv is not None:
            for o_idx, arr in enumerate(prev_kv):
                aliases[len(args)] = 1 + o_idx
                in_specs.append(ANY_SPEC)
                args.append(arr)
    kern = functools.partial(_attn_kernel, tq=tq, t_len=t_len, hb=hb, has_ctx=has_ctx, lam_init=lam_init,
                             n_alias=len(args) - n_fixed, emit_kv=kv_out is not None)
    return pl.pallas_call(
        kern,
        grid=(nb, N_HEADS // hb, nq),
        in_specs=in_specs,
        out_specs=out_specs,
        out_shape=out_shape,
        scratch_shapes=[
            pltpu.VMEM((hb, n_keys, LANES), BF16),
            pltpu.VMEM((hb, n_keys, 2 * VALUE_DIM), BF16),
            pltpu.VMEM((2, tq, width), BF16),
            pltpu.VMEM((2, ATTN_ROW_BLOCK, n_keys), F32),
        ],
        input_output_aliases=aliases,
        compiler_params=_cparams(("arbitrary", "arbitrary", "arbitrary")),
        name="diff_attn",
    )(*args)


def _rope_tables(t_len):
    rows = t_len // GRID_W
    pos_row = jnp.broadcast_to(jnp.arange(rows, dtype=F32)[:, None], (rows, GRID_W)).reshape(-1)
    pos_col = jnp.broadcast_to(jnp.arange(GRID_W, dtype=F32)[None, :], (rows, GRID_W)).reshape(-1)
    inv = ROPE_BASE ** (-jnp.arange(0, ROPE_AXIS_DIM, 2, dtype=F32) / ROPE_AXIS_DIM)
    ang_r = pos_row[:, None] * inv
    ang_c = pos_col[:, None] * inv
    cos = jnp.concatenate([jnp.cos(ang_r), jnp.cos(ang_r), jnp.cos(ang_c), jnp.cos(ang_c)], axis=-1)
    sin = jnp.concatenate([jnp.sin(ang_r), jnp.sin(ang_r), jnp.sin(ang_c), jnp.sin(ang_c)], axis=-1)
    return jnp.concatenate([cos, cos], axis=-1), jnp.concatenate([sin, sin], axis=-1)


def _merge_kernel(h_ref, ya_ref, yb_ref, yc_ref, wga_ref, wgb_ref, wgc_ref, bga_ref, bgb_ref, bgc_ref,
                  wa_ref, wb_ref, wc_ref, o_ref):
    h = h_ref[...]
    acc = None
    for wg, bg, y, w in ((wga_ref, bga_ref, ya_ref, wa_ref), (wgb_ref, bgb_ref, yb_ref, wb_ref),
                         (wgc_ref, bgc_ref, yc_ref, wc_ref)):
        gate = _sigmoid(jnp.dot(h, wg[...], preferred_element_type=F32) + bg[...])
        term = gate * jnp.dot(y[...], w[...], preferred_element_type=F32)
        acc = term if acc is None else acc + term
    o_ref[...] = acc.astype(BF16)


def _merge(h, ya, yb, yc, w_gate, b_gate, w_a, w_b, w_c, layer):
    tm, tn = 1024, 512
    nj = D // tn
    row = lambda i, j: (i, 0)
    wspecs = [pl.BlockSpec((None, D, tn), lambda i, j, k=k: (layer, 0, k * nj + j)) for k in range(3)]
    bspecs = [pl.BlockSpec((None, 1, tn), lambda i, j, k=k: (layer, 0, k * nj + j)) for k in range(3)]
    return pl.pallas_call(
        _merge_kernel,
        grid=(M // tm, nj),
        in_specs=[
            pl.BlockSpec((tm, D), row),
            pl.BlockSpec((tm, S5_WIDTH), row),
            pl.BlockSpec((tm, FNET_WIDTH), row),
            pl.BlockSpec((tm, V_WIDTH), row),
            *wspecs, *bspecs,
            pl.BlockSpec((None, S5_WIDTH, tn), lambda i, j: (layer, 0, j)),
            pl.BlockSpec((None, FNET_WIDTH, tn), lambda i, j: (layer, 0, j)),
            pl.BlockSpec((None, V_WIDTH, tn), lambda i, j: (layer, 0, j)),
        ],
        out_specs=pl.BlockSpec((tm, tn), lambda i, j: (i, j)),
        out_shape=jax.ShapeDtypeStruct((M, D), BF16),
        compiler_params=_cparams(("arbitrary", "arbitrary")),
        name="merge",
    )(h, ya, yb, yc, w_gate, w_gate, w_gate, b_gate, b_gate, b_gate, w_a, w_b, w_c)


def _out_proj_kernel(m_ref, w_ref, x_ref, g1_ref, n2_ref, sc_ref, sh_ref, *rest):
    x1_ref, h2_ref = rest[-2:]
    x1 = x_ref[...] + g1_ref[...] * jnp.dot(m_ref[...], w_ref[...], preferred_element_type=F32)
    x1_ref[...] = x1
    r = x1 * lax.rsqrt(jnp.mean(x1 * x1, axis=-1, keepdims=True) + EPS)
    h2_ref[...] = ((r * n2_ref[...]) * (1.0 + sc_ref[...]) + sh_ref[...]).astype(BF16)


def _out_proj(merged, w_out, layer, x, x_row0, norm2_g, mod4, is_ctx, prev):
    tm = 512
    rows, row0 = _part(is_ctx)
    xoff, ooff = x_row0 // tm, row0 // tm
    mrow = _mod_row_fn(is_ctx, tm)
    orow = lambda i: (ooff + i, 0)
    mspec = lambda k: pl.BlockSpec((None, None, 1, D), lambda i, k=k: (mrow(i), k, 0, 0))
    prev = list(prev) if prev is not None else []
    return pl.pallas_call(
        _out_proj_kernel,
        grid=(rows // tm,),
        in_specs=[
            pl.BlockSpec((tm, D), orow),
            pl.BlockSpec((None, D, D), lambda i: (layer, 0, 0)),
            pl.BlockSpec((tm, D), lambda i: (xoff + i, 0)),
            mspec(2),
            pl.BlockSpec((1, D), lambda i: (0, 0)),
            mspec(4),
            mspec(3),
        ] + [ANY_SPEC] * len(prev),
        out_specs=[pl.BlockSpec((tm, D), orow), pl.BlockSpec((tm, D), orow)],
        out_shape=[jax.ShapeDtypeStruct((M, D), F32), jax.ShapeDtypeStruct((M, D), BF16)],
        input_output_aliases={7: 0, 8: 1} if prev else {},
        compiler_params=_cparams(("arbitrary",)),
        name="out_proj",
    )(merged, w_out, x, mod4, norm2_g.reshape(1, D), mod4, mod4, *prev)


def _segment_sum(x, seg):
    parts = []
    for s in range(ROUTE_TILE // seg):
        tot = jnp.sum(x[:, s * seg:(s + 1) * seg], axis=1, keepdims=True)
        parts.append(jnp.broadcast_to(tot, (x.shape[0], seg)))
    return parts[0] if len(parts) == 1 else jnp.concatenate(parts, axis=1)


def _segment_cumsum(x, seg):
    pos = lax.broadcasted_iota(jnp.int32, x.shape, 1) % seg
    sh = 1
    while sh < seg:
        x = x + jnp.where(pos >= sh, pltpu.roll(x, sh, 1), 0.0)
        sh *= 2
    return x


CUT_ROWS = 256


def _router_kernel(h_ref, wr_ref, slot_ref, gcol_ref, cuts_ref, *, seg):
    cap = CAPACITY_FACTOR * seg // N_EXPERTS
    logits = lax.dot_general(wr_ref[...], h_ref[...], (((1,), (1,)), ((), ())),
                             preferred_element_type=F32)
    p = jnp.exp(logits - jnp.max(logits, axis=0, keepdims=True))
    aff = p / jnp.sum(p, axis=0, keepdims=True)
    bits = pltpu.bitcast(aff, jnp.int32)

    def bis(i, thr):
        cand = thr | jnp.left_shift(jnp.int32(1), 29 - i)
        cnt = _segment_sum(jnp.where(bits >= cand, 1.0, 0.0), seg)
        return jnp.where(cnt >= cap, cand, thr)

    thr = lax.fori_loop(0, 30, bis, jnp.zeros(bits.shape, jnp.int32))
    gt = jnp.where(bits > thr, 1.0, 0.0)
    eq = jnp.where(bits == thr, 1.0, 0.0)
    need = cap - _segment_sum(gt, seg)
    eq_rank = _segment_cumsum(eq, seg)
    sel = gt + eq * jnp.where(eq_rank <= need, 1.0, 0.0)
    rank = _segment_cumsum(sel, seg)
    seg_id = lax.broadcasted_iota(jnp.int32, bits.shape, 1) // seg
    slot = seg_id * cap + rank.astype(jnp.int32) - 1
    slot = jnp.where(sel > 0.0, slot, -1)
    slot_ref[...] = slot
    ends = [rank[:, (k + 1) * CUT_ROWS - 1:(k + 1) * CUT_ROWS] for k in range(ROUTE_TILE // CUT_ROWS)]
    pad = jnp.zeros((N_EXPERTS, LANES - len(ends)), F32)
    cuts_ref[...] = jnp.concatenate(ends + [pad], axis=1).astype(jnp.int32)

    iota_c = lax.broadcasted_iota(jnp.int32, (ROUTE_SLOTS, ROUTE_TILE), 0)
    for e in range(N_EXPERTS):
        hit = iota_c == slot[e:e + 1, :]
        gcol_ref[e] = jnp.sum(jnp.where(hit, aff[e:e + 1, :], 0.0), axis=1, keepdims=True)


def _router(h2, w_router_t):
    def call(seg, tile0, ntiles):
        return pl.pallas_call(
            functools.partial(_router_kernel, seg=seg),
            grid=(ntiles,),
            in_specs=[
                pl.BlockSpec((ROUTE_TILE, D), lambda i: (tile0 + i, 0)),
                pl.BlockSpec((N_EXPERTS, D), lambda i: (0, 0)),
            ],
            out_specs=[
                pl.BlockSpec((None, N_EXPERTS, ROUTE_TILE), lambda i: (i, 0, 0)),
                pl.BlockSpec((N_EXPERTS, None, ROUTE_SLOTS, 1), lambda i: (0, i, 0, 0)),
                pl.BlockSpec((None, N_EXPERTS, LANES), lambda i: (i, 0, 0)),
            ],
            out_shape=[
                jax.ShapeDtypeStruct((ntiles, N_EXPERTS, ROUTE_TILE), jnp.int32),
                jax.ShapeDtypeStruct((N_EXPERTS, ntiles, ROUTE_SLOTS, 1), F32),
                jax.ShapeDtypeStruct((ntiles, N_EXPERTS, LANES), jnp.int32),
            ],
            compiler_params=_cparams(("arbitrary",)),
            name="router",
        )(h2, w_router_t)

    nct = M_CTX // ROUTE_TILE
    slot_c, g_c, _ = call(SEQ, 0, nct)
    slot_l, g_l, cuts_l = call(DEC_SEQ, nct, N_ROUTE_TILES - nct)
    return jnp.concatenate([slot_c, slot_l], axis=0), jnp.concatenate([g_c, g_l], axis=1), cuts_l


CTX_CAP = CAPACITY_FACTOR * SEQ // N_EXPERTS
CTX_SETS_PER_TILE = ROUTE_TILE // SEQ
N_CTX_TILES = M_CTX // ROUTE_TILE


def _dispatch_ctx_kernel(h_ref, slot_ref, o_ref):
    base = (pl.program_id(0) % CTX_SETS_PER_TILE) * CTX_CAP
    slot = slot_ref[...]
    iota_c = lax.broadcasted_iota(jnp.int32, (CTX_CAP, SEQ), 0) + base
    onehot = jnp.concatenate(
        [jnp.where(iota_c == slot[e:e + 1, :], 1.0, 0.0).astype(BF16) for e in range(N_EXPERTS)], axis=0)
    res = jnp.dot(onehot, h_ref[...], preferred_element_type=F32).astype(BF16)
    for e in range(N_EXPERTS):
        o_ref[e] = res[e * CTX_CAP:(e + 1) * CTX_CAP, :]


DISPATCH_EXPERTS = 4


DISPATCH_WINDOW = 2 * ROUTE_SLOTS * CUT_ROWS // ROUTE_TILE


def _dispatch_lat_kernel(cuts_ref, h_ref, slot_ref, prev_ref, o_ref, acc_ref):
    e0 = pl.program_id(1) * DISPATCH_EXPERTS
    n_blk = ROUTE_TILE // CUT_ROWS
    win = DISPATCH_WINDOW
    starts = {}
    fits = None
    for j in range(DISPATCH_EXPERTS):
        for k in range(n_blk):
            lo = cuts_ref[e0 + j, k - 1] if k > 0 else 0
            hi = cuts_ref[e0 + j, k]
            start = jnp.minimum((lo // BF16_SUBLANES) * BF16_SUBLANES, ROUTE_SLOTS - win)
            starts[j, k] = pl.multiple_of(start, BF16_SUBLANES)
            ok = hi - start <= win
            fits = ok if fits is None else jnp.logical_and(fits, ok)

    @pl.when(fits)
    def _():
        acc_ref[...] = jnp.zeros(acc_ref.shape, F32)
        iota_c = lax.broadcasted_iota(jnp.int32, (win, CUT_ROWS), 0)
        for k in range(n_blk):
            cols = slice(k * CUT_ROWS, (k + 1) * CUT_ROWS)
            onehot = jnp.concatenate(
                [jnp.where(iota_c + starts[j, k] == slot_ref[pl.ds(e0 + j, 1), cols], 1.0, 0.0).astype(BF16)
                 for j in range(DISPATCH_EXPERTS)], axis=0)
            part = jnp.dot(onehot, h_ref[cols, :], preferred_element_type=F32)
            for j in range(DISPATCH_EXPERTS):
                acc_ref[j, pl.ds(starts[j, k], win), :] += part[j * win:(j + 1) * win, :]
        o_ref[...] = acc_ref[...].astype(BF16)

    @pl.when(jnp.logical_not(fits))
    def _():
        iota_c = lax.broadcasted_iota(jnp.int32, (ROUTE_SLOTS, ROUTE_TILE), 0)
        onehot = jnp.concatenate(
            [jnp.where(iota_c == slot_ref[pl.ds(e0 + j, 1), :], 1.0, 0.0).astype(BF16)
             for j in range(DISPATCH_EXPERTS)], axis=0)
        res = jnp.dot(onehot, h_ref[...], preferred_element_type=F32).astype(BF16)
        for j in range(DISPATCH_EXPERTS):
            o_ref[j] = res[j * ROUTE_SLOTS:(j + 1) * ROUTE_SLOTS, :]


def _dispatch(h2, slot_et, cuts):
    xg = pl.pallas_call(
        _dispatch_ctx_kernel,
        grid=(BATCH,),
        in_specs=[
            pl.BlockSpec((SEQ, D), lambda s: (s, 0)),
            pl.BlockSpec((None, N_EXPERTS, SEQ), lambda s: (s // CTX_SETS_PER_TILE, 0, s % CTX_SETS_PER_TILE)),
        ],
        out_specs=pl.BlockSpec((N_EXPERTS, None, CTX_CAP, D), lambda s: (0, s, 0, 0)),
        out_shape=jax.ShapeDtypeStruct((N_EXPERTS, SLOTS_PER_EXPERT // CTX_CAP, CTX_CAP, D), BF16),
        compiler_params=_cparams(("arbitrary",)),
        name="moe_dispatch_ctx",
    )(h2, slot_et)
    return pl.pallas_call(
        _dispatch_lat_kernel,
        grid=(N_ROUTE_TILES - N_CTX_TILES, N_EXPERTS // DISPATCH_EXPERTS),
        in_specs=[
            pl.BlockSpec((None, N_EXPERTS, LANES), lambda i, e: (i, 0, 0), memory_space=pltpu.SMEM),
            pl.BlockSpec((ROUTE_TILE, D), lambda i, e: (N_CTX_TILES + i, 0)),
            pl.BlockSpec((None, N_EXPERTS, ROUTE_TILE), lambda i, e: (N_CTX_TILES + i, 0, 0)),
            ANY_SPEC,
        ],
        out_specs=pl.BlockSpec((DISPATCH_EXPERTS, ROUTE_SLOTS, D), lambda i, e: (e, N_CTX_TILES + i, 0)),
        out_shape=jax.ShapeDtypeStruct((N_EXPERTS, SLOTS_PER_EXPERT, D), BF16),
        scratch_shapes=[pltpu.VMEM((DISPATCH_EXPERTS, ROUTE_SLOTS, D), F32)],
        input_output_aliases={3: 0},
        compiler_params=_cparams(("arbitrary", "arbitrary")),
        name="moe_dispatch",
    )(cuts, h2, slot_et, xg.reshape(N_EXPERTS, SLOTS_PER_EXPERT, D))


def _ffn_kernel(x_ref, w1_ref, w3_ref, w2_ref, g_ref, o_ref, acc_ref, *, tn):
    f = pl.program_id(2)
    nf = pl.num_programs(2)

    @pl.when(f == 0)
    def _():
        acc_ref[...] = jnp.zeros(acc_ref.shape, F32)

    a = b = None
    for k in range(D // tn):
        ks = slice(k * tn, (k + 1) * tn)
        xk = x_ref[:, ks]
        pa = jnp.dot(xk, w1_ref[ks, :].astype(BF16), preferred_element_type=F32)
        pb = jnp.dot(xk, w3_ref[ks, :].astype(BF16), preferred_element_type=F32)
        a = pa if a is None else a + pa
        b = pb if b is None else b + pb
    hmid = ((a * _sigmoid(a)) * b).astype(BF16)
    for n in range(D // tn):
        cols = slice(n * tn, (n + 1) * tn)
        acc_ref[:, cols] += jnp.dot(hmid, w2_ref[:, cols].astype(BF16), preferred_element_type=F32)

    @pl.when(f == nf - 1)
    def _():
        o_ref[...] = (acc_ref[...] * g_ref[...]).astype(BF16)


def _ffn(xg, w1, w3, w2, gcol, layer):
    tr, tf, tn = 1024, 512, 512
    nr = SLOTS_PER_EXPERT // tr
    return pl.pallas_call(
        functools.partial(_ffn_kernel, tn=tn),
        grid=(N_EXPERTS, nr, EXPERT_FF // tf),
        in_specs=[
            pl.BlockSpec((None, tr, D), lambda e, r, f: (e, r, 0)),
            pl.BlockSpec((None, None, D, tf), lambda e, r, f: (layer, e, 0, f)),
            pl.BlockSpec((None, None, D, tf), lambda e, r, f: (layer, e, 0, f)),
            pl.BlockSpec((None, None, tf, D), lambda e, r, f: (layer, e, f, 0)),
            pl.BlockSpec((None, tr, 1), lambda e, r, f: (e, r, 0)),
        ],
        out_specs=pl.BlockSpec((None, tr, D), lambda e, r, f: (e, r, 0)),
        out_shape=jax.ShapeDtypeStruct((N_EXPERTS, SLOTS_PER_EXPERT, D), BF16),
        scratch_shapes=[pltpu.VMEM((tr, D), F32)],
        compiler_params=_cparams(("arbitrary", "arbitrary", "arbitrary")),
        name="moe_ffn",
    )(xg, w1, w3, w2, gcol)


def _combine_rows(slot, bases, n_slots, y_parts, x_ref, g2_ref, fg_ref):
    rows = slot.shape[0]
    iota_c = lax.broadcasted_iota(jnp.int32, (rows, n_slots), 1)
    onehot = jnp.concatenate(
        [jnp.where(iota_c + bases[e] == slot[:, e:e + 1], 1.0, 0.0).astype(BF16) for e in range(N_EXPERTS)],
        axis=1)
    y = jnp.concatenate(y_parts, axis=0)
    x = x_ref[...] + g2_ref[...] * jnp.dot(onehot, y, preferred_element_type=F32)
    if fg_ref is not None:
        x = (x * lax.rsqrt(jnp.mean(x * x, axis=-1, keepdims=True) + EPS)) * fg_ref[...]
    return x


def _combine_ctx_kernel(slot_ref, y_ref, x_ref, g2_ref, *rest, final):
    o_ref = rest[-1]
    base = (pl.program_id(0) % CTX_SETS_PER_TILE) * CTX_CAP
    o_ref[...] = _combine_rows(slot_ref[...], [base] * N_EXPERTS, CTX_CAP, [y_ref[e] for e in range(N_EXPERTS)],
                               x_ref, g2_ref, rest[0] if final else None)


def _combine_lat_kernel(cuts_ref, slot_ref, y_ref, x_ref, g2_ref, *rest, tr, final):
    o_ref = rest[-1]
    fg_ref = rest[0] if final else None
    r = pl.program_id(2)
    slot = slot_ref[pl.ds(pl.multiple_of(r * tr, tr), tr), :]
    win = 2 * ROUTE_SLOTS * tr // ROUTE_TILE
    bpt = tr // CUT_ROWS
    starts = []
    fits = None
    for e in range(N_EXPERTS):
        lo = jnp.where(r == 0, 0, cuts_ref[e, jnp.maximum(r * bpt - 1, 0)])
        hi = cuts_ref[e, (r + 1) * bpt - 1]
        start = jnp.minimum((lo // BF16_SUBLANES) * BF16_SUBLANES, ROUTE_SLOTS - win)
        starts.append(pl.multiple_of(start, BF16_SUBLANES))
        ok = hi - start <= win
        fits = ok if fits is None else jnp.logical_and(fits, ok)

    @pl.when(fits)
    def _():
        parts = [y_ref[e, pl.ds(starts[e], win), :] for e in range(N_EXPERTS)]
        o_ref[...] = _combine_rows(slot, starts, win, parts, x_ref, g2_ref, fg_ref)

    @pl.when(jnp.logical_not(fits))
    def _():
        parts = [y_ref[e] for e in range(N_EXPERTS)]
        o_ref[...] = _combine_rows(slot, [0] * N_EXPERTS, ROUTE_SLOTS, parts, x_ref, g2_ref, fg_ref)


def _combine(slot_te, cuts, yg, x1, mod4, final_g=None):
    final = final_g is not None
    extra = [final_g.reshape(1, D)] if final else []
    nct = N_CTX_TILES
    y_ctx = pl.pallas_call(
        functools.partial(_combine_ctx_kernel, final=final),
        grid=(BATCH,),
        in_specs=[
            pl.BlockSpec((None, SEQ, N_EXPERTS), lambda s: (s // CTX_SETS_PER_TILE, s % CTX_SETS_PER_TILE, 0)),
            pl.BlockSpec((N_EXPERTS, None, CTX_CAP, D), lambda s: (0, s, 0, 0)),
            pl.BlockSpec((SEQ, D), lambda s: (s, 0)),
            pl.BlockSpec((None, None, 1, D), lambda s: (0, 5, 0, 0)),
        ] + [pl.BlockSpec((1, D), lambda s: (0, 0))] * len(extra),
        out_specs=pl.BlockSpec((SEQ, D), lambda s: (s, 0)),
        out_shape=jax.ShapeDtypeStruct((M_CTX if final else M, D), F32),
        compiler_params=_cparams(("arbitrary",)),
        name="moe_combine_ctx",
    )(slot_te, yg.reshape(N_EXPERTS, SLOTS_PER_EXPERT // CTX_CAP, CTX_CAP, D), x1, mod4, *extra)
    tn, tr = (D, 256) if final else (1024, 512)
    nr = ROUTE_TILE // tr
    orow0 = 0 if final else nct * nr
    in_specs = [
        pl.BlockSpec((None, N_EXPERTS, LANES), lambda i, j, r: (i, 0, 0), memory_space=pltpu.SMEM),
        pl.BlockSpec((None, ROUTE_TILE, N_EXPERTS), lambda i, j, r: (nct + i, 0, 0)),
        pl.BlockSpec((N_EXPERTS, None, ROUTE_SLOTS, tn), lambda i, j, r: (0, nct + i, 0, j)),
        pl.BlockSpec((tr, tn), lambda i, j, r: ((nct + i) * nr + r, j)),
        pl.BlockSpec((None, None, 1, tn), lambda i, j, r: (1 + i, 5, 0, j)),
    ]
    if final:
        in_specs.append(pl.BlockSpec((1, D), lambda i, j, r: (0, 0)))
        args, aliases = extra, {}
    else:
        in_specs.append(ANY_SPEC)
        args, aliases = [y_ctx], {5: 0}
    y_lat = pl.pallas_call(
        functools.partial(_combine_lat_kernel, tr=tr, final=final),
        grid=(N_ROUTE_TILES - nct, D // tn, nr),
        in_specs=in_specs,
        out_specs=pl.BlockSpec((tr, tn), lambda i, j, r: (orow0 + i * nr + r, j)),
        out_shape=jax.ShapeDtypeStruct((M_LAT if final else M, D), F32),
        input_output_aliases=aliases,
        compiler_params=_cparams(("arbitrary", "arbitrary", "arbitrary")),
        name="moe_combine",
    )(cuts, slot_te, yg.reshape(N_EXPERTS, N_ROUTE_TILES, ROUTE_SLOTS, D), x1, mod4, *args)
    return (y_ctx, y_lat) if final else y_lat


def _layer(x_parts, mod, lp, layer, lam_init, tables, cache_k4, cache_v4, st_re, st_im, new_kv, final_g):
    (xc, xc_row0), (xl, xl_row0) = x_parts
    rope_cos, rope_sin, dft_ctx, dft_lat, cs, perms = tables
    mod4 = mod.reshape(N_MOD_ROWS, 6, 1, D)
    w_in = lp['stacked']['w_in']
    h, u, ua_ctx = _norm_proj(xc, xc_row0, lp['norm1_g'], mod4, w_in, layer, True, None)
    h, u, ua_lat = _norm_proj(xl, xl_row0, lp['norm1_g'], mod4, w_in, layer, False, (h, u))

    wb, wc, a_re, a_im = _s5_params(lp)
    ns = S5_GROUPS * S5_STATE
    ngc = BATCH // SUBLANES
    zeros = jnp.zeros((ngc, 2, SUBLANES, ns), F32)
    y_ctx, fin_re, fin_im = _s5(ua_ctx, perms, wb, wc, a_re, a_im, zeros, zeros)
    h0_re = st_re.reshape(DEC_BATCH, 2, ns).transpose(1, 0, 2)[None]
    h0_im = st_im.reshape(DEC_BATCH, 2, ns).transpose(1, 0, 2)[None]
    y_lat, _, _ = _s5(ua_lat, perms, wb, wc, a_re, a_im, h0_re, h0_im)
    d_row = lp['s5_d'].reshape(1, S5_WIDTH)
    w_glu = lp['w_glu'].astype(BF16)
    b_glu = lp['b_glu'].reshape(1, S5_WIDTH)
    ya = _glu(y_ctx, u, d_row, w_glu, b_glu, True, SEQ, None)
    ya = _glu(y_lat, u, d_row, w_glu, b_glu, False, 512, ya)

    yb = _fnet(u, dft_ctx, cs, True, SEQ, None)
    yb = _fnet(u, dft_lat, cs, False, 512, yb)

    lam = (jnp.exp(jnp.sum(lp['lam_q1'].astype(F32) * lp['lam_k1'].astype(F32)))
           - jnp.exp(jnp.sum(lp['lam_q2'].astype(F32) * lp['lam_k2'].astype(F32))) + lam_init).reshape(1)
    subln = lp['subln_g'].reshape(1, VALUE_DIM)
    yc, new_k, new_v = _attention(u, lam, subln, lam_init, True, SEQ, N_HEADS, None, kv_out=(layer, new_kv))
    (yc,) = _attention(u, lam, subln, lam_init, False, DEC_SEQ, 1, yc,
                    ctx=(cache_k4, cache_v4, layer, rope_cos, rope_sin))

    st = lp['stacked']
    merged = _merge(h, ya, yb, yc, st['w_gate'], st['b_gate'], st['w_branch_a'], st['w_branch_b'],
                    st['w_branch_c'], layer)
    x1, h2 = _out_proj(merged, st['w_out'], layer, xc, xc_row0, lp['norm2_g'], mod4, True, None)
    x1, h2 = _out_proj(merged, st['w_out'], layer, xl, xl_row0, lp['norm2_g'], mod4, False, (x1, h2))

    slot_et, gcol, cuts = _router(h2, lp['w_router'].T.astype(BF16))
    xg = _dispatch(h2, slot_et, cuts)
    yg = _ffn(xg, lp['moe_w1_all'], lp['moe_w3_all'], lp['moe_w2_all'],
              gcol.reshape(N_EXPERTS, SLOTS_PER_EXPERT, 1), layer)
    x2 = _combine(slot_et.transpose(0, 2, 1), cuts, yg, x1, mod4, final_g)

    fin_shape = (ngc, 2, SUBLANES, S5_GROUPS, S5_STATE)
    s_re = fin_re.reshape(fin_shape).transpose(0, 2, 1, 3, 4).reshape(BATCH, 2, S5_GROUPS, S5_STATE)
    s_im = fin_im.reshape(fin_shape).transpose(0, 2, 1, 3, 4).reshape(BATCH, 2, S5_GROUPS, S5_STATE)
    return x2, (new_k, new_v), s_re, s_im


def kernel(x_prompt, x_sample, cache_k, cache_v, state_s5_re, state_s5_im, c, c_ctx, norm1_g, norm2_g, final_norm_g, w_ada, b_ada, w_in, s5_lam_re, s5_lam_im, s5_log_dt, s5_b_re, s5_b_im, s5_c_re, s5_c_im, s5_d, w_glu, b_glu, lam_q1, lam_k1, lam_q2, lam_k2, subln_g, w_branch_a, w_branch_b, w_branch_c, w_gate, b_gate, w_out, w_router, moe_w1, moe_w3, moe_w2):
    params = dict(norm1_g=norm1_g, norm2_g=norm2_g,
                  s5_lam_re=s5_lam_re, s5_lam_im=s5_lam_im, s5_log_dt=s5_log_dt,
                  s5_b_re=s5_b_re, s5_b_im=s5_b_im, s5_c_re=s5_c_re, s5_c_im=s5_c_im, s5_d=s5_d,
                  w_glu=w_glu, b_glu=b_glu, lam_q1=lam_q1, lam_k1=lam_k1, lam_q2=lam_q2, lam_k2=lam_k2,
                  subln_g=subln_g, w_router=w_router)
    stacked = dict(w_in=w_in.astype(BF16), w_gate=w_gate.astype(BF16), b_gate=b_gate.reshape(DEPTH, 1, 3 * D),
                   w_branch_a=w_branch_a.astype(BF16), w_branch_b=w_branch_b.astype(BF16),
                   w_branch_c=w_branch_c.astype(BF16), w_out=w_out.astype(BF16))
    cc = jnp.zeros((N_MOD_ROWS, D), F32).at[0].set(c_ctx).at[1:1 + DEC_BATCH].set(c)
    mods = _ada(cc, w_ada, b_ada)
    rope_cos, rope_sin = _rope_tables(DEC_SEQ)
    c_c, s_c = _dft_tables(FNET_GROUP_CH)
    cs = jnp.concatenate([c_c, s_c], axis=1).astype(BF16)
    tables = (rope_cos, rope_sin, _fnet_tables(SEQ), _fnet_tables(DEC_SEQ), cs, _s5_permutations())
    cache_k4 = cache_k.reshape(DEC_BATCH, DEPTH, PAST_LEN, QK_WIDTH)
    cache_v4 = cache_v.reshape(DEC_BATCH, DEPTH, PAST_LEN, V_WIDTH)
    x_parts = ((x_prompt.reshape(M_CTX, D), 0), (x_sample.reshape(M_LAT, D), 0))
    new_kv = None
    s_res, s_ims = [], []
    for l in range(DEPTH):
        lp = {name: arr[l] for name, arr in params.items()}
        lp['moe_w1_all'], lp['moe_w3_all'], lp['moe_w2_all'] = moe_w1, moe_w3, moe_w2
        lp['stacked'] = stacked
        lam_init = 0.8 - 0.6 * math.exp(-0.3 * l)
        x, new_kv, s_re, s_im = _layer(x_parts, mods[l], lp, l, lam_init, tables, cache_k4, cache_v4,
                                       state_s5_re[:, l], state_s5_im[:, l], new_kv,
                                       final_norm_g if l == DEPTH - 1 else None)
        x_parts = ((x, 0), (x, M_CTX))
        s_res.append(s_re)
        s_ims.append(s_im)
    y_prompt = x[0].reshape(BATCH, SEQ, D)
    y_sample = x[1].reshape(DEC_BATCH, DEC_SEQ, D)
    kv_shape = (BATCH, DEPTH, SEQ, N_HEADS, VALUE_DIM)
    return (y_prompt, y_sample, new_kv[0].reshape(kv_shape), new_kv[1].reshape(kv_shape),
            jnp.stack(s_res, axis=1), jnp.stack(s_ims, axis=1))
```

```python
import functools
import math

import jax
import jax.numpy as jnp
import numpy as np
from jax import lax
from jax.experimental import pallas as pl
from jax.experimental.pallas import tpu as pltpu

F32 = jnp.float32
BF16 = jnp.bfloat16

D = 2048
BATCH, SEQ = 32, 256
DEC_BATCH, DEC_SEQ = 8, 2048
DEPTH = 2
PAST_LEN = 256
GRID_W = 64
EPS = 1e-6
S5_GROUP_CH, S5_GROUPS, S5_STATE = 16, 32, 64
S5_WIDTH = S5_GROUPS * S5_GROUP_CH
FNET_GROUPS, FNET_GROUP_CH = 4, 128
FNET_WIDTH = FNET_GROUPS * FNET_GROUP_CH
N_HEADS, HEAD_DIM = 8, 64
VALUE_DIM = 2 * HEAD_DIM
QK_WIDTH = N_HEADS * 2 * HEAD_DIM
V_WIDTH = N_HEADS * VALUE_DIM
IN_WIDTH = S5_WIDTH + FNET_WIDTH + 2 * QK_WIDTH + V_WIDTH
ROPE_BASE = 10000.0
ROPE_AXIS_DIM = HEAD_DIM // 2
N_EXPERTS = 16
EXPERT_FF = 2048
CAPACITY_FACTOR = 2

M_CTX = BATCH * SEQ
M_LAT = DEC_BATCH * DEC_SEQ
M = M_CTX + M_LAT
N_MOD_ROWS = 16

LANES = 128
SUBLANES = 8
BF16_SUBLANES = 16
VMEM_LIMIT = 56 * 1024 * 1024

ROUTE_TILE = 2048
ROUTE_SLOTS = CAPACITY_FACTOR * ROUTE_TILE // N_EXPERTS
N_ROUTE_TILES = M // ROUTE_TILE
SLOTS_PER_EXPERT = N_ROUTE_TILES * ROUTE_SLOTS

S5_BLK = 4
S5_BLK_STATES = 512

ANY_SPEC = pl.BlockSpec(memory_space=pl.ANY)


def _cparams(sem):
    return pltpu.CompilerParams(dimension_semantics=sem, vmem_limit_bytes=VMEM_LIMIT)


def _sigmoid(x):
    return 1.0 / (1.0 + jnp.exp(-x))


def _part(is_ctx):
    return (M_CTX, 0) if is_ctx else (M_LAT, M_CTX)


def _mod_row_fn(is_ctx, tm):
    if is_ctx:
        return lambda i: 0
    return lambda i: 1 + i // (DEC_SEQ // tm)


def _ada_kernel(c_ref, w_ref, b_ref, o_ref):
    c = c_ref[...]
    s = (c * _sigmoid(c)).astype(BF16)
    o_ref[...] = jnp.dot(s, w_ref[...].astype(BF16), preferred_element_type=F32) + b_ref[...]


def _ada(cc, w_ada, b_ada):
    tn = 1024
    n = 6 * D
    return pl.pallas_call(
        _ada_kernel,
        grid=(DEPTH, n // tn),
        in_specs=[
            pl.BlockSpec((N_MOD_ROWS, D), lambda l, j: (0, 0)),
            pl.BlockSpec((None, D, tn), lambda l, j: (l, 0, j)),
            pl.BlockSpec((None, 1, tn), lambda l, j: (l, 0, j)),
        ],
        out_specs=pl.BlockSpec((None, N_MOD_ROWS, tn), lambda l, j: (l, 0, j)),
        out_shape=jax.ShapeDtypeStruct((DEPTH, N_MOD_ROWS, n), F32),
        compiler_params=_cparams(("arbitrary", "arbitrary")),
        name="ada",
    )(cc, w_ada, b_ada.reshape(DEPTH, 1, n))


NP_TM, NP_TN = 1024, 1024


NP_NJ = IN_WIDTH // NP_TN


def _norm_proj_kernel(x_ref, g_ref, sc_ref, sh_ref, w_ref, *rest, is_ctx, n_alias, tm):
    h_ref, u_ref, ua_ref, ha_ref, hb_ref = rest[n_alias:]
    i = pl.program_id(0)
    j = pl.program_id(1)
    q = tm // NP_NJ
    rows = pl.ds(pl.multiple_of(j * q, q), q)

    def norm_quarter(dst_ref):
        x = x_ref[rows, :]
        r = x * lax.rsqrt(jnp.mean(x * x, axis=-1, keepdims=True) + EPS)
        dst_ref[rows, :] = ((r * g_ref[...]) * (1.0 + sc_ref[...]) + sh_ref[...]).astype(BF16)

    def project(src_ref):
        h_ref[rows, :] = src_ref[rows, :]
        u_ref[...] = jnp.dot(src_ref[...], w_ref[...], preferred_element_type=F32)

    @pl.when(i == 0)
    def _():
        norm_quarter(ha_ref)

    @pl.when((i > 0) & (i % 2 == 0))
    def _():
        norm_quarter(ha_ref)
        project(hb_ref)

    @pl.when(i % 2 == 1)
    def _():
        norm_quarter(hb_ref)
        project(ha_ref)

    @pl.when((i > 0) & (j == 0))
    def _():
        if is_ctx:
            for b in range(tm // SEQ):
                ua_ref[:, b * S5_WIDTH:(b + 1) * S5_WIDTH] = u_ref[b * SEQ:(b + 1) * SEQ, :S5_WIDTH]
        else:
            ua_ref[...] = u_ref[:, :S5_WIDTH]


def _norm_proj(x, x_row0, g, mod4, w_in, layer, is_ctx, prev):
    tm, tn = NP_TM, NP_TN
    assert S5_WIDTH <= tn
    rows, row0 = _part(is_ctx)
    n_i = rows // tm
    xoff, ooff = x_row0 // tm, row0 // tm
    mrow = _mod_row_fn(is_ctx, tm)
    cur = lambda i: jnp.minimum(i, n_i - 1)
    prv = lambda i: jnp.maximum(i - 1, 0)
    prev = list(prev) if prev is not None else []
    in_specs = [
        pl.BlockSpec((tm, D), lambda i, j: (xoff + cur(i), 0)),
        pl.BlockSpec((1, D), lambda i, j: (0, 0)),
        pl.BlockSpec((None, None, 1, D), lambda i, j: (mrow(cur(i)), 1, 0, 0)),
        pl.BlockSpec((None, None, 1, D), lambda i, j: (mrow(cur(i)), 0, 0, 0)),
        pl.BlockSpec((None, D, tn), lambda i, j: (layer, 0, j)),
    ] + [ANY_SPEC] * len(prev)
    out_specs = [
        pl.BlockSpec((tm, D), lambda i, j: (ooff + prv(i), 0)),
        pl.BlockSpec((tm, tn), lambda i, j: (ooff + prv(i), j)),
    ]
    out_shape = [jax.ShapeDtypeStruct((M, D), BF16), jax.ShapeDtypeStruct((M, IN_WIDTH), F32)]
    if is_ctx:
        nbt = tm // SEQ
        out_specs.append(pl.BlockSpec((SEQ, nbt * S5_WIDTH), lambda i, j: (0, prv(i))))
        out_shape.append(jax.ShapeDtypeStruct((SEQ, BATCH * S5_WIDTH), F32))
        aliases = {}
    else:
        tpb = DEC_SEQ // tm
        out_specs.append(pl.BlockSpec((tm, S5_WIDTH), lambda i, j: (prv(i) % tpb, prv(i) // tpb)))
        out_shape.append(jax.ShapeDtypeStruct((DEC_SEQ, DEC_BATCH * S5_WIDTH), F32))
        aliases = {5: 0, 6: 1}
    return pl.pallas_call(
        functools.partial(_norm_proj_kernel, is_ctx=is_ctx, n_alias=len(prev), tm=tm),
        grid=(n_i + 1, NP_NJ),
        in_specs=in_specs,
        out_specs=out_specs,
        out_shape=out_shape,
        scratch_shapes=[pltpu.VMEM((tm, D), BF16), pltpu.VMEM((tm, D), BF16)],
        input_output_aliases=aliases,
        compiler_params=_cparams(("arbitrary", "arbitrary")),
        name="norm_proj",
    )(x, g.reshape(1, D), mod4, mod4, w_in, *prev)


S5_CHUNK = 64


def _s5_kernel(u_ref, pin_ref, pout_ref, wb_ref, wc_ref, are_ref, aim_ref, h0re_ref, h0im_ref,
               y_ref, finre_ref, finim_ref, xs0_ref, xs1_ref, xs2_ref, xs3_ref, stre_ref, stim_ref):
    c = pl.program_id(2)
    nc = pl.num_programs(2)
    bs = S5_BLK_STATES
    chunk = S5_CHUNK
    xs_refs = (xs0_ref, xs1_ref, xs2_ref, xs3_ref)

    @pl.when(c == 0)
    def _():
        stre_ref[...] = h0re_ref[...]
        stim_ref[...] = h0im_ref[...]

    ustack = jnp.concatenate(
        [u_ref[:, j * S5_WIDTH:(j + 1) * S5_WIDTH] for j in range(SUBLANES)], axis=0).astype(BF16)
    up = jnp.dot(pin_ref[...], ustack, preferred_element_type=F32).astype(BF16)

    for blk in range(S5_BLK):
        xs_refs[blk][...] = jnp.dot(up[:, blk * LANES:(blk + 1) * LANES], wb_ref[blk],
                                    preferred_element_type=F32)

    for blk in range(S5_BLK):
        xs_ref = xs_refs[blk]
        ar = are_ref[:, blk * bs:(blk + 1) * bs]
        ai = aim_ref[:, blk * bs:(blk + 1) * bs]
        sr = stre_ref[:, blk * bs:(blk + 1) * bs]
        si = stim_ref[:, blk * bs:(blk + 1) * bs]
        for i in range(chunk):
            r8 = slice(i * SUBLANES, (i + 1) * SUBLANES)
            nr = ar * sr - ai * si + xs_ref[r8, 0:bs]
            ni = ar * si + ai * sr + xs_ref[r8, bs:2 * bs]
            xs_ref[r8, 0:bs] = nr
            xs_ref[r8, bs:2 * bs] = ni
            sr, si = nr, ni
        stre_ref[:, blk * bs:(blk + 1) * bs] = sr
        stim_ref[:, blk * bs:(blk + 1) * bs] = si

    y = jnp.concatenate(
        [jnp.dot(xs_refs[blk][...].astype(BF16), wc_ref[blk], preferred_element_type=F32)
         for blk in range(S5_BLK)], axis=1)
    y_hi = y.astype(BF16)
    y_lo = (y - y_hi.astype(F32)).astype(BF16)
    ys = (jnp.dot(pout_ref[...], y_hi, preferred_element_type=F32)
          + jnp.dot(pout_ref[...], y_lo, preferred_element_type=F32))
    for j in range(SUBLANES):
        y_ref[:, j * S5_WIDTH:(j + 1) * S5_WIDTH] = ys[j * chunk:(j + 1) * chunk, :]

    @pl.when(c == nc - 1)
    def _():
        finre_ref[...] = stre_ref[...]
        finim_ref[...] = stim_ref[...]


def _s5_permutations():
    chunk = S5_CHUNK
    rows = chunk * SUBLANES
    r_out = lax.broadcasted_iota(jnp.int32, (rows, rows), 0)
    r_in = lax.broadcasted_iota(jnp.int32, (rows, rows), 1)
    step, seq = r_out // SUBLANES, r_out % SUBLANES
    fwd = r_in == seq * chunk + step
    bwd = r_in == seq * chunk + (chunk - 1 - step)
    p = jnp.stack([fwd, bwd]).astype(BF16)
    return p, p.transpose(0, 2, 1)


def _s5(u2d, perms, wb, wc, a_re, a_im, h0_re, h0_im):
    t_len = u2d.shape[0]
    ng = u2d.shape[1] // (SUBLANES * S5_WIDTH)
    chunk = S5_CHUNK
    nc = t_len // chunk
    rows = chunk * SUBLANES
    ns = S5_BLK * S5_BLK_STATES
    pin, pout = perms

    def ceff(d, c):
        return jnp.where(d == 0, c, nc - 1 - c)

    return pl.pallas_call(
        _s5_kernel,
        grid=(ng, 2, nc),
        in_specs=[
            pl.BlockSpec((chunk, SUBLANES * S5_WIDTH), lambda g, d, c: (ceff(d, c), g)),
            pl.BlockSpec((None, rows, rows), lambda g, d, c: (d, 0, 0)),
            pl.BlockSpec((None, rows, rows), lambda g, d, c: (d, 0, 0)),
            pl.BlockSpec((None, S5_BLK, LANES, 2 * S5_BLK_STATES), lambda g, d, c: (d, 0, 0, 0)),
            pl.BlockSpec((None, S5_BLK, 2 * S5_BLK_STATES, LANES), lambda g, d, c: (d, 0, 0, 0)),
            pl.BlockSpec((None, SUBLANES, ns), lambda g, d, c: (d, 0, 0)),
            pl.BlockSpec((None, SUBLANES, ns), lambda g, d, c: (d, 0, 0)),
            pl.BlockSpec((None, None, SUBLANES, ns), lambda g, d, c: (g, d, 0, 0)),
            pl.BlockSpec((None, None, SUBLANES, ns), lambda g, d, c: (g, d, 0, 0)),
        ],
        out_specs=[
            pl.BlockSpec((None, chunk, SUBLANES * S5_WIDTH), lambda g, d, c: (d, ceff(d, c), g)),
            pl.BlockSpec((None, None, SUBLANES, ns), lambda g, d, c: (g, d, 0, 0)),
            pl.BlockSpec((None, None, SUBLANES, ns), lambda g, d, c: (g, d, 0, 0)),
        ],
        out_shape=[
            jax.ShapeDtypeStruct((2,) + u2d.shape, F32),
            jax.ShapeDtypeStruct((ng, 2, SUBLANES, ns), F32),
            jax.ShapeDtypeStruct((ng, 2, SUBLANES, ns), F32),
        ],
        scratch_shapes=[pltpu.VMEM((rows, 2 * S5_BLK_STATES), F32)] * S5_BLK + [
            pltpu.VMEM((SUBLANES, ns), F32),
            pltpu.VMEM((SUBLANES, ns), F32),
        ],
        compiler_params=_cparams(("arbitrary", "arbitrary", "arbitrary")),
        name="s5_scan",
    )(u2d, pin, pout, wb, wc, a_re, a_im, h0_re, h0_im)


def _s5_params(lp):
    lam_re = lp['s5_lam_re'].astype(F32)
    lam_im = lp['s5_lam_im'].astype(F32)
    dt = jnp.exp(lp['s5_log_dt'].astype(F32))[..., None]
    b_re = lp['s5_b_re'].astype(F32)
    b_im = lp['s5_b_im'].astype(F32)
    c_re = lp['s5_c_re'].astype(F32)
    c_im = lp['s5_c_im'].astype(F32)
    mag = jnp.exp(lam_re * dt)
    ab_re = mag * jnp.cos(lam_im * dt)
    ab_im = mag * jnp.sin(lam_im * dt)
    den = lam_re * lam_re + lam_im * lam_im
    f_re = ((ab_re - 1.0) * lam_re + ab_im * lam_im) / den
    f_im = (ab_im * lam_re - (ab_re - 1.0) * lam_im) / den
    bb_re = f_re[..., None] * b_re - f_im[..., None] * b_im
    bb_im = f_re[..., None] * b_im + f_im[..., None] * b_re
    gpb = S5_GROUPS // S5_BLK
    eye = jnp.eye(gpb, dtype=F32)

    def pack_b(bb):
        bb = bb.reshape(2, S5_BLK, gpb, S5_STATE, S5_GROUP_CH)
        w = jnp.einsum('dbgph,gk->dbghkp', bb, eye)
        return w.reshape(2, S5_BLK, gpb * S5_GROUP_CH, gpb * S5_STATE)

    def pack_c(cc):
        cc = cc.reshape(2, S5_BLK, gpb, S5_GROUP_CH, S5_STATE)
        w = jnp.einsum('dbghp,gk->dbkpgh', cc, eye)
        return w.reshape(2, S5_BLK, gpb * S5_STATE, gpb * S5_GROUP_CH)

    wb = jnp.concatenate([pack_b(bb_re), pack_b(bb_im)], axis=-1).astype(BF16)
    wc = jnp.concatenate([pack_c(c_re), -pack_c(c_im)], axis=-2).astype(BF16)
    ns = S5_GROUPS * S5_STATE
    a_re = jnp.broadcast_to(ab_re.reshape(2, 1, ns), (2, SUBLANES, ns))
    a_im = jnp.broadcast_to(ab_im.reshape(2, 1, ns), (2, SUBLANES, ns))
    return wb, wc, a_re, a_im


def _gelu_tanh(x):
    return 0.5 * x * (1.0 + jnp.tanh(math.sqrt(2.0 / math.pi) * (x + 0.044715 * (x * x * x))))


def _glu_kernel(y_ref, u_ref, d_ref, w_ref, b_ref, *rest):
    o_ref = rest[-1]
    y = y_ref[0] + y_ref[1] + d_ref[...] * u_ref[...]
    y = _gelu_tanh(y)
    z = jnp.dot(y.astype(BF16), w_ref[...], preferred_element_type=F32) + b_ref[...]
    o_ref[...] = (y * _sigmoid(z)).astype(BF16)


def _glu(y_dirs, u, d_row, w_glu, b_glu, is_ctx, tt, prev):
    t_len = y_dirs.shape[1]
    nb = y_dirs.shape[2] // S5_WIDTH
    ntt = t_len // tt
    off = _part(is_ctx)[1] // tt
    prev = [prev] if prev is not None else []
    return pl.pallas_call(
        _glu_kernel,
        grid=(nb, ntt),
        in_specs=[
            pl.BlockSpec((2, tt, S5_WIDTH), lambda b, t: (0, t, b)),
            pl.BlockSpec((tt, S5_WIDTH), lambda b, t: (off + b * ntt + t, 0)),
            pl.BlockSpec((1, S5_WIDTH), lambda b, t: (0, 0)),
            pl.BlockSpec((S5_WIDTH, S5_WIDTH), lambda b, t: (0, 0)),
            pl.BlockSpec((1, S5_WIDTH), lambda b, t: (0, 0)),
        ] + [ANY_SPEC] * len(prev),
        out_specs=pl.BlockSpec((tt, S5_WIDTH), lambda b, t: (off + b * ntt + t, 0)),
        out_shape=jax.ShapeDtypeStruct((M, S5_WIDTH), BF16),
        input_output_aliases={5: 0} if prev else {},
        compiler_params=_cparams(("arbitrary", "arbitrary")),
        name="s5_glu",
    )(y_dirs, u, d_row, w_glu, b_glu, *prev)


def _fnet_kernel(x_ref, cs_ref, dft_ref, *rest, t_len):
    o_ref, z_ref = rest[-2:]

    @pl.when(pl.program_id(1) == 0)
    def _():
        x = x_ref[...].astype(BF16)
        for g in range(FNET_GROUPS):
            z = jnp.dot(x[:, g * LANES:(g + 1) * LANES], cs_ref[...], preferred_element_type=F32)
            z_ref[0:t_len, g * LANES:(g + 1) * LANES] = z[:, :LANES].astype(BF16)
            z_ref[t_len:2 * t_len, g * LANES:(g + 1) * LANES] = z[:, LANES:].astype(BF16)

    scale = 1.0 / math.sqrt(t_len * FNET_GROUP_CH)
    o_ref[...] = (jnp.dot(dft_ref[...], z_ref[...], preferred_element_type=F32) * scale).astype(BF16)


def _dft_tables(n):
    j = lax.broadcasted_iota(jnp.int32, (n, n), 0)
    k = lax.broadcasted_iota(jnp.int32, (n, n), 1)
    ang = ((j * k) % n).astype(F32) * (2.0 * math.pi / n)
    return jnp.cos(ang), jnp.sin(ang)


def _fnet_tables(t_len):
    j = np.arange(t_len, dtype=np.int64)
    ang = ((j[:, None] * j[None, :]) % t_len).astype(np.float64) * (2.0 * math.pi / t_len)
    return jnp.asarray(np.concatenate([np.cos(ang), -np.sin(ang)], axis=1), dtype=BF16)


def _fnet(u, dft, cs, is_ctx, tr, prev):
    t_len = dft.shape[0]
    rows, row0 = _part(is_ctx)
    nb = rows // t_len
    ntr = t_len // tr
    off = row0 // t_len
    offr = row0 // tr
    col = S5_WIDTH // FNET_WIDTH
    prev = [prev] if prev is not None else []
    return pl.pallas_call(
        functools.partial(_fnet_kernel, t_len=t_len),
        grid=(nb, ntr),
        in_specs=[
            pl.BlockSpec((t_len, FNET_WIDTH), lambda b, r: (off + b, col)),
            pl.BlockSpec((FNET_GROUP_CH, 2 * FNET_GROUP_CH), lambda b, r: (0, 0)),
            pl.BlockSpec((tr, 2 * t_len), lambda b, r: (r, 0)),
        ] + [ANY_SPEC] * len(prev),
        out_specs=pl.BlockSpec((tr, FNET_WIDTH), lambda b, r: (offr + b * ntr + r, 0)),
        out_shape=jax.ShapeDtypeStruct((M, FNET_WIDTH), BF16),
        scratch_shapes=[pltpu.VMEM((2 * t_len, FNET_WIDTH), BF16)],
        input_output_aliases={3: 0} if prev else {},
        compiler_params=_cparams(("arbitrary", "arbitrary")),
        name="fnet",
    )(u, cs, dft, *prev)


def _rope(x, cos, sin):
    half = ROPE_AXIS_DIM // 2
    lane = lax.broadcasted_iota(jnp.int32, x.shape, 1)
    up = pltpu.roll(x, LANES - half, 1)
    dn = pltpu.roll(x, half, 1)
    rot = jnp.where((lane % ROPE_AXIS_DIM) < half, -up, dn)
    return x * cos + rot * sin


ATTN_KEY_CHUNK = 512
ATTN_ROW_BLOCK = 256


def _attn_kernel(lam_ref, q_ref, k_ref, v_ref, *rest, tq, t_len, hb, has_ctx, lam_init, n_alias, emit_kv):
    pos = 0
    if has_ctx:
        ck_ref, cv_ref, cos_ref, sin_ref = rest[:4]
        pos = 4
    g_ref = rest[pos]
    pos += 1 + n_alias
    o_ref = rest[pos]
    pos += 1
    if emit_kv:
        ko_ref, vo_ref = rest[pos:pos + 2]
        pos += 2
    kk_ref, vv_ref, qs_ref, s_ref = rest[pos:]
    qi = pl.program_id(2)
    if emit_kv:
        @pl.when(qi == 0)
        def _():
            ko_ref[...] = k_ref[...]
            vo_ref[...] = v_ref[...]
    n_keys = kk_ref.shape[1]
    kc, rb = ATTN_KEY_CHUNK, ATTN_ROW_BLOCK
    chunks = [(c0, min(c0 + kc, n_keys)) for c0 in range(0, n_keys, kc)]

    @pl.when(qi == 0)
    def _():
        for h in range(hb):
            hs = slice(h * LANES, (h + 1) * LANES)
            k = k_ref[:, hs]
            if has_ctx:
                k = _rope(k, cos_ref[...], sin_ref[...])
                kk_ref[h, t_len:, :] = ck_ref[:, hs].astype(BF16)
                vv_ref[h, t_len:, 0:VALUE_DIM] = cv_ref[:, hs].astype(BF16)
            kk_ref[h, 0:t_len, :] = k.astype(BF16)
            vv_ref[h, 0:t_len, 0:VALUE_DIM] = v_ref[:, hs].astype(BF16)
            vv_ref[h, :, VALUE_DIM:] = jnp.ones((n_keys, VALUE_DIM), BF16)

    lane = lax.broadcasted_iota(jnp.int32, (tq, LANES), 1)
    for h in range(hb):
        hs = slice(h * LANES, (h + 1) * LANES)
        q = q_ref[:, hs]
        if has_ctx:
            r0 = pl.multiple_of(qi * tq, tq)
            q = _rope(q, cos_ref[pl.ds(r0, tq), :], sin_ref[pl.ds(r0, tq), :])
        q = q * (HEAD_DIM ** -0.5)
        qs_ref[0, :, hs] = jnp.where(lane < HEAD_DIM, q, 0.0).astype(BF16)
        qs_ref[1, :, hs] = jnp.where(lane >= HEAD_DIM, q, 0.0).astype(BF16)

    lam = lam_ref[0]
    gain = g_ref[...]
    dn = (((1,), (1,)), ((), ()))

    unit = 0
    for r in range(tq // rb):
        rows = slice(r * rb, (r + 1) * rb)
        for h in range(hb):
            hs = slice(h * LANES, (h + 1) * LANES)
            o = None
            for n in range(2):
                slot = unit % 2
                unit += 1
                qn = qs_ref[n, rows, hs]
                mrun = None
                for c0, c1 in chunks:
                    s_c = lax.dot_general(qn, kk_ref[h, c0:c1, :], dn, preferred_element_type=F32)
                    s_ref[slot, :, c0:c1] = s_c
                    for l0 in range(0, c1 - c0, LANES):
                        mc = s_c[:, l0:l0 + LANES]
                        mrun = mc if mrun is None else jnp.maximum(mrun, mc)
                m_b = jnp.broadcast_to(jnp.max(mrun, axis=-1, keepdims=True), (rb, LANES))
                oa = None
                for c0, c1 in chunks:
                    p = jnp.concatenate(
                        [jnp.exp(s_ref[slot, :, l0:l0 + LANES] - m_b) for l0 in range(c0, c1, LANES)],
                        axis=1).astype(BF16)
                    part = jnp.dot(p, vv_ref[h, c0:c1, :], preferred_element_type=F32)
                    oa = part if oa is None else oa + part
                on = oa[:, :VALUE_DIM] * (1.0 / oa[:, VALUE_DIM:VALUE_DIM + 1])
                o = on if n == 0 else o - lam * on
            o = o * lax.rsqrt(jnp.mean(o * o, axis=-1, keepdims=True) + EPS)
            o_ref[rows, hs] = ((o * gain) * (1.0 - lam_init)).astype(BF16)


def _attention(u, lam, subln_g, lam_init, is_ctx, tq, hb, prev, ctx=None, kv_out=None):
    rows, row0 = _part(is_ctx)
    t_len = SEQ if is_ctx else DEC_SEQ
    nb = rows // t_len
    nq = t_len // tq
    off = row0 // t_len
    offq = row0 // tq
    width = hb * LANES
    qcol = (S5_WIDTH + FNET_WIDTH) // width
    kcol = qcol + QK_WIDTH // width
    vcol = kcol + QK_WIDTH // width
    has_ctx = ctx is not None
    n_keys = t_len + (PAST_LEN if has_ctx else 0)
    assert tq % ATTN_ROW_BLOCK == 0 and n_keys % (2 * LANES) == 0
    in_specs = [
        pl.BlockSpec(memory_space=pltpu.SMEM),
        pl.BlockSpec((tq, width), lambda b, h, i: (offq + b * nq + i, qcol + h)),
        pl.BlockSpec((t_len, width), lambda b, h, i: (off + b, kcol + h)),
        pl.BlockSpec((t_len, width), lambda b, h, i: (off + b, vcol + h)),
    ]
    args = [lam, u, u, u]
    if has_ctx:
        ck, cv, layer, cos, sin = ctx
        in_specs += [
            pl.BlockSpec((None, None, PAST_LEN, width), lambda b, h, i: (b, layer, 0, h)),
            pl.BlockSpec((None, None, PAST_LEN, width), lambda b, h, i: (b, layer, 0, h)),
            pl.BlockSpec((t_len, LANES), lambda b, h, i: (0, 0)),
            pl.BlockSpec((t_len, LANES), lambda b, h, i: (0, 0)),
        ]
        args += [ck, cv, cos, sin]
    in_specs.append(pl.BlockSpec((1, LANES), lambda b, h, i: (0, 0)))
    args.append(subln_g)
    aliases = {}
    n_fixed = len(args)
    if prev is not None:
        aliases[len(args)] = 0
        in_specs.append(ANY_SPEC)
        args.append(prev)
    out_specs = [pl.BlockSpec((tq, width), lambda b, h, i: (offq + b * nq + i, h))]
    out_shape = [jax.ShapeDtypeStruct((M, V_WIDTH), BF16)]
    if kv_out is not None:
        kv_layer, prev_kv = kv_out
        assert hb == N_HEADS and tq == t_len and nb == BATCH
        out_specs += [pl.BlockSpec((None, None, SEQ, QK_WIDTH), lambda b, h, i: (b, kv_layer, 0, 0)),
                      pl.BlockSpec((None, None, SEQ, V_WIDTH), lambda b, h, i: (b, kv_layer, 0, 0))]
        out_shape += [jax.ShapeDtypeStruct((BATCH, DEPTH, SEQ, QK_WIDTH), F32),
                      jax.ShapeDtypeStruct((BATCH, DEPTH, SEQ, V_WIDTH), F32)]
        if prev_kv is not None:
            for o_idx, arr in enumerate(prev_kv):
                aliases[len(args)] = 1 + o_idx
                in_specs.append(ANY_SPEC)
                args.append(arr)
    kern = functools.partial(_attn_kernel, tq=tq, t_len=t_len, hb=hb, has_ctx=has_ctx, lam_init=lam_init,
                             n_alias=len(args) - n_fixed, emit_kv=kv_out is not None)
    return pl.pallas_call(
        kern,
        grid=(nb, N_HEADS // hb, nq),
        in_specs=in_specs,
        out_specs=out_specs,
        out_shape=out_shape,
        scratch_shapes=[
            pltpu.VMEM((hb, n_keys, LANES), BF16),
            pltpu.VMEM((hb, n_keys, 2 * VALUE_DIM), BF16),
            pltpu.VMEM((2, tq, width), BF16),
            pltpu.VMEM((2, ATTN_ROW_BLOCK, n_keys), F32),
        ],
        input_output_aliases=aliases,
        compiler_params=_cparams(("arbitrary", "arbitrary", "arbitrary")),
        name="diff_attn",
    )(*args)


def _rope_tables(t_len):
    rows = t_len // GRID_W
    pos_row = jnp.broadcast_to(jnp.arange(rows, dtype=F32)[:, None], (rows, GRID_W)).reshape(-1)
    pos_col = jnp.broadcast_to(jnp.arange(GRID_W, dtype=F32)[None, :], (rows, GRID_W)).reshape(-1)
    inv = ROPE_BASE ** (-jnp.arange(0, ROPE_AXIS_DIM, 2, dtype=F32) / ROPE_AXIS_DIM)
    ang_r = pos_row[:, None] * inv
    ang_c = pos_col[:, None] * inv
    cos = jnp.concatenate([jnp.cos(ang_r), jnp.cos(ang_r), jnp.cos(ang_c), jnp.cos(ang_c)], axis=-1)
    sin = jnp.concatenate([jnp.sin(ang_r), jnp.sin(ang_r), jnp.sin(ang_c), jnp.sin(ang_c)], axis=-1)
    return jnp.concatenate([cos, cos], axis=-1), jnp.concatenate([sin, sin], axis=-1)


def _merge_kernel(h_ref, ya_ref, yb_ref, yc_ref, wga_ref, wgb_ref, wgc_ref, bga_ref, bgb_ref, bgc_ref,
                  wa_ref, wb_ref, wc_ref, o_ref):
    h = h_ref[...]
    acc = None
    for wg, bg, y, w in ((wga_ref, bga_ref, ya_ref, wa_ref), (wgb_ref, bgb_ref, yb_ref, wb_ref),
                         (wgc_ref, bgc_ref, yc_ref, wc_ref)):
        gate = _sigmoid(jnp.dot(h, wg[...], preferred_element_type=F32) + bg[...])
        term = gate * jnp.dot(y[...], w[...], preferred_element_type=F32)
        acc = term if acc is None else acc + term
    o_ref[...] = acc.astype(BF16)


def _merge(h, ya, yb, yc, w_gate, b_gate, w_a, w_b, w_c, layer):
    tm, tn = 1024, 512
    nj = D // tn
    row = lambda i, j: (i, 0)
    wspecs = [pl.BlockSpec((None, D, tn), lambda i, j, k=k: (layer, 0, k * nj + j)) for k in range(3)]
    bspecs = [pl.BlockSpec((None, 1, tn), lambda i, j, k=k: (layer, 0, k * nj + j)) for k in range(3)]
    return pl.pallas_call(
        _merge_kernel,
        grid=(M // tm, nj),
        in_specs=[
            pl.BlockSpec((tm, D), row),
            pl.BlockSpec((tm, S5_WIDTH), row),
            pl.BlockSpec((tm, FNET_WIDTH), row),
            pl.BlockSpec((tm, V_WIDTH), row),
            *wspecs, *bspecs,
            pl.BlockSpec((None, S5_WIDTH, tn), lambda i, j: (layer, 0, j)),
            pl.BlockSpec((None, FNET_WIDTH, tn), lambda i, j: (layer, 0, j)),
            pl.BlockSpec((None, V_WIDTH, tn), lambda i, j: (layer, 0, j)),
        ],
        out_specs=pl.BlockSpec((tm, tn), lambda i, j: (i, j)),
        out_shape=jax.ShapeDtypeStruct((M, D), BF16),
        compiler_params=_cparams(("arbitrary", "arbitrary")),
        name="merge",
    )(h, ya, yb, yc, w_gate, w_gate, w_gate, b_gate, b_gate, b_gate, w_a, w_b, w_c)


def _out_proj_kernel(m_ref, w_ref, x_ref, g1_ref, n2_ref, sc_ref, sh_ref, *rest):
    x1_ref, h2_ref = rest[-2:]
    x1 = x_ref[...] + g1_ref[...] * jnp.dot(m_ref[...], w_ref[...], preferred_element_type=F32)
    x1_ref[...] = x1
    r = x1 * lax.rsqrt(jnp.mean(x1 * x1, axis=-1, keepdims=True) + EPS)
    h2_ref[...] = ((r * n2_ref[...]) * (1.0 + sc_ref[...]) + sh_ref[...]).astype(BF16)


def _out_proj(merged, w_out, layer, x, x_row0, norm2_g, mod4, is_ctx, prev):
    tm = 512
    rows, row0 = _part(is_ctx)
    xoff, ooff = x_row0 // tm, row0 // tm
    mrow = _mod_row_fn(is_ctx, tm)
    orow = lambda i: (ooff + i, 0)
    mspec = lambda k: pl.BlockSpec((None, None, 1, D), lambda i, k=k: (mrow(i), k, 0, 0))
    prev = list(prev) if prev is not None else []
    return pl.pallas_call(
        _out_proj_kernel,
        grid=(rows // tm,),
        in_specs=[
            pl.BlockSpec((tm, D), orow),
            pl.BlockSpec((None, D, D), lambda i: (layer, 0, 0)),
            pl.BlockSpec((tm, D), lambda i: (xoff + i, 0)),
            mspec(2),
            pl.BlockSpec((1, D), lambda i: (0, 0)),
            mspec(4),
            mspec(3),
        ] + [ANY_SPEC] * len(prev),
        out_specs=[pl.BlockSpec((tm, D), orow), pl.BlockSpec((tm, D), orow)],
        out_shape=[jax.ShapeDtypeStruct((M, D), F32), jax.ShapeDtypeStruct((M, D), BF16)],
        input_output_aliases={7: 0, 8: 1} if prev else {},
        compiler_params=_cparams(("arbitrary",)),
        name="out_proj",
    )(merged, w_out, x, mod4, norm2_g.reshape(1, D), mod4, mod4, *prev)


def _segment_sum(x, seg):
    parts = []
    for s in range(ROUTE_TILE // seg):
        tot = jnp.sum(x[:, s * seg:(s + 1) * seg], axis=1, keepdims=True)
        parts.append(jnp.broadcast_to(tot, (x.shape[0], seg)))
    return parts[0] if len(parts) == 1 else jnp.concatenate(parts, axis=1)


def _segment_cumsum(x, seg):
    pos = lax.broadcasted_iota(jnp.int32, x.shape, 1) % seg
    sh = 1
    while sh < seg:
        x = x + jnp.where(pos >= sh, pltpu.roll(x, sh, 1), 0.0)
        sh *= 2
    return x


CUT_ROWS = 256


def _router_kernel(h_ref, wr_ref, slot_ref, gcol_ref, cuts_ref, *, seg):
    cap = CAPACITY_FACTOR * seg // N_EXPERTS
    logits = lax.dot_general(wr_ref[...], h_ref[...], (((1,), (1,)), ((), ())),
                             preferred_element_type=F32)
    p = jnp.exp(logits - jnp.max(logits, axis=0, keepdims=True))
    aff = p / jnp.sum(p, axis=0, keepdims=True)
    bits = pltpu.bitcast(aff, jnp.int32)

    def bis(i, thr):
        cand = thr | jnp.left_shift(jnp.int32(1), 29 - i)
        cnt = _segment_sum(jnp.where(bits >= cand, 1.0, 0.0), seg)
        return jnp.where(cnt >= cap, cand, thr)

    thr = lax.fori_loop(0, 30, bis, jnp.zeros(bits.shape, jnp.int32))
    gt = jnp.where(bits > thr, 1.0, 0.0)
    eq = jnp.where(bits == thr, 1.0, 0.0)
    need = cap - _segment_sum(gt, seg)
    eq_rank = _segment_cumsum(eq, seg)
    sel = gt + eq * jnp.where(eq_rank <= need, 1.0, 0.0)
    rank = _segment_cumsum(sel, seg)
    seg_id = lax.broadcasted_iota(jnp.int32, bits.shape, 1) // seg
    slot = seg_id * cap + rank.astype(jnp.int32) - 1
    slot = jnp.where(sel > 0.0, slot, -1)
    slot_ref[...] = slot
    ends = [rank[:, (k + 1) * CUT_ROWS - 1:(k + 1) * CUT_ROWS] for k in range(ROUTE_TILE // CUT_ROWS)]
    pad = jnp.zeros((N_EXPERTS, LANES - len(ends)), F32)
    cuts_ref[...] = jnp.concatenate(ends + [pad], axis=1).astype(jnp.int32)

    iota_c = lax.broadcasted_iota(jnp.int32, (ROUTE_SLOTS, ROUTE_TILE), 0)
    for e in range(N_EXPERTS):
        hit = iota_c == slot[e:e + 1, :]
        gcol_ref[e] = jnp.sum(jnp.where(hit, aff[e:e + 1, :], 0.0), axis=1, keepdims=True)


def _router(h2, w_router_t):
    def call(seg, tile0, ntiles):
        return pl.pallas_call(
            functools.partial(_router_kernel, seg=seg),
            grid=(ntiles,),
            in_specs=[
                pl.BlockSpec((ROUTE_TILE, D), lambda i: (tile0 + i, 0)),
                pl.BlockSpec((N_EXPERTS, D), lambda i: (0, 0)),
            ],
            out_specs=[
                pl.BlockSpec((None, N_EXPERTS, ROUTE_TILE), lambda i: (i, 0, 0)),
                pl.BlockSpec((N_EXPERTS, None, ROUTE_SLOTS, 1), lambda i: (0, i, 0, 0)),
                pl.BlockSpec((None, N_EXPERTS, LANES), lambda i: (i, 0, 0)),
            ],
            out_shape=[
                jax.ShapeDtypeStruct((ntiles, N_EXPERTS, ROUTE_TILE), jnp.int32),
                jax.ShapeDtypeStruct((N_EXPERTS, ntiles, ROUTE_SLOTS, 1), F32),
                jax.ShapeDtypeStruct((ntiles, N_EXPERTS, LANES), jnp.int32),
            ],
            compiler_params=_cparams(("arbitrary",)),
            name="router",
        )(h2, w_router_t)

    nct = M_CTX // ROUTE_TILE
    slot_c, g_c, _ = call(SEQ, 0, nct)
    slot_l, g_l, cuts_l = call(DEC_SEQ, nct, N_ROUTE_TILES - nct)
    return jnp.concatenate([slot_c, slot_l], axis=0), jnp.concatenate([g_c, g_l], axis=1), cuts_l


CTX_CAP = CAPACITY_FACTOR * SEQ // N_EXPERTS
CTX_SETS_PER_TILE = ROUTE_TILE // SEQ
N_CTX_TILES = M_CTX // ROUTE_TILE


def _dispatch_ctx_kernel(h_ref, slot_ref, o_ref):
    base = (pl.program_id(0) % CTX_SETS_PER_TILE) * CTX_CAP
    slot = slot_ref[...]
    iota_c = lax.broadcasted_iota(jnp.int32, (CTX_CAP, SEQ), 0) + base
    onehot = jnp.concatenate(
        [jnp.where(iota_c == slot[e:e + 1, :], 1.0, 0.0).astype(BF16) for e in range(N_EXPERTS)], axis=0)
    res = jnp.dot(onehot, h_ref[...], preferred_element_type=F32).astype(BF16)
    for e in range(N_EXPERTS):
        o_ref[e] = res[e * CTX_CAP:(e + 1) * CTX_CAP, :]


DISPATCH_EXPERTS = 4


DISPATCH_WINDOW = 2 * ROUTE_SLOTS * CUT_ROWS // ROUTE_TILE


def _dispatch_lat_kernel(cuts_ref, h_ref, slot_ref, prev_ref, o_ref, acc_ref):
    e0 = pl.program_id(1) * DISPATCH_EXPERTS
    n_blk = ROUTE_TILE // CUT_ROWS
    win = DISPATCH_WINDOW
    starts = {}
    fits = None
    for j in range(DISPATCH_EXPERTS):
        for k in range(n_blk):
            lo = cuts_ref[e0 + j, k - 1] if k > 0 else 0
            hi = cuts_ref[e0 + j, k]
            start = jnp.minimum((lo // BF16_SUBLANES) * BF16_SUBLANES, ROUTE_SLOTS - win)
            starts[j, k] = pl.multiple_of(start, BF16_SUBLANES)
            ok = hi - start <= win
            fits = ok if fits is None else jnp.logical_and(fits, ok)

    @pl.when(fits)
    def _():
        acc_ref[...] = jnp.zeros(acc_ref.shape, F32)
        iota_c = lax.broadcasted_iota(jnp.int32, (win, CUT_ROWS), 0)
        for k in range(n_blk):
            cols = slice(k * CUT_ROWS, (k + 1) * CUT_ROWS)
            onehot = jnp.concatenate(
                [jnp.where(iota_c + starts[j, k] == slot_ref[pl.ds(e0 + j, 1), cols], 1.0, 0.0).astype(BF16)
                 for j in range(DISPATCH_EXPERTS)], axis=0)
            part = jnp.dot(onehot, h_ref[cols, :], preferred_element_type=F32)
            for j in range(DISPATCH_EXPERTS):
                acc_ref[j, pl.ds(starts[j, k], win), :] += part[j * win:(j + 1) * win, :]
        o_ref[...] = acc_ref[...].astype(BF16)

    @pl.when(jnp.logical_not(fits))
    def _():
        iota_c = lax.broadcasted_iota(jnp.int32, (ROUTE_SLOTS, ROUTE_TILE), 0)
        onehot = jnp.concatenate(
            [jnp.where(iota_c == slot_ref[pl.ds(e0 + j, 1), :], 1.0, 0.0).astype(BF16)
             for j in range(DISPATCH_EXPERTS)], axis=0)
        res = jnp.dot(onehot, h_ref[...], preferred_element_type=F32).astype(BF16)
        for j in range(DISPATCH_EXPERTS):
            o_ref[j] = res[j * ROUTE_SLOTS:(j + 1) * ROUTE_SLOTS, :]


def _dispatch(h2, slot_et, cuts):
    xg = pl.pallas_call(
        _dispatch_ctx_kernel,
        grid=(BATCH,),
        in_specs=[
            pl.BlockSpec((SEQ, D), lambda s: (s, 0)),
            pl.BlockSpec((None, N_EXPERTS, SEQ), lambda s: (s // CTX_SETS_PER_TILE, 0, s % CTX_SETS_PER_TILE)),
        ],
        out_specs=pl.BlockSpec((N_EXPERTS, None, CTX_CAP, D), lambda s: (0, s, 0, 0)),
        out_shape=jax.ShapeDtypeStruct((N_EXPERTS, SLOTS_PER_EXPERT // CTX_CAP, CTX_CAP, D), BF16),
        compiler_params=_cparams(("arbitrary",)),
        name="moe_dispatch_ctx",
    )(h2, slot_et)
    return pl.pallas_call(
        _dispatch_lat_kernel,
        grid=(N_ROUTE_TILES - N_CTX_TILES, N_EXPERTS // DISPATCH_EXPERTS),
        in_specs=[
            pl.BlockSpec((None, N_EXPERTS, LANES), lambda i, e: (i, 0, 0), memory_space=pltpu.SMEM),
            pl.BlockSpec((ROUTE_TILE, D), lambda i, e: (N_CTX_TILES + i, 0)),
            pl.BlockSpec((None, N_EXPERTS, ROUTE_TILE), lambda i, e: (N_CTX_TILES + i, 0, 0)),
            ANY_SPEC,
        ],
        out_specs=pl.BlockSpec((DISPATCH_EXPERTS, ROUTE_SLOTS, D), lambda i, e: (e, N_CTX_TILES + i, 0)),
        out_shape=jax.ShapeDtypeStruct((N_EXPERTS, SLOTS_PER_EXPERT, D), BF16),
        scratch_shapes=[pltpu.VMEM((DISPATCH_EXPERTS, ROUTE_SLOTS, D), F32)],
        input_output_aliases={3: 0},
        compiler_params=_cparams(("arbitrary", "arbitrary")),
        name="moe_dispatch",
    )(cuts, h2, slot_et, xg.reshape(N_EXPERTS, SLOTS_PER_EXPERT, D))


def _ffn_kernel(x_ref, w1_ref, w3_ref, w2_ref, g_ref, o_ref, acc_ref, *, tn):
    f = pl.program_id(2)
    nf = pl.num_programs(2)

    @pl.when(f == 0)
    def _():
        acc_ref[...] = jnp.zeros(acc_ref.shape, F32)

    a = b = None
    for k in range(D // tn):
        ks = slice(k * tn, (k + 1) * tn)
        xk = x_ref[:, ks]
        pa = jnp.dot(xk, w1_ref[ks, :].astype(BF16), preferred_element_type=F32)
        pb = jnp.dot(xk, w3_ref[ks, :].astype(BF16), preferred_element_type=F32)
        a = pa if a is None else a + pa
        b = pb if b is None else b + pb
    hmid = ((a * _sigmoid(a)) * b).astype(BF16)
    for n in range(D // tn):
        cols = slice(n * tn, (n + 1) * tn)
        acc_ref[:, cols] += jnp.dot(hmid, w2_ref[:, cols].astype(BF16), preferred_element_type=F32)

    @pl.when(f == nf - 1)
    def _():
        o_ref[...] = (acc_ref[...] * g_ref[...]).astype(BF16)


def _ffn(xg, w1, w3, w2, gcol, layer):
    tr, tf, tn = 1024, 512, 512
    nr = SLOTS_PER_EXPERT // tr
    return pl.pallas_call(
        functools.partial(_ffn_kernel, tn=tn),
        grid=(N_EXPERTS, nr, EXPERT_FF // tf),
        in_specs=[
            pl.BlockSpec((None, tr, D), lambda e, r, f: (e, r, 0)),
            pl.BlockSpec((None, None, D, tf), lambda e, r, f: (layer, e, 0, f)),
            pl.BlockSpec((None, None, D, tf), lambda e, r, f: (layer, e, 0, f)),
            pl.BlockSpec((None, None, tf, D), lambda e, r, f: (layer, e, f, 0)),
            pl.BlockSpec((None, tr, 1), lambda e, r, f: (e, r, 0)),
        ],
        out_specs=pl.BlockSpec((None, tr, D), lambda e, r, f: (e, r, 0)),
        out_shape=jax.ShapeDtypeStruct((N_EXPERTS, SLOTS_PER_EXPERT, D), BF16),
        scratch_shapes=[pltpu.VMEM((tr, D), F32)],
        compiler_params=_cparams(("arbitrary", "arbitrary", "arbitrary")),
        name="moe_ffn",
    )(xg, w1, w3, w2, gcol)


def _combine_rows(slot, bases, n_slots, y_parts, x_ref, g2_ref, fg_ref):
    rows = slot.shape[0]
    iota_c = lax.broadcasted_iota(jnp.int32, (rows, n_slots), 1)
    onehot = jnp.concatenate(
        [jnp.where(iota_c + bases[e] == slot[:, e:e + 1], 1.0, 0.0).astype(BF16) for e in range(N_EXPERTS)],
        axis=1)
    y = jnp.concatenate(y_parts, axis=0)
    x = x_ref[...] + g2_ref[...] * jnp.dot(onehot, y, preferred_element_type=F32)
    if fg_ref is not None:
        x = (x * lax.rsqrt(jnp.mean(x * x, axis=-1, keepdims=True) + EPS)) * fg_ref[...]
    return x


def _combine_ctx_kernel(slot_ref, y_ref, x_ref, g2_ref, *rest, final):
    o_ref = rest[-1]
    base = (pl.program_id(0) % CTX_SETS_PER_TILE) * CTX_CAP
    o_ref[...] = _combine_rows(slot_ref[...], [base] * N_EXPERTS, CTX_CAP, [y_ref[e] for e in range(N_EXPERTS)],
                               x_ref, g2_ref, rest[0] if final else None)


def _combine_lat_kernel(cuts_ref, slot_ref, y_ref, x_ref, g2_ref, *rest, tr, final):
    o_ref = rest[-1]
    fg_ref = rest[0] if final else None
    r = pl.program_id(2)
    slot = slot_ref[pl.ds(pl.multiple_of(r * tr, tr), tr), :]
    win = 2 * ROUTE_SLOTS * tr // ROUTE_TILE
    bpt = tr // CUT_ROWS
    starts = []
    fits = None
    for e in range(N_EXPERTS):
        lo = jnp.where(r == 0, 0, cuts_ref[e, jnp.maximum(r * bpt - 1, 0)])
        hi = cuts_ref[e, (r + 1) * bpt - 1]
        start = jnp.minimum((lo // BF16_SUBLANES) * BF16_SUBLANES, ROUTE_SLOTS - win)
        starts.append(pl.multiple_of(start, BF16_SUBLANES))
        ok = hi - start <= win
        fits = ok if fits is None else jnp.logical_and(fits, ok)

    @pl.when(fits)
    def _():
        parts = [y_ref[e, pl.ds(starts[e], win), :] for e in range(N_EXPERTS)]
        o_ref[...] = _combine_rows(slot, starts, win, parts, x_ref, g2_ref, fg_ref)

    @pl.when(jnp.logical_not(fits))
    def _():
        parts = [y_ref[e] for e in range(N_EXPERTS)]
        o_ref[...] = _combine_rows(slot, [0] * N_EXPERTS, ROUTE_SLOTS, parts, x_ref, g2_ref, fg_ref)


def _combine(slot_te, cuts, yg, x1, mod4, final_g=None):
    final = final_g is not None
    extra = [final_g.reshape(1, D)] if final else []
    nct = N_CTX_TILES
    y_ctx = pl.pallas_call(
        functools.partial(_combine_ctx_kernel, final=final),
        grid=(BATCH,),
        in_specs=[
            pl.BlockSpec((None, SEQ, N_EXPERTS), lambda s: (s // CTX_SETS_PER_TILE, s % CTX_SETS_PER_TILE, 0)),
            pl.BlockSpec((N_EXPERTS, None, CTX_CAP, D), lambda s: (0, s, 0, 0)),
            pl.BlockSpec((SEQ, D), lambda s: (s, 0)),
            pl.BlockSpec((None, None, 1, D), lambda s: (0, 5, 0, 0)),
        ] + [pl.BlockSpec((1, D), lambda s: (0, 0))] * len(extra),
        out_specs=pl.BlockSpec((SEQ, D), lambda s: (s, 0)),
        out_shape=jax.ShapeDtypeStruct((M_CTX if final else M, D), F32),
        compiler_params=_cparams(("arbitrary",)),
        name="moe_combine_ctx",
    )(slot_te, yg.reshape(N_EXPERTS, SLOTS_PER_EXPERT // CTX_CAP, CTX_CAP, D), x1, mod4, *extra)
    tn, tr = D, 256
    nr = ROUTE_TILE // tr
    orow0 = 0 if final else nct * nr
    in_specs = [
        pl.BlockSpec((None, N_EXPERTS, LANES), lambda i, j, r: (i, 0, 0), memory_space=pltpu.SMEM),
        pl.BlockSpec((None, ROUTE_TILE, N_EXPERTS), lambda i, j, r: (nct + i, 0, 0)),
        pl.BlockSpec((N_EXPERTS, None, ROUTE_SLOTS, tn), lambda i, j, r: (0, nct + i, 0, j)),
        pl.BlockSpec((tr, tn), lambda i, j, r: ((nct + i) * nr + r, j)),
        pl.BlockSpec((None, None, 1, tn), lambda i, j, r: (1 + i, 5, 0, j)),
    ]
    if final:
        in_specs.append(pl.BlockSpec((1, D), lambda i, j, r: (0, 0)))
        args, aliases = extra, {}
    else:
        in_specs.append(ANY_SPEC)
        args, aliases = [y_ctx], {5: 0}
    y_lat = pl.pallas_call(
        functools.partial(_combine_lat_kernel, tr=tr, final=final),
        grid=(N_ROUTE_TILES - nct, D // tn, nr),
        in_specs=in_specs,
        out_specs=pl.BlockSpec((tr, tn), lambda i, j, r: (orow0 + i * nr + r, j)),
        out_shape=jax.ShapeDtypeStruct((M_LAT if final else M, D), F32),
        input_output_aliases=aliases,
        compiler_params=_cparams(("arbitrary", "arbitrary", "arbitrary")),
        name="moe_combine",
    )(cuts, slot_te, yg.reshape(N_EXPERTS, N_ROUTE_TILES, ROUTE_SLOTS, D), x1, mod4, *args)
    return (y_ctx, y_lat) if final else y_lat


def _layer(x_parts, mod, lp, layer, lam_init, tables, cache_k4, cache_v4, st_re, st_im, new_kv, final_g):
    (xc, xc_row0), (xl, xl_row0) = x_parts
    rope_cos, rope_sin, dft_ctx, dft_lat, cs, perms = tables
    mod4 = mod.reshape(N_MOD_ROWS, 6, 1, D)
    w_in = lp['stacked']['w_in']
    h, u, ua_ctx = _norm_proj(xc, xc_row0, lp['norm1_g'], mod4, w_in, layer, True, None)
    h, u, ua_lat = _norm_proj(xl, xl_row0, lp['norm1_g'], mod4, w_in, layer, False, (h, u))

    wb, wc, a_re, a_im = _s5_params(lp)
    ns = S5_GROUPS * S5_STATE
    ngc = BATCH // SUBLANES
    zeros = jnp.zeros((ngc, 2, SUBLANES, ns), F32)
    y_ctx, fin_re, fin_im = _s5(ua_ctx, perms, wb, wc, a_re, a_im, zeros, zeros)
    h0_re = st_re.reshape(DEC_BATCH, 2, ns).transpose(1, 0, 2)[None]
    h0_im = st_im.reshape(DEC_BATCH, 2, ns).transpose(1, 0, 2)[None]
    y_lat, _, _ = _s5(ua_lat, perms, wb, wc, a_re, a_im, h0_re, h0_im)
    d_row = lp['s5_d'].reshape(1, S5_WIDTH)
    w_glu = lp['w_glu'].astype(BF16)
    b_glu = lp['b_glu'].reshape(1, S5_WIDTH)
    ya = _glu(y_ctx, u, d_row, w_glu, b_glu, True, SEQ, None)
    ya = _glu(y_lat, u, d_row, w_glu, b_glu, False, 512, ya)

    yb = _fnet(u, dft_ctx, cs, True, SEQ, None)
    yb = _fnet(u, dft_lat, cs, False, 512, yb)

    lam = (jnp.exp(jnp.sum(lp['lam_q1'].astype(F32) * lp['lam_k1'].astype(F32)))
           - jnp.exp(jnp.sum(lp['lam_q2'].astype(F32) * lp['lam_k2'].astype(F32))) + lam_init).reshape(1)
    subln = lp['subln_g'].reshape(1, VALUE_DIM)
    yc, new_k, new_v = _attention(u, lam, subln, lam_init, True, SEQ, N_HEADS, None, kv_out=(layer, new_kv))
    (yc,) = _attention(u, lam, subln, lam_init, False, DEC_SEQ, 1, yc,
                    ctx=(cache_k4, cache_v4, layer, rope_cos, rope_sin))

    st = lp['stacked']
    merged = _merge(h, ya, yb, yc, st['w_gate'], st['b_gate'], st['w_branch_a'], st['w_branch_b'],
                    st['w_branch_c'], layer)
    x1, h2 = _out_proj(merged, st['w_out'], layer, xc, xc_row0, lp['norm2_g'], mod4, True, None)
    x1, h2 = _out_proj(merged, st['w_out'], layer, xl, xl_row0, lp['norm2_g'], mod4, False, (x1, h2))

    slot_et, gcol, cuts = _router(h2, lp['w_router'].T.astype(BF16))
    xg = _dispatch(h2, slot_et, cuts)
    yg = _ffn(xg, lp['moe_w1_all'], lp['moe_w3_all'], lp['moe_w2_all'],
              gcol.reshape(N_EXPERTS, SLOTS_PER_EXPERT, 1), layer)
    x2 = _combine(slot_et.transpose(0, 2, 1), cuts, yg, x1, mod4, final_g)

    fin_shape = (ngc, 2, SUBLANES, S5_GROUPS, S5_STATE)
    s_re = fin_re.reshape(fin_shape).transpose(0, 2, 1, 3, 4).reshape(BATCH, 2, S5_GROUPS, S5_STATE)
    s_im = fin_im.reshape(fin_shape).transpose(0, 2, 1, 3, 4).reshape(BATCH, 2, S5_GROUPS, S5_STATE)
    return x2, (new_k, new_v), s_re, s_im


def kernel(x_prompt, x_sample, cache_k, cache_v, state_s5_re, state_s5_im, c, c_ctx, norm1_g, norm2_g, final_norm_g, w_ada, b_ada, w_in, s5_lam_re, s5_lam_im, s5_log_dt, s5_b_re, s5_b_im, s5_c_re, s5_c_im, s5_d, w_glu, b_glu, lam_q1, lam_k1, lam_q2, lam_k2, subln_g, w_branch_a, w_branch_b, w_branch_c, w_gate, b_gate, w_out, w_router, moe_w1, moe_w3, moe_w2):
    params = dict(norm1_g=norm1_g, norm2_g=norm2_g,
                  s5_lam_re=s5_lam_re, s5_lam_im=s5_lam_im, s5_log_dt=s5_log_dt,
                  s5_b_re=s5_b_re, s5_b_im=s5_b_im, s5_c_re=s5_c_re, s5_c_im=s5_c_im, s5_d=s5_d,
                  w_glu=w_glu, b_glu=b_glu, lam_q1=lam_q1, lam_k1=lam_k1, lam_q2=lam_q2, lam_k2=lam_k2,
                  subln_g=subln_g, w_router=w_router)
    stacked = dict(w_in=w_in.astype(BF16), w_gate=w_gate.astype(BF16), b_gate=b_gate.reshape(DEPTH, 1, 3 * D),
                   w_branch_a=w_branch_a.astype(BF16), w_branch_b=w_branch_b.astype(BF16),
                   w_branch_c=w_branch_c.astype(BF16), w_out=w_out.astype(BF16))
    cc = jnp.zeros((N_MOD_ROWS, D), F32).at[0].set(c_ctx).at[1:1 + DEC_BATCH].set(c)
    mods = _ada(cc, w_ada, b_ada)
    rope_cos, rope_sin = _rope_tables(DEC_SEQ)
    c_c, s_c = _dft_tables(FNET_GROUP_CH)
    cs = jnp.concatenate([c_c, s_c], axis=1).astype(BF16)
    tables = (rope_cos, rope_sin, _fnet_tables(SEQ), _fnet_tables(DEC_SEQ), cs, _s5_permutations())
    cache_k4 = cache_k.reshape(DEC_BATCH, DEPTH, PAST_LEN, QK_WIDTH)
    cache_v4 = cache_v.reshape(DEC_BATCH, DEPTH, PAST_LEN, V_WIDTH)
    x_parts = ((x_prompt.reshape(M_CTX, D), 0), (x_sample.reshape(M_LAT, D), 0))
    new_kv = None
    s_res, s_ims = [], []
    for l in range(DEPTH):
        lp = {name: arr[l] for name, arr in params.items()}
        lp['moe_w1_all'], lp['moe_w3_all'], lp['moe_w2_all'] = moe_w1, moe_w3, moe_w2
        lp['stacked'] = stacked
        lam_init = 0.8 - 0.6 * math.exp(-0.3 * l)
        x, new_kv, s_re, s_im = _layer(x_parts, mods[l], lp, l, lam_init, tables, cache_k4, cache_v4,
                                       state_s5_re[:, l], state_s5_im[:, l], new_kv,
                                       final_norm_g if l == DEPTH - 1 else None)
        x_parts = ((x, 0), (x, M_CTX))
        s_res.append(s_re)
        s_ims.append(s_im)
    y_prompt = x[0].reshape(BATCH, SEQ, D)
    y_sample = x[1].reshape(DEC_BATCH, DEC_SEQ, D)
    kv_shape = (BATCH, DEPTH, SEQ, N_HEADS, VALUE_DIM)
    return (y_prompt, y_sample, new_kv[0].reshape(kv_shape), new_kv[1].reshape(kv_shape),
            jnp.stack(s_res, axis=1), jnp.stack(s_ims, axis=1))
```
